```python
import math
import jax
import jax.numpy as jnp
from jax import lax
import numpy as np

D_MODEL = 1024
BATCH = 16
SEQ = 256
DEPTH = 4
DEC_BATCH = 8
DEC_SEQ = 2048
PAST_LEN = 512

GRID_W = 64
ROPE_BASE = 10000.0
NORM_EPS = 1e-6
Q_BLOCK = 128
F_GROUPS = 4
F_CH = 64
D_FOURIER = F_GROUPS * F_CH
DIFF_HEADS = 4
DIFF_DH = 64
D_DIFF = DIFF_HEADS * 2 * DIFF_DH
MLA_HEADS = 4
MLA_Q_RANK = 256
MLA_KV_RANK = 128
MLA_NOPE = 64
MLA_ROPE = 32
MLA_V = 64
D_MLA = MLA_HEADS * MLA_V
D_MIX = D_FOURIER + D_DIFF + D_MLA
IN_SPLITS = (D_FOURIER, D_DIFF, D_DIFF, D_DIFF, MLA_Q_RANK, MLA_KV_RANK, MLA_ROPE)
IN_COLS = sum(IN_SPLITS)
IN_OFFSETS = tuple(int(o) for o in np.cumsum(IN_SPLITS)[:-1])
N_EXPERTS = 16
EC_CAPACITY_FACTOR = 2
EXPERT_FF = 1024
N_MOD = 6

kernel_name = 'hybrid_fourier_diffattn_mla_ec_dit_step'


def rms_norm(x, g):
    xf = x.astype(jnp.float32)
    y = xf * lax.rsqrt(jnp.mean(xf * xf, axis=-1, keepdims=True) + NORM_EPS)
    return (y * g.astype(jnp.float32)).astype(x.dtype)


def axial_rope(n_tok, dim):
    rows = n_tok // GRID_W
    row = jnp.repeat(jnp.arange(rows, dtype=jnp.float32), GRID_W)
    col = jnp.tile(jnp.arange(GRID_W, dtype=jnp.float32), rows)
    nf = dim // 4
    inv = ROPE_BASE ** (-jnp.arange(nf, dtype=jnp.float32) / nf)
    ang = jnp.concatenate([row[:, None] * inv, col[:, None] * inv], axis=-1)
    return jnp.cos(ang), jnp.sin(ang)


def apply_rope(x, cos, sin):
    half = x.shape[-1] // 2
    shape = (x.shape[1],) + (1,) * (x.ndim - 3) + (half,)
    c = cos.reshape(shape).astype(x.dtype)
    s = sin.reshape(shape).astype(x.dtype)
    x1, x2 = x[..., :half], x[..., half:]
    return jnp.concatenate([x1 * c - x2 * s, x2 * c + x1 * s], axis=-1)


def blocked_over_queries(fn, q):
    B, sq = q.shape[:2]
    nb = sq // Q_BLOCK
    qb = jnp.moveaxis(q.reshape((B, nb, Q_BLOCK) + q.shape[2:]), 1, 0)
    out = lax.map(fn, qb)
    out = jnp.moveaxis(out, 0, 1)
    return out.reshape((B, sq) + out.shape[3:])


def diff_attention_block(qb, k, v, lam):
    s = jnp.einsum('bqhcd,bkhcd->bhcqk', qb, k).astype(jnp.float32) * (DIFF_DH ** -0.5)
    p = jax.nn.softmax(s, axis=-1).astype(v.dtype)
    o = jnp.einsum('bhcqk,bkhe->bqhce', p, v)
    return o[..., 0, :] - lam.astype(v.dtype) * o[..., 1, :]


def softmax_attention_block(qb, k, v, scale):
    s = jnp.einsum('bqhd,bkhd->bhqk', qb, k).astype(jnp.float32) * scale
    p = jax.nn.softmax(s, axis=-1).astype(v.dtype)
    return jnp.einsum('bhqk,bkhe->bqhe', p, v)


def token_mixers(h, layer_idx, ctx, rope_diff, rope_mla, w_in, w_four, diff_lambda, g_subln,
                 g_mla_q, w_mla_uq, g_mla_kv, w_mla_ukv, w_out):
    latent = ctx is not None
    B, n, _ = h.shape
    zf, zq, zk, zv, zcq, zckv, zkpe = jnp.split(h @ w_in, IN_OFFSETS, axis=-1)

    f = zf.reshape(B, n, F_GROUPS, F_CH).astype(jnp.float32)
    f = jnp.fft.fft2(f, axes=(1, 3), norm='ortho').real.astype(h.dtype)
    y_four = jnp.einsum('bngc,gce->bnge', f, w_four).reshape(B, n, D_FOURIER)

    q = zq.reshape(B, n, DIFF_HEADS, 2, DIFF_DH)
    k = zk.reshape(B, n, DIFF_HEADS, 2, DIFF_DH)
    v = zv.reshape(B, n, DIFF_HEADS, 2 * DIFF_DH)
    lam_init = 0.8 - 0.6 * math.exp(-0.3 * layer_idx)
    lf = diff_lambda.astype(jnp.float32)
    lam = jnp.exp(jnp.sum(lf[0] * lf[1])) - jnp.exp(jnp.sum(lf[2] * lf[3])) + lam_init
    if latent:
        q_r = apply_rope(q, *rope_diff)
        k_r = apply_rope(k, *rope_diff)
        k_all = jnp.concatenate([k_r, ctx[0].reshape(B, -1, DIFF_HEADS, 2, DIFF_DH)], axis=1)
        v_all = jnp.concatenate([v, ctx[1]], axis=1)
    else:
        q_r, k_all, v_all = q, k, v
    o = blocked_over_queries(lambda qb: diff_attention_block(qb, k_all, v_all, lam), q_r)
    y_diff = (rms_norm(o, g_subln) * (1.0 - lam_init)).reshape(B, n, D_DIFF)

    cq = rms_norm(zcq, g_mla_q)
    qm = (cq @ w_mla_uq).reshape(B, n, MLA_HEADS, MLA_NOPE + MLA_ROPE)
    q_nope, q_pe = qm[..., :MLA_NOPE], qm[..., MLA_NOPE:]
    ckv = rms_norm(zckv, g_mla_kv)
    kpe = zkpe
    if latent:
        q_pe = apply_rope(q_pe, *rope_mla)
        kpe_lat = apply_rope(kpe, *rope_mla)
        ckv_all = jnp.concatenate([ckv, ctx[2]], axis=1)
        kpe_all = jnp.concatenate([kpe_lat, ctx[3]], axis=1)
    else:
        ckv_all, kpe_all = ckv, kpe
    kv = (ckv_all @ w_mla_ukv).reshape(B, -1, MLA_HEADS, MLA_NOPE + MLA_V)
    k_nope, v_m = kv[..., :MLA_NOPE], kv[..., MLA_NOPE:]
    k_m = jnp.concatenate([k_nope, jnp.broadcast_to(kpe_all[:, :, None, :], k_nope.shape[:-1] + (MLA_ROPE,))], axis=-1)
    q_m = jnp.concatenate([q_nope, q_pe], axis=-1)
    mla_scale = (MLA_NOPE + MLA_ROPE) ** -0.5
    o_m = blocked_over_queries(lambda qb: softmax_attention_block(qb, k_m, v_m, mla_scale), q_m)
    y_mla = o_m.reshape(B, n, D_MLA)

    y = jnp.concatenate([y_four, y_diff, y_mla], axis=-1) @ w_out
    new_ctx = None if latent else (k.reshape(B, n, DIFF_HEADS, 2 * DIFF_DH), v, ckv, kpe)
    return y, new_ctx


def expert_choice_ffn(h, w_router, w_e_gate, w_e_up, w_e_down):
    B, n, D = h.shape
    cap = max(1, EC_CAPACITY_FACTOR * n // N_EXPERTS)
    aff = jax.nn.softmax(jnp.einsum('bnd,de->bne', h, w_router).astype(jnp.float32), axis=-1)
    gate, idx = lax.top_k(jnp.swapaxes(aff, 1, 2), cap)
    xs = jax.vmap(lambda hb, ib: hb[ib])(h, idx)
    a = jnp.einsum('becd,edf->becf', xs, w_e_gate)
    u = jnp.einsum('becd,edf->becf', xs, w_e_up)
    ye = jnp.einsum('becf,efd->becd', jax.nn.silu(a) * u, w_e_down) * gate[..., None].astype(h.dtype)
    return jax.vmap(lambda ib, yb: jnp.zeros((n, D), yb.dtype).at[ib.reshape(-1)].add(yb.reshape(-1, D)))(idx, ye)


def setup_inputs(seed: int = 0) -> dict:
    key = jax.random.key(seed)
    ks = jax.random.split(key, 26)
    f32 = jnp.float32
    nrm = lambda k, shape, s=1.0: (jax.random.normal(k, shape, f32) * s)
    return {
        'x_prompt': nrm(ks[0], (BATCH, SEQ, D_MODEL)),
        'x_sample': nrm(ks[1], (DEC_BATCH, DEC_SEQ, D_MODEL)),
        'cache_diff_k': nrm(ks[2], (DEC_BATCH, DEPTH, PAST_LEN, DIFF_HEADS, 2 * DIFF_DH)),
        'cache_diff_v': nrm(ks[3], (DEC_BATCH, DEPTH, PAST_LEN, DIFF_HEADS, 2 * DIFF_DH)),
        'cache_mla_ckv': nrm(ks[4], (DEC_BATCH, DEPTH, PAST_LEN, MLA_KV_RANK)),
        'cache_mla_kpe': nrm(ks[5], (DEC_BATCH, DEPTH, PAST_LEN, MLA_ROPE)),
        'c': nrm(ks[6], (DEC_BATCH, D_MODEL)),
        'c_ctx': nrm(ks[7], (D_MODEL,)),
        'w_ada': nrm(ks[8], (DEPTH, D_MODEL, N_MOD * D_MODEL), 0.5 * D_MODEL ** -0.5),
        'b_ada': nrm(ks[9], (DEPTH, N_MOD * D_MODEL), 0.02),
        'g_attn': 1.0 + nrm(ks[10], (DEPTH, D_MODEL), 0.1),
        'g_ffn': 1.0 + nrm(ks[11], (DEPTH, D_MODEL), 0.1),
        'w_in': nrm(ks[12], (DEPTH, D_MODEL, IN_COLS), D_MODEL ** -0.5),
        'w_four': nrm(ks[13], (DEPTH, F_GROUPS, F_CH, F_CH), F_CH ** -0.5),
        'diff_lambda': nrm(ks[14], (DEPTH, 4, DIFF_DH), 0.1),
        'g_subln': 1.0 + nrm(ks[15], (DEPTH, 2 * DIFF_DH), 0.1),
        'g_mla_q': 1.0 + nrm(ks[16], (DEPTH, MLA_Q_RANK), 0.1),
        'w_mla_uq': nrm(ks[17], (DEPTH, MLA_Q_RANK, MLA_HEADS * (MLA_NOPE + MLA_ROPE)), MLA_Q_RANK ** -0.5),
        'g_mla_kv': 1.0 + nrm(ks[18], (DEPTH, MLA_KV_RANK), 0.1),
        'w_mla_ukv': nrm(ks[19], (DEPTH, MLA_KV_RANK, MLA_HEADS * (MLA_NOPE + MLA_V)), MLA_KV_RANK ** -0.5),
        'w_out': nrm(ks[20], (DEPTH, D_MIX, D_MODEL), D_MIX ** -0.5),
        'w_router': nrm(ks[21], (DEPTH, D_MODEL, N_EXPERTS), D_MODEL ** -0.5),
        'w_e_gate': nrm(ks[22], (DEPTH, N_EXPERTS, D_MODEL, EXPERT_FF), D_MODEL ** -0.5),
        'w_e_up': nrm(ks[23], (DEPTH, N_EXPERTS, D_MODEL, EXPERT_FF), D_MODEL ** -0.5),
        'w_e_down': nrm(ks[24], (DEPTH, N_EXPERTS, EXPERT_FF, D_MODEL), EXPERT_FF ** -0.5),
        'g_final': 1.0 + nrm(ks[25], (D_MODEL,), 0.1),
    }


def reference(x_prompt, x_sample, cache_diff_k, cache_diff_v, cache_mla_ckv, cache_mla_kpe, c, c_ctx,
              w_ada, b_ada, g_attn, g_ffn, w_in, w_four, diff_lambda, g_subln, g_mla_q, w_mla_uq,
              g_mla_kv, w_mla_ukv, w_out, w_router, w_e_gate, w_e_up, w_e_down, g_final):

    def layer(x, cond, l, ctx, rope_diff, rope_mla):
        mod = (jax.nn.silu(cond) @ w_ada[l] + b_ada[l])[:, None, :]
        sh_a, sc_a, gt_a, sh_f, sc_f, gt_f = jnp.split(mod, N_MOD, axis=-1)
        h = rms_norm(x, g_attn[l]) * (1.0 + sc_a) + sh_a
        mix, new_ctx = token_mixers(h, l, ctx, rope_diff, rope_mla, w_in[l], w_four[l], diff_lambda[l],
                                    g_subln[l], g_mla_q[l], w_mla_uq[l], g_mla_kv[l], w_mla_ukv[l], w_out[l])
        x = x + gt_a * mix
        h = rms_norm(x, g_ffn[l]) * (1.0 + sc_f) + sh_f
        x = x + gt_f * expert_choice_ffn(h, w_router[l], w_e_gate[l], w_e_up[l], w_e_down[l])
        return x, new_ctx

    x = x_prompt
    cond_ctx = c_ctx[None, :]
    dk, dv, dckv, dkpe = [], [], [], []
    for l in range(DEPTH):
        x, (k_l, v_l, ckv_l, kpe_l) = layer(x, cond_ctx, l, None, None, None)
        dk.append(k_l)
        dv.append(v_l)
        dckv.append(ckv_l)
        dkpe.append(kpe_l)
    y_prompt = rms_norm(x, g_final)
    new_diff_k = jnp.stack(dk, axis=1)
    new_diff_v = jnp.stack(dv, axis=1)
    new_mla_ckv = jnp.stack(dckv, axis=1)
    new_mla_kpe = jnp.stack(dkpe, axis=1)

    n_lat = x_sample.shape[1]
    rope_diff = axial_rope(n_lat, DIFF_DH)
    rope_mla = axial_rope(n_lat, MLA_ROPE)
    x = x_sample
    for l in range(DEPTH):
        ctx = (cache_diff_k[:, l], cache_diff_v[:, l], cache_mla_ckv[:, l], cache_mla_kpe[:, l])
        x, _ = layer(x, c, l, ctx, rope_diff, rope_mla)
    y_sample = rms_norm(x, g_final)

    return (y_prompt, y_sample, new_diff_k, new_diff_v, new_mla_ckv, new_mla_kpe)
```

```python
import functools
import math

import jax
import jax.numpy as jnp
from jax import lax
from jax.experimental import pallas as pl
from jax.experimental.pallas import tpu as pltpu

BF = jnp.bfloat16
F32 = jnp.float32

GRID_W = 64
ROPE_BASE = 10000.0
NORM_EPS = 1e-6
F_GROUPS, F_CH = 4, 64
D_FOURIER = F_GROUPS * F_CH
DIFF_HEADS, DIFF_DH = 4, 64
D_DIFF = DIFF_HEADS * 2 * DIFF_DH
MLA_HEADS, MLA_Q_RANK, MLA_KV_RANK = 4, 256, 128
MLA_NOPE, MLA_ROPE, MLA_V = 64, 32, 64
HEAD_W = 128
D_MLA_P = MLA_HEADS * HEAD_W
N_EXPERTS = 16
EC_CAPACITY_FACTOR = 2
N_MOD = 6
IN_COLS_P = D_FOURIER + 3 * D_DIFF + MLA_Q_RANK + MLA_KV_RANK + HEAD_W
MIX_P = D_FOURIER + D_DIFF + D_MLA_P
VMEM_LIMIT = 56 * 1024 * 1024


def _cparams(n_axes, vmem=VMEM_LIMIT):
    return pltpu.CompilerParams(dimension_semantics=("arbitrary",) * n_axes, vmem_limit_bytes=vmem)


def _dot(a, b):
    return jnp.dot(a, b, preferred_element_type=F32)


def _dot_nt(a, b):
    return lax.dot_general(a, b, (((1,), (1,)), ((), ())), preferred_element_type=F32)


def _rms(x, g):
    return x * lax.rsqrt(jnp.mean(x * x, axis=-1, keepdims=True) + NORM_EPS) * g


def _rope(z, cos, sin_signed, half, group, lo):
    w = z.shape[1]
    reps = w // cos.shape[1]
    cos_w = jnp.concatenate([cos] * reps, axis=1) if reps > 1 else cos
    sin_w = jnp.concatenate([sin_signed] * reps, axis=1) if reps > 1 else sin_signed
    from_right = pltpu.roll(z, w - half, 1)
    from_left = pltpu.roll(z, half, 1)
    lane = lax.broadcasted_iota(jnp.int32, z.shape, 1) % group
    first = (lane >= lo) & (lane < lo + half)
    partner = jnp.where(first, from_right, from_left)
    return z * cos_w + partner * sin_w


def _mod_kernel(c_ref, w_ref, b_ref, o_ref):
    c = c_ref[...]
    a = (c * (1.0 / (1.0 + jnp.exp(-c)))).astype(BF)
    o_ref[0] = _dot(a, w_ref[0].astype(BF)) + b_ref[0]


def _modulation(cond, w_ada, b_ada):
    depth, d, n6 = w_ada.shape
    r = cond.shape[0]
    tn = 1536
    return pl.pallas_call(
        _mod_kernel,
        grid=(depth, n6 // tn),
        in_specs=[
            pl.BlockSpec((r, d), lambda l, j: (0, 0)),
            pl.BlockSpec((1, d, tn), lambda l, j: (l, 0, j)),
            pl.BlockSpec((1, 1, tn), lambda l, j: (l, 0, j)),
        ],
        out_specs=pl.BlockSpec((1, r, tn), lambda l, j: (l, 0, j)),
        out_shape=jax.ShapeDtypeStruct((depth, r, n6), F32),
        compiler_params=_cparams(2),
    )(cond, w_ada, b_ada.reshape(depth, 1, n6))


def _pre_kernel(latent, d, *refs):
    if latent:
        (x_ref, mod_ref, g_ref, win_ref, gq_ref, wuq_ref, gkv_ref, wk_ref, wv_ref,
         cd_ref, sd_ref, cm_ref, sm_ref,
         zf_ref, qd_ref, kd_ref, vd_ref, qm_ref, km_ref, vm_ref) = refs
    else:
        (x_ref, mod_ref, g_ref, win_ref, gq_ref, wuq_ref, gkv_ref, wk_ref, wv_ref,
         zf_ref, qd_ref, kd_ref, vd_ref, qm_ref, km_ref, vm_ref,
         k32_ref, v32_ref, ckv32_ref, kpe32_ref) = refs
    x = x_ref[0]
    sh = mod_ref[0, :, 0:d]
    sc = mod_ref[0, :, d:2 * d]
    h = _rms(x, g_ref[...]) * (1.0 + sc) + sh
    z = _dot(h.astype(BF), win_ref[...])
    o = 0
    zf = z[:, o:o + D_FOURIER]; o += D_FOURIER
    zq = z[:, o:o + D_DIFF]; o += D_DIFF
    zk = z[:, o:o + D_DIFF]; o += D_DIFF
    zv = z[:, o:o + D_DIFF]; o += D_DIFF
    zcq = z[:, o:o + MLA_Q_RANK]; o += MLA_Q_RANK
    zckv = z[:, o:o + MLA_KV_RANK]; o += MLA_KV_RANK
    kpe = z[:, o:o + HEAD_W]

    zf_ref[0] = zf.astype(BF)
    vd_ref[0] = zv.astype(BF)
    cq = _rms(zcq, gq_ref[...])
    qm = _dot(cq.astype(BF), wuq_ref[...])
    ckv = _rms(zckv, gkv_ref[...])
    ckv_b = ckv.astype(BF)
    k_nope = _dot(ckv_b, wk_ref[...])
    vm_ref[0] = _dot(ckv_b, wv_ref[...]).astype(BF)
    if latent:
        cd, sd, cm, sm = cd_ref[...], sd_ref[...], cm_ref[...], sm_ref[...]
        zq_r = _rope(zq, cd, sd, DIFF_DH // 2, DIFF_DH, 0)
        zk_r = _rope(zk, cd, sd, DIFF_DH // 2, DIFF_DH, 0)
        qm = _rope(qm, cm, sm, MLA_ROPE // 2, HEAD_W, MLA_NOPE)
        kpe_r = _rope(kpe, cm, sm, MLA_ROPE // 2, HEAD_W, MLA_NOPE)
    else:
        zq_r, zk_r, kpe_r = zq, zk, kpe
        k32_ref[0] = zk
        v32_ref[0] = zv
        ckv32_ref[0] = ckv
        kpe32_ref[0] = kpe[:, MLA_NOPE:MLA_NOPE + MLA_ROPE]
    qd_ref[0] = (zq_r * (DIFF_DH ** -0.5)).astype(BF)
    kd_ref[0] = zk_r.astype(BF)
    qm_ref[0] = qm.astype(BF)
    km_ref[0] = (k_nope + jnp.concatenate([kpe_r] * MLA_HEADS, axis=1)).astype(BF)


def _pre(latent, x, modv, g_attn, w_in_p, g_q, w_uq_p, g_kv, w_k_p, w_v_p, ropes):
    b, n, d = x.shape
    tm = min(256, n)
    bm = modv.shape[0]
    mod_idx = (lambda i, j: (i, 0, 0)) if bm > 1 else (lambda i, j: (0, 0, 0))
    const = lambda i, j: (0, 0)
    tok = lambda i, j: (i, j, 0)
    in_specs = [
        pl.BlockSpec((1, tm, d), tok),
        pl.BlockSpec((1, 1, modv.shape[2]), mod_idx),
        pl.BlockSpec((1, d), const),
        pl.BlockSpec(w_in_p.shape, const),
        pl.BlockSpec((1, MLA_Q_RANK), const),
        pl.BlockSpec(w_uq_p.shape, const),
        pl.BlockSpec((1, MLA_KV_RANK), const),
        pl.BlockSpec(w_k_p.shape, const),
        pl.BlockSpec(w_v_p.shape, const),
    ]
    args = [x, modv, g_attn.reshape(1, d), w_in_p, g_q.reshape(1, -1), w_uq_p, g_kv.reshape(1, -1), w_k_p, w_v_p]
    if latent:
        in_specs += [pl.BlockSpec((tm, HEAD_W), lambda i, j: (j, 0))] * 4
        args += list(ropes)
    widths = [D_FOURIER, D_DIFF, D_DIFF, D_DIFF, D_MLA_P, D_MLA_P, D_MLA_P]
    out_shape = [jax.ShapeDtypeStruct((b, n, w), BF) for w in widths]
    out_specs = [pl.BlockSpec((1, tm, w), tok) for w in widths]
    if not latent:
        for w in (D_DIFF, D_DIFF, MLA_KV_RANK, MLA_ROPE):
            out_shape.append(jax.ShapeDtypeStruct((b, n, w), F32))
            out_specs.append(pl.BlockSpec((1, tm, w), tok))
    return pl.pallas_call(
        functools.partial(_pre_kernel, latent, d),
        grid=(b, n // tm),
        in_specs=in_specs,
        out_specs=out_specs,
        out_shape=out_shape,
        compiler_params=_cparams(2),
    )(*args)


def _cache_kernel(dk_ref, dv_ref, ckv_ref, kpe_ref, wk_ref, wv_ref, place_ref, kd_ref, vd_ref, km_ref, vm_ref):
    kd_ref[0] = dk_ref[0, 0].astype(BF)
    vd_ref[0] = dv_ref[0, 0].astype(BF)
    ckv_b = ckv_ref[0, 0].astype(BF)
    kpe_w = _dot(kpe_ref[0, 0].astype(BF), place_ref[...])
    km_ref[0] = (_dot(ckv_b, wk_ref[...]) + kpe_w).astype(BF)
    vm_ref[0] = _dot(ckv_b, wv_ref[...]).astype(BF)


def _cache_prep(l, cache_dk, cache_dv, cache_ckv, cache_kpe, w_k_p, w_v_p, place):
    b, _, p, _ = cache_dk.shape
    at_l = lambda i: (i, l, 0, 0)
    const = lambda i: (0, 0)
    widths = [D_DIFF, D_DIFF, D_MLA_P, D_MLA_P]
    return pl.pallas_call(
        _cache_kernel,
        grid=(b,),
        in_specs=[
            pl.BlockSpec((1, 1, p, D_DIFF), at_l),
            pl.BlockSpec((1, 1, p, D_DIFF), at_l),
            pl.BlockSpec((1, 1, p, MLA_KV_RANK), at_l),
            pl.BlockSpec((1, 1, p, MLA_ROPE), at_l),
            pl.BlockSpec(w_k_p.shape, const),
            pl.BlockSpec(w_v_p.shape, const),
            pl.BlockSpec(place.shape, const),
        ],
        out_specs=[pl.BlockSpec((1, p, w), lambda i: (i, 0, 0)) for w in widths],
        out_shape=[jax.ShapeDtypeStruct((b, p, w), BF) for w in widths],
        compiler_params=_cparams(1),
    )(cache_dk, cache_dv, cache_ckv, cache_kpe, w_k_p, w_v_p, place)


def _fourier_kernel(scale, zf_ref, cn_ref, sn_ref, cc_ref, sc_ref, w_ref, o_ref, u_ref, v_ref):
    @pl.when(pl.program_id(1) == 0)
    def _():
        zf = zf_ref[0]
        u_ref[...] = _dot(zf, cc_ref[...]).astype(BF)
        v_ref[...] = _dot(zf, sc_ref[...]).astype(BF)

    f = (_dot(cn_ref[...], u_ref[...]) - _dot(sn_ref[...], v_ref[...])) * scale
    o_ref[0] = _dot(f.astype(BF), w_ref[...]).astype(BF)


def _fourier(zf, cn, sn, cc_bd, sc_bd, w_bd):
    b, n, w = zf.shape
    tm = min(512, n)
    const = lambda i, j: (0, 0)
    return pl.pallas_call(
        functools.partial(_fourier_kernel, 1.0 / math.sqrt(n * F_CH)),
        grid=(b, n // tm),
        in_specs=[
            pl.BlockSpec((1, n, w), lambda i, j: (i, 0, 0)),
            pl.BlockSpec((tm, n), lambda i, j: (j, 0)),
            pl.BlockSpec((tm, n), lambda i, j: (j, 0)),
            pl.BlockSpec((w, w), const),
            pl.BlockSpec((w, w), const),
            pl.BlockSpec((w, w), const),
        ],
        out_specs=pl.BlockSpec((1, tm, w), lambda i, j: (i, j, 0)),
        out_shape=jax.ShapeDtypeStruct((b, n, w), BF),
        scratch_shapes=[pltpu.VMEM((n, w), BF), pltpu.VMEM((n, w), BF)],
        compiler_params=_cparams(2),
    )(zf, cn, sn, cc_bd, sc_bd, w_bd)


def _softmax_pv(q, kv_parts, scale):
    ss = [_dot_nt(q, k) for k, _ in kv_parts]
    if scale != 1.0:
        ss = [s * scale for s in ss]
    m = jnp.max(ss[0], axis=-1, keepdims=True)
    for s in ss[1:]:
        m = jnp.maximum(m, jnp.max(s, axis=-1, keepdims=True))
    ps = [jnp.exp(s - m) for s in ss]
    den = jnp.sum(ps[0], axis=-1, keepdims=True)
    for p in ps[1:]:
        den = den + jnp.sum(p, axis=-1, keepdims=True)
    o = _dot(ps[0].astype(BF), kv_parts[0][1])
    for p, (_, v) in zip(ps[1:], kv_parts[1:]):
        o = o + _dot(p.astype(BF), v)
    return o * (1.0 / den)


def _attn_kernel(latent, lam_init, d, *refs):
    if latent:
        (x_ref, mod_ref, yf_ref, qd_ref, kd_ref, vd_ref, kdc_ref, vdc_ref, qm_ref, km_ref, vm_ref, kmc_ref, vmc_ref,
         lam_ref, gs_ref, wo_ref, o_ref, mix_ref) = refs
    else:
        (x_ref, mod_ref, yf_ref, qd_ref, kd_ref, vd_ref, qm_ref, km_ref, vm_ref,
         lam_ref, gs_ref, wo_ref, o_ref, mix_ref) = refs
    lf = lam_ref[...]
    lam = (jnp.exp(jnp.sum(lf[0:1] * lf[1:2], axis=-1, keepdims=True))
           - jnp.exp(jnp.sum(lf[2:3] * lf[3:4], axis=-1, keepdims=True)) + lam_init)
    mix_ref[:, 0:D_FOURIER] = yf_ref[0]
    gs = gs_ref[...]
    for hd in range(DIFF_HEADS):
        sl = slice(hd * HEAD_W, (hd + 1) * HEAD_W)
        parts = [(kd_ref[0, :, sl], vd_ref[0, :, sl])]
        if latent:
            parts.append((kdc_ref[0, :, sl], vdc_ref[0, :, sl]))
        qh = qd_ref[0, :, sl]
        lane = lax.broadcasted_iota(jnp.int32, qh.shape, 1)
        zero = jnp.zeros_like(qh)
        o0 = _softmax_pv(jnp.where(lane < DIFF_DH, qh, zero), parts, 1.0)
        o1 = _softmax_pv(jnp.where(lane >= DIFF_DH, qh, zero), parts, 1.0)
        o = o0 - lam * o1
        mix_ref[:, D_FOURIER + hd * HEAD_W:D_FOURIER + (hd + 1) * HEAD_W] = (_rms(o, gs) * (1.0 - lam_init)).astype(BF)
    mla_scale = (MLA_NOPE + MLA_ROPE) ** -0.5
    for hd in range(MLA_HEADS):
        sl = slice(hd * HEAD_W, (hd + 1) * HEAD_W)
        parts = [(km_ref[0, :, sl], vm_ref[0, :, sl])]
        if latent:
            parts.append((kmc_ref[0, :, sl], vmc_ref[0, :, sl]))
        o = _softmax_pv(qm_ref[0, :, sl], parts, mla_scale)
        base = D_FOURIER + D_DIFF + hd * HEAD_W
        mix_ref[:, base:base + HEAD_W] = o.astype(BF)
    y = _dot(mix_ref[...], wo_ref[...])
    gt = mod_ref[0, :, 2 * d:3 * d]
    o_ref[0] = x_ref[0] + gt * y


def _attention(latent, l, x, modv, yfour, qd, kd, vd, qm, km, vm, cache, diff_lambda_l, g_subln_l, w_out_p):
    b, n, d = x.shape
    tq = min(256, n)
    bm = modv.shape[0]
    mod_idx = (lambda i, j: (i, 0, 0)) if bm > 1 else (lambda i, j: (0, 0, 0))
    const = lambda i, j: (0, 0)
    tok = lambda i, j: (i, j, 0)
    full = lambda i, j: (i, 0, 0)
    in_specs = [
        pl.BlockSpec((1, tq, d), tok),
        pl.BlockSpec((1, 1, modv.shape[2]), mod_idx),
        pl.BlockSpec((1, tq, D_FOURIER), tok),
        pl.BlockSpec((1, tq, D_DIFF), tok),
        pl.BlockSpec((1, n, D_DIFF), full),
        pl.BlockSpec((1, n, D_DIFF), full),
    ]
    args = [x, modv, yfour, qd, kd, vd]
    if latent:
        kdc, vdc, kmc, vmc = cache
        p = kdc.shape[1]
        in_specs += [pl.BlockSpec((1, p, D_DIFF), full)] * 2
        args += [kdc, vdc]
    in_specs += [pl.BlockSpec((1, tq, D_MLA_P), tok), pl.BlockSpec((1, n, D_MLA_P), full), pl.BlockSpec((1, n, D_MLA_P), full)]
    args += [qm, km, vm]
    if latent:
        in_specs += [pl.BlockSpec((1, p, D_MLA_P), full)] * 2
        args += [kmc, vmc]
    in_specs += [pl.BlockSpec((4, DIFF_DH), const), pl.BlockSpec((1, HEAD_W), const), pl.BlockSpec(w_out_p.shape, const)]
    args += [diff_lambda_l, g_subln_l.reshape(1, HEAD_W), w_out_p]
    lam_init = 0.8 - 0.6 * math.exp(-0.3 * l)
    return pl.pallas_call(
        functools.partial(_attn_kernel, latent, lam_init, d),
        grid=(b, n // tq),
        in_specs=in_specs,
        out_specs=pl.BlockSpec((1, tq, d), tok),
        out_shape=jax.ShapeDtypeStruct((b, n, d), F32),
        scratch_shapes=[pltpu.VMEM((tq, MIX_P), BF)],
        compiler_params=_cparams(2),
    )(*args)


def _router_kernel(d, x_ref, mod_ref, g_ref, wr_ref, h_ref, aff_ref):
    sh = mod_ref[0, :, 3 * d:4 * d]
    sc = mod_ref[0, :, 4 * d:5 * d]
    hb = (_rms(x_ref[0], g_ref[...]) * (1.0 + sc) + sh).astype(BF)
    h_ref[0] = hb
    logits = _dot_nt(wr_ref[...], hb)
    m = jnp.max(logits, axis=0, keepdims=True)
    e = jnp.exp(logits - m)
    aff_ref[0] = e / jnp.sum(e, axis=0, keepdims=True)


def _router(x, modv, g_ffn_l, w_router_t):
    b, n, d = x.shape
    tm = min(256, n)
    bm = modv.shape[0]
    mod_idx = (lambda i, j: (i, 0, 0)) if bm > 1 else (lambda i, j: (0, 0, 0))
    const = lambda i, j: (0, 0)
    return pl.pallas_call(
        functools.partial(_router_kernel, d),
        grid=(b, n // tm),
        in_specs=[
            pl.BlockSpec((1, tm, d), lambda i, j: (i, j, 0)),
            pl.BlockSpec((1, 1, modv.shape[2]), mod_idx),
            pl.BlockSpec((1, d), const),
            pl.BlockSpec(w_router_t.shape, const),
        ],
        out_specs=[pl.BlockSpec((1, tm, d), lambda i, j: (i, j, 0)), pl.BlockSpec((1, N_EXPERTS, tm), lambda i, j: (i, 0, j))],
        out_shape=[jax.ShapeDtypeStruct((b, n, d), BF), jax.ShapeDtypeStruct((b, N_EXPERTS, n), F32)],
        compiler_params=_cparams(2),
    )(x, modv, g_ffn_l.reshape(1, d), w_router_t)


def _select_kernel(cap, aff_ref, sel_ref):
    a = aff_ref[0]
    e, n = a.shape
    bits = pltpu.bitcast(a, jnp.int32)
    capf = float(cap)

    def body(_, carry):
        lo, hi = carry
        mid = lo + ((hi - lo) >> 1)
        cnt = jnp.sum(jnp.where(bits >= mid, 1.0, 0.0), axis=1, keepdims=True)
        up = cnt >= capf
        return jnp.where(up, mid, lo), jnp.where(up, hi, mid)

    lo0 = jnp.zeros((e, 1), jnp.int32)
    hi0 = jnp.full((e, 1), 0x7F800000, jnp.int32)
    thr, _ = lax.fori_loop(0, 31, body, (lo0, hi0))
    gt = bits > thr
    eq = bits == thr
    need = capf - jnp.sum(jnp.where(gt, 1.0, 0.0), axis=1, keepdims=True)
    both = jnp.concatenate([jnp.where(gt, 1.0, 0.0), jnp.where(eq, 1.0, 0.0)], axis=0).astype(BF)
    ck = min(256, n)
    tri = jnp.where(lax.broadcasted_iota(jnp.int32, (ck, ck), 0) < lax.broadcasted_iota(jnp.int32, (ck, ck), 1), 1.0, 0.0).astype(BF)
    off = jnp.zeros((2 * e, 1), F32)
    pieces = []
    for k in range(n // ck):
        blk = both[:, k * ck:(k + 1) * ck]
        pieces.append(_dot(blk, tri) + off)
        off = off + jnp.sum(blk.astype(F32), axis=1, keepdims=True)
    cum = jnp.concatenate(pieces, axis=1) if len(pieces) > 1 else pieces[0]
    cum_gt, cum_eq = cum[:e], cum[e:]
    chosen = gt | (eq & (cum_eq < need))
    pos = cum_gt + jnp.minimum(cum_eq, need)
    sel_ref[0] = jnp.where(chosen, pos, -1.0).astype(jnp.int32)


def _select(aff, cap):
    b, e, n = aff.shape
    return pl.pallas_call(
        functools.partial(_select_kernel, cap),
        grid=(b,),
        in_specs=[pl.BlockSpec((1, e, n), lambda i: (i, 0, 0))],
        out_specs=pl.BlockSpec((1, e, n), lambda i: (i, 0, 0)),
        out_shape=jax.ShapeDtypeStruct((b, e, n), jnp.int32),
        compiler_params=_cparams(1),
    )(aff)


def _gather_kernel(cap, sel_ref, aff_ref, h_ref, xs_ref, gate_ref):
    h = h_ref[0]
    n = h.shape[0]
    e_tot = sel_ref.shape[1]

    def one(e_idx):
        sel_e = sel_ref[0, pl.ds(e_idx, 1), :]
        aff_e = aff_ref[0, pl.ds(e_idx, 1), :]
        hit = sel_e == lax.broadcasted_iota(jnp.int32, (cap, n), 0)
        g = jnp.sum(jnp.where(hit, aff_e, 0.0), axis=1, keepdims=True)
        return jnp.where(hit, 1.0, 0.0).astype(BF), jnp.broadcast_to(g, (cap, HEAD_W))

    if cap >= 128:
        def body(e_idx, carry):
            p, g = one(e_idx)
            xs_ref[e_idx, 0] = _dot(p, h).astype(BF)
            gate_ref[e_idx, 0] = g
            return carry

        lax.fori_loop(0, e_tot, body, 0)
    else:
        ps, gs = zip(*[one(e_idx) for e_idx in range(e_tot)])
        xs = _dot(jnp.concatenate(ps, axis=0), h).astype(BF)
        for e_idx in range(e_tot):
            xs_ref[e_idx, 0] = xs[e_idx * cap:(e_idx + 1) * cap]
            gate_ref[e_idx, 0] = gs[e_idx]


def _gather(sel, aff, h2, cap):
    b, e, n = sel.shape
    d = h2.shape[2]
    return pl.pallas_call(
        functools.partial(_gather_kernel, cap),
        grid=(b,),
        in_specs=[
            pl.BlockSpec((1, e, n), lambda i: (i, 0, 0)),
            pl.BlockSpec((1, e, n), lambda i: (i, 0, 0)),
            pl.BlockSpec((1, n, d), lambda i: (i, 0, 0)),
        ],
        out_specs=[pl.BlockSpec((e, 1, cap, d), lambda i: (0, i, 0, 0)), pl.BlockSpec((e, 1, cap, HEAD_W), lambda i: (0, i, 0, 0))],
        out_shape=[jax.ShapeDtypeStruct((e, b, cap, d), BF), jax.ShapeDtypeStruct((e, b, cap, HEAD_W), F32)],
        compiler_params=_cparams(1),
    )(sel, aff, h2)


def _expert_kernel(xs_ref, gate_ref, wg_ref, wu_ref, wd_ref, ye_ref, wg_s, wu_s, wd_s):
    @pl.when(pl.program_id(1) == 0)
    def _():
        wg_s[...] = wg_ref[0, 0].astype(BF)
        wu_s[...] = wu_ref[0, 0].astype(BF)
        wd_s[...] = wd_ref[0, 0].astype(BF)

    _, bt, cap, d = xs_ref.shape
    x = xs_ref[0].reshape(bt * cap, d)
    a = _dot(x, wg_s[...])
    u = _dot(x, wu_s[...])
    mid = (a * (1.0 / (1.0 + jnp.exp(-a))) * u).astype(BF)
    y = _dot(mid, wd_s[...]) * gate_ref[0].reshape(bt * cap, HEAD_W)[:, 0:1]
    ye_ref[0] = y.astype(BF).reshape(bt, cap, d)


def _experts(l, xs, gate, w_e_gate, w_e_up, w_e_down):
    e, b, cap, d = xs.shape
    ff = w_e_gate.shape[3]
    bt = max(1, min(b, 512 // cap))
    w_idx = lambda i, j: (l, i, 0, 0)
    return pl.pallas_call(
        _expert_kernel,
        grid=(e, b // bt),
        in_specs=[
            pl.BlockSpec((1, bt, cap, d), lambda i, j: (i, j, 0, 0)),
            pl.BlockSpec((1, bt, cap, HEAD_W), lambda i, j: (i, j, 0, 0)),
            pl.BlockSpec((1, 1, d, ff), w_idx),
            pl.BlockSpec((1, 1, d, ff), w_idx),
            pl.BlockSpec((1, 1, ff, d), w_idx),
        ],
        out_specs=pl.BlockSpec((1, bt, cap, d), lambda i, j: (i, j, 0, 0)),
        out_shape=jax.ShapeDtypeStruct((e, b, cap, d), BF),
        scratch_shapes=[pltpu.VMEM((d, ff), BF), pltpu.VMEM((d, ff), BF), pltpu.VMEM((ff, d), BF)],
        compiler_params=_cparams(2),
    )(xs, gate, w_e_gate, w_e_up, w_e_down)


def _scatter_kernel(cap, d, final, *refs):
    if final:
        x_ref, mod_ref, selt_ref, ye_ref, gf_ref, o_ref, pt_ref = refs
    else:
        x_ref, mod_ref, selt_ref, ye_ref, o_ref, pt_ref = refs
    e_tot = ye_ref.shape[0]
    tn = x_ref.shape[1]
    selt = selt_ref[0]
    if cap % 128 == 0:
        lane = lax.broadcasted_iota(jnp.int32, (tn, cap), 1)
        for e_idx in range(e_tot):
            pt_ref[:, e_idx * cap:(e_idx + 1) * cap] = jnp.where(selt[:, e_idx:e_idx + 1] == lane, 1.0, 0.0).astype(BF)
    else:
        lane = lax.broadcasted_iota(jnp.int32, (tn, e_tot * cap), 1)
        acc = jnp.zeros((tn, e_tot * cap), F32)
        for e_idx in range(e_tot):
            s = selt[:, e_idx:e_idx + 1]
            acc = acc + jnp.where((s >= 0) & (s + e_idx * cap == lane), 1.0, 0.0)
        pt_ref[...] = acc.astype(BF)
    y = _dot(pt_ref[...], ye_ref[...].reshape(e_tot * cap, d))
    gt = mod_ref[0, :, 5 * d:6 * d]
    out = x_ref[0] + gt * y
    if final:
        out = _rms(out, gf_ref[...])
    o_ref[0] = out


def _scatter(x, modv, sel_t, ye, cap, g_final):
    b, n, d = x.shape
    e = ye.shape[0]
    tn = min(512, n)
    bm = modv.shape[0]
    mod_idx = (lambda i, j: (i, 0, 0)) if bm > 1 else (lambda i, j: (0, 0, 0))
    final = g_final is not None
    in_specs = [
        pl.BlockSpec((1, tn, d), lambda i, j: (i, j, 0)),
        pl.BlockSpec((1, 1, modv.shape[2]), mod_idx),
        pl.BlockSpec((1, tn, e), lambda i, j: (i, j, 0)),
        pl.BlockSpec((e, 1, cap, d), lambda i, j: (0, i, 0, 0)),
    ]
    args = [x, modv, sel_t, ye]
    if final:
        in_specs.append(pl.BlockSpec((1, d), lambda i, j: (0, 0)))
        args.append(g_final.reshape(1, d))
    return pl.pallas_call(
        functools.partial(_scatter_kernel, cap, d, final),
        grid=(b, n // tn),
        in_specs=in_specs,
        out_specs=pl.BlockSpec((1, tn, d), lambda i, j: (i, j, 0)),
        out_shape=jax.ShapeDtypeStruct((b, n, d), F32),
        scratch_shapes=[pltpu.VMEM((tn, e * cap), BF)],
        compiler_params=_cparams(2),
    )(*args)


def _dft_tables(n):
    j = jnp.arange(n, dtype=jnp.int32)
    ang = ((j[:, None] * j[None, :]) % n).astype(F32) * (2.0 * math.pi / n)
    return jnp.cos(ang).astype(BF), jnp.sin(ang).astype(BF)


def _block_diag(blocks):
    g, r, c = blocks.shape
    out = jnp.zeros((g * r, g * c), blocks.dtype)
    for i in range(g):
        out = out.at[i * r:(i + 1) * r, i * c:(i + 1) * c].set(blocks[i])
    return out


def _rope_tables(n):
    rows = n // GRID_W
    row = jnp.repeat(jnp.arange(rows, dtype=F32), GRID_W)
    col = jnp.tile(jnp.arange(GRID_W, dtype=F32), rows)

    def ang(dim):
        nf = dim // 4
        inv = ROPE_BASE ** (-jnp.arange(nf, dtype=F32) / nf)
        return jnp.concatenate([row[:, None] * inv, col[:, None] * inv], axis=-1)

    a = ang(DIFF_DH)
    cos_d = jnp.tile(jnp.cos(a), (1, 4))
    sin_d = jnp.tile(jnp.concatenate([-jnp.sin(a), jnp.sin(a)], axis=1), (1, 2))
    a = ang(MLA_ROPE)
    ones = jnp.ones((n, MLA_NOPE), F32)
    pad = HEAD_W - MLA_NOPE - MLA_ROPE
    cos_m = jnp.concatenate([ones, jnp.cos(a), jnp.cos(a), jnp.ones((n, pad), F32)], axis=1)
    sin_m = jnp.concatenate([0 * ones, -jnp.sin(a), jnp.sin(a), jnp.zeros((n, pad), F32)], axis=1)
    return cos_d, sin_d, cos_m, sin_m


def _pad_heads(w, heads, lo, hi):
    k = w.shape[0]
    w3 = w.reshape(k, heads, -1)[:, :, lo:hi]
    return jnp.pad(w3, ((0, 0), (0, 0), (0, HEAD_W - (hi - lo)))).reshape(k, heads * HEAD_W)


def kernel(x_prompt, x_sample, cache_diff_k, cache_diff_v, cache_mla_ckv, cache_mla_kpe, c, c_ctx, w_ada, b_ada, g_attn, g_ffn, w_in, w_four, diff_lambda, g_subln, g_mla_q, w_mla_uq, g_mla_kv, w_mla_ukv, w_out, w_router, w_e_gate, w_e_up, w_e_down, g_final):
    depth, d = g_attn.shape
    b_ctx, n_ctx, _ = x_prompt.shape
    b_lat, n_lat, _ = x_sample.shape
    p_len = cache_diff_k.shape[2]

    r = b_lat + 1
    r_pad = -(-r // 8) * 8
    cond = jnp.concatenate([c, c_ctx[None, :], jnp.zeros((r_pad - r, d), F32)], axis=0)
    mod = _modulation(cond, w_ada, b_ada)

    n_main = w_in.shape[2] - MLA_ROPE
    w_in_p = jnp.concatenate(
        [w_in[:, :, :n_main], jnp.zeros((depth, d, MLA_NOPE), F32), w_in[:, :, n_main:],
         jnp.zeros((depth, d, HEAD_W - MLA_NOPE - MLA_ROPE), F32)], axis=2).astype(BF)
    w_uq_p = jnp.stack([_pad_heads(w_mla_uq[l], MLA_HEADS, 0, MLA_NOPE + MLA_ROPE) for l in range(depth)]).astype(BF)
    w_k_p = jnp.stack([_pad_heads(w_mla_ukv[l], MLA_HEADS, 0, MLA_NOPE) for l in range(depth)]).astype(BF)
    w_v_p = jnp.stack([_pad_heads(w_mla_ukv[l], MLA_HEADS, MLA_NOPE, MLA_NOPE + MLA_V) for l in range(depth)]).astype(BF)
    wo_mla = jnp.pad(w_out[:, D_FOURIER + D_DIFF:].reshape(depth, MLA_HEADS, MLA_V, d), ((0, 0), (0, 0), (0, HEAD_W - MLA_V), (0, 0)))
    w_out_p = jnp.concatenate([w_out[:, :D_FOURIER + D_DIFF], wo_mla.reshape(depth, D_MLA_P, d)], axis=1).astype(BF)
    w_router_t = jnp.swapaxes(w_router, 1, 2).astype(BF)
    place = jnp.zeros((MLA_ROPE, HEAD_W), F32).at[jnp.arange(MLA_ROPE), MLA_NOPE + jnp.arange(MLA_ROPE)].set(1.0)
    place = jnp.tile(place, (1, MLA_HEADS)).astype(BF)

    jc = jnp.arange(F_CH, dtype=jnp.int32)
    ang_c = ((jc[:, None] * jc[None, :]) % F_CH).astype(F32) * (2.0 * math.pi / F_CH)
    cc_bd = _block_diag(jnp.broadcast_to(jnp.cos(ang_c), (F_GROUPS, F_CH, F_CH))).astype(BF)
    sc_bd = _block_diag(jnp.broadcast_to(jnp.sin(ang_c), (F_GROUPS, F_CH, F_CH))).astype(BF)
    dft = {n: _dft_tables(n) for n in {n_ctx, n_lat}}
    ropes = _rope_tables(n_lat)

    cdk = cache_diff_k.reshape(b_lat, depth, p_len, D_DIFF)
    cdv = cache_diff_v.reshape(b_lat, depth, p_len, D_DIFF)

    def run_path(latent, x):
        b, n, _ = x.shape
        cap = max(1, EC_CAPACITY_FACTOR * n // N_EXPERTS)
        new = []
        for l in range(depth):
            modv = (mod[l, :b_lat] if latent else mod[l, b_lat:b_lat + 1]).reshape(-1, 1, N_MOD * d)
            outs = _pre(latent, x, modv, g_attn[l], w_in_p[l], g_mla_q[l], w_uq_p[l], g_mla_kv[l], w_k_p[l], w_v_p[l], ropes)
            zf, qd, kd, vd, qm, km, vm = outs[:7]
            cache = None
            if latent:
                cache = _cache_prep(l, cdk, cdv, cache_mla_ckv, cache_mla_kpe, w_k_p[l], w_v_p[l], place)
            else:
                new.append(outs[7:])
            cn, sn = dft[n]
            yfour = _fourier(zf, cn, sn, cc_bd, sc_bd, _block_diag(w_four[l]).astype(BF))
            x = _attention(latent, l, x, modv, yfour, qd, kd, vd, qm, km, vm, cache, diff_lambda[l], g_subln[l], w_out_p[l])
            h2, aff = _router(x, modv, g_ffn[l], w_router_t[l])
            sel = _select(aff, cap)
            xs, gate = _gather(sel, aff, h2, cap)
            ye = _experts(l, xs, gate, w_e_gate, w_e_up, w_e_down)
            x = _scatter(x, modv, jnp.swapaxes(sel, 1, 2), ye, cap, g_final if l == depth - 1 else None)
        return x, new

    y_prompt, new = run_path(False, x_prompt)
    new_diff_k = jnp.stack([t[0] for t in new], axis=1).reshape(b_ctx, depth, n_ctx, DIFF_HEADS, 2 * DIFF_DH)
    new_diff_v = jnp.stack([t[1] for t in new], axis=1).reshape(b_ctx, depth, n_ctx, DIFF_HEADS, 2 * DIFF_DH)
    new_mla_ckv = jnp.stack([t[2] for t in new], axis=1)
    new_mla_kpe = jnp.stack([t[3] for t in new], axis=1)
    y_sample, _ = run_path(True, x_sample)
    return (y_prompt, y_sample, new_diff_k, new_diff_v, new_mla_ckv, new_mla_kpe)
```

```python
import functools
import math

import jax
import jax.numpy as jnp
import numpy as np
from jax import lax
from jax.experimental import pallas as pl
from jax.experimental.pallas import tpu as pltpu

BF = jnp.bfloat16
F32 = jnp.float32

GRID_W = 64
ROPE_BASE = 10000.0
NORM_EPS = 1e-6
F_GROUPS, F_CH = 4, 64
D_FOURIER = F_GROUPS * F_CH
DIFF_HEADS, DIFF_DH = 4, 64
D_DIFF = DIFF_HEADS * 2 * DIFF_DH
MLA_HEADS, MLA_Q_RANK, MLA_KV_RANK = 4, 256, 128
MLA_NOPE, MLA_ROPE, MLA_V = 64, 32, 64
HEAD_W = 128
D_MLA_P = MLA_HEADS * HEAD_W
D_MLA = MLA_HEADS * MLA_V
D_MIX = D_FOURIER + D_DIFF + D_MLA
KEY_CHUNK = 256
SCORE_BUFS = 2
BF_ROWS = 16
N_EXPERTS = 16
EC_CAPACITY_FACTOR = 2
N_MOD = 6
IN_COLS_P = D_FOURIER + 3 * D_DIFF + MLA_Q_RANK + MLA_KV_RANK + HEAD_W
LOG2E = 1.4426950408889634
VMEM_LIMIT = 56 * 1024 * 1024


def _cparams(n_axes, vmem=VMEM_LIMIT):
    return pltpu.CompilerParams(dimension_semantics=("arbitrary",) * n_axes, vmem_limit_bytes=vmem)


def _dot(a, b):
    return jnp.dot(a, b, preferred_element_type=F32)


def _dot_nt(a, b):
    return lax.dot_general(a, b, (((1,), (1,)), ((), ())), preferred_element_type=F32)


def _rms(x, g):
    return x * lax.rsqrt(jnp.mean(x * x, axis=-1, keepdims=True) + NORM_EPS) * g


def _rope(z, cos, sin_signed, half, group, lo):
    w = z.shape[1]
    reps = w // cos.shape[1]
    cos_w = jnp.concatenate([cos] * reps, axis=1) if reps > 1 else cos
    sin_w = jnp.concatenate([sin_signed] * reps, axis=1) if reps > 1 else sin_signed
    from_right = pltpu.roll(z, w - half, 1)
    from_left = pltpu.roll(z, half, 1)
    lane = lax.broadcasted_iota(jnp.int32, z.shape, 1) % group
    first = (lane >= lo) & (lane < lo + half)
    partner = jnp.where(first, from_right, from_left)
    return z * cos_w + partner * sin_w


def _mod_kernel(c_ref, w_ref, b_ref, o_ref):
    c = c_ref[...]
    a = (c * (1.0 / (1.0 + jnp.exp(-c)))).astype(BF)
    o_ref[0] = _dot(a, w_ref[0].astype(BF)) + b_ref[0]


def _modulation(cond, w_ada, b_ada):
    depth, d, n6 = w_ada.shape
    r = cond.shape[0]
    tn = 1536
    return pl.pallas_call(
        _mod_kernel,
        name="modulation",
        grid=(depth, n6 // tn),
        in_specs=[
            pl.BlockSpec((r, d), lambda l, j: (0, 0)),
            pl.BlockSpec((1, d, tn), lambda l, j: (l, 0, j)),
            pl.BlockSpec((1, 1, tn), lambda l, j: (l, 0, j)),
        ],
        out_specs=pl.BlockSpec((1, r, tn), lambda l, j: (l, 0, j)),
        out_shape=jax.ShapeDtypeStruct((depth, r, n6), F32),
        compiler_params=_cparams(2),
    )(cond, w_ada, b_ada.reshape(depth, 1, n6))


def _pre_kernel(latent, d, *refs):
    if latent:
        (x_ref, mod_ref, g_ref, win_ref, gq_ref, wuq_ref, gkv_ref, wk_ref, wv_ref,
         cd_ref, sd_ref, cm_ref, sm_ref,
         zf_ref, qd_ref, kd_ref, vd_ref, qm_ref, km_ref, vm_ref) = refs
    else:
        (x_ref, mod_ref, g_ref, win_ref, gq_ref, wuq_ref, gkv_ref, wk_ref, wv_ref,
         zf_ref, qd_ref, kd_ref, vd_ref, qm_ref, km_ref, vm_ref,
         k32_ref, v32_ref, ckv32_ref, kpe32_ref) = refs
    x = x_ref[0]
    sh = mod_ref[0, :, 0:d]
    sc = mod_ref[0, :, d:2 * d]
    h = _rms(x, g_ref[...]) * (1.0 + sc) + sh
    z = _dot(h.astype(BF), win_ref[...])
    o = 0
    zf = z[:, o:o + D_FOURIER]; o += D_FOURIER
    zq = z[:, o:o + D_DIFF]; o += D_DIFF
    zk = z[:, o:o + D_DIFF]; o += D_DIFF
    zv = z[:, o:o + D_DIFF]; o += D_DIFF
    zcq = z[:, o:o + MLA_Q_RANK]; o += MLA_Q_RANK
    zckv = z[:, o:o + MLA_KV_RANK]; o += MLA_KV_RANK
    kpe = z[:, o:o + HEAD_W]

    zf_ref[0] = zf.astype(BF)
    vd_ref[0] = zv.T.astype(BF)
    cq = _rms(zcq, gq_ref[...])
    qm = _dot(cq.astype(BF), wuq_ref[...])
    ckv = _rms(zckv, gkv_ref[...])
    ckv_b = ckv.astype(BF)
    k_nope = _dot(ckv_b, wk_ref[...])
    vm_ref[0] = _dot(ckv_b, wv_ref[...]).T.astype(BF)
    if latent:
        cd, sd, cm, sm = cd_ref[...], sd_ref[...], cm_ref[...], sm_ref[...]
        zq_r = _rope(zq, cd, sd, DIFF_DH // 2, DIFF_DH, 0)
        zk_r = _rope(zk, cd, sd, DIFF_DH // 2, DIFF_DH, 0)
        qm = _rope(qm, cm, sm, MLA_ROPE // 2, HEAD_W, MLA_NOPE)
        kpe_r = _rope(kpe, cm, sm, MLA_ROPE // 2, HEAD_W, MLA_NOPE)
    else:
        zq_r, zk_r, kpe_r = zq, zk, kpe
        k32_ref[0] = zk
        v32_ref[0] = zv
        ckv32_ref[0] = ckv
        kpe32_ref[0] = kpe[:, MLA_NOPE:MLA_NOPE + MLA_ROPE]
    qd_ref[0] = (zq_r * (DIFF_DH ** -0.5)).astype(BF)
    kd_ref[0] = zk_r.astype(BF)
    qm_ref[0] = qm.astype(BF)
    km_ref[0] = (k_nope + jnp.concatenate([kpe_r] * MLA_HEADS, axis=1)).astype(BF)


def _pre(latent, x, modv, g_attn, w_in_p, g_q, w_uq_p, g_kv, w_k_p, w_v_p, ropes):
    b, n, d = x.shape
    tm = min(256, n)
    bm = modv.shape[0]
    mod_idx = (lambda i, j: (i, 0, 0)) if bm > 1 else (lambda i, j: (0, 0, 0))
    const = lambda i, j: (0, 0)
    tok = lambda i, j: (i, j, 0)
    in_specs = [
        pl.BlockSpec((1, tm, d), tok),
        pl.BlockSpec((1, 1, modv.shape[2]), mod_idx),
        pl.BlockSpec((1, d), const),
        pl.BlockSpec(w_in_p.shape, const),
        pl.BlockSpec((1, MLA_Q_RANK), const),
        pl.BlockSpec(w_uq_p.shape, const),
        pl.BlockSpec((1, MLA_KV_RANK), const),
        pl.BlockSpec(w_k_p.shape, const),
        pl.BlockSpec(w_v_p.shape, const),
    ]
    args = [x, modv, g_attn.reshape(1, d), w_in_p, g_q.reshape(1, -1), w_uq_p, g_kv.reshape(1, -1), w_k_p, w_v_p]
    if latent:
        in_specs += [pl.BlockSpec((tm, HEAD_W), lambda i, j: (j, 0))] * 4
        args += list(ropes)
    feat = lambda i, j: (i, 0, j)
    out_shape, out_specs = [], []
    for w, token_major in ((D_FOURIER, True), (D_DIFF, True), (D_DIFF, True), (D_DIFF, False),
                           (D_MLA_P, True), (D_MLA_P, True), (D_MLA, False)):
        out_shape.append(jax.ShapeDtypeStruct((b, n, w) if token_major else (b, w, n), BF))
        out_specs.append(pl.BlockSpec((1, tm, w), tok) if token_major else pl.BlockSpec((1, w, tm), feat))
    if not latent:
        for w in (D_DIFF, D_DIFF, MLA_KV_RANK, MLA_ROPE):
            out_shape.append(jax.ShapeDtypeStruct((b, n, w), F32))
            out_specs.append(pl.BlockSpec((1, tm, w), tok))
    return pl.pallas_call(
        functools.partial(_pre_kernel, latent, d),
        name="pre_lat" if latent else "pre_ctx",
        grid=(b, n // tm),
        in_specs=in_specs,
        out_specs=out_specs,
        out_shape=out_shape,
        compiler_params=_cparams(2),
    )(*args)


def _cache_kernel(dk_ref, dv_ref, ckv_ref, kpe_ref, wk_ref, wv_ref, place_ref, kd_ref, vd_ref, km_ref, vm_ref):
    kd_ref[0] = dk_ref[0, 0].astype(BF)
    vd_ref[0] = dv_ref[0, 0].T.astype(BF)
    ckv_b = ckv_ref[0, 0].astype(BF)
    kpe_w = _dot(kpe_ref[0, 0].astype(BF), place_ref[...])
    km_ref[0] = (_dot(ckv_b, wk_ref[...]) + kpe_w).astype(BF)
    vm_ref[0] = _dot(ckv_b, wv_ref[...]).T.astype(BF)


def _cache_prep(l, cache_dk, cache_dv, cache_ckv, cache_kpe, w_k_p, w_v_p, place):
    b, _, p, _ = cache_dk.shape
    at_l = lambda i: (i, l, 0, 0)
    const = lambda i: (0, 0)
    shapes = [(p, D_DIFF), (D_DIFF, p), (p, D_MLA_P), (D_MLA, p)]
    return pl.pallas_call(
        _cache_kernel,
        name="cache_prep",
        grid=(b,),
        in_specs=[
            pl.BlockSpec((1, 1, p, D_DIFF), at_l),
            pl.BlockSpec((1, 1, p, D_DIFF), at_l),
            pl.BlockSpec((1, 1, p, MLA_KV_RANK), at_l),
            pl.BlockSpec((1, 1, p, MLA_ROPE), at_l),
            pl.BlockSpec(w_k_p.shape, const),
            pl.BlockSpec(w_v_p.shape, const),
            pl.BlockSpec(place.shape, const),
        ],
        out_specs=[pl.BlockSpec((1,) + s, lambda i: (i, 0, 0)) for s in shapes],
        out_shape=[jax.ShapeDtypeStruct((b,) + s, BF) for s in shapes],
        compiler_params=_cparams(1),
    )(cache_dk, cache_dv, cache_ckv, cache_kpe, w_k_p, w_v_p, place)


def _fourier_kernel(scale, zf_ref, cn_ref, sn_ref, cc_ref, sc_ref, w_ref, o_ref, u_ref, v_ref):
    @pl.when(pl.program_id(1) == 0)
    def _():
        zf = zf_ref[0]
        u_ref[...] = _dot(zf, cc_ref[...]).astype(BF)
        v_ref[...] = _dot(zf, sc_ref[...]).astype(BF)

    f = (_dot(cn_ref[...], u_ref[...]) - _dot(sn_ref[...], v_ref[...])) * scale
    o_ref[0] = _dot(f.astype(BF), w_ref[...]).astype(BF)


def _fourier(zf, cn, sn, cc_bd, sc_bd, w_bd):
    b, n, w = zf.shape
    tm = min(512, n)
    const = lambda i, j: (0, 0)
    return pl.pallas_call(
        functools.partial(_fourier_kernel, 1.0 / math.sqrt(n * F_CH)),
        name="fourier_%d" % n,
        grid=(b, n // tm),
        in_specs=[
            pl.BlockSpec((1, n, w), lambda i, j: (i, 0, 0)),
            pl.BlockSpec((tm, n), lambda i, j: (j, 0)),
            pl.BlockSpec((tm, n), lambda i, j: (j, 0)),
            pl.BlockSpec((w, w), const),
            pl.BlockSpec((w, w), const),
            pl.BlockSpec((w, w), const),
        ],
        out_specs=pl.BlockSpec((1, tm, w), lambda i, j: (i, j, 0)),
        out_shape=jax.ShapeDtypeStruct((b, n, w), BF),
        scratch_shapes=[pltpu.VMEM((n, w), BF), pltpu.VMEM((n, w), BF)],
        compiler_params=_cparams(2),
    )(zf, cn, sn, cc_bd, sc_bd, w_bd)


def _scores(q, k_parts, s_ref):
    off, m, bounds = 0, None, []
    for k in k_parts:
        s = _dot_nt(k(), q)
        n_i = s.shape[0]
        s_ref[off:off + n_i, :] = s
        mi = jnp.max(s, axis=0, keepdims=True)
        m = mi if m is None else jnp.maximum(m, mi)
        bounds.append((off, n_i))
        off += n_i
    return m, bounds


def _probs(m, scale, total, s_ref, p_ref):
    c = scale * LOG2E
    for r in range(0, total, KEY_CHUNK):
        rows = min(KEY_CHUNK, total - r)
        p_ref[r:r + rows, :] = jnp.exp2((s_ref[r:r + rows, :] - m) * c).astype(BF)


def _weighted(vt_parts, bounds, p_ref):
    o = None
    for vt, (r, n_i) in zip(vt_parts, bounds):
        v = vt()
        ones = jnp.where(lax.broadcasted_iota(jnp.int32, (BF_ROWS, n_i), 0) == 0, 1.0, 0.0).astype(BF)
        oi = _dot(jnp.concatenate([v, ones], axis=0), p_ref[r:r + n_i, :])
        o = oi if o is None else o + oi
    dv = o.shape[0] - BF_ROWS
    return o[:dv] * (1.0 / o[dv:dv + 1])


def _attn_kernel(latent, lam_init, d, *refs):
    if latent:
        (x_ref, mod_ref, yf_ref, qd_ref, kd_ref, vd_ref, kdc_ref, vdc_ref, qm_ref, km_ref, vm_ref, kmc_ref, vmc_ref,
         lam_ref, gs_ref, wo_ref, o_ref, mix_ref, s_ref, p_ref) = refs
    else:
        (x_ref, mod_ref, yf_ref, qd_ref, kd_ref, vd_ref, qm_ref, km_ref, vm_ref,
         lam_ref, gs_ref, wo_ref, o_ref, mix_ref, s_ref, p_ref) = refs
        kdc_ref = vdc_ref = kmc_ref = vmc_ref = None
    lf = lam_ref[...]
    lam = (jnp.exp(jnp.sum(lf[0:1] * lf[1:2], axis=-1, keepdims=True))
           - jnp.exp(jnp.sum(lf[2:3] * lf[3:4], axis=-1, keepdims=True)) + lam_init)
    mix_ref[:, 0:D_FOURIER] = yf_ref[0]
    gs = gs_ref[...]

    def keys(ref, cache_ref, sl):
        parts = [lambda: ref[0, :, sl]]
        if latent:
            parts.append(lambda: cache_ref[0, :, sl])
        return parts

    def values_t(ref, cache_ref, sl):
        parts = [lambda: ref[0, sl, :]]
        if latent:
            parts.append(lambda: cache_ref[0, sl, :])
        return parts

    def diff_query(sl, comp):
        def load():
            qh = qd_ref[0, :, sl]
            lane = lax.broadcasted_iota(jnp.int32, qh.shape, 1)
            keep = (lane >= DIFF_DH) if comp else (lane < DIFF_DH)
            return jnp.where(keep, qh, jnp.zeros_like(qh))
        return load

    items = []
    for hd in range(DIFF_HEADS):
        sl = slice(hd * HEAD_W, (hd + 1) * HEAD_W)
        for comp in range(2):
            items.append((diff_query(sl, comp), keys(kd_ref, kdc_ref, sl), values_t(vd_ref, vdc_ref, sl), 1.0))
    mla_scale = (MLA_NOPE + MLA_ROPE) ** -0.5
    for hd in range(MLA_HEADS):
        sl = slice(hd * HEAD_W, (hd + 1) * HEAD_W)
        vsl = slice(hd * MLA_V, (hd + 1) * MLA_V)
        items.append(((lambda sl=sl: qm_ref[0, :, sl]), keys(km_ref, kmc_ref, sl), values_t(vm_ref, vmc_ref, vsl), mla_scale))

    def start(i):
        return _scores(items[i][0](), items[i][1], s_ref.at[i % SCORE_BUFS])

    total = s_ref.shape[1]
    outs = []
    state = start(0)
    for i in range(len(items)):
        nxt = start(i + 1) if i + 1 < len(items) else None
        m, bounds = state
        buf = i % SCORE_BUFS
        _probs(m, items[i][3], total, s_ref.at[buf], p_ref.at[buf])
        outs.append(_weighted(items[i][2], bounds, p_ref.at[buf]))
        state = nxt

    for hd in range(DIFF_HEADS):
        o = outs[2 * hd] - lam * outs[2 * hd + 1]
        o = o * lax.rsqrt(jnp.mean(o * o, axis=0, keepdims=True) + NORM_EPS) * gs * (1.0 - lam_init)
        mix_ref[:, D_FOURIER + hd * HEAD_W:D_FOURIER + (hd + 1) * HEAD_W] = o.T.astype(BF)
    mix_ref[:, D_FOURIER + D_DIFF:D_MIX] = jnp.concatenate(outs[2 * DIFF_HEADS:], axis=0).T.astype(BF)
    y = _dot(mix_ref[...], wo_ref[...])
    gt = mod_ref[0, :, 2 * d:3 * d]
    o_ref[0] = x_ref[0] + gt * y


def _attention(latent, l, x, modv, yfour, qd, kd, vd, qm, km, vm, cache, diff_lambda_l, g_subln_l, w_out_b):
    b, n, d = x.shape
    tq = min(256, n)
    bm = modv.shape[0]
    mod_idx = (lambda i, j: (i, 0, 0)) if bm > 1 else (lambda i, j: (0, 0, 0))
    const = lambda i, j: (0, 0)
    tok = lambda i, j: (i, j, 0)
    full = lambda i, j: (i, 0, 0)
    p = cache[0].shape[1] if latent else 0
    in_specs = [
        pl.BlockSpec((1, tq, d), tok),
        pl.BlockSpec((1, 1, modv.shape[2]), mod_idx),
        pl.BlockSpec((1, tq, D_FOURIER), tok),
        pl.BlockSpec((1, tq, D_DIFF), tok),
        pl.BlockSpec((1, n, D_DIFF), full),
        pl.BlockSpec((1, D_DIFF, n), full),
    ]
    args = [x, modv, yfour, qd, kd, vd]
    if latent:
        kdc, vdc, kmc, vmc = cache
        in_specs += [pl.BlockSpec((1, p, D_DIFF), full), pl.BlockSpec((1, D_DIFF, p), full)]
        args += [kdc, vdc]
    in_specs += [pl.BlockSpec((1, tq, D_MLA_P), tok), pl.BlockSpec((1, n, D_MLA_P), full), pl.BlockSpec((1, D_MLA, n), full)]
    args += [qm, km, vm]
    if latent:
        in_specs += [pl.BlockSpec((1, p, D_MLA_P), full), pl.BlockSpec((1, D_MLA, p), full)]
        args += [kmc, vmc]
    in_specs += [pl.BlockSpec((4, DIFF_DH), const), pl.BlockSpec((HEAD_W, 1), const), pl.BlockSpec(w_out_b.shape, const)]
    args += [diff_lambda_l, g_subln_l.reshape(HEAD_W, 1), w_out_b]
    lam_init = 0.8 - 0.6 * math.exp(-0.3 * l)
    return pl.pallas_call(
        functools.partial(_attn_kernel, latent, lam_init, d),
        name="attn_lat" if latent else "attn_ctx",
        grid=(b, n // tq),
        in_specs=in_specs,
        out_specs=pl.BlockSpec((1, tq, d), tok),
        out_shape=jax.ShapeDtypeStruct((b, n, d), F32),
        scratch_shapes=[pltpu.VMEM((tq, D_MIX), BF), pltpu.VMEM((SCORE_BUFS, n + p, tq), F32),
                        pltpu.VMEM((SCORE_BUFS, n + p, tq), BF)],
        compiler_params=_cparams(2),
    )(*args)


def _router_kernel(d, x_ref, mod_ref, g_ref, wr_ref, h_ref, aff_ref):
    sh = mod_ref[0, :, 3 * d:4 * d]
    sc = mod_ref[0, :, 4 * d:5 * d]
    hb = (_rms(x_ref[0], g_ref[...]) * (1.0 + sc) + sh).astype(BF)
    h_ref[0] = hb
    logits = _dot_nt(wr_ref[...], hb)
    m = jnp.max(logits, axis=0, keepdims=True)
    e = jnp.exp(logits - m)
    aff_ref[0] = e / jnp.sum(e, axis=0, keepdims=True)


def _router(x, modv, g_ffn_l, w_router_t):
    b, n, d = x.shape
    tm = min(256, n)
    bm = modv.shape[0]
    mod_idx = (lambda i, j: (i, 0, 0)) if bm > 1 else (lambda i, j: (0, 0, 0))
    const = lambda i, j: (0, 0)
    return pl.pallas_call(
        functools.partial(_router_kernel, d),
        name="router_%d" % n,
        grid=(b, n // tm),
        in_specs=[
            pl.BlockSpec((1, tm, d), lambda i, j: (i, j, 0)),
            pl.BlockSpec((1, 1, modv.shape[2]), mod_idx),
            pl.BlockSpec((1, d), const),
            pl.BlockSpec(w_router_t.shape, const),
        ],
        out_specs=[pl.BlockSpec((1, tm, d), lambda i, j: (i, j, 0)), pl.BlockSpec((1, N_EXPERTS, tm), lambda i, j: (i, 0, j))],
        out_shape=[jax.ShapeDtypeStruct((b, n, d), BF), jax.ShapeDtypeStruct((b, N_EXPERTS, n), F32)],
        compiler_params=_cparams(2),
    )(x, modv, g_ffn_l.reshape(1, d), w_router_t)


def _select_kernel(cap, aff_ref, sel_ref):
    a = aff_ref[0]
    e, n = a.shape
    bits = pltpu.bitcast(a, jnp.int32)
    capf = float(cap)

    def body(_, carry):
        lo, hi = carry
        mid = lo + ((hi - lo) >> 1)
        cnt = jnp.sum(jnp.where(bits >= mid, 1.0, 0.0), axis=1, keepdims=True)
        up = cnt >= capf
        return jnp.where(up, mid, lo), jnp.where(up, hi, mid)

    lo0 = jnp.zeros((e, 1), jnp.int32)
    hi0 = jnp.full((e, 1), 0x7F800000, jnp.int32)
    thr, _ = lax.fori_loop(0, 31, body, (lo0, hi0))
    gt = bits > thr
    eq = bits == thr
    need = capf - jnp.sum(jnp.where(gt, 1.0, 0.0), axis=1, keepdims=True)
    both = jnp.concatenate([jnp.where(gt, 1.0, 0.0), jnp.where(eq, 1.0, 0.0)], axis=0).astype(BF)
    ck = min(256, n)
    tri = jnp.where(lax.broadcasted_iota(jnp.int32, (ck, ck), 0) < lax.broadcasted_iota(jnp.int32, (ck, ck), 1), 1.0, 0.0).astype(BF)
    off = jnp.zeros((2 * e, 1), F32)
    pieces = []
    for k in range(n // ck):
        blk = both[:, k * ck:(k + 1) * ck]
        pieces.append(_dot(blk, tri) + off)
        off = off + jnp.sum(blk.astype(F32), axis=1, keepdims=True)
    cum = jnp.concatenate(pieces, axis=1) if len(pieces) > 1 else pieces[0]
    cum_gt, cum_eq = cum[:e], cum[e:]
    chosen = gt | (eq & (cum_eq < need))
    pos = cum_gt + jnp.minimum(cum_eq, need)
    sel_ref[0] = jnp.where(chosen, pos, -1.0).astype(jnp.int32)


def _select(aff, cap):
    b, e, n = aff.shape
    return pl.pallas_call(
        functools.partial(_select_kernel, cap),
        name="select_%d" % n,
        grid=(b,),
        in_specs=[pl.BlockSpec((1, e, n), lambda i: (i, 0, 0))],
        out_specs=pl.BlockSpec((1, e, n), lambda i: (i, 0, 0)),
        out_shape=jax.ShapeDtypeStruct((b, e, n), jnp.int32),
        compiler_params=_cparams(1),
    )(aff)


def _gather_kernel(cap, sel_ref, aff_ref, h_ref, xs_ref, gate_ref):
    h = h_ref[0]
    n = h.shape[0]
    e_tot = sel_ref.shape[1]

    def one(e_idx):
        sel_e = sel_ref[0, pl.ds(e_idx, 1), :]
        aff_e = aff_ref[0, pl.ds(e_idx, 1), :]
        hit = sel_e == lax.broadcasted_iota(jnp.int32, (cap, n), 0)
        g = jnp.sum(jnp.where(hit, aff_e, 0.0), axis=1, keepdims=True)
        return jnp.where(hit, 1.0, 0.0).astype(BF), jnp.broadcast_to(g, (cap, HEAD_W))

    if cap >= 128:
        def body(e_idx, carry):
            p, g = one(e_idx)
            xs_ref[e_idx, 0] = _dot(p, h).astype(BF)
            gate_ref[e_idx, 0] = g
            return carry

        lax.fori_loop(0, e_tot, body, 0)
    else:
        ps, gs = zip(*[one(e_idx) for e_idx in range(e_tot)])
        xs = _dot(jnp.concatenate(ps, axis=0), h).astype(BF)
        for e_idx in range(e_tot):
            xs_ref[e_idx, 0] = xs[e_idx * cap:(e_idx + 1) * cap]
            gate_ref[e_idx, 0] = gs[e_idx]


def _gather(sel, aff, h2, cap):
    b, e, n = sel.shape
    d = h2.shape[2]
    return pl.pallas_call(
        functools.partial(_gather_kernel, cap),
        name="gather_%d" % n,
        grid=(b,),
        in_specs=[
            pl.BlockSpec((1, e, n), lambda i: (i, 0, 0)),
            pl.BlockSpec((1, e, n), lambda i: (i, 0, 0)),
            pl.BlockSpec((1, n, d), lambda i: (i, 0, 0)),
        ],
        out_specs=[pl.BlockSpec((e, 1, cap, d), lambda i: (0, i, 0, 0)), pl.BlockSpec((e, 1, cap, HEAD_W), lambda i: (0, i, 0, 0))],
        out_shape=[jax.ShapeDtypeStruct((e, b, cap, d), BF), jax.ShapeDtypeStruct((e, b, cap, HEAD_W), F32)],
        compiler_params=_cparams(1),
    )(sel, aff, h2)


def _expert_kernel(xs_ref, gate_ref, wg_ref, wu_ref, wd_ref, ye_ref, wg_s, wu_s, wd_s):
    @pl.when(pl.program_id(1) == 0)
    def _():
        wg_s[...] = wg_ref[0, 0].astype(BF)
        wu_s[...] = wu_ref[0, 0].astype(BF)
        wd_s[...] = wd_ref[0, 0].astype(BF)

    _, bt, cap, d = xs_ref.shape
    x = xs_ref[0].reshape(bt * cap, d)
    a = _dot(x, wg_s[...])
    u = _dot(x, wu_s[...])
    mid = (a * (1.0 / (1.0 + jnp.exp(-a))) * u).astype(BF)
    y = _dot(mid, wd_s[...]) * gate_ref[0].reshape(bt * cap, HEAD_W)[:, 0:1]
    ye_ref[0] = y.astype(BF).reshape(bt, cap, d)


def _experts(l, xs, gate, w_e_gate, w_e_up, w_e_down):
    e, b, cap, d = xs.shape
    ff = w_e_gate.shape[3]
    bt = max(1, min(b, 512 // cap))
    w_idx = lambda i, j: (l, i, 0, 0)
    return pl.pallas_call(
        _expert_kernel,
        name="experts_%d" % cap,
        grid=(e, b // bt),
        in_specs=[
            pl.BlockSpec((1, bt, cap, d), lambda i, j: (i, j, 0, 0)),
            pl.BlockSpec((1, bt, cap, HEAD_W), lambda i, j: (i, j, 0, 0)),
            pl.BlockSpec((1, 1, d, ff), w_idx),
            pl.BlockSpec((1, 1, d, ff), w_idx),
            pl.BlockSpec((1, 1, ff, d), w_idx),
        ],
        out_specs=pl.BlockSpec((1, bt, cap, d), lambda i, j: (i, j, 0, 0)),
        out_shape=jax.ShapeDtypeStruct((e, b, cap, d), BF),
        scratch_shapes=[pltpu.VMEM((d, ff), BF), pltpu.VMEM((d, ff), BF), pltpu.VMEM((ff, d), BF)],
        compiler_params=_cparams(2),
    )(xs, gate, w_e_gate, w_e_up, w_e_down)


def _scatter_kernel(cap, d, final, *refs):
    if final:
        x_ref, mod_ref, selt_ref, ye_ref, gf_ref, o_ref, pt_ref = refs
    else:
        x_ref, mod_ref, selt_ref, ye_ref, o_ref, pt_ref = refs
    e_tot = ye_ref.shape[0]
    tn = x_ref.shape[1]
    selt = selt_ref[0]
    if cap % 128 == 0:
        lane = lax.broadcasted_iota(jnp.int32, (tn, cap), 1)
        for e_idx in range(e_tot):
            pt_ref[:, e_idx * cap:(e_idx + 1) * cap] = jnp.where(selt[:, e_idx:e_idx + 1] == lane, 1.0, 0.0).astype(BF)
    else:
        lane = lax.broadcasted_iota(jnp.int32, (tn, e_tot * cap), 1)
        acc = jnp.zeros((tn, e_tot * cap), F32)
        for e_idx in range(e_tot):
            s = selt[:, e_idx:e_idx + 1]
            acc = acc + jnp.where((s >= 0) & (s + e_idx * cap == lane), 1.0, 0.0)
        pt_ref[...] = acc.astype(BF)
    y = _dot(pt_ref[...], ye_ref[...].reshape(e_tot * cap, d))
    gt = mod_ref[0, :, 5 * d:6 * d]
    out = x_ref[0] + gt * y
    if final:
        out = _rms(out, gf_ref[...])
    o_ref[0] = out


def _scatter(x, modv, sel_t, ye, cap, g_final):
    b, n, d = x.shape
    e = ye.shape[0]
    tn = min(512, n)
    bm = modv.shape[0]
    mod_idx = (lambda i, j: (i, 0, 0)) if bm > 1 else (lambda i, j: (0, 0, 0))
    final = g_final is not None
    in_specs = [
        pl.BlockSpec((1, tn, d), lambda i, j: (i, j, 0)),
        pl.BlockSpec((1, 1, modv.shape[2]), mod_idx),
        pl.BlockSpec((1, tn, e), lambda i, j: (i, j, 0)),
        pl.BlockSpec((e, 1, cap, d), lambda i, j: (0, i, 0, 0)),
    ]
    args = [x, modv, sel_t, ye]
    if final:
        in_specs.append(pl.BlockSpec((1, d), lambda i, j: (0, 0)))
        args.append(g_final.reshape(1, d))
    return pl.pallas_call(
        functools.partial(_scatter_kernel, cap, d, final),
        name="scatter_%d" % n,
        grid=(b, n // tn),
        in_specs=in_specs,
        out_specs=pl.BlockSpec((1, tn, d), lambda i, j: (i, j, 0)),
        out_shape=jax.ShapeDtypeStruct((b, n, d), F32),
        scratch_shapes=[pltpu.VMEM((tn, e * cap), BF)],
        compiler_params=_cparams(2),
    )(*args)


def _dft_tables(n):
    j = np.arange(n, dtype=np.int64)
    ang = ((j[:, None] * j[None, :]) % n).astype(np.float64) * (2.0 * math.pi / n)
    return jnp.asarray(np.cos(ang), dtype=BF), jnp.asarray(np.sin(ang), dtype=BF)


def _block_diag(blocks):
    g, r, c = blocks.shape
    out = jnp.zeros((g * r, g * c), blocks.dtype)
    for i in range(g):
        out = out.at[i * r:(i + 1) * r, i * c:(i + 1) * c].set(blocks[i])
    return out


def _rope_tables(n):
    rows = n // GRID_W
    row = jnp.repeat(jnp.arange(rows, dtype=F32), GRID_W)
    col = jnp.tile(jnp.arange(GRID_W, dtype=F32), rows)

    def ang(dim):
        nf = dim // 4
        inv = ROPE_BASE ** (-jnp.arange(nf, dtype=F32) / nf)
        return jnp.concatenate([row[:, None] * inv, col[:, None] * inv], axis=-1)

    a = ang(DIFF_DH)
    cos_d = jnp.tile(jnp.cos(a), (1, 4))
    sin_d = jnp.tile(jnp.concatenate([-jnp.sin(a), jnp.sin(a)], axis=1), (1, 2))
    a = ang(MLA_ROPE)
    ones = jnp.ones((n, MLA_NOPE), F32)
    pad = HEAD_W - MLA_NOPE - MLA_ROPE
    cos_m = jnp.concatenate([ones, jnp.cos(a), jnp.cos(a), jnp.ones((n, pad), F32)], axis=1)
    sin_m = jnp.concatenate([0 * ones, -jnp.sin(a), jnp.sin(a), jnp.zeros((n, pad), F32)], axis=1)
    return cos_d, sin_d, cos_m, sin_m


def _pad_heads(w, heads, lo, hi):
    k = w.shape[0]
    w3 = w.reshape(k, heads, -1)[:, :, lo:hi]
    return jnp.pad(w3, ((0, 0), (0, 0), (0, HEAD_W - (hi - lo)))).reshape(k, heads * HEAD_W)


def kernel(x_prompt, x_sample, cache_diff_k, cache_diff_v, cache_mla_ckv, cache_mla_kpe, c, c_ctx, w_ada, b_ada, g_attn, g_ffn, w_in, w_four, diff_lambda, g_subln, g_mla_q, w_mla_uq, g_mla_kv, w_mla_ukv, w_out, w_router, w_e_gate, w_e_up, w_e_down, g_final):
    depth, d = g_attn.shape
    b_ctx, n_ctx, _ = x_prompt.shape
    b_lat, n_lat, _ = x_sample.shape
    p_len = cache_diff_k.shape[2]

    r = b_lat + 1
    r_pad = -(-r // 8) * 8
    cond = jnp.concatenate([c, c_ctx[None, :], jnp.zeros((r_pad - r, d), F32)], axis=0)
    mod = _modulation(cond, w_ada, b_ada)

    n_main = w_in.shape[2] - MLA_ROPE
    w_in_p = jnp.concatenate(
        [w_in[:, :, :n_main], jnp.zeros((depth, d, MLA_NOPE), F32), w_in[:, :, n_main:],
         jnp.zeros((depth, d, HEAD_W - MLA_NOPE - MLA_ROPE), F32)], axis=2).astype(BF)
    w_uq_p = jnp.stack([_pad_heads(w_mla_uq[l], MLA_HEADS, 0, MLA_NOPE + MLA_ROPE) for l in range(depth)]).astype(BF)
    w_k_p = jnp.stack([_pad_heads(w_mla_ukv[l], MLA_HEADS, 0, MLA_NOPE) for l in range(depth)]).astype(BF)
    w_v_p = w_mla_ukv.reshape(depth, MLA_KV_RANK, MLA_HEADS, MLA_NOPE + MLA_V)[..., MLA_NOPE:].reshape(depth, MLA_KV_RANK, D_MLA).astype(BF)
    w_out_p = w_out.astype(BF)
    w_router_t = jnp.swapaxes(w_router, 1, 2).astype(BF)
    place = jnp.zeros((MLA_ROPE, HEAD_W), F32).at[jnp.arange(MLA_ROPE), MLA_NOPE + jnp.arange(MLA_ROPE)].set(1.0)
    place = jnp.tile(place, (1, MLA_HEADS)).astype(BF)

    jc = jnp.arange(F_CH, dtype=jnp.int32)
    ang_c = ((jc[:, None] * jc[None, :]) % F_CH).astype(F32) * (2.0 * math.pi / F_CH)
    cc_bd = _block_diag(jnp.broadcast_to(jnp.cos(ang_c), (F_GROUPS, F_CH, F_CH))).astype(BF)
    sc_bd = _block_diag(jnp.broadcast_to(jnp.sin(ang_c), (F_GROUPS, F_CH, F_CH))).astype(BF)
    dft = {n: _dft_tables(n) for n in {n_ctx, n_lat}}
    ropes = _rope_tables(n_lat)

    cdk = cache_diff_k.reshape(b_lat, depth, p_len, D_DIFF)
    cdv = cache_diff_v.reshape(b_lat, depth, p_len, D_DIFF)

    def run_path(latent, x):
        b, n, _ = x.shape
        cap = max(1, EC_CAPACITY_FACTOR * n // N_EXPERTS)
        new = []
        for l in range(depth):
            modv = (mod[l, :b_lat] if latent else mod[l, b_lat:b_lat + 1]).reshape(-1, 1, N_MOD * d)
            outs = _pre(latent, x, modv, g_attn[l], w_in_p[l], g_mla_q[l], w_uq_p[l], g_mla_kv[l], w_k_p[l], w_v_p[l], ropes)
            zf, qd, kd, vd, qm, km, vm = outs[:7]
            cache = None
            if latent:
                cache = _cache_prep(l, cdk, cdv, cache_mla_ckv, cache_mla_kpe, w_k_p[l], w_v_p[l], place)
            else:
                new.append(outs[7:])
            cn, sn = dft[n]
            yfour = _fourier(zf, cn, sn, cc_bd, sc_bd, _block_diag(w_four[l]).astype(BF))
            x = _attention(latent, l, x, modv, yfour, qd, kd, vd, qm, km, vm, cache, diff_lambda[l], g_subln[l], w_out_p[l])
            h2, aff = _router(x, modv, g_ffn[l], w_router_t[l])
            sel = _select(aff, cap)
            xs, gate = _gather(sel, aff, h2, cap)
            ye = _experts(l, xs, gate, w_e_gate, w_e_up, w_e_down)
            x = _scatter(x, modv, jnp.swapaxes(sel, 1, 2), ye, cap, g_final if l == depth - 1 else None)
        return x, new

    y_prompt, new = run_path(False, x_prompt)
    new_diff_k = jnp.stack([t[0] for t in new], axis=1).reshape(b_ctx, depth, n_ctx, DIFF_HEADS, 2 * DIFF_DH)
    new_diff_v = jnp.stack([t[1] for t in new], axis=1).reshape(b_ctx, depth, n_ctx, DIFF_HEADS, 2 * DIFF_DH)
    new_mla_ckv = jnp.stack([t[2] for t in new], axis=1)
    new_mla_kpe = jnp.stack([t[3] for t in new], axis=1)
    y_sample, _ = run_path(True, x_sample)
    return (y_prompt, y_sample, new_diff_k, new_diff_v, new_mla_ckv, new_mla_kpe)
```

```python
import functools
import math

import jax
import jax.numpy as jnp
import numpy as np
from jax import lax
from jax.experimental import pallas as pl
from jax.experimental.pallas import tpu as pltpu

BF = jnp.bfloat16
F32 = jnp.float32

GRID_W = 64
ROPE_BASE = 10000.0
NORM_EPS = 1e-6
F_GROUPS, F_CH = 4, 64
D_FOURIER = F_GROUPS * F_CH
DIFF_HEADS, DIFF_DH = 4, 64
D_DIFF = DIFF_HEADS * 2 * DIFF_DH
MLA_HEADS, MLA_Q_RANK, MLA_KV_RANK = 4, 256, 128
MLA_NOPE, MLA_ROPE, MLA_V = 64, 32, 64
HEAD_W = 128
D_MLA_P = MLA_HEADS * HEAD_W
D_MLA = MLA_HEADS * MLA_V
D_MIX = D_FOURIER + D_DIFF + D_MLA
KEY_CHUNK = 256
SCORE_BUFS = 2
BF_ROWS = 16
SELECT_ROWS = 128
EXPERT_ROWS = 512
N_EXPERTS = 16
EC_CAPACITY_FACTOR = 2
N_MOD = 6
IN_COLS_P = D_FOURIER + 3 * D_DIFF + MLA_Q_RANK + MLA_KV_RANK + HEAD_W
LOG2E = 1.4426950408889634
VMEM_LIMIT = 56 * 1024 * 1024


def _cparams(n_axes, vmem=VMEM_LIMIT):
    return pltpu.CompilerParams(dimension_semantics=("arbitrary",) * n_axes, vmem_limit_bytes=vmem)


def _dot(a, b):
    return jnp.dot(a, b, preferred_element_type=F32)


def _dot_nt(a, b):
    return lax.dot_general(a, b, (((1,), (1,)), ((), ())), preferred_element_type=F32)


def _rms(x, g):
    return x * lax.rsqrt(jnp.mean(x * x, axis=-1, keepdims=True) + NORM_EPS) * g


def _rope(z, cos, sin_signed, half, group, lo):
    w = z.shape[1]
    reps = w // cos.shape[1]
    cos_w = jnp.concatenate([cos] * reps, axis=1) if reps > 1 else cos
    sin_w = jnp.concatenate([sin_signed] * reps, axis=1) if reps > 1 else sin_signed
    from_right = pltpu.roll(z, w - half, 1)
    from_left = pltpu.roll(z, half, 1)
    lane = lax.broadcasted_iota(jnp.int32, z.shape, 1) % group
    first = (lane >= lo) & (lane < lo + half)
    partner = jnp.where(first, from_right, from_left)
    return z * cos_w + partner * sin_w


def _mod_kernel(c_ref, w_ref, b_ref, o_ref):
    c = c_ref[...]
    a = (c * (1.0 / (1.0 + jnp.exp(-c)))).astype(BF)
    o_ref[0] = _dot(a, w_ref[0].astype(BF)) + b_ref[0]


def _modulation(cond, w_ada, b_ada):
    depth, d, n6 = w_ada.shape
    r = cond.shape[0]
    tn = 1536
    return pl.pallas_call(
        _mod_kernel,
        name="modulation",
        grid=(depth, n6 // tn),
        in_specs=[
            pl.BlockSpec((r, d), lambda l, j: (0, 0)),
            pl.BlockSpec((1, d, tn), lambda l, j: (l, 0, j)),
            pl.BlockSpec((1, 1, tn), lambda l, j: (l, 0, j)),
        ],
        out_specs=pl.BlockSpec((1, r, tn), lambda l, j: (l, 0, j)),
        out_shape=jax.ShapeDtypeStruct((depth, r, n6), F32),
        compiler_params=_cparams(2),
    )(cond, w_ada, b_ada.reshape(depth, 1, n6))


def _pre_kernel(latent, d, *refs):
    if latent:
        (x_ref, mod_ref, g_ref, win_ref, gq_ref, wuq_ref, gkv_ref, wk_ref, wv_ref, cc_ref, sc_ref,
         cd_ref, sd_ref, cm_ref, sm_ref,
         u_ref, v_ref, qd_ref, kd_ref, vd_ref, qm_ref, km_ref, vm_ref) = refs
    else:
        (x_ref, mod_ref, g_ref, win_ref, gq_ref, wuq_ref, gkv_ref, wk_ref, wv_ref, cc_ref, sc_ref,
         u_ref, v_ref, qd_ref, kd_ref, vd_ref, qm_ref, km_ref, vm_ref,
         k32_ref, v32_ref, ckv32_ref, kpe32_ref) = refs[:11] + refs[-12:]
    x = x_ref[0]
    sh = mod_ref[0, :, 0:d]
    sc = mod_ref[0, :, d:2 * d]
    h = _rms(x, g_ref[...]) * (1.0 + sc) + sh
    z = _dot(h.astype(BF), win_ref[...])
    o = 0
    zf = z[:, o:o + D_FOURIER]; o += D_FOURIER
    zq = z[:, o:o + D_DIFF]; o += D_DIFF
    zk = z[:, o:o + D_DIFF]; o += D_DIFF
    zv = z[:, o:o + D_DIFF]; o += D_DIFF
    zcq = z[:, o:o + MLA_Q_RANK]; o += MLA_Q_RANK
    zckv = z[:, o:o + MLA_KV_RANK]; o += MLA_KV_RANK
    kpe = z[:, o:o + HEAD_W]

    zf_b = zf.astype(BF)
    u_ref[0] = _dot(zf_b, cc_ref[...]).astype(BF)
    v_ref[0] = _dot(zf_b, sc_ref[...]).astype(BF)
    vd_ref[0] = zv.T.astype(BF)
    cq = _rms(zcq, gq_ref[...])
    qm = _dot(cq.astype(BF), wuq_ref[...])
    ckv = _rms(zckv, gkv_ref[...])
    ckv_b = ckv.astype(BF)
    k_nope = _dot(ckv_b, wk_ref[...])
    vm_ref[0] = _dot(ckv_b, wv_ref[...]).T.astype(BF)
    if latent:
        cd, sd, cm, sm = cd_ref[...], sd_ref[...], cm_ref[...], sm_ref[...]
        zq_r = _rope(zq, cd, sd, DIFF_DH // 2, DIFF_DH, 0)
        zk_r = _rope(zk, cd, sd, DIFF_DH // 2, DIFF_DH, 0)
        qm = _rope(qm, cm, sm, MLA_ROPE // 2, HEAD_W, MLA_NOPE)
        kpe_r = _rope(kpe, cm, sm, MLA_ROPE // 2, HEAD_W, MLA_NOPE)
    else:
        zq_r, zk_r, kpe_r = zq, zk, kpe
        for hd in range(DIFF_HEADS):
            k32_ref[0, 0, :, hd, :] = zk[:, hd * HEAD_W:(hd + 1) * HEAD_W]
            v32_ref[0, 0, :, hd, :] = zv[:, hd * HEAD_W:(hd + 1) * HEAD_W]
        ckv32_ref[0, 0] = ckv
        kpe32_ref[0, 0] = kpe[:, MLA_NOPE:MLA_NOPE + MLA_ROPE]
    qd_ref[0] = (zq_r * (DIFF_DH ** -0.5 * LOG2E)).astype(BF)
    kd_ref[0] = zk_r.astype(BF)
    qm_ref[0] = (qm * ((MLA_NOPE + MLA_ROPE) ** -0.5 * LOG2E)).astype(BF)
    km_ref[0] = (k_nope + jnp.concatenate([kpe_r] * MLA_HEADS, axis=1)).astype(BF)


def _pre(latent, x, modv, g_attn, w_in_p, g_q, w_uq_p, g_kv, w_k_p, w_v_p, cc_bd, sc_bd, ropes, l=0, depth=1, prev=None):
    b, n, d = x.shape
    tm = min(256, n)
    bm = modv.shape[0]
    mod_idx = (lambda i, j: (i, 0, 0)) if bm > 1 else (lambda i, j: (0, 0, 0))
    const = lambda i, j: (0, 0)
    tok = lambda i, j: (i, j, 0)
    in_specs = [
        pl.BlockSpec((1, tm, d), tok),
        pl.BlockSpec((1, 1, modv.shape[2]), mod_idx),
        pl.BlockSpec((1, d), const),
        pl.BlockSpec(w_in_p.shape, const),
        pl.BlockSpec((1, MLA_Q_RANK), const),
        pl.BlockSpec(w_uq_p.shape, const),
        pl.BlockSpec((1, MLA_KV_RANK), const),
        pl.BlockSpec(w_k_p.shape, const),
        pl.BlockSpec(w_v_p.shape, const),
        pl.BlockSpec(cc_bd.shape, const),
        pl.BlockSpec(sc_bd.shape, const),
    ]
    args = [x, modv, g_attn.reshape(1, d), w_in_p, g_q.reshape(1, -1), w_uq_p, g_kv.reshape(1, -1), w_k_p, w_v_p, cc_bd, sc_bd]
    if latent:
        in_specs += [pl.BlockSpec((tm, HEAD_W), lambda i, j: (j, 0))] * 4
        args += list(ropes)
    feat = lambda i, j: (i, 0, j)
    out_shape, out_specs = [], []
    for w, token_major in ((D_FOURIER, True), (D_FOURIER, True), (D_DIFF, True), (D_DIFF, True), (D_DIFF, False),
                           (D_MLA_P, True), (D_MLA_P, True), (D_MLA, False)):
        out_shape.append(jax.ShapeDtypeStruct((b, n, w) if token_major else (b, w, n), BF))
        out_specs.append(pl.BlockSpec((1, tm, w), tok) if token_major else pl.BlockSpec((1, w, tm), feat))
    aliases = {}
    if not latent:
        for tail in ((DIFF_HEADS, HEAD_W), (DIFF_HEADS, HEAD_W), (MLA_KV_RANK,), (MLA_ROPE,)):
            out_shape.append(jax.ShapeDtypeStruct((b, depth, n) + tail, F32))
            out_specs.append(pl.BlockSpec((1, 1, tm) + tail, lambda i, j, t=len(tail): (i, l, j) + (0,) * t))
        if prev is not None:
            for k, buf in enumerate(prev):
                aliases[len(args)] = len(out_shape) - len(prev) + k
                in_specs.append(pl.BlockSpec(memory_space=pl.ANY))
                args.append(buf)
    return pl.pallas_call(
        functools.partial(_pre_kernel, latent, d),
        name="pre_lat" if latent else "pre_ctx",
        grid=(b, n // tm),
        in_specs=in_specs,
        out_specs=out_specs,
        out_shape=out_shape,
        input_output_aliases=aliases,
        compiler_params=_cparams(2),
    )(*args)


def _cache_kernel(dk_ref, dv_ref, ckv_ref, kpe_ref, wk_ref, wv_ref, place_ref, kd_ref, vd_ref, km_ref, vm_ref):
    for hd in range(DIFF_HEADS):
        sl = slice(hd * HEAD_W, (hd + 1) * HEAD_W)
        kd_ref[0, :, sl] = dk_ref[0, 0, :, hd, :].astype(BF)
        vd_ref[0, sl, :] = dv_ref[0, 0, :, hd, :].T.astype(BF)
    ckv_b = ckv_ref[0, 0].astype(BF)
    kpe_w = _dot(kpe_ref[0, 0].astype(BF), place_ref[...])
    km_ref[0] = (_dot(ckv_b, wk_ref[...]) + kpe_w).astype(BF)
    vm_ref[0] = _dot(ckv_b, wv_ref[...]).T.astype(BF)


def _cache_prep(l, cache_dk, cache_dv, cache_ckv, cache_kpe, w_k_p, w_v_p, place):
    b, _, p = cache_dk.shape[:3]
    at_l = lambda i: (i, l, 0, 0)
    at_l5 = lambda i: (i, l, 0, 0, 0)
    const = lambda i: (0, 0)
    shapes = [(p, D_DIFF), (D_DIFF, p), (p, D_MLA_P), (D_MLA, p)]
    return pl.pallas_call(
        _cache_kernel,
        name="cache_prep",
        grid=(b,),
        in_specs=[
            pl.BlockSpec((1, 1, p, DIFF_HEADS, HEAD_W), at_l5),
            pl.BlockSpec((1, 1, p, DIFF_HEADS, HEAD_W), at_l5),
            pl.BlockSpec((1, 1, p, MLA_KV_RANK), at_l),
            pl.BlockSpec((1, 1, p, MLA_ROPE), at_l),
            pl.BlockSpec(w_k_p.shape, const),
            pl.BlockSpec(w_v_p.shape, const),
            pl.BlockSpec(place.shape, const),
        ],
        out_specs=[pl.BlockSpec((1,) + s, lambda i: (i, 0, 0)) for s in shapes],
        out_shape=[jax.ShapeDtypeStruct((b,) + s, BF) for s in shapes],
        compiler_params=_cparams(1),
    )(cache_dk, cache_dv, cache_ckv, cache_kpe, w_k_p, w_v_p, place)


def _fourier_kernel(scale, u_ref, v_ref, cn_ref, sn_ref, w_ref, o_ref):
    f = (_dot(cn_ref[...], u_ref[0]) - _dot(sn_ref[...], v_ref[0])) * scale
    o_ref[0] = _dot(f.astype(BF), w_ref[...]).astype(BF)


def _fourier(u, v, cn, sn, w_bd):
    b, n, w = u.shape
    tm = min(512, n)
    return pl.pallas_call(
        functools.partial(_fourier_kernel, 1.0 / math.sqrt(n * F_CH)),
        name="fourier_%d" % n,
        grid=(n // tm, b),
        in_specs=[
            pl.BlockSpec((1, n, w), lambda i, j: (j, 0, 0)),
            pl.BlockSpec((1, n, w), lambda i, j: (j, 0, 0)),
            pl.BlockSpec((tm, n), lambda i, j: (i, 0)),
            pl.BlockSpec((tm, n), lambda i, j: (i, 0)),
            pl.BlockSpec((w, w), lambda i, j: (0, 0)),
        ],
        out_specs=pl.BlockSpec((1, tm, w), lambda i, j: (j, i, 0)),
        out_shape=jax.ShapeDtypeStruct((b, n, w), BF),
        compiler_params=_cparams(2),
    )(u, v, cn, sn, w_bd)


def _scores(q, k_parts, s_ref):
    off, m, bounds = 0, None, []
    for k in k_parts:
        s = _dot_nt(k(), q)
        n_i = s.shape[0]
        s_ref[off:off + n_i, :] = s
        mi = jnp.max(s, axis=0, keepdims=True)
        m = mi if m is None else jnp.maximum(m, mi)
        bounds.append((off, n_i))
        off += n_i
    return m, bounds


def _probs(m, total, s_ref, p_ref):
    for r in range(0, total, KEY_CHUNK):
        rows = min(KEY_CHUNK, total - r)
        p_ref[r:r + rows, :] = jnp.exp2(s_ref[r:r + rows, :] - m).astype(BF)


def _weighted(vt_parts, bounds, p_ref):
    o = None
    for vt, (r, n_i) in zip(vt_parts, bounds):
        v = vt()
        ones = jnp.where(lax.broadcasted_iota(jnp.int32, (BF_ROWS, n_i), 0) == 0, 1.0, 0.0).astype(BF)
        oi = _dot(jnp.concatenate([v, ones], axis=0), p_ref[r:r + n_i, :])
        o = oi if o is None else o + oi
    dv = o.shape[0] - BF_ROWS
    return o[:dv] * (1.0 / o[dv:dv + 1])


def _attn_kernel(latent, lam_init, d, *refs):
    if latent:
        (x_ref, mod_ref, yf_ref, qd_ref, kd_ref, vd_ref, kdc_ref, vdc_ref, qm_ref, km_ref, vm_ref, kmc_ref, vmc_ref,
         lam_ref, gs_ref, wo_ref, o_ref, mix_ref, s_ref, p_ref) = refs
    else:
        (x_ref, mod_ref, yf_ref, qd_ref, kd_ref, vd_ref, qm_ref, km_ref, vm_ref,
         lam_ref, gs_ref, wo_ref, o_ref, mix_ref, s_ref, p_ref) = refs
        kdc_ref = vdc_ref = kmc_ref = vmc_ref = None
    lf = lam_ref[...]
    lam = (jnp.exp(jnp.sum(lf[0:1] * lf[1:2], axis=-1, keepdims=True))
           - jnp.exp(jnp.sum(lf[2:3] * lf[3:4], axis=-1, keepdims=True)) + lam_init)
    mix_ref[:, 0:D_FOURIER] = yf_ref[0]
    gs = gs_ref[...]

    def keys(ref, cache_ref, sl):
        parts = [lambda: ref[0, :, sl]]
        if latent:
            parts.append(lambda: cache_ref[0, :, sl])
        return parts

    def values_t(ref, cache_ref, sl):
        parts = [lambda: ref[0, sl, :]]
        if latent:
            parts.append(lambda: cache_ref[0, sl, :])
        return parts

    def diff_query(sl, comp):
        def load():
            qh = qd_ref[0, :, sl]
            lane = lax.broadcasted_iota(jnp.int32, qh.shape, 1)
            keep = (lane >= DIFF_DH) if comp else (lane < DIFF_DH)
            return jnp.where(keep, qh, jnp.zeros_like(qh))
        return load

    items = []
    for hd in range(DIFF_HEADS):
        sl = slice(hd * HEAD_W, (hd + 1) * HEAD_W)
        for comp in range(2):
            items.append((diff_query(sl, comp), keys(kd_ref, kdc_ref, sl), values_t(vd_ref, vdc_ref, sl)))
    for hd in range(MLA_HEADS):
        sl = slice(hd * HEAD_W, (hd + 1) * HEAD_W)
        vsl = slice(hd * MLA_V, (hd + 1) * MLA_V)
        items.append(((lambda sl=sl: qm_ref[0, :, sl]), keys(km_ref, kmc_ref, sl), values_t(vm_ref, vmc_ref, vsl)))

    def start(i):
        return _scores(items[i][0](), items[i][1], s_ref.at[i % SCORE_BUFS])

    total = s_ref.shape[1]
    outs = []
    state = start(0)
    for i in range(len(items)):
        nxt = start(i + 1) if i + 1 < len(items) else None
        m, bounds = state
        buf = i % SCORE_BUFS
        _probs(m, total, s_ref.at[buf], p_ref.at[buf])
        outs.append(_weighted(items[i][2], bounds, p_ref.at[buf]))
        state = nxt

    for hd in range(DIFF_HEADS):
        o = outs[2 * hd] - lam * outs[2 * hd + 1]
        o = o * lax.rsqrt(jnp.mean(o * o, axis=0, keepdims=True) + NORM_EPS) * gs * (1.0 - lam_init)
        mix_ref[:, D_FOURIER + hd * HEAD_W:D_FOURIER + (hd + 1) * HEAD_W] = o.T.astype(BF)
    mix_ref[:, D_FOURIER + D_DIFF:D_MIX] = jnp.concatenate(outs[2 * DIFF_HEADS:], axis=0).T.astype(BF)
    y = _dot(mix_ref[...], wo_ref[...])
    gt = mod_ref[0, :, 2 * d:3 * d]
    o_ref[0] = x_ref[0] + gt * y


def _attention(latent, l, x, modv, yfour, qd, kd, vd, qm, km, vm, cache, diff_lambda_l, g_subln_l, w_out_b):
    b, n, d = x.shape
    tq = min(256, n)
    bm = modv.shape[0]
    mod_idx = (lambda i, j: (i, 0, 0)) if bm > 1 else (lambda i, j: (0, 0, 0))
    const = lambda i, j: (0, 0)
    tok = lambda i, j: (i, j, 0)
    full = lambda i, j: (i, 0, 0)
    p = cache[0].shape[1] if latent else 0
    in_specs = [
        pl.BlockSpec((1, tq, d), tok),
        pl.BlockSpec((1, 1, modv.shape[2]), mod_idx),
        pl.BlockSpec((1, tq, D_FOURIER), tok),
        pl.BlockSpec((1, tq, D_DIFF), tok),
        pl.BlockSpec((1, n, D_DIFF), full),
        pl.BlockSpec((1, D_DIFF, n), full),
    ]
    args = [x, modv, yfour, qd, kd, vd]
    if latent:
        kdc, vdc, kmc, vmc = cache
        in_specs += [pl.BlockSpec((1, p, D_DIFF), full), pl.BlockSpec((1, D_DIFF, p), full)]
        args += [kdc, vdc]
    in_specs += [pl.BlockSpec((1, tq, D_MLA_P), tok), pl.BlockSpec((1, n, D_MLA_P), full), pl.BlockSpec((1, D_MLA, n), full)]
    args += [qm, km, vm]
    if latent:
        in_specs += [pl.BlockSpec((1, p, D_MLA_P), full), pl.BlockSpec((1, D_MLA, p), full)]
        args += [kmc, vmc]
    in_specs += [pl.BlockSpec((4, DIFF_DH), const), pl.BlockSpec((HEAD_W, 1), const), pl.BlockSpec(w_out_b.shape, const)]
    args += [diff_lambda_l, g_subln_l.reshape(HEAD_W, 1), w_out_b]
    lam_init = 0.8 - 0.6 * math.exp(-0.3 * l)
    return pl.pallas_call(
        functools.partial(_attn_kernel, latent, lam_init, d),
        name="attn_lat" if latent else "attn_ctx",
        grid=(b, n // tq),
        in_specs=in_specs,
        out_specs=pl.BlockSpec((1, tq, d), tok),
        out_shape=jax.ShapeDtypeStruct((b, n, d), F32),
        scratch_shapes=[pltpu.VMEM((tq, D_MIX), BF), pltpu.VMEM((SCORE_BUFS, n + p, tq), F32),
                        pltpu.VMEM((SCORE_BUFS, n + p, tq), BF)],
        compiler_params=_cparams(2),
    )(*args)


def _router_kernel(d, x_ref, mod_ref, g_ref, wr_ref, h_ref, aff_ref):
    sh = mod_ref[0, :, 3 * d:4 * d]
    sc = mod_ref[0, :, 4 * d:5 * d]
    hb = (_rms(x_ref[0], g_ref[...]) * (1.0 + sc) + sh).astype(BF)
    h_ref[0] = hb
    logits = _dot_nt(wr_ref[...], hb)
    m = jnp.max(logits, axis=0, keepdims=True)
    e = jnp.exp(logits - m)
    aff_ref[0] = e / jnp.sum(e, axis=0, keepdims=True)


def _router(x, modv, g_ffn_l, w_router_t):
    b, n, d = x.shape
    tm = min(256, n)
    bm = modv.shape[0]
    mod_idx = (lambda i, j: (i, 0, 0)) if bm > 1 else (lambda i, j: (0, 0, 0))
    const = lambda i, j: (0, 0)
    return pl.pallas_call(
        functools.partial(_router_kernel, d),
        name="router_%d" % n,
        grid=(b, n // tm),
        in_specs=[
            pl.BlockSpec((1, tm, d), lambda i, j: (i, j, 0)),
            pl.BlockSpec((1, 1, modv.shape[2]), mod_idx),
            pl.BlockSpec((1, d), const),
            pl.BlockSpec(w_router_t.shape, const),
        ],
        out_specs=[pl.BlockSpec((1, tm, d), lambda i, j: (i, j, 0)), pl.BlockSpec((1, N_EXPERTS, tm), lambda i, j: (i, 0, j))],
        out_shape=[jax.ShapeDtypeStruct((b, n, d), BF), jax.ShapeDtypeStruct((b, N_EXPERTS, n), F32)],
        compiler_params=_cparams(2),
    )(x, modv, g_ffn_l.reshape(1, d), w_router_t)


def _select_kernel(cap, aff_ref, sel_ref):
    a = aff_ref[...]
    e, n = a.shape
    bits = pltpu.bitcast(a, jnp.int32)
    capf = float(cap)

    def body(_, carry):
        lo, hi = carry
        mid = lo + ((hi - lo) >> 1)
        cnt = jnp.sum(jnp.where(bits >= mid, 1.0, 0.0), axis=1, keepdims=True)
        up = cnt >= capf
        return jnp.where(up, mid, lo), jnp.where(up, hi, mid)

    lo0 = jnp.zeros((e, 1), jnp.int32)
    hi0 = jnp.full((e, 1), 0x7F800000, jnp.int32)
    thr, _ = lax.fori_loop(0, 31, body, (lo0, hi0))
    gt = bits > thr
    eq = bits == thr
    need = capf - jnp.sum(jnp.where(gt, 1.0, 0.0), axis=1, keepdims=True)
    both = jnp.concatenate([jnp.where(gt, 1.0, 0.0), jnp.where(eq, 1.0, 0.0)], axis=0).astype(BF)
    ck = min(256, n)
    tri = jnp.where(lax.broadcasted_iota(jnp.int32, (ck, ck), 0) < lax.broadcasted_iota(jnp.int32, (ck, ck), 1), 1.0, 0.0).astype(BF)
    off = jnp.zeros((2 * e, 1), F32)
    pieces = []
    for k in range(n // ck):
        blk = both[:, k * ck:(k + 1) * ck]
        pieces.append(_dot(blk, tri) + off)
        off = off + jnp.sum(blk.astype(F32), axis=1, keepdims=True)
    cum = jnp.concatenate(pieces, axis=1) if len(pieces) > 1 else pieces[0]
    cum_gt, cum_eq = cum[:e], cum[e:]
    chosen = gt | (eq & (cum_eq < need))
    pos = cum_gt + jnp.minimum(cum_eq, need)
    sel_ref[...] = jnp.where(chosen, pos, -1.0).astype(jnp.int32)


def _select(aff, cap):
    b, e, n = aff.shape
    rows = b * e
    rb = min(rows, SELECT_ROWS)
    sel = pl.pallas_call(
        functools.partial(_select_kernel, cap),
        name="select_%d" % n,
        grid=(rows // rb,),
        in_specs=[pl.BlockSpec((rb, n), lambda i: (i, 0))],
        out_specs=pl.BlockSpec((rb, n), lambda i: (i, 0)),
        out_shape=jax.ShapeDtypeStruct((rows, n), jnp.int32),
        compiler_params=_cparams(1),
    )(aff.reshape(rows, n))
    return sel.reshape(b, e, n)


def _gather_kernel(cap, sel_ref, aff_ref, h_ref, xs_ref, gate_ref):
    h = h_ref[0]
    n = h.shape[0]
    e_tot = sel_ref.shape[1]

    def one(e_idx):
        sel_e = sel_ref[0, pl.ds(e_idx, 1), :]
        aff_e = aff_ref[0, pl.ds(e_idx, 1), :]
        hit = sel_e == lax.broadcasted_iota(jnp.int32, (cap, n), 0)
        g = jnp.sum(jnp.where(hit, aff_e, 0.0), axis=1, keepdims=True)
        return jnp.where(hit, 1.0, 0.0).astype(BF), jnp.broadcast_to(g, (cap, HEAD_W))

    if cap >= 128:
        def body(e_idx, carry):
            p, g = one(e_idx)
            xs_ref[e_idx, 0] = _dot(p, h).astype(BF)
            gate_ref[e_idx, 0] = g
            return carry

        lax.fori_loop(0, e_tot, body, 0)
    else:
        ps, gs = zip(*[one(e_idx) for e_idx in range(e_tot)])
        xs = _dot(jnp.concatenate(ps, axis=0), h).astype(BF)
        for e_idx in range(e_tot):
            xs_ref[e_idx, 0] = xs[e_idx * cap:(e_idx + 1) * cap]
            gate_ref[e_idx, 0] = gs[e_idx]


def _gather(sel, aff, h2, cap):
    b, e, n = sel.shape
    d = h2.shape[2]
    return pl.pallas_call(
        functools.partial(_gather_kernel, cap),
        name="gather_%d" % n,
        grid=(b,),
        in_specs=[
            pl.BlockSpec((1, e, n), lambda i: (i, 0, 0)),
            pl.BlockSpec((1, e, n), lambda i: (i, 0, 0)),
            pl.BlockSpec((1, n, d), lambda i: (i, 0, 0)),
        ],
        out_specs=[pl.BlockSpec((e, 1, cap, d), lambda i: (0, i, 0, 0)), pl.BlockSpec((e, 1, cap, HEAD_W), lambda i: (0, i, 0, 0))],
        out_shape=[jax.ShapeDtypeStruct((e, b, cap, d), BF), jax.ShapeDtypeStruct((e, b, cap, HEAD_W), F32)],
        compiler_params=_cparams(1),
    )(sel, aff, h2)


def _expert_kernel(steps_a, xa_ref, ga_ref, xb_ref, gb_ref, wg_ref, wu_ref, wd_ref, ya_ref, yb_ref, wg_s, wu_s, wd_s):
    j = pl.program_id(1)

    @pl.when(j == 0)
    def _():
        wg_s[...] = wg_ref[0, 0].astype(BF)
        wu_s[...] = wu_ref[0, 0].astype(BF)
        wd_s[...] = wd_ref[0, 0].astype(BF)

    def ffn(x_ref, g_ref, y_ref):
        x = x_ref[0]
        a = _dot(x, wg_s[...])
        u = _dot(x, wu_s[...])
        mid = (a * (1.0 / (1.0 + jnp.exp(-a))) * u).astype(BF)
        y_ref[0] = (_dot(mid, wd_s[...]) * g_ref[0][:, 0:1]).astype(BF)

    @pl.when(j < steps_a)
    def _():
        ffn(xa_ref, ga_ref, ya_ref)

    @pl.when(j >= steps_a)
    def _():
        ffn(xb_ref, gb_ref, yb_ref)


def _experts(l, xs_a, gate_a, xs_b, gate_b, w_e_gate, w_e_up, w_e_down):
    e, rows_a, d = xs_a.shape
    rows_b = xs_b.shape[1]
    ff = w_e_gate.shape[3]
    rb = math.gcd(math.gcd(rows_a, rows_b), EXPERT_ROWS)
    steps_a, steps_b = rows_a // rb, rows_b // rb
    w_idx = lambda i, j: (l, i, 0, 0)
    idx_a = lambda i, j: (i, jnp.minimum(j, steps_a - 1), 0)
    idx_b = lambda i, j: (i, jnp.maximum(j - steps_a, 0), 0)
    return pl.pallas_call(
        functools.partial(_expert_kernel, steps_a),
        name="experts",
        grid=(e, steps_a + steps_b),
        in_specs=[
            pl.BlockSpec((1, rb, d), idx_a),
            pl.BlockSpec((1, rb, HEAD_W), idx_a),
            pl.BlockSpec((1, rb, d), idx_b),
            pl.BlockSpec((1, rb, HEAD_W), idx_b),
            pl.BlockSpec((1, 1, d, ff), w_idx),
            pl.BlockSpec((1, 1, d, ff), w_idx),
            pl.BlockSpec((1, 1, ff, d), w_idx),
        ],
        out_specs=[pl.BlockSpec((1, rb, d), idx_a), pl.BlockSpec((1, rb, d), idx_b)],
        out_shape=[jax.ShapeDtypeStruct((e, rows_a, d), BF), jax.ShapeDtypeStruct((e, rows_b, d), BF)],
        scratch_shapes=[pltpu.VMEM((d, ff), BF), pltpu.VMEM((d, ff), BF), pltpu.VMEM((ff, d), BF)],
        compiler_params=_cparams(2),
    )(xs_a, gate_a, xs_b, gate_b, w_e_gate, w_e_up, w_e_down)


def _scatter_kernel(cap, d, final, *refs):
    if final:
        x_ref, mod_ref, selt_ref, ye_ref, gf_ref, o_ref, pt_ref = refs
    else:
        x_ref, mod_ref, selt_ref, ye_ref, o_ref, pt_ref = refs
    e_tot = ye_ref.shape[0]
    tn = x_ref.shape[1]
    selt = selt_ref[0]
    if cap % 128 == 0:
        lane = lax.broadcasted_iota(jnp.int32, (tn, cap), 1)
        for e_idx in range(e_tot):
            pt_ref[:, e_idx * cap:(e_idx + 1) * cap] = jnp.where(selt[:, e_idx:e_idx + 1] == lane, 1.0, 0.0).astype(BF)
    else:
        lane = lax.broadcasted_iota(jnp.int32, (tn, e_tot * cap), 1)
        acc = jnp.zeros((tn, e_tot * cap), F32)
        for e_idx in range(e_tot):
            s = selt[:, e_idx:e_idx + 1]
            acc = acc + jnp.where((s >= 0) & (s + e_idx * cap == lane), 1.0, 0.0)
        pt_ref[...] = acc.astype(BF)
    y = _dot(pt_ref[...], ye_ref[...].reshape(e_tot * cap, d))
    gt = mod_ref[0, :, 5 * d:6 * d]
    out = x_ref[0] + gt * y
    if final:
        out = _rms(out, gf_ref[...])
    o_ref[0] = out


def _scatter(x, modv, sel_t, ye, cap, g_final):
    b, n, d = x.shape
    e = ye.shape[0]
    tn = min(512, n)
    bm = modv.shape[0]
    mod_idx = (lambda i, j: (i, 0, 0)) if bm > 1 else (lambda i, j: (0, 0, 0))
    final = g_final is not None
    in_specs = [
        pl.BlockSpec((1, tn, d), lambda i, j: (i, j, 0)),
        pl.BlockSpec((1, 1, modv.shape[2]), mod_idx),
        pl.BlockSpec((1, tn, e), lambda i, j: (i, j, 0)),
        pl.BlockSpec((e, 1, cap, d), lambda i, j: (0, i, 0, 0)),
    ]
    args = [x, modv, sel_t, ye]
    if final:
        in_specs.append(pl.BlockSpec((1, d), lambda i, j: (0, 0)))
        args.append(g_final.reshape(1, d))
    return pl.pallas_call(
        functools.partial(_scatter_kernel, cap, d, final),
        name="scatter_%d" % n,
        grid=(b, n // tn),
        in_specs=in_specs,
        out_specs=pl.BlockSpec((1, tn, d), lambda i, j: (i, j, 0)),
        out_shape=jax.ShapeDtypeStruct((b, n, d), F32),
        scratch_shapes=[pltpu.VMEM((tn, e * cap), BF)],
        compiler_params=_cparams(2),
    )(*args)


def _dft_tables(n):
    j = np.arange(n, dtype=np.int64)
    ang = ((j[:, None] * j[None, :]) % n).astype(np.float64) * (2.0 * math.pi / n)
    return jnp.asarray(np.cos(ang), dtype=BF), jnp.asarray(np.sin(ang), dtype=BF)


def _block_diag(blocks):
    g, r, c = blocks.shape
    out = jnp.zeros((g * r, g * c), blocks.dtype)
    for i in range(g):
        out = out.at[i * r:(i + 1) * r, i * c:(i + 1) * c].set(blocks[i])
    return out


def _rope_tables(n):
    rows = n // GRID_W
    row = jnp.repeat(jnp.arange(rows, dtype=F32), GRID_W)
    col = jnp.tile(jnp.arange(GRID_W, dtype=F32), rows)

    def ang(dim):
        nf = dim // 4
        inv = ROPE_BASE ** (-jnp.arange(nf, dtype=F32) / nf)
        return jnp.concatenate([row[:, None] * inv, col[:, None] * inv], axis=-1)

    a = ang(DIFF_DH)
    cos_d = jnp.tile(jnp.cos(a), (1, 4))
    sin_d = jnp.tile(jnp.concatenate([-jnp.sin(a), jnp.sin(a)], axis=1), (1, 2))
    a = ang(MLA_ROPE)
    ones = jnp.ones((n, MLA_NOPE), F32)
    pad = HEAD_W - MLA_NOPE - MLA_ROPE
    cos_m = jnp.concatenate([ones, jnp.cos(a), jnp.cos(a), jnp.ones((n, pad), F32)], axis=1)
    sin_m = jnp.concatenate([0 * ones, -jnp.sin(a), jnp.sin(a), jnp.zeros((n, pad), F32)], axis=1)
    return cos_d, sin_d, cos_m, sin_m


def _pad_heads(w, heads, lo, hi):
    k = w.shape[0]
    w3 = w.reshape(k, heads, -1)[:, :, lo:hi]
    return jnp.pad(w3, ((0, 0), (0, 0), (0, HEAD_W - (hi - lo)))).reshape(k, heads * HEAD_W)


def kernel(x_prompt, x_sample, cache_diff_k, cache_diff_v, cache_mla_ckv, cache_mla_kpe, c, c_ctx, w_ada, b_ada, g_attn, g_ffn, w_in, w_four, diff_lambda, g_subln, g_mla_q, w_mla_uq, g_mla_kv, w_mla_ukv, w_out, w_router, w_e_gate, w_e_up, w_e_down, g_final):
    depth, d = g_attn.shape
    b_ctx, n_ctx, _ = x_prompt.shape
    b_lat, n_lat, _ = x_sample.shape

    r = b_lat + 1
    r_pad = -(-r // 8) * 8
    cond = jnp.concatenate([c, c_ctx[None, :], jnp.zeros((r_pad - r, d), F32)], axis=0)
    mod = _modulation(cond, w_ada, b_ada)

    n_main = w_in.shape[2] - MLA_ROPE
    w_in_p = jnp.concatenate(
        [w_in[:, :, :n_main], jnp.zeros((depth, d, MLA_NOPE), F32), w_in[:, :, n_main:],
         jnp.zeros((depth, d, HEAD_W - MLA_NOPE - MLA_ROPE), F32)], axis=2).astype(BF)
    w_uq_p = jnp.stack([_pad_heads(w_mla_uq[l], MLA_HEADS, 0, MLA_NOPE + MLA_ROPE) for l in range(depth)]).astype(BF)
    w_k_p = jnp.stack([_pad_heads(w_mla_ukv[l], MLA_HEADS, 0, MLA_NOPE) for l in range(depth)]).astype(BF)
    w_v_p = w_mla_ukv.reshape(depth, MLA_KV_RANK, MLA_HEADS, MLA_NOPE + MLA_V)[..., MLA_NOPE:].reshape(depth, MLA_KV_RANK, D_MLA).astype(BF)
    w_out_p = w_out.astype(BF)
    w_router_t = jnp.swapaxes(w_router, 1, 2).astype(BF)
    place = jnp.zeros((MLA_ROPE, HEAD_W), F32).at[jnp.arange(MLA_ROPE), MLA_NOPE + jnp.arange(MLA_ROPE)].set(1.0)
    place = jnp.tile(place, (1, MLA_HEADS)).astype(BF)

    jc = jnp.arange(F_CH, dtype=jnp.int32)
    ang_c = ((jc[:, None] * jc[None, :]) % F_CH).astype(F32) * (2.0 * math.pi / F_CH)
    cc_bd = _block_diag(jnp.broadcast_to(jnp.cos(ang_c), (F_GROUPS, F_CH, F_CH))).astype(BF)
    sc_bd = _block_diag(jnp.broadcast_to(jnp.sin(ang_c), (F_GROUPS, F_CH, F_CH))).astype(BF)
    dft = {n: _dft_tables(n) for n in {n_ctx, n_lat}}
    ropes = _rope_tables(n_lat)

    def mix_and_route(latent, l, x, prev=None):
        b, n, _ = x.shape
        cap = max(1, EC_CAPACITY_FACTOR * n // N_EXPERTS)
        modv = (mod[l, :b_lat] if latent else mod[l, b_lat:b_lat + 1]).reshape(-1, 1, N_MOD * d)
        outs = _pre(latent, x, modv, g_attn[l], w_in_p[l], g_mla_q[l], w_uq_p[l], g_mla_kv[l], w_k_p[l], w_v_p[l],
                    cc_bd, sc_bd, ropes, l, depth, prev)
        u, v, qd, kd, vd, qm, km, vm = outs[:8]
        cache = _cache_prep(l, cache_diff_k, cache_diff_v, cache_mla_ckv, cache_mla_kpe, w_k_p[l], w_v_p[l], place) if latent else None
        cn, sn = dft[n]
        yfour = _fourier(u, v, cn, sn, _block_diag(w_four[l]).astype(BF))
        x = _attention(latent, l, x, modv, yfour, qd, kd, vd, qm, km, vm, cache, diff_lambda[l], g_subln[l], w_out_p[l])
        h2, aff = _router(x, modv, g_ffn[l], w_router_t[l])
        sel = _select(aff, cap)
        xs, gate = _gather(sel, aff, h2, cap)
        e = xs.shape[0]
        return x, modv, sel, cap, xs.reshape(e, b * cap, d), gate.reshape(e, b * cap, HEAD_W), tuple(outs[8:])

    def combine(x, modv, sel, cap, ye, l):
        b = x.shape[0]
        ye = ye.reshape(ye.shape[0], b, cap, d)
        return _scatter(x, modv, jnp.swapaxes(sel, 1, 2), ye, cap, g_final if l == depth - 1 else None)

    x_c, x_l, new = x_prompt, x_sample, None
    for l in range(depth):
        x_c, mod_c, sel_c, cap_c, xs_c, gate_c, new = mix_and_route(False, l, x_c, new)
        x_l, mod_l, sel_l, cap_l, xs_l, gate_l, _ = mix_and_route(True, l, x_l)
        ye_c, ye_l = _experts(l, xs_c, gate_c, xs_l, gate_l, w_e_gate, w_e_up, w_e_down)
        x_c = combine(x_c, mod_c, sel_c, cap_c, ye_c, l)
        x_l = combine(x_l, mod_l, sel_l, cap_l, ye_l, l)
    y_prompt, y_sample = x_c, x_l
    new_diff_k, new_diff_v, new_mla_ckv, new_mla_kpe = new
    return (y_prompt, y_sample, new_diff_k, new_diff_v, new_mla_ckv, new_mla_kpe)
```

```python
import functools
import math

import jax
import jax.numpy as jnp
import numpy as np
from jax import lax
from jax.experimental import pallas as pl
from jax.experimental.pallas import tpu as pltpu

BF = jnp.bfloat16
F32 = jnp.float32

GRID_W = 64
ROPE_BASE = 10000.0
NORM_EPS = 1e-6
F_GROUPS, F_CH = 4, 64
D_FOURIER = F_GROUPS * F_CH
DIFF_HEADS, DIFF_DH = 4, 64
D_DIFF = DIFF_HEADS * 2 * DIFF_DH
MLA_HEADS, MLA_Q_RANK, MLA_KV_RANK = 4, 256, 128
MLA_NOPE, MLA_ROPE, MLA_V = 64, 32, 64
HEAD_W = 128
D_MLA_P = MLA_HEADS * HEAD_W
D_MLA = MLA_HEADS * MLA_V
D_MIX = D_FOURIER + D_DIFF + D_MLA
KEY_CHUNK = 256
SCORE_BUFS = 2
BF_ROWS = 16
SELECT_ROWS = 128
EXPERT_ROWS = 512
PRE_ROWS = 512
SUB_ROWS = 256
N_EXPERTS = 16
EC_CAPACITY_FACTOR = 2
N_MOD = 6
IN_COLS_P = D_FOURIER + 3 * D_DIFF + MLA_Q_RANK + MLA_KV_RANK + HEAD_W
LOG2E = 1.4426950408889634
VMEM_LIMIT = 56 * 1024 * 1024


def _cparams(n_axes, vmem=VMEM_LIMIT):
    return pltpu.CompilerParams(dimension_semantics=("arbitrary",) * n_axes, vmem_limit_bytes=vmem)


def _dot(a, b):
    return jnp.dot(a, b, preferred_element_type=F32)


def _dot_nt(a, b):
    return lax.dot_general(a, b, (((1,), (1,)), ((), ())), preferred_element_type=F32)


def _rms(x, g):
    return x * lax.rsqrt(jnp.mean(x * x, axis=-1, keepdims=True) + NORM_EPS) * g


def _rope(z, cos, sin_signed, half, group, lo):
    w = z.shape[1]
    reps = w // cos.shape[1]
    cos_w = jnp.concatenate([cos] * reps, axis=1) if reps > 1 else cos
    sin_w = jnp.concatenate([sin_signed] * reps, axis=1) if reps > 1 else sin_signed
    from_right = pltpu.roll(z, w - half, 1)
    from_left = pltpu.roll(z, half, 1)
    lane = lax.broadcasted_iota(jnp.int32, z.shape, 1) % group
    first = (lane >= lo) & (lane < lo + half)
    partner = jnp.where(first, from_right, from_left)
    return z * cos_w + partner * sin_w


def _mod_kernel(c_ref, w_ref, b_ref, o_ref):
    c = c_ref[...]
    a = (c * (1.0 / (1.0 + jnp.exp(-c)))).astype(BF)
    o_ref[0] = _dot(a, w_ref[0].astype(BF)) + b_ref[0]


def _modulation(cond, w_ada, b_ada):
    depth, d, n6 = w_ada.shape
    r = cond.shape[0]
    tn = 1536
    return pl.pallas_call(
        _mod_kernel,
        name="modulation",
        grid=(depth, n6 // tn),
        in_specs=[
            pl.BlockSpec((r, d), lambda l, j: (0, 0)),
            pl.BlockSpec((1, d, tn), lambda l, j: (l, 0, j)),
            pl.BlockSpec((1, 1, tn), lambda l, j: (l, 0, j)),
        ],
        out_specs=pl.BlockSpec((1, r, tn), lambda l, j: (l, 0, j)),
        out_shape=jax.ShapeDtypeStruct((depth, r, n6), F32),
        compiler_params=_cparams(2),
    )(cond, w_ada, b_ada.reshape(depth, 1, n6))


def _pre_kernel(latent, d, *refs):
    if latent:
        (x_ref, mod_ref, g_ref, win_ref, gq_ref, wuq_ref, gkv_ref, wk_ref, wv_ref, cc_ref, sc_ref,
         cd_ref, sd_ref, cm_ref, sm_ref,
         u_ref, v_ref, qd_ref, kd_ref, vd_ref, qm_ref, km_ref, vm_ref) = refs
    else:
        (x_ref, mod_ref, g_ref, win_ref, gq_ref, wuq_ref, gkv_ref, wk_ref, wv_ref, cc_ref, sc_ref,
         u_ref, v_ref, qd_ref, kd_ref, vd_ref, qm_ref, km_ref, vm_ref,
         k32_ref, v32_ref, ckv32_ref, kpe32_ref) = refs[:11] + refs[-12:]
    sh = mod_ref[0, :, 0:d]
    sc = mod_ref[0, :, d:2 * d]
    tm = x_ref.shape[1]
    sub = min(tm, SUB_ROWS)
    def project(r0):
        h = _rms(x_ref[0, r0:r0 + sub, :], g_ref[...]) * (1.0 + sc) + sh
        return _dot(h.astype(BF), win_ref[...])

    starts = list(range(0, tm, sub))
    z_next = project(starts[0])
    for i, r0 in enumerate(starts):
        rows = slice(r0, r0 + sub)
        z = z_next
        z_next = project(starts[i + 1]) if i + 1 < len(starts) else None
        o = 0
        zf = z[:, o:o + D_FOURIER]; o += D_FOURIER
        zq = z[:, o:o + D_DIFF]; o += D_DIFF
        zk = z[:, o:o + D_DIFF]; o += D_DIFF
        zv = z[:, o:o + D_DIFF]; o += D_DIFF
        zcq = z[:, o:o + MLA_Q_RANK]; o += MLA_Q_RANK
        zckv = z[:, o:o + MLA_KV_RANK]; o += MLA_KV_RANK
        kpe = z[:, o:o + HEAD_W]

        zf_b = zf.astype(BF)
        u_ref[0, rows, :] = _dot(zf_b, cc_ref[...]).astype(BF)
        v_ref[0, rows, :] = _dot(zf_b, sc_ref[...]).astype(BF)
        vd_ref[0, :, rows] = zv.T.astype(BF)
        cq = _rms(zcq, gq_ref[...])
        qm = _dot(cq.astype(BF), wuq_ref[...])
        ckv = _rms(zckv, gkv_ref[...])
        ckv_b = ckv.astype(BF)
        k_nope = _dot(ckv_b, wk_ref[...])
        vm_ref[0, :, rows] = _dot(ckv_b, wv_ref[...]).T.astype(BF)
        if latent:
            cd, sd, cm, sm = cd_ref[rows, :], sd_ref[rows, :], cm_ref[rows, :], sm_ref[rows, :]
            zq_r = _rope(zq, cd, sd, DIFF_DH // 2, DIFF_DH, 0)
            zk_r = _rope(zk, cd, sd, DIFF_DH // 2, DIFF_DH, 0)
            qm = _rope(qm, cm, sm, MLA_ROPE // 2, HEAD_W, MLA_NOPE)
            kpe_r = _rope(kpe, cm, sm, MLA_ROPE // 2, HEAD_W, MLA_NOPE)
        else:
            zq_r, zk_r, kpe_r = zq, zk, kpe
            for hd in range(DIFF_HEADS):
                k32_ref[0, 0, rows, hd, :] = zk[:, hd * HEAD_W:(hd + 1) * HEAD_W]
                v32_ref[0, 0, rows, hd, :] = zv[:, hd * HEAD_W:(hd + 1) * HEAD_W]
            ckv32_ref[0, 0, rows, :] = ckv
            kpe32_ref[0, 0, rows, :] = kpe[:, MLA_NOPE:MLA_NOPE + MLA_ROPE]
        qd_ref[0, rows, :] = (zq_r * (DIFF_DH ** -0.5 * LOG2E)).astype(BF)
        kd_ref[0, rows, :] = zk_r.astype(BF)
        qm_ref[0, rows, :] = (qm * ((MLA_NOPE + MLA_ROPE) ** -0.5 * LOG2E)).astype(BF)
        km_ref[0, rows, :] = (k_nope + jnp.concatenate([kpe_r] * MLA_HEADS, axis=1)).astype(BF)


def _combine_pre_kernel(latent, d, cap, n_pre_in, *refs):
    x_ref, modp_ref, selt_ref, ye_ref = refs[:4]
    pre_in = refs[4:4 + n_pre_in]
    xo_ref = refs[4 + n_pre_in]
    pre_out = refs[5 + n_pre_in:-1]
    _scatter_kernel(cap, d, False, x_ref, modp_ref, selt_ref, ye_ref, xo_ref, refs[-1])
    _pre_kernel(latent, d, xo_ref, *pre_in, *pre_out)


def _pre(latent, x, modv, g_attn, w_in_p, g_q, w_uq_p, g_kv, w_k_p, w_v_p, cc_bd, sc_bd, ropes, l=0, depth=1, prev=None,
         combine=None):
    b, n, d = x.shape
    tm = min(PRE_ROWS, n)
    bm = modv.shape[0]
    mod_idx = (lambda i, j: (i, 0, 0)) if bm > 1 else (lambda i, j: (0, 0, 0))
    const = lambda i, j: (0, 0)
    tok = lambda i, j: (i, j, 0)
    in_specs = [
        pl.BlockSpec((1, tm, d), tok),
        pl.BlockSpec((1, 1, modv.shape[2]), mod_idx),
        pl.BlockSpec((1, d), const),
        pl.BlockSpec(w_in_p.shape, const),
        pl.BlockSpec((1, MLA_Q_RANK), const),
        pl.BlockSpec(w_uq_p.shape, const),
        pl.BlockSpec((1, MLA_KV_RANK), const),
        pl.BlockSpec(w_k_p.shape, const),
        pl.BlockSpec(w_v_p.shape, const),
        pl.BlockSpec(cc_bd.shape, const),
        pl.BlockSpec(sc_bd.shape, const),
    ]
    args = [x, modv, g_attn.reshape(1, d), w_in_p, g_q.reshape(1, -1), w_uq_p, g_kv.reshape(1, -1), w_k_p, w_v_p, cc_bd, sc_bd]
    if latent:
        in_specs += [pl.BlockSpec((tm, HEAD_W), lambda i, j: (j, 0))] * 4
        args += list(ropes)
    feat = lambda i, j: (i, 0, j)
    out_shape, out_specs = [], []
    for w, token_major in ((D_FOURIER, True), (D_FOURIER, True), (D_DIFF, True), (D_DIFF, True), (D_DIFF, False),
                           (D_MLA_P, True), (D_MLA_P, True), (D_MLA, False)):
        out_shape.append(jax.ShapeDtypeStruct((b, n, w) if token_major else (b, w, n), BF))
        out_specs.append(pl.BlockSpec((1, tm, w), tok) if token_major else pl.BlockSpec((1, w, tm), feat))
    aliases = {}
    if not latent:
        for tail in ((DIFF_HEADS, HEAD_W), (DIFF_HEADS, HEAD_W), (MLA_KV_RANK,), (MLA_ROPE,)):
            out_shape.append(jax.ShapeDtypeStruct((b, depth, n) + tail, F32))
            out_specs.append(pl.BlockSpec((1, 1, tm) + tail, lambda i, j, t=len(tail): (i, l, j) + (0,) * t))
        if prev is not None:
            for k, buf in enumerate(prev):
                aliases[len(args)] = len(out_shape) - len(prev) + k
                in_specs.append(pl.BlockSpec(memory_space=pl.ANY))
                args.append(buf)
    body, scratch, tag = functools.partial(_pre_kernel, latent, d), [], "pre"
    if combine is not None:
        modp, sel_t, ye, cap = combine
        e = ye.shape[0]
        modp_idx = (lambda i, j: (i, 0, 0)) if modp.shape[0] > 1 else (lambda i, j: (0, 0, 0))
        extra_specs = [pl.BlockSpec((1, 1, modp.shape[2]), modp_idx), pl.BlockSpec((1, tm, e), tok),
                       pl.BlockSpec((e, 1, cap, d), lambda i, j: (0, i, 0, 0))]
        n_extra = len(extra_specs)
        body = functools.partial(_combine_pre_kernel, latent, d, cap, len(args) - 1)
        in_specs = in_specs[:1] + extra_specs + in_specs[1:]
        args = args[:1] + [modp, sel_t, ye] + args[1:]
        out_specs = [pl.BlockSpec((1, tm, d), tok)] + out_specs
        out_shape = [jax.ShapeDtypeStruct((b, n, d), F32)] + out_shape
        aliases = {k + n_extra: v + 1 for k, v in aliases.items()}
        scratch, tag = [pltpu.VMEM((tm, e * cap), BF)], "combine_pre"
    return pl.pallas_call(
        body,
        name=tag + ("_lat" if latent else "_ctx"),
        grid=(b, n // tm),
        in_specs=in_specs,
        out_specs=out_specs,
        out_shape=out_shape,
        input_output_aliases=aliases,
        scratch_shapes=scratch,
        compiler_params=_cparams(2),
    )(*args)


def _cache_kernel(dk_ref, dv_ref, ckv_ref, kpe_ref, wk_ref, wv_ref, place_ref, kd_ref, vd_ref, km_ref, vm_ref):
    for hd in range(DIFF_HEADS):
        sl = slice(hd * HEAD_W, (hd + 1) * HEAD_W)
        kd_ref[0, :, sl] = dk_ref[0, 0, :, hd, :].astype(BF)
        vd_ref[0, sl, :] = dv_ref[0, 0, :, hd, :].T.astype(BF)
    ckv_b = ckv_ref[0, 0].astype(BF)
    kpe_w = _dot(kpe_ref[0, 0].astype(BF), place_ref[...])
    km_ref[0] = (_dot(ckv_b, wk_ref[...]) + kpe_w).astype(BF)
    vm_ref[0] = _dot(ckv_b, wv_ref[...]).T.astype(BF)


def _cache_prep(l, cache_dk, cache_dv, cache_ckv, cache_kpe, w_k_p, w_v_p, place):
    b, _, p = cache_dk.shape[:3]
    at_l = lambda i: (i, l, 0, 0)
    at_l5 = lambda i: (i, l, 0, 0, 0)
    const = lambda i: (0, 0)
    shapes = [(p, D_DIFF), (D_DIFF, p), (p, D_MLA_P), (D_MLA, p)]
    return pl.pallas_call(
        _cache_kernel,
        name="cache_prep",
        grid=(b,),
        in_specs=[
            pl.BlockSpec((1, 1, p, DIFF_HEADS, HEAD_W), at_l5),
            pl.BlockSpec((1, 1, p, DIFF_HEADS, HEAD_W), at_l5),
            pl.BlockSpec((1, 1, p, MLA_KV_RANK), at_l),
            pl.BlockSpec((1, 1, p, MLA_ROPE), at_l),
            pl.BlockSpec(w_k_p.shape, const),
            pl.BlockSpec(w_v_p.shape, const),
            pl.BlockSpec(place.shape, const),
        ],
        out_specs=[pl.BlockSpec((1,) + s, lambda i: (i, 0, 0)) for s in shapes],
        out_shape=[jax.ShapeDtypeStruct((b,) + s, BF) for s in shapes],
        compiler_params=_cparams(1),
    )(cache_dk, cache_dv, cache_ckv, cache_kpe, w_k_p, w_v_p, place)


def _fourier_kernel(scale, u_ref, v_ref, cn_ref, sn_ref, w_ref, o_ref):
    f = (_dot(cn_ref[...], u_ref[0]) - _dot(sn_ref[...], v_ref[0])) * scale
    o_ref[0] = _dot(f.astype(BF), w_ref[...]).astype(BF)


def _fourier(u, v, cn, sn, w_bd):
    b, n, w = u.shape
    tm = min(512, n)
    return pl.pallas_call(
        functools.partial(_fourier_kernel, 1.0 / math.sqrt(n * F_CH)),
        name="fourier_%d" % n,
        grid=(n // tm, b),
        in_specs=[
            pl.BlockSpec((1, n, w), lambda i, j: (j, 0, 0)),
            pl.BlockSpec((1, n, w), lambda i, j: (j, 0, 0)),
            pl.BlockSpec((tm, n), lambda i, j: (i, 0)),
            pl.BlockSpec((tm, n), lambda i, j: (i, 0)),
            pl.BlockSpec((w, w), lambda i, j: (0, 0)),
        ],
        out_specs=pl.BlockSpec((1, tm, w), lambda i, j: (j, i, 0)),
        out_shape=jax.ShapeDtypeStruct((b, n, w), BF),
        compiler_params=_cparams(2),
    )(u, v, cn, sn, w_bd)


def _scores(q, k_parts, s_ref):
    off, m, bounds = 0, None, []
    for k in k_parts:
        s = _dot_nt(k(), q)
        n_i = s.shape[0]
        s_ref[off:off + n_i, :] = s
        mi = jnp.max(s, axis=0, keepdims=True)
        m = mi if m is None else jnp.maximum(m, mi)
        bounds.append((off, n_i))
        off += n_i
    return m, bounds


def _probs(m, total, s_ref, p_ref):
    for r in range(0, total, KEY_CHUNK):
        rows = min(KEY_CHUNK, total - r)
        p_ref[r:r + rows, :] = jnp.exp2(s_ref[r:r + rows, :] - m).astype(BF)


def _weighted(vt_parts, bounds, p_ref):
    o = None
    for vt, (r, n_i) in zip(vt_parts, bounds):
        v = vt()
        ones = jnp.where(lax.broadcasted_iota(jnp.int32, (BF_ROWS, n_i), 0) == 0, 1.0, 0.0).astype(BF)
        oi = _dot(jnp.concatenate([v, ones], axis=0), p_ref[r:r + n_i, :])
        o = oi if o is None else o + oi
    dv = o.shape[0] - BF_ROWS
    return o[:dv] * (1.0 / o[dv:dv + 1])


def _attn_kernel(latent, lam_init, d, *refs):
    if latent:
        (x_ref, mod_ref, yf_ref, qd_ref, kd_ref, vd_ref, kdc_ref, vdc_ref, qm_ref, km_ref, vm_ref, kmc_ref, vmc_ref,
         lam_ref, gs_ref, wo_ref, o_ref, mix_ref, s_ref, p_ref) = refs
    else:
        (x_ref, mod_ref, yf_ref, qd_ref, kd_ref, vd_ref, qm_ref, km_ref, vm_ref,
         lam_ref, gs_ref, wo_ref, o_ref, mix_ref, s_ref, p_ref) = refs
        kdc_ref = vdc_ref = kmc_ref = vmc_ref = None
    lf = lam_ref[...]
    lam = (jnp.exp(jnp.sum(lf[0:1] * lf[1:2], axis=-1, keepdims=True))
           - jnp.exp(jnp.sum(lf[2:3] * lf[3:4], axis=-1, keepdims=True)) + lam_init)
    mix_ref[:, 0:D_FOURIER] = yf_ref[0]
    gs = gs_ref[...]

    def keys(ref, cache_ref, sl):
        parts = [lambda: ref[0, :, sl]]
        if latent:
            parts.append(lambda: cache_ref[0, :, sl])
        return parts

    def values_t(ref, cache_ref, sl):
        parts = [lambda: ref[0, sl, :]]
        if latent:
            parts.append(lambda: cache_ref[0, sl, :])
        return parts

    def diff_query(sl, comp):
        def load():
            qh = qd_ref[0, :, sl]
            lane = lax.broadcasted_iota(jnp.int32, qh.shape, 1)
            keep = (lane >= DIFF_DH) if comp else (lane < DIFF_DH)
            return jnp.where(keep, qh, jnp.zeros_like(qh))
        return load

    items = []
    for hd in range(DIFF_HEADS):
        sl = slice(hd * HEAD_W, (hd + 1) * HEAD_W)
        for comp in range(2):
            items.append((diff_query(sl, comp), keys(kd_ref, kdc_ref, sl), values_t(vd_ref, vdc_ref, sl)))
    for hd in range(MLA_HEADS):
        sl = slice(hd * HEAD_W, (hd + 1) * HEAD_W)
        vsl = slice(hd * MLA_V, (hd + 1) * MLA_V)
        items.append(((lambda sl=sl: qm_ref[0, :, sl]), keys(km_ref, kmc_ref, sl), values_t(vm_ref, vmc_ref, vsl)))

    def start(i):
        return _scores(items[i][0](), items[i][1], s_ref.at[i % SCORE_BUFS])

    total = s_ref.shape[1]
    outs = []
    state = start(0)
    for i in range(len(items)):
        nxt = start(i + 1) if i + 1 < len(items) else None
        m, bounds = state
        buf = i % SCORE_BUFS
        _probs(m, total, s_ref.at[buf], p_ref.at[buf])
        outs.append(_weighted(items[i][2], bounds, p_ref.at[buf]))
        state = nxt

    for hd in range(DIFF_HEADS):
        o = outs[2 * hd] - lam * outs[2 * hd + 1]
        o = o * lax.rsqrt(jnp.mean(o * o, axis=0, keepdims=True) + NORM_EPS) * gs * (1.0 - lam_init)
        mix_ref[:, D_FOURIER + hd * HEAD_W:D_FOURIER + (hd + 1) * HEAD_W] = o.T.astype(BF)
    mix_ref[:, D_FOURIER + D_DIFF:D_MIX] = jnp.concatenate(outs[2 * DIFF_HEADS:], axis=0).T.astype(BF)
    y = _dot(mix_ref[...], wo_ref[...])
    gt = mod_ref[0, :, 2 * d:3 * d]
    o_ref[0] = x_ref[0] + gt * y


def _attention(latent, l, x, modv, yfour, qd, kd, vd, qm, km, vm, cache, diff_lambda_l, g_subln_l, w_out_b):
    b, n, d = x.shape
    tq = min(256, n)
    bm = modv.shape[0]
    mod_idx = (lambda i, j: (i, 0, 0)) if bm > 1 else (lambda i, j: (0, 0, 0))
    const = lambda i, j: (0, 0)
    tok = lambda i, j: (i, j, 0)
    full = lambda i, j: (i, 0, 0)
    p = cache[0].shape[1] if latent else 0
    in_specs = [
        pl.BlockSpec((1, tq, d), tok),
        pl.BlockSpec((1, 1, modv.shape[2]), mod_idx),
        pl.BlockSpec((1, tq, D_FOURIER), tok),
        pl.BlockSpec((1, tq, D_DIFF), tok),
        pl.BlockSpec((1, n, D_DIFF), full),
        pl.BlockSpec((1, D_DIFF, n), full),
    ]
    args = [x, modv, yfour, qd, kd, vd]
    if latent:
        kdc, vdc, kmc, vmc = cache
        in_specs += [pl.BlockSpec((1, p, D_DIFF), full), pl.BlockSpec((1, D_DIFF, p), full)]
        args += [kdc, vdc]
    in_specs += [pl.BlockSpec((1, tq, D_MLA_P), tok), pl.BlockSpec((1, n, D_MLA_P), full), pl.BlockSpec((1, D_MLA, n), full)]
    args += [qm, km, vm]
    if latent:
        in_specs += [pl.BlockSpec((1, p, D_MLA_P), full), pl.BlockSpec((1, D_MLA, p), full)]
        args += [kmc, vmc]
    in_specs += [pl.BlockSpec((4, DIFF_DH), const), pl.BlockSpec((HEAD_W, 1), const), pl.BlockSpec(w_out_b.shape, const)]
    args += [diff_lambda_l, g_subln_l.reshape(HEAD_W, 1), w_out_b]
    lam_init = 0.8 - 0.6 * math.exp(-0.3 * l)
    return pl.pallas_call(
        functools.partial(_attn_kernel, latent, lam_init, d),
        name="attn_lat" if latent else "attn_ctx",
        grid=(b, n // tq),
        in_specs=in_specs,
        out_specs=pl.BlockSpec((1, tq, d), tok),
        out_shape=jax.ShapeDtypeStruct((b, n, d), F32),
        scratch_shapes=[pltpu.VMEM((tq, D_MIX), BF), pltpu.VMEM((SCORE_BUFS, n + p, tq), F32),
                        pltpu.VMEM((SCORE_BUFS, n + p, tq), BF)],
        compiler_params=_cparams(2),
    )(*args)


def _router_kernel(d, x_ref, mod_ref, g_ref, wr_ref, h_ref, aff_ref):
    sh = mod_ref[0, :, 3 * d:4 * d]
    sc = mod_ref[0, :, 4 * d:5 * d]
    tm = x_ref.shape[1]
    sub = min(tm, SUB_ROWS)
    for r0 in range(0, tm, sub):
        rows = slice(r0, r0 + sub)
        hb = (_rms(x_ref[0, rows, :], g_ref[...]) * (1.0 + sc) + sh).astype(BF)
        h_ref[0, rows, :] = hb
        logits = _dot_nt(wr_ref[...], hb)
        e = jnp.exp(logits - jnp.max(logits, axis=0, keepdims=True))
        aff_ref[0, :, rows] = e / jnp.sum(e, axis=0, keepdims=True)


def _router(x, modv, g_ffn_l, w_router_t):
    b, n, d = x.shape
    tm = min(PRE_ROWS, n)
    bm = modv.shape[0]
    mod_idx = (lambda i, j: (i, 0, 0)) if bm > 1 else (lambda i, j: (0, 0, 0))
    const = lambda i, j: (0, 0)
    return pl.pallas_call(
        functools.partial(_router_kernel, d),
        name="router_%d" % n,
        grid=(b, n // tm),
        in_specs=[
            pl.BlockSpec((1, tm, d), lambda i, j: (i, j, 0)),
            pl.BlockSpec((1, 1, modv.shape[2]), mod_idx),
            pl.BlockSpec((1, d), const),
            pl.BlockSpec(w_router_t.shape, const),
        ],
        out_specs=[pl.BlockSpec((1, tm, d), lambda i, j: (i, j, 0)), pl.BlockSpec((1, N_EXPERTS, tm), lambda i, j: (i, 0, j))],
        out_shape=[jax.ShapeDtypeStruct((b, n, d), BF), jax.ShapeDtypeStruct((b, N_EXPERTS, n), F32)],
        compiler_params=_cparams(2),
    )(x, modv, g_ffn_l.reshape(1, d), w_router_t)


def _select_kernel(cap, aff_ref, sel_ref):
    a = aff_ref[...]
    e, n = a.shape
    bits = pltpu.bitcast(a, jnp.int32)
    capf = float(cap)

    def body(_, carry):
        lo, hi = carry
        mid = lo + ((hi - lo) >> 1)
        cnt = jnp.sum(jnp.where(bits >= mid, 1.0, 0.0), axis=1, keepdims=True)
        up = cnt >= capf
        return jnp.where(up, mid, lo), jnp.where(up, hi, mid)

    lo0 = jnp.zeros((e, 1), jnp.int32)
    hi0 = jnp.full((e, 1), 0x7F800000, jnp.int32)
    thr, _ = lax.fori_loop(0, 31, body, (lo0, hi0))
    gt = bits > thr
    eq = bits == thr
    need = capf - jnp.sum(jnp.where(gt, 1.0, 0.0), axis=1, keepdims=True)
    both = jnp.concatenate([jnp.where(gt, 1.0, 0.0), jnp.where(eq, 1.0, 0.0)], axis=0).astype(BF)
    ck = min(256, n)
    tri = jnp.where(lax.broadcasted_iota(jnp.int32, (ck, ck), 0) < lax.broadcasted_iota(jnp.int32, (ck, ck), 1), 1.0, 0.0).astype(BF)
    off = jnp.zeros((2 * e, 1), F32)
    pieces = []
    for k in range(n // ck):
        blk = both[:, k * ck:(k + 1) * ck]
        pieces.append(_dot(blk, tri) + off)
        off = off + jnp.sum(blk.astype(F32), axis=1, keepdims=True)
    cum = jnp.concatenate(pieces, axis=1) if len(pieces) > 1 else pieces[0]
    cum_gt, cum_eq = cum[:e], cum[e:]
    chosen = gt | (eq & (cum_eq < need))
    pos = cum_gt + jnp.minimum(cum_eq, need)
    sel_ref[...] = jnp.where(chosen, pos, -1.0).astype(jnp.int32)


def _select(aff, cap):
    b, e, n = aff.shape
    rows = b * e
    rb = min(rows, SELECT_ROWS)
    sel = pl.pallas_call(
        functools.partial(_select_kernel, cap),
        name="select_%d" % n,
        grid=(rows // rb,),
        in_specs=[pl.BlockSpec((rb, n), lambda i: (i, 0))],
        out_specs=pl.BlockSpec((rb, n), lambda i: (i, 0)),
        out_shape=jax.ShapeDtypeStruct((rows, n), jnp.int32),
        compiler_params=_cparams(1),
    )(aff.reshape(rows, n))
    return sel.reshape(b, e, n)


def _gather_kernel(cap, sel_ref, aff_ref, h_ref, xs_ref, gate_ref):
    h = h_ref[0]
    n = h.shape[0]
    e_tot = sel_ref.shape[1]

    def one(e_idx):
        sel_e = sel_ref[0, pl.ds(e_idx, 1), :]
        aff_e = aff_ref[0, pl.ds(e_idx, 1), :]
        hit = sel_e == lax.broadcasted_iota(jnp.int32, (cap, n), 0)
        g = jnp.sum(jnp.where(hit, aff_e, 0.0), axis=1, keepdims=True)
        return jnp.where(hit, 1.0, 0.0).astype(BF), jnp.broadcast_to(g, (cap, HEAD_W))

    if cap >= 128:
        def body(e_idx, carry):
            p, g = one(e_idx)
            xs_ref[e_idx, 0] = _dot(p, h).astype(BF)
            gate_ref[e_idx, 0] = g
            return carry

        lax.fori_loop(0, e_tot, body, 0)
    else:
        ps, gs = zip(*[one(e_idx) for e_idx in range(e_tot)])
        xs = _dot(jnp.concatenate(ps, axis=0), h).astype(BF)
        for e_idx in range(e_tot):
            xs_ref[e_idx, 0] = xs[e_idx * cap:(e_idx + 1) * cap]
            gate_ref[e_idx, 0] = gs[e_idx]


def _gather(sel, aff, h2, cap):
    b, e, n = sel.shape
    d = h2.shape[2]
    return pl.pallas_call(
        functools.partial(_gather_kernel, cap),
        name="gather_%d" % n,
        grid=(b,),
        in_specs=[
            pl.BlockSpec((1, e, n), lambda i: (i, 0, 0)),
            pl.BlockSpec((1, e, n), lambda i: (i, 0, 0)),
            pl.BlockSpec((1, n, d), lambda i: (i, 0, 0)),
        ],
        out_specs=[pl.BlockSpec((e, 1, cap, d), lambda i: (0, i, 0, 0)), pl.BlockSpec((e, 1, cap, HEAD_W), lambda i: (0, i, 0, 0))],
        out_shape=[jax.ShapeDtypeStruct((e, b, cap, d), BF), jax.ShapeDtypeStruct((e, b, cap, HEAD_W), F32)],
        compiler_params=_cparams(1),
    )(sel, aff, h2)


def _expert_kernel(steps_a, xa_ref, ga_ref, xb_ref, gb_ref, wg_ref, wu_ref, wd_ref, ya_ref, yb_ref, wg_s, wu_s, wd_s):
    j = pl.program_id(1)

    @pl.when(j == 0)
    def _():
        wg_s[...] = wg_ref[0, 0].astype(BF)
        wu_s[...] = wu_ref[0, 0].astype(BF)
        wd_s[...] = wd_ref[0, 0].astype(BF)

    def ffn(x_ref, g_ref, y_ref):
        x = x_ref[0]
        a = _dot(x, wg_s[...])
        u = _dot(x, wu_s[...])
        mid = (a * (1.0 / (1.0 + jnp.exp(-a))) * u).astype(BF)
        y_ref[0] = (_dot(mid, wd_s[...]) * g_ref[0][:, 0:1]).astype(BF)

    @pl.when(j < steps_a)
    def _():
        ffn(xa_ref, ga_ref, ya_ref)

    @pl.when(j >= steps_a)
    def _():
        ffn(xb_ref, gb_ref, yb_ref)


def _experts(l, xs_a, gate_a, xs_b, gate_b, w_e_gate, w_e_up, w_e_down):
    e, rows_a, d = xs_a.shape
    rows_b = xs_b.shape[1]
    ff = w_e_gate.shape[3]
    rb = math.gcd(math.gcd(rows_a, rows_b), EXPERT_ROWS)
    steps_a, steps_b = rows_a // rb, rows_b // rb
    w_idx = lambda i, j: (l, i, 0, 0)
    idx_a = lambda i, j: (i, jnp.minimum(j, steps_a - 1), 0)
    idx_b = lambda i, j: (i, jnp.maximum(j - steps_a, 0), 0)
    return pl.pallas_call(
        functools.partial(_expert_kernel, steps_a),
        name="experts",
        grid=(e, steps_a + steps_b),
        in_specs=[
            pl.BlockSpec((1, rb, d), idx_a),
            pl.BlockSpec((1, rb, HEAD_W), idx_a),
            pl.BlockSpec((1, rb, d), idx_b),
            pl.BlockSpec((1, rb, HEAD_W), idx_b),
            pl.BlockSpec((1, 1, d, ff), w_idx),
            pl.BlockSpec((1, 1, d, ff), w_idx),
            pl.BlockSpec((1, 1, ff, d), w_idx),
        ],
        out_specs=[pl.BlockSpec((1, rb, d), idx_a), pl.BlockSpec((1, rb, d), idx_b)],
        out_shape=[jax.ShapeDtypeStruct((e, rows_a, d), BF), jax.ShapeDtypeStruct((e, rows_b, d), BF)],
        scratch_shapes=[pltpu.VMEM((d, ff), BF), pltpu.VMEM((d, ff), BF), pltpu.VMEM((ff, d), BF)],
        compiler_params=_cparams(2),
    )(xs_a, gate_a, xs_b, gate_b, w_e_gate, w_e_up, w_e_down)


def _scatter_kernel(cap, d, final, *refs):
    if final:
        x_ref, mod_ref, selt_ref, ye_ref, gf_ref, o_ref, pt_ref = refs
    else:
        x_ref, mod_ref, selt_ref, ye_ref, o_ref, pt_ref = refs
    e_tot = ye_ref.shape[0]
    tn = x_ref.shape[1]
    selt = selt_ref[0]
    if cap % 128 == 0:
        lane = lax.broadcasted_iota(jnp.int32, (tn, cap), 1)
        for e_idx in range(e_tot):
            pt_ref[:, e_idx * cap:(e_idx + 1) * cap] = jnp.where(selt[:, e_idx:e_idx + 1] == lane, 1.0, 0.0).astype(BF)
    else:
        lane = lax.broadcasted_iota(jnp.int32, (tn, e_tot * cap), 1)
        acc = jnp.zeros((tn, e_tot * cap), F32)
        for e_idx in range(e_tot):
            s = selt[:, e_idx:e_idx + 1]
            acc = acc + jnp.where((s >= 0) & (s + e_idx * cap == lane), 1.0, 0.0)
        pt_ref[...] = acc.astype(BF)
    y = _dot(pt_ref[...], ye_ref[...].reshape(e_tot * cap, d))
    gt = mod_ref[0, :, 5 * d:6 * d]
    out = x_ref[0] + gt * y
    if final:
        out = _rms(out, gf_ref[...])
    o_ref[0] = out


def _scatter(x, modv, sel_t, ye, cap, g_final):
    b, n, d = x.shape
    e = ye.shape[0]
    tn = min(PRE_ROWS, n)
    bm = modv.shape[0]
    mod_idx = (lambda i, j: (i, 0, 0)) if bm > 1 else (lambda i, j: (0, 0, 0))
    final = g_final is not None
    in_specs = [
        pl.BlockSpec((1, tn, d), lambda i, j: (i, j, 0)),
        pl.BlockSpec((1, 1, modv.shape[2]), mod_idx),
        pl.BlockSpec((1, tn, e), lambda i, j: (i, j, 0)),
        pl.BlockSpec((e, 1, cap, d), lambda i, j: (0, i, 0, 0)),
    ]
    args = [x, modv, sel_t, ye]
    if final:
        in_specs.append(pl.BlockSpec((1, d), lambda i, j: (0, 0)))
        args.append(g_final.reshape(1, d))
    return pl.pallas_call(
        functools.partial(_scatter_kernel, cap, d, final),
        name="scatter_%d" % n,
        grid=(b, n // tn),
        in_specs=in_specs,
        out_specs=pl.BlockSpec((1, tn, d), lambda i, j: (i, j, 0)),
        out_shape=jax.ShapeDtypeStruct((b, n, d), F32),
        scratch_shapes=[pltpu.VMEM((tn, e * cap), BF)],
        compiler_params=_cparams(2),
    )(*args)


def _dft_tables(n):
    j = np.arange(n, dtype=np.int64)
    ang = ((j[:, None] * j[None, :]) % n).astype(np.float64) * (2.0 * math.pi / n)
    return jnp.asarray(np.cos(ang), dtype=BF), jnp.asarray(np.sin(ang), dtype=BF)


def _block_diag(blocks):
    g, r, c = blocks.shape
    out = jnp.zeros((g * r, g * c), blocks.dtype)
    for i in range(g):
        out = out.at[i * r:(i + 1) * r, i * c:(i + 1) * c].set(blocks[i])
    return out


def _rope_tables(n):
    rows = n // GRID_W
    row = jnp.repeat(jnp.arange(rows, dtype=F32), GRID_W)
    col = jnp.tile(jnp.arange(GRID_W, dtype=F32), rows)

    def ang(dim):
        nf = dim // 4
        inv = ROPE_BASE ** (-jnp.arange(nf, dtype=F32) / nf)
        return jnp.concatenate([row[:, None] * inv, col[:, None] * inv], axis=-1)

    a = ang(DIFF_DH)
    cos_d = jnp.tile(jnp.cos(a), (1, 4))
    sin_d = jnp.tile(jnp.concatenate([-jnp.sin(a), jnp.sin(a)], axis=1), (1, 2))
    a = ang(MLA_ROPE)
    ones = jnp.ones((n, MLA_NOPE), F32)
    pad = HEAD_W - MLA_NOPE - MLA_ROPE
    cos_m = jnp.concatenate([ones, jnp.cos(a), jnp.cos(a), jnp.ones((n, pad), F32)], axis=1)
    sin_m = jnp.concatenate([0 * ones, -jnp.sin(a), jnp.sin(a), jnp.zeros((n, pad), F32)], axis=1)
    return cos_d, sin_d, cos_m, sin_m


def _pad_heads(w, heads, lo, hi):
    k = w.shape[0]
    w3 = w.reshape(k, heads, -1)[:, :, lo:hi]
    return jnp.pad(w3, ((0, 0), (0, 0), (0, HEAD_W - (hi - lo)))).reshape(k, heads * HEAD_W)


def kernel(x_prompt, x_sample, cache_diff_k, cache_diff_v, cache_mla_ckv, cache_mla_kpe, c, c_ctx, w_ada, b_ada, g_attn, g_ffn, w_in, w_four, diff_lambda, g_subln, g_mla_q, w_mla_uq, g_mla_kv, w_mla_ukv, w_out, w_router, w_e_gate, w_e_up, w_e_down, g_final):
    depth, d = g_attn.shape
    b_ctx, n_ctx, _ = x_prompt.shape
    b_lat, n_lat, _ = x_sample.shape

    r = b_lat + 1
    r_pad = -(-r // 8) * 8
    cond = jnp.concatenate([c, c_ctx[None, :], jnp.zeros((r_pad - r, d), F32)], axis=0)
    mod = _modulation(cond, w_ada, b_ada)

    n_main = w_in.shape[2] - MLA_ROPE
    w_in_p = jnp.concatenate(
        [w_in[:, :, :n_main], jnp.zeros((depth, d, MLA_NOPE), F32), w_in[:, :, n_main:],
         jnp.zeros((depth, d, HEAD_W - MLA_NOPE - MLA_ROPE), F32)], axis=2).astype(BF)
    w_uq_p = jnp.stack([_pad_heads(w_mla_uq[l], MLA_HEADS, 0, MLA_NOPE + MLA_ROPE) for l in range(depth)]).astype(BF)
    w_k_p = jnp.stack([_pad_heads(w_mla_ukv[l], MLA_HEADS, 0, MLA_NOPE) for l in range(depth)]).astype(BF)
    w_v_p = w_mla_ukv.reshape(depth, MLA_KV_RANK, MLA_HEADS, MLA_NOPE + MLA_V)[..., MLA_NOPE:].reshape(depth, MLA_KV_RANK, D_MLA).astype(BF)
    w_out_p = w_out.astype(BF)
    w_router_t = jnp.swapaxes(w_router, 1, 2).astype(BF)
    place = jnp.zeros((MLA_ROPE, HEAD_W), F32).at[jnp.arange(MLA_ROPE), MLA_NOPE + jnp.arange(MLA_ROPE)].set(1.0)
    place = jnp.tile(place, (1, MLA_HEADS)).astype(BF)

    jc = jnp.arange(F_CH, dtype=jnp.int32)
    ang_c = ((jc[:, None] * jc[None, :]) % F_CH).astype(F32) * (2.0 * math.pi / F_CH)
    cc_bd = _block_diag(jnp.broadcast_to(jnp.cos(ang_c), (F_GROUPS, F_CH, F_CH))).astype(BF)
    sc_bd = _block_diag(jnp.broadcast_to(jnp.sin(ang_c), (F_GROUPS, F_CH, F_CH))).astype(BF)
    dft = {n: _dft_tables(n) for n in {n_ctx, n_lat}}
    ropes = _rope_tables(n_lat)

    def pre(latent, l, x, prev=None, combine=None):
        modv = (mod[l, :b_lat] if latent else mod[l, b_lat:b_lat + 1]).reshape(-1, 1, N_MOD * d)
        outs = _pre(latent, x, modv, g_attn[l], w_in_p[l], g_mla_q[l], w_uq_p[l], g_mla_kv[l], w_k_p[l], w_v_p[l],
                    cc_bd, sc_bd, ropes, l, depth, prev, combine)
        if combine is not None:
            x, outs = outs[0], outs[1:]
        return x, modv, outs

    def mix_and_route(latent, l, x, modv, outs):
        b, n, _ = x.shape
        cap = max(1, EC_CAPACITY_FACTOR * n // N_EXPERTS)
        u, v, qd, kd, vd, qm, km, vm = outs[:8]
        cache = _cache_prep(l, cache_diff_k, cache_diff_v, cache_mla_ckv, cache_mla_kpe, w_k_p[l], w_v_p[l], place) if latent else None
        cn, sn = dft[n]
        yfour = _fourier(u, v, cn, sn, _block_diag(w_four[l]).astype(BF))
        x = _attention(latent, l, x, modv, yfour, qd, kd, vd, qm, km, vm, cache, diff_lambda[l], g_subln[l], w_out_p[l])
        h2, aff = _router(x, modv, g_ffn[l], w_router_t[l])
        sel = _select(aff, cap)
        xs, gate = _gather(sel, aff, h2, cap)
        e = xs.shape[0]
        return x, jnp.swapaxes(sel, 1, 2), cap, xs.reshape(e, b * cap, d), gate.reshape(e, b * cap, HEAD_W)

    x_c, mod_c, pre_c = pre(False, 0, x_prompt)
    x_l, mod_l, pre_l = pre(True, 0, x_sample)
    for l in range(depth):
        new = tuple(pre_c[8:])
        x_c, selt_c, cap_c, xs_c, gate_c = mix_and_route(False, l, x_c, mod_c, pre_c)
        x_l, selt_l, cap_l, xs_l, gate_l = mix_and_route(True, l, x_l, mod_l, pre_l)
        ye_c, ye_l = _experts(l, xs_c, gate_c, xs_l, gate_l, w_e_gate, w_e_up, w_e_down)
        ye_c = ye_c.reshape(ye_c.shape[0], b_ctx, cap_c, d)
        ye_l = ye_l.reshape(ye_l.shape[0], b_lat, cap_l, d)
        if l + 1 < depth:
            x_c, mod_c, pre_c = pre(False, l + 1, x_c, new, (mod_c, selt_c, ye_c, cap_c))
            x_l, mod_l, pre_l = pre(True, l + 1, x_l, None, (mod_l, selt_l, ye_l, cap_l))
        else:
            x_c = _scatter(x_c, mod_c, selt_c, ye_c, cap_c, g_final)
            x_l = _scatter(x_l, mod_l, selt_l, ye_l, cap_l, g_final)
    y_prompt, y_sample = x_c, x_l
    new_diff_k, new_diff_v, new_mla_ckv, new_mla_kpe = new
    return (y_prompt, y_sample, new_diff_k, new_diff_v, new_mla_ckv, new_mla_kpe)
```

```python
import functools
import math

import jax
import jax.numpy as jnp
import numpy as np
from jax import lax
from jax.experimental import pallas as pl
from jax.experimental.pallas import tpu as pltpu
from jax.experimental.pallas import tpu_sc as plsc

BF = jnp.bfloat16
F32 = jnp.float32

GRID_W = 64
ROPE_BASE = 10000.0
NORM_EPS = 1e-6
F_GROUPS, F_CH = 4, 64
D_FOURIER = F_GROUPS * F_CH
DIFF_HEADS, DIFF_DH = 4, 64
D_DIFF = DIFF_HEADS * 2 * DIFF_DH
MLA_HEADS, MLA_Q_RANK, MLA_KV_RANK = 4, 256, 128
MLA_NOPE, MLA_ROPE, MLA_V = 64, 32, 64
HEAD_W = 128
D_MLA_P = MLA_HEADS * HEAD_W
D_MLA = MLA_HEADS * MLA_V
D_MIX = D_FOURIER + D_DIFF + D_MLA
KEY_CHUNK = 256
SCORE_BUFS = 2
BF_ROWS = 16
SELECT_ROWS = 128
EXPERT_ROWS = 512
PRE_ROWS = 512
SC_CHUNK = 64
SUB_ROWS = 256
N_EXPERTS = 16
EC_CAPACITY_FACTOR = 2
N_MOD = 6
IN_COLS_P = D_FOURIER + 3 * D_DIFF + MLA_Q_RANK + MLA_KV_RANK + HEAD_W
LOG2E = 1.4426950408889634
VMEM_LIMIT = 56 * 1024 * 1024


def _cparams(n_axes, vmem=VMEM_LIMIT):
    return pltpu.CompilerParams(dimension_semantics=("arbitrary",) * n_axes, vmem_limit_bytes=vmem)


def _dot(a, b):
    return jnp.dot(a, b, preferred_element_type=F32)


def _dot_nt(a, b):
    return lax.dot_general(a, b, (((1,), (1,)), ((), ())), preferred_element_type=F32)


def _rms(x, g):
    return x * lax.rsqrt(jnp.mean(x * x, axis=-1, keepdims=True) + NORM_EPS) * g


def _pack_halves(hb):
    w = hb.shape[1] // 2
    bits = pltpu.bitcast(hb.astype(F32), jnp.uint32)
    packed = (bits[:, :w] >> 16) | (bits[:, w:] & jnp.uint32(0xFFFF0000))
    return pltpu.bitcast(packed, jnp.int32)


def _unpack_halves(xi):
    bits = pltpu.bitcast(xi, jnp.uint32)
    lo = pltpu.bitcast(bits << 16, F32).astype(BF)
    hi = pltpu.bitcast(bits & jnp.uint32(0xFFFF0000), F32).astype(BF)
    return jnp.concatenate([lo, hi], axis=1)


def _rope(z, cos, sin_signed, half, group, lo):
    w = z.shape[1]
    reps = w // cos.shape[1]
    cos_w = jnp.concatenate([cos] * reps, axis=1) if reps > 1 else cos
    sin_w = jnp.concatenate([sin_signed] * reps, axis=1) if reps > 1 else sin_signed
    from_right = pltpu.roll(z, w - half, 1)
    from_left = pltpu.roll(z, half, 1)
    lane = lax.broadcasted_iota(jnp.int32, z.shape, 1) % group
    first = (lane >= lo) & (lane < lo + half)
    partner = jnp.where(first, from_right, from_left)
    return z * cos_w + partner * sin_w


def _mod_kernel(c_ref, w_ref, b_ref, o_ref):
    c = c_ref[...]
    a = (c * (1.0 / (1.0 + jnp.exp(-c)))).astype(BF)
    o_ref[0] = _dot(a, w_ref[0].astype(BF)) + b_ref[0]


def _modulation(cond, w_ada, b_ada):
    depth, d, n6 = w_ada.shape
    r = cond.shape[0]
    tn = 1536
    return pl.pallas_call(
        _mod_kernel,
        name="modulation",
        grid=(depth, n6 // tn),
        in_specs=[
            pl.BlockSpec((r, d), lambda l, j: (0, 0)),
            pl.BlockSpec((1, d, tn), lambda l, j: (l, 0, j)),
            pl.BlockSpec((1, 1, tn), lambda l, j: (l, 0, j)),
        ],
        out_specs=pl.BlockSpec((1, r, tn), lambda l, j: (l, 0, j)),
        out_shape=jax.ShapeDtypeStruct((depth, r, n6), F32),
        compiler_params=_cparams(2),
    )(cond, w_ada, b_ada.reshape(depth, 1, n6))


def _pre_kernel(latent, d, *refs):
    if latent:
        (x_ref, mod_ref, g_ref, win_ref, gq_ref, wuq_ref, gkv_ref, wk_ref, wv_ref, cc_ref, sc_ref,
         cd_ref, sd_ref, cm_ref, sm_ref,
         u_ref, v_ref, qd_ref, kd_ref, vd_ref, qm_ref, km_ref, vm_ref) = refs
    else:
        (x_ref, mod_ref, g_ref, win_ref, gq_ref, wuq_ref, gkv_ref, wk_ref, wv_ref, cc_ref, sc_ref,
         u_ref, v_ref, qd_ref, kd_ref, vd_ref, qm_ref, km_ref, vm_ref,
         k32_ref, v32_ref, ckv32_ref, kpe32_ref) = refs[:11] + refs[-12:]
    sh = mod_ref[0, :, 0:d]
    sc = mod_ref[0, :, d:2 * d]
    tm = x_ref.shape[1]
    sub = min(tm, SUB_ROWS)
    def project(r0):
        h = _rms(x_ref[0, r0:r0 + sub, :], g_ref[...]) * (1.0 + sc) + sh
        return _dot(h.astype(BF), win_ref[...])

    starts = list(range(0, tm, sub))
    z_next = project(starts[0])
    for i, r0 in enumerate(starts):
        rows = slice(r0, r0 + sub)
        z = z_next
        z_next = project(starts[i + 1]) if i + 1 < len(starts) else None
        o = 0
        zf = z[:, o:o + D_FOURIER]; o += D_FOURIER
        zq = z[:, o:o + D_DIFF]; o += D_DIFF
        zk = z[:, o:o + D_DIFF]; o += D_DIFF
        zv = z[:, o:o + D_DIFF]; o += D_DIFF
        zcq = z[:, o:o + MLA_Q_RANK]; o += MLA_Q_RANK
        zckv = z[:, o:o + MLA_KV_RANK]; o += MLA_KV_RANK
        kpe = z[:, o:o + HEAD_W]

        zf_b = zf.astype(BF)
        u_ref[0, rows, :] = _dot(zf_b, cc_ref[...]).astype(BF)
        v_ref[0, rows, :] = _dot(zf_b, sc_ref[...]).astype(BF)
        vd_ref[0, :, rows] = zv.T.astype(BF)
        cq = _rms(zcq, gq_ref[...])
        qm = _dot(cq.astype(BF), wuq_ref[...])
        ckv = _rms(zckv, gkv_ref[...])
        ckv_b = ckv.astype(BF)
        k_nope = _dot(ckv_b, wk_ref[...])
        vm_ref[0, :, rows] = _dot(ckv_b, wv_ref[...]).T.astype(BF)
        if latent:
            cd, sd, cm, sm = cd_ref[rows, :], sd_ref[rows, :], cm_ref[rows, :], sm_ref[rows, :]
            zq_r = _rope(zq, cd, sd, DIFF_DH // 2, DIFF_DH, 0)
            zk_r = _rope(zk, cd, sd, DIFF_DH // 2, DIFF_DH, 0)
            qm = _rope(qm, cm, sm, MLA_ROPE // 2, HEAD_W, MLA_NOPE)
            kpe_r = _rope(kpe, cm, sm, MLA_ROPE // 2, HEAD_W, MLA_NOPE)
        else:
            zq_r, zk_r, kpe_r = zq, zk, kpe
            for hd in range(DIFF_HEADS):
                k32_ref[0, 0, rows, hd, :] = zk[:, hd * HEAD_W:(hd + 1) * HEAD_W]
                v32_ref[0, 0, rows, hd, :] = zv[:, hd * HEAD_W:(hd + 1) * HEAD_W]
            ckv32_ref[0, 0, rows, :] = ckv
            kpe32_ref[0, 0, rows, :] = kpe[:, MLA_NOPE:MLA_NOPE + MLA_ROPE]
        qd_ref[0, rows, :] = (zq_r * (DIFF_DH ** -0.5 * LOG2E)).astype(BF)
        kd_ref[0, rows, :] = zk_r.astype(BF)
        qm_ref[0, rows, :] = (qm * ((MLA_NOPE + MLA_ROPE) ** -0.5 * LOG2E)).astype(BF)
        km_ref[0, rows, :] = (k_nope + jnp.concatenate([kpe_r] * MLA_HEADS, axis=1)).astype(BF)


def _combine_pre_kernel(latent, d, cap, n_pre_in, *refs):
    x_ref, modp_ref, selt_ref, ye_ref = refs[:4]
    pre_in = refs[4:4 + n_pre_in]
    xo_ref = refs[4 + n_pre_in]
    pre_out = refs[5 + n_pre_in:-1]
    _scatter_kernel(cap, d, False, x_ref, modp_ref, selt_ref, ye_ref, xo_ref, refs[-1])
    _pre_kernel(latent, d, xo_ref, *pre_in, *pre_out)


def _pre(latent, x, modv, g_attn, w_in_p, g_q, w_uq_p, g_kv, w_k_p, w_v_p, cc_bd, sc_bd, ropes, l=0, depth=1, prev=None,
         combine=None):
    b, n, d = x.shape
    tm = min(PRE_ROWS, n)
    bm = modv.shape[0]
    mod_idx = (lambda i, j: (i, 0, 0)) if bm > 1 else (lambda i, j: (0, 0, 0))
    const = lambda i, j: (0, 0)
    tok = lambda i, j: (i, j, 0)
    in_specs = [
        pl.BlockSpec((1, tm, d), tok),
        pl.BlockSpec((1, 1, modv.shape[2]), mod_idx),
        pl.BlockSpec((1, d), const),
        pl.BlockSpec(w_in_p.shape, const),
        pl.BlockSpec((1, MLA_Q_RANK), const),
        pl.BlockSpec(w_uq_p.shape, const),
        pl.BlockSpec((1, MLA_KV_RANK), const),
        pl.BlockSpec(w_k_p.shape, const),
        pl.BlockSpec(w_v_p.shape, const),
        pl.BlockSpec(cc_bd.shape, const),
        pl.BlockSpec(sc_bd.shape, const),
    ]
    args = [x, modv, g_attn.reshape(1, d), w_in_p, g_q.reshape(1, -1), w_uq_p, g_kv.reshape(1, -1), w_k_p, w_v_p, cc_bd, sc_bd]
    if latent:
        in_specs += [pl.BlockSpec((tm, HEAD_W), lambda i, j: (j, 0))] * 4
        args += list(ropes)
    feat = lambda i, j: (i, 0, j)
    out_shape, out_specs = [], []
    for w, token_major in ((D_FOURIER, True), (D_FOURIER, True), (D_DIFF, True), (D_DIFF, True), (D_DIFF, False),
                           (D_MLA_P, True), (D_MLA_P, True), (D_MLA, False)):
        out_shape.append(jax.ShapeDtypeStruct((b, n, w) if token_major else (b, w, n), BF))
        out_specs.append(pl.BlockSpec((1, tm, w), tok) if token_major else pl.BlockSpec((1, w, tm), feat))
    aliases = {}
    if not latent:
        for tail in ((DIFF_HEADS, HEAD_W), (DIFF_HEADS, HEAD_W), (MLA_KV_RANK,), (MLA_ROPE,)):
            out_shape.append(jax.ShapeDtypeStruct((b, depth, n) + tail, F32))
            out_specs.append(pl.BlockSpec((1, 1, tm) + tail, lambda i, j, t=len(tail): (i, l, j) + (0,) * t))
        if prev is not None:
            for k, buf in enumerate(prev):
                aliases[len(args)] = len(out_shape) - len(prev) + k
                in_specs.append(pl.BlockSpec(memory_space=pl.ANY))
                args.append(buf)
    body, scratch, tag = functools.partial(_pre_kernel, latent, d), [], "pre"
    if combine is not None:
        modp, sel_t, ye, cap = combine
        e = ye.shape[0]
        modp_idx = (lambda i, j: (i, 0, 0)) if modp.shape[0] > 1 else (lambda i, j: (0, 0, 0))
        extra_specs = [pl.BlockSpec((1, 1, modp.shape[2]), modp_idx), pl.BlockSpec((1, tm, e), tok),
                       pl.BlockSpec((e, 1, cap, d), lambda i, j: (0, i, 0, 0))]
        n_extra = len(extra_specs)
        body = functools.partial(_combine_pre_kernel, latent, d, cap, len(args) - 1)
        in_specs = in_specs[:1] + extra_specs + in_specs[1:]
        args = args[:1] + [modp, sel_t, ye] + args[1:]
        out_specs = [pl.BlockSpec((1, tm, d), tok)] + out_specs
        out_shape = [jax.ShapeDtypeStruct((b, n, d), F32)] + out_shape
        aliases = {k + n_extra: v + 1 for k, v in aliases.items()}
        scratch, tag = [pltpu.VMEM((tm, e * cap), BF)], "combine_pre"
    return pl.pallas_call(
        body,
        name=tag + ("_lat" if latent else "_ctx"),
        grid=(b, n // tm),
        in_specs=in_specs,
        out_specs=out_specs,
        out_shape=out_shape,
        input_output_aliases=aliases,
        scratch_shapes=scratch,
        compiler_params=_cparams(2),
    )(*args)


def _cache_kernel(dk_ref, dv_ref, ckv_ref, kpe_ref, wk_ref, wv_ref, place_ref, kd_ref, vd_ref, km_ref, vm_ref):
    for hd in range(DIFF_HEADS):
        sl = slice(hd * HEAD_W, (hd + 1) * HEAD_W)
        kd_ref[0, :, sl] = dk_ref[0, 0, :, hd, :].astype(BF)
        vd_ref[0, sl, :] = dv_ref[0, 0, :, hd, :].T.astype(BF)
    ckv_b = ckv_ref[0, 0].astype(BF)
    kpe_w = _dot(kpe_ref[0, 0].astype(BF), place_ref[...])
    km_ref[0] = (_dot(ckv_b, wk_ref[...]) + kpe_w).astype(BF)
    vm_ref[0] = _dot(ckv_b, wv_ref[...]).T.astype(BF)


def _cache_prep(l, cache_dk, cache_dv, cache_ckv, cache_kpe, w_k_p, w_v_p, place):
    b, _, p = cache_dk.shape[:3]
    at_l = lambda i: (i, l, 0, 0)
    at_l5 = lambda i: (i, l, 0, 0, 0)
    const = lambda i: (0, 0)
    shapes = [(p, D_DIFF), (D_DIFF, p), (p, D_MLA_P), (D_MLA, p)]
    return pl.pallas_call(
        _cache_kernel,
        name="cache_prep",
        grid=(b,),
        in_specs=[
            pl.BlockSpec((1, 1, p, DIFF_HEADS, HEAD_W), at_l5),
            pl.BlockSpec((1, 1, p, DIFF_HEADS, HEAD_W), at_l5),
            pl.BlockSpec((1, 1, p, MLA_KV_RANK), at_l),
            pl.BlockSpec((1, 1, p, MLA_ROPE), at_l),
            pl.BlockSpec(w_k_p.shape, const),
            pl.BlockSpec(w_v_p.shape, const),
            pl.BlockSpec(place.shape, const),
        ],
        out_specs=[pl.BlockSpec((1,) + s, lambda i: (i, 0, 0)) for s in shapes],
        out_shape=[jax.ShapeDtypeStruct((b,) + s, BF) for s in shapes],
        compiler_params=_cparams(1),
    )(cache_dk, cache_dv, cache_ckv, cache_kpe, w_k_p, w_v_p, place)


def _fourier_kernel(scale, u_ref, v_ref, cn_ref, sn_ref, w_ref, o_ref):
    f = (_dot(cn_ref[...], u_ref[0]) - _dot(sn_ref[...], v_ref[0])) * scale
    o_ref[0] = _dot(f.astype(BF), w_ref[...]).astype(BF)


def _fourier(u, v, cn, sn, w_bd):
    b, n, w = u.shape
    tm = min(512, n)
    return pl.pallas_call(
        functools.partial(_fourier_kernel, 1.0 / math.sqrt(n * F_CH)),
        name="fourier_%d" % n,
        grid=(n // tm, b),
        in_specs=[
            pl.BlockSpec((1, n, w), lambda i, j: (j, 0, 0)),
            pl.BlockSpec((1, n, w), lambda i, j: (j, 0, 0)),
            pl.BlockSpec((tm, n), lambda i, j: (i, 0)),
            pl.BlockSpec((tm, n), lambda i, j: (i, 0)),
            pl.BlockSpec((w, w), lambda i, j: (0, 0)),
        ],
        out_specs=pl.BlockSpec((1, tm, w), lambda i, j: (j, i, 0)),
        out_shape=jax.ShapeDtypeStruct((b, n, w), BF),
        compiler_params=_cparams(2),
    )(u, v, cn, sn, w_bd)


def _scores(q, k_parts, s_ref):
    off, m, bounds = 0, None, []
    for k in k_parts:
        s = _dot_nt(k(), q)
        n_i = s.shape[0]
        s_ref[off:off + n_i, :] = s
        mi = jnp.max(s, axis=0, keepdims=True)
        m = mi if m is None else jnp.maximum(m, mi)
        bounds.append((off, n_i))
        off += n_i
    return m, bounds


def _probs(m, total, s_ref, p_ref):
    for r in range(0, total, KEY_CHUNK):
        rows = min(KEY_CHUNK, total - r)
        p_ref[r:r + rows, :] = jnp.exp2(s_ref[r:r + rows, :] - m).astype(BF)


def _weighted(vt_parts, bounds, p_ref):
    o = None
    for vt, (r, n_i) in zip(vt_parts, bounds):
        v = vt()
        ones = jnp.where(lax.broadcasted_iota(jnp.int32, (BF_ROWS, n_i), 0) == 0, 1.0, 0.0).astype(BF)
        oi = _dot(jnp.concatenate([v, ones], axis=0), p_ref[r:r + n_i, :])
        o = oi if o is None else o + oi
    dv = o.shape[0] - BF_ROWS
    return o[:dv] * (1.0 / o[dv:dv + 1])


def _attn_kernel(latent, lam_init, d, *refs):
    if latent:
        (x_ref, mod_ref, yf_ref, qd_ref, kd_ref, vd_ref, kdc_ref, vdc_ref, qm_ref, km_ref, vm_ref, kmc_ref, vmc_ref,
         lam_ref, gs_ref, wo_ref, o_ref, mix_ref, s_ref, p_ref) = refs
    else:
        (x_ref, mod_ref, yf_ref, qd_ref, kd_ref, vd_ref, qm_ref, km_ref, vm_ref,
         lam_ref, gs_ref, wo_ref, o_ref, mix_ref, s_ref, p_ref) = refs
        kdc_ref = vdc_ref = kmc_ref = vmc_ref = None
    lf = lam_ref[...]
    lam = (jnp.exp(jnp.sum(lf[0:1] * lf[1:2], axis=-1, keepdims=True))
           - jnp.exp(jnp.sum(lf[2:3] * lf[3:4], axis=-1, keepdims=True)) + lam_init)
    mix_ref[:, 0:D_FOURIER] = yf_ref[0]
    gs = gs_ref[...]

    def keys(ref, cache_ref, sl):
        parts = [lambda: ref[0, :, sl]]
        if latent:
            parts.append(lambda: cache_ref[0, :, sl])
        return parts

    def values_t(ref, cache_ref, sl):
        parts = [lambda: ref[0, sl, :]]
        if latent:
            parts.append(lambda: cache_ref[0, sl, :])
        return parts

    def diff_query(sl, comp):
        def load():
            qh = qd_ref[0, :, sl]
            lane = lax.broadcasted_iota(jnp.int32, qh.shape, 1)
            keep = (lane >= DIFF_DH) if comp else (lane < DIFF_DH)
            return jnp.where(keep, qh, jnp.zeros_like(qh))
        return load

    items = []
    for hd in range(DIFF_HEADS):
        sl = slice(hd * HEAD_W, (hd + 1) * HEAD_W)
        for comp in range(2):
            items.append((diff_query(sl, comp), keys(kd_ref, kdc_ref, sl), values_t(vd_ref, vdc_ref, sl)))
    for hd in range(MLA_HEADS):
        sl = slice(hd * HEAD_W, (hd + 1) * HEAD_W)
        vsl = slice(hd * MLA_V, (hd + 1) * MLA_V)
        items.append(((lambda sl=sl: qm_ref[0, :, sl]), keys(km_ref, kmc_ref, sl), values_t(vm_ref, vmc_ref, vsl)))

    def start(i):
        return _scores(items[i][0](), items[i][1], s_ref.at[i % SCORE_BUFS])

    total = s_ref.shape[1]
    outs = []
    state = start(0)
    for i in range(len(items)):
        nxt = start(i + 1) if i + 1 < len(items) else None
        m, bounds = state
        buf = i % SCORE_BUFS
        _probs(m, total, s_ref.at[buf], p_ref.at[buf])
        outs.append(_weighted(items[i][2], bounds, p_ref.at[buf]))
        state = nxt

    for hd in range(DIFF_HEADS):
        o = outs[2 * hd] - lam * outs[2 * hd + 1]
        o = o * lax.rsqrt(jnp.mean(o * o, axis=0, keepdims=True) + NORM_EPS) * gs * (1.0 - lam_init)
        mix_ref[:, D_FOURIER + hd * HEAD_W:D_FOURIER + (hd + 1) * HEAD_W] = o.T.astype(BF)
    mix_ref[:, D_FOURIER + D_DIFF:D_MIX] = jnp.concatenate(outs[2 * DIFF_HEADS:], axis=0).T.astype(BF)
    y = _dot(mix_ref[...], wo_ref[...])
    gt = mod_ref[0, :, 2 * d:3 * d]
    o_ref[0] = x_ref[0] + gt * y


def _attention(latent, l, x, modv, yfour, qd, kd, vd, qm, km, vm, cache, diff_lambda_l, g_subln_l, w_out_b):
    b, n, d = x.shape
    tq = min(256, n)
    bm = modv.shape[0]
    mod_idx = (lambda i, j: (i, 0, 0)) if bm > 1 else (lambda i, j: (0, 0, 0))
    const = lambda i, j: (0, 0)
    tok = lambda i, j: (i, j, 0)
    full = lambda i, j: (i, 0, 0)
    p = cache[0].shape[1] if latent else 0
    in_specs = [
        pl.BlockSpec((1, tq, d), tok),
        pl.BlockSpec((1, 1, modv.shape[2]), mod_idx),
        pl.BlockSpec((1, tq, D_FOURIER), tok),
        pl.BlockSpec((1, tq, D_DIFF), tok),
        pl.BlockSpec((1, n, D_DIFF), full),
        pl.BlockSpec((1, D_DIFF, n), full),
    ]
    args = [x, modv, yfour, qd, kd, vd]
    if latent:
        kdc, vdc, kmc, vmc = cache
        in_specs += [pl.BlockSpec((1, p, D_DIFF), full), pl.BlockSpec((1, D_DIFF, p), full)]
        args += [kdc, vdc]
    in_specs += [pl.BlockSpec((1, tq, D_MLA_P), tok), pl.BlockSpec((1, n, D_MLA_P), full), pl.BlockSpec((1, D_MLA, n), full)]
    args += [qm, km, vm]
    if latent:
        in_specs += [pl.BlockSpec((1, p, D_MLA_P), full), pl.BlockSpec((1, D_MLA, p), full)]
        args += [kmc, vmc]
    in_specs += [pl.BlockSpec((4, DIFF_DH), const), pl.BlockSpec((HEAD_W, 1), const), pl.BlockSpec(w_out_b.shape, const)]
    args += [diff_lambda_l, g_subln_l.reshape(HEAD_W, 1), w_out_b]
    lam_init = 0.8 - 0.6 * math.exp(-0.3 * l)
    return pl.pallas_call(
        functools.partial(_attn_kernel, latent, lam_init, d),
        name="attn_lat" if latent else "attn_ctx",
        grid=(b, n // tq),
        in_specs=in_specs,
        out_specs=pl.BlockSpec((1, tq, d), tok),
        out_shape=jax.ShapeDtypeStruct((b, n, d), F32),
        scratch_shapes=[pltpu.VMEM((tq, D_MIX), BF), pltpu.VMEM((SCORE_BUFS, n + p, tq), F32),
                        pltpu.VMEM((SCORE_BUFS, n + p, tq), BF)],
        compiler_params=_cparams(2),
    )(*args)


def _router_kernel(d, packed, x_ref, mod_ref, g_ref, wr_ref, h_ref, aff_ref):
    sh = mod_ref[0, :, 3 * d:4 * d]
    sc = mod_ref[0, :, 4 * d:5 * d]
    tm = x_ref.shape[1]
    sub = min(tm, SUB_ROWS)
    for r0 in range(0, tm, sub):
        rows = slice(r0, r0 + sub)
        hb = (_rms(x_ref[0, rows, :], g_ref[...]) * (1.0 + sc) + sh).astype(BF)
        h_ref[0, rows, :] = _pack_halves(hb) if packed else hb
        logits = _dot_nt(wr_ref[...], hb)
        e = jnp.exp(logits - jnp.max(logits, axis=0, keepdims=True))
        aff_ref[0, :, rows] = e / jnp.sum(e, axis=0, keepdims=True)


def _router(x, modv, g_ffn_l, w_router_t, packed):
    b, n, d = x.shape
    hw, hdt = (d // 2, jnp.int32) if packed else (d, BF)
    tm = min(PRE_ROWS, n)
    bm = modv.shape[0]
    mod_idx = (lambda i, j: (i, 0, 0)) if bm > 1 else (lambda i, j: (0, 0, 0))
    const = lambda i, j: (0, 0)
    return pl.pallas_call(
        functools.partial(_router_kernel, d, packed),
        name="router_%d" % n,
        grid=(b, n // tm),
        in_specs=[
            pl.BlockSpec((1, tm, d), lambda i, j: (i, j, 0)),
            pl.BlockSpec((1, 1, modv.shape[2]), mod_idx),
            pl.BlockSpec((1, d), const),
            pl.BlockSpec(w_router_t.shape, const),
        ],
        out_specs=[pl.BlockSpec((1, tm, hw), lambda i, j: (i, j, 0)), pl.BlockSpec((1, N_EXPERTS, tm), lambda i, j: (i, 0, j))],
        out_shape=[jax.ShapeDtypeStruct((b, n, hw), hdt), jax.ShapeDtypeStruct((b, N_EXPERTS, n), F32)],
        compiler_params=_cparams(2),
    )(x, modv, g_ffn_l.reshape(1, d), w_router_t)


def _select_kernel(cap, aff_ref, sel_ref):
    a = aff_ref[...]
    e, n = a.shape
    bits = pltpu.bitcast(a, jnp.int32)
    capf = float(cap)

    def body(_, carry):
        lo, hi = carry
        mid = lo + ((hi - lo) >> 1)
        cnt = jnp.sum(jnp.where(bits >= mid, 1.0, 0.0), axis=1, keepdims=True)
        up = cnt >= capf
        return jnp.where(up, mid, lo), jnp.where(up, hi, mid)

    lo0 = jnp.zeros((e, 1), jnp.int32)
    hi0 = jnp.full((e, 1), 0x7F800000, jnp.int32)
    thr, _ = lax.fori_loop(0, 31, body, (lo0, hi0))
    gt = bits > thr
    eq = bits == thr
    need = capf - jnp.sum(jnp.where(gt, 1.0, 0.0), axis=1, keepdims=True)
    both = jnp.concatenate([jnp.where(gt, 1.0, 0.0), jnp.where(eq, 1.0, 0.0)], axis=0).astype(BF)
    ck = min(256, n)
    tri = jnp.where(lax.broadcasted_iota(jnp.int32, (ck, ck), 0) < lax.broadcasted_iota(jnp.int32, (ck, ck), 1), 1.0, 0.0).astype(BF)
    off = jnp.zeros((2 * e, 1), F32)
    pieces = []
    for k in range(n // ck):
        blk = both[:, k * ck:(k + 1) * ck]
        pieces.append(_dot(blk, tri) + off)
        off = off + jnp.sum(blk.astype(F32), axis=1, keepdims=True)
    cum = jnp.concatenate(pieces, axis=1) if len(pieces) > 1 else pieces[0]
    cum_gt, cum_eq = cum[:e], cum[e:]
    chosen = gt | (eq & (cum_eq < need))
    pos = cum_gt + jnp.minimum(cum_eq, need)
    sel_ref[...] = jnp.where(chosen, pos, -1.0).astype(jnp.int32)


def _select(aff, cap):
    b, e, n = aff.shape
    rows = b * e
    rb = min(rows, SELECT_ROWS)
    sel = pl.pallas_call(
        functools.partial(_select_kernel, cap),
        name="select_%d" % n,
        grid=(rows // rb,),
        in_specs=[pl.BlockSpec((rb, n), lambda i: (i, 0))],
        out_specs=pl.BlockSpec((rb, n), lambda i: (i, 0)),
        out_shape=jax.ShapeDtypeStruct((rows, n), jnp.int32),
        compiler_params=_cparams(1),
    )(aff.reshape(rows, n))
    return sel.reshape(b, e, n)


def _gather_kernel(cap, sel_ref, aff_ref, h_ref, xs_ref, gate_ref):
    h = h_ref[0]
    n = h.shape[0]
    e_tot = sel_ref.shape[1]

    def one(e_idx):
        sel_e = sel_ref[0, pl.ds(e_idx, 1), :]
        aff_e = aff_ref[0, pl.ds(e_idx, 1), :]
        hit = sel_e == lax.broadcasted_iota(jnp.int32, (cap, n), 0)
        g = jnp.sum(jnp.where(hit, aff_e, 0.0), axis=1, keepdims=True)
        return jnp.where(hit, 1.0, 0.0).astype(BF), jnp.broadcast_to(g, (cap, HEAD_W))

    if cap >= 128:
        def body(e_idx, carry):
            p, g = one(e_idx)
            xs_ref[e_idx, 0] = _dot(p, h).astype(BF)
            gate_ref[e_idx, 0] = g
            return carry

        lax.fori_loop(0, e_tot, body, 0)
    else:
        ps, gs = zip(*[one(e_idx) for e_idx in range(e_tot)])
        xs = _dot(jnp.concatenate(ps, axis=0), h).astype(BF)
        for e_idx in range(e_tot):
            xs_ref[e_idx, 0] = xs[e_idx * cap:(e_idx + 1) * cap]
            gate_ref[e_idx, 0] = gs[e_idx]


def _gather(sel, aff, h2, cap):
    b, e, n = sel.shape
    d = h2.shape[2]
    return pl.pallas_call(
        functools.partial(_gather_kernel, cap),
        name="gather_%d" % n,
        grid=(b,),
        in_specs=[
            pl.BlockSpec((1, e, n), lambda i: (i, 0, 0)),
            pl.BlockSpec((1, e, n), lambda i: (i, 0, 0)),
            pl.BlockSpec((1, n, d), lambda i: (i, 0, 0)),
        ],
        out_specs=[pl.BlockSpec((e, 1, cap, d), lambda i: (0, i, 0, 0)), pl.BlockSpec((e, 1, cap, HEAD_W), lambda i: (0, i, 0, 0))],
        out_shape=[jax.ShapeDtypeStruct((e, b, cap, d), BF), jax.ShapeDtypeStruct((e, b, cap, HEAD_W), F32)],
        compiler_params=_cparams(1),
    )(sel, aff, h2)


def _slot_kernel(cap, sel_ref, aff_ref, gate_ref, row_ref):
    n = sel_ref.shape[2]
    base = pl.program_id(0) * n
    tok = lax.broadcasted_iota(jnp.int32, (cap, n), 1).astype(F32)

    def body(e_idx, carry):
        hit = sel_ref[0, pl.ds(e_idx, 1), :] == lax.broadcasted_iota(jnp.int32, (cap, n), 0)
        g = jnp.sum(jnp.where(hit, aff_ref[0, pl.ds(e_idx, 1), :], 0.0), axis=1, keepdims=True)
        t = jnp.sum(jnp.where(hit, tok, 0.0), axis=1, keepdims=True)
        gate_ref[e_idx, 0] = jnp.broadcast_to(g, (cap, HEAD_W))
        row_ref[e_idx, 0] = jnp.broadcast_to(t, (cap, HEAD_W)).astype(jnp.int32) + base
        return carry

    lax.fori_loop(0, sel_ref.shape[1], body, 0)


def _slots(sel, aff, cap):
    b, e, n = sel.shape
    spec = pl.BlockSpec((e, 1, cap, HEAD_W), lambda i: (0, i, 0, 0))
    return pl.pallas_call(
        functools.partial(_slot_kernel, cap),
        name="slots_%d" % n,
        grid=(b,),
        in_specs=[pl.BlockSpec((1, e, n), lambda i: (i, 0, 0)), pl.BlockSpec((1, e, n), lambda i: (i, 0, 0))],
        out_specs=[spec, spec],
        out_shape=[jax.ShapeDtypeStruct((e, b, cap, HEAD_W), F32), jax.ShapeDtypeStruct((e, b, cap, HEAD_W), jnp.int32)],
        compiler_params=_cparams(1),
    )(sel, aff)


def _sc_gather(table, idx):
    info = plsc.get_sparse_core_info()
    n_workers = info.num_cores * info.num_subcores
    r, w = idx.shape[0], table.shape[1]
    per_w = r // n_workers
    ch = min(SC_CHUNK, per_w)
    assert r % n_workers == 0 and per_w % ch == 0 and ch % 8 == 0
    mesh = plsc.VectorSubcoreMesh(core_axis_name="c", subcore_axis_name="s")

    @functools.partial(
        pl.kernel, mesh=mesh, name="sc_gather",
        out_type=jax.ShapeDtypeStruct((r, w), jnp.int32),
        scratch_types=[pltpu.VMEM((ch,), jnp.int32), pltpu.VMEM((ch, w), jnp.int32), pltpu.SemaphoreType.DMA],
    )
    def k(table_hbm, idx_hbm, out_hbm, idx_v, rows_v, sem):
        base = (lax.axis_index("s") * info.num_cores + lax.axis_index("c")) * per_w

        @pl.loop(0, per_w // ch)
        def _(c):
            off = pl.multiple_of(base + c * ch, 8)
            pltpu.sync_copy(idx_hbm.at[pl.ds(off, ch)], idx_v)
            pltpu.async_copy(table_hbm.at[idx_v], rows_v, sem).wait()
            pltpu.sync_copy(rows_v, out_hbm.at[pl.ds(off, ch)])

    return k(table, idx)


def _expert_kernel(packed, x_ref, g_ref, wg_ref, wu_ref, wd_ref, y_ref, wg_s, wu_s, wd_s):
    @pl.when(pl.program_id(1) == 0)
    def _():
        wg_s[...] = wg_ref[0, 0].astype(BF)
        wu_s[...] = wu_ref[0, 0].astype(BF)
        wd_s[...] = wd_ref[0, 0].astype(BF)

    x = _unpack_halves(x_ref[0]) if packed else x_ref[0]
    a = _dot(x, wg_s[...])
    u = _dot(x, wu_s[...])
    mid = (a * (1.0 / (1.0 + jnp.exp(-a))) * u).astype(BF)
    y_ref[0] = (_dot(mid, wd_s[...]) * g_ref[0][:, 0:1]).astype(BF)


def _experts(l, xs, gate, w_e_gate, w_e_up, w_e_down):
    e, rows, w = xs.shape
    d, ff = w_e_gate.shape[2:]
    rb = math.gcd(rows, EXPERT_ROWS)
    w_idx = lambda i, j: (l, i, 0, 0)
    blk = lambda i, j: (i, j, 0)
    return pl.pallas_call(
        functools.partial(_expert_kernel, xs.dtype == jnp.int32),
        name="experts_%d" % rows,
        grid=(e, rows // rb),
        in_specs=[
            pl.BlockSpec((1, rb, w), blk),
            pl.BlockSpec((1, rb, HEAD_W), blk),
            pl.BlockSpec((1, 1, d, ff), w_idx),
            pl.BlockSpec((1, 1, d, ff), w_idx),
            pl.BlockSpec((1, 1, ff, d), w_idx),
        ],
        out_specs=pl.BlockSpec((1, rb, d), blk),
        out_shape=jax.ShapeDtypeStruct((e, rows, d), BF),
        scratch_shapes=[pltpu.VMEM((d, ff), BF), pltpu.VMEM((d, ff), BF), pltpu.VMEM((ff, d), BF)],
        compiler_params=_cparams(2),
    )(xs, gate, w_e_gate, w_e_up, w_e_down)


def _scatter_kernel(cap, d, final, *refs):
    if final:
        x_ref, mod_ref, selt_ref, ye_ref, gf_ref, o_ref, pt_ref = refs
    else:
        x_ref, mod_ref, selt_ref, ye_ref, o_ref, pt_ref = refs
    e_tot = ye_ref.shape[0]
    tn = x_ref.shape[1]
    selt = selt_ref[0]
    if cap % 128 == 0:
        lane = lax.broadcasted_iota(jnp.int32, (tn, cap), 1)
        for e_idx in range(e_tot):
            pt_ref[:, e_idx * cap:(e_idx + 1) * cap] = jnp.where(selt[:, e_idx:e_idx + 1] == lane, 1.0, 0.0).astype(BF)
    else:
        lane = lax.broadcasted_iota(jnp.int32, (tn, e_tot * cap), 1)
        acc = jnp.zeros((tn, e_tot * cap), F32)
        for e_idx in range(e_tot):
            s = selt[:, e_idx:e_idx + 1]
            acc = acc + jnp.where((s >= 0) & (s + e_idx * cap == lane), 1.0, 0.0)
        pt_ref[...] = acc.astype(BF)
    y = _dot(pt_ref[...], ye_ref[...].reshape(e_tot * cap, d))
    gt = mod_ref[0, :, 5 * d:6 * d]
    out = x_ref[0] + gt * y
    if final:
        out = _rms(out, gf_ref[...])
    o_ref[0] = out


def _scatter(x, modv, sel_t, ye, cap, g_final):
    b, n, d = x.shape
    e = ye.shape[0]
    tn = min(PRE_ROWS, n)
    bm = modv.shape[0]
    mod_idx = (lambda i, j: (i, 0, 0)) if bm > 1 else (lambda i, j: (0, 0, 0))
    final = g_final is not None
    in_specs = [
        pl.BlockSpec((1, tn, d), lambda i, j: (i, j, 0)),
        pl.BlockSpec((1, 1, modv.shape[2]), mod_idx),
        pl.BlockSpec((1, tn, e), lambda i, j: (i, j, 0)),
        pl.BlockSpec((e, 1, cap, d), lambda i, j: (0, i, 0, 0)),
    ]
    args = [x, modv, sel_t, ye]
    if final:
        in_specs.append(pl.BlockSpec((1, d), lambda i, j: (0, 0)))
        args.append(g_final.reshape(1, d))
    return pl.pallas_call(
        functools.partial(_scatter_kernel, cap, d, final),
        name="scatter_%d" % n,
        grid=(b, n // tn),
        in_specs=in_specs,
        out_specs=pl.BlockSpec((1, tn, d), lambda i, j: (i, j, 0)),
        out_shape=jax.ShapeDtypeStruct((b, n, d), F32),
        scratch_shapes=[pltpu.VMEM((tn, e * cap), BF)],
        compiler_params=_cparams(2),
    )(*args)


def _dft_tables(n):
    j = np.arange(n, dtype=np.int64)
    ang = ((j[:, None] * j[None, :]) % n).astype(np.float64) * (2.0 * math.pi / n)
    return jnp.asarray(np.cos(ang), dtype=BF), jnp.asarray(np.sin(ang), dtype=BF)


def _block_diag(blocks):
    g, r, c = blocks.shape
    out = jnp.zeros((g * r, g * c), blocks.dtype)
    for i in range(g):
        out = out.at[i * r:(i + 1) * r, i * c:(i + 1) * c].set(blocks[i])
    return out


def _rope_tables(n):
    rows = n // GRID_W
    row = jnp.repeat(jnp.arange(rows, dtype=F32), GRID_W)
    col = jnp.tile(jnp.arange(GRID_W, dtype=F32), rows)

    def ang(dim):
        nf = dim // 4
        inv = ROPE_BASE ** (-jnp.arange(nf, dtype=F32) / nf)
        return jnp.concatenate([row[:, None] * inv, col[:, None] * inv], axis=-1)

    a = ang(DIFF_DH)
    cos_d = jnp.tile(jnp.cos(a), (1, 4))
    sin_d = jnp.tile(jnp.concatenate([-jnp.sin(a), jnp.sin(a)], axis=1), (1, 2))
    a = ang(MLA_ROPE)
    ones = jnp.ones((n, MLA_NOPE), F32)
    pad = HEAD_W - MLA_NOPE - MLA_ROPE
    cos_m = jnp.concatenate([ones, jnp.cos(a), jnp.cos(a), jnp.ones((n, pad), F32)], axis=1)
    sin_m = jnp.concatenate([0 * ones, -jnp.sin(a), jnp.sin(a), jnp.zeros((n, pad), F32)], axis=1)
    return cos_d, sin_d, cos_m, sin_m


def _pad_heads(w, heads, lo, hi):
    k = w.shape[0]
    w3 = w.reshape(k, heads, -1)[:, :, lo:hi]
    return jnp.pad(w3, ((0, 0), (0, 0), (0, HEAD_W - (hi - lo)))).reshape(k, heads * HEAD_W)


def kernel(x_prompt, x_sample, cache_diff_k, cache_diff_v, cache_mla_ckv, cache_mla_kpe, c, c_ctx, w_ada, b_ada, g_attn, g_ffn, w_in, w_four, diff_lambda, g_subln, g_mla_q, w_mla_uq, g_mla_kv, w_mla_ukv, w_out, w_router, w_e_gate, w_e_up, w_e_down, g_final):
    depth, d = g_attn.shape
    b_ctx, n_ctx, _ = x_prompt.shape
    b_lat, n_lat, _ = x_sample.shape

    r = b_lat + 1
    r_pad = -(-r // 8) * 8
    cond = jnp.concatenate([c, c_ctx[None, :], jnp.zeros((r_pad - r, d), F32)], axis=0)
    mod = _modulation(cond, w_ada, b_ada)

    n_main = w_in.shape[2] - MLA_ROPE
    w_in_p = jnp.concatenate(
        [w_in[:, :, :n_main], jnp.zeros((depth, d, MLA_NOPE), F32), w_in[:, :, n_main:],
         jnp.zeros((depth, d, HEAD_W - MLA_NOPE - MLA_ROPE), F32)], axis=2).astype(BF)
    w_uq_p = jnp.stack([_pad_heads(w_mla_uq[l], MLA_HEADS, 0, MLA_NOPE + MLA_ROPE) for l in range(depth)]).astype(BF)
    w_k_p = jnp.stack([_pad_heads(w_mla_ukv[l], MLA_HEADS, 0, MLA_NOPE) for l in range(depth)]).astype(BF)
    w_v_p = w_mla_ukv.reshape(depth, MLA_KV_RANK, MLA_HEADS, MLA_NOPE + MLA_V)[..., MLA_NOPE:].reshape(depth, MLA_KV_RANK, D_MLA).astype(BF)
    w_out_p = w_out.astype(BF)
    w_router_t = jnp.swapaxes(w_router, 1, 2).astype(BF)
    place = jnp.zeros((MLA_ROPE, HEAD_W), F32).at[jnp.arange(MLA_ROPE), MLA_NOPE + jnp.arange(MLA_ROPE)].set(1.0)
    place = jnp.tile(place, (1, MLA_HEADS)).astype(BF)

    jc = jnp.arange(F_CH, dtype=jnp.int32)
    ang_c = ((jc[:, None] * jc[None, :]) % F_CH).astype(F32) * (2.0 * math.pi / F_CH)
    cc_bd = _block_diag(jnp.broadcast_to(jnp.cos(ang_c), (F_GROUPS, F_CH, F_CH))).astype(BF)
    sc_bd = _block_diag(jnp.broadcast_to(jnp.sin(ang_c), (F_GROUPS, F_CH, F_CH))).astype(BF)
    dft = {n: _dft_tables(n) for n in {n_ctx, n_lat}}
    ropes = _rope_tables(n_lat)

    def pre(latent, l, x, prev=None, combine=None):
        modv = (mod[l, :b_lat] if latent else mod[l, b_lat:b_lat + 1]).reshape(-1, 1, N_MOD * d)
        outs = _pre(latent, x, modv, g_attn[l], w_in_p[l], g_mla_q[l], w_uq_p[l], g_mla_kv[l], w_k_p[l], w_v_p[l],
                    cc_bd, sc_bd, ropes, l, depth, prev, combine)
        if combine is not None:
            x, outs = outs[0], outs[1:]
        return x, modv, outs

    def mix_and_route(latent, l, x, modv, outs):
        b, n, _ = x.shape
        cap = max(1, EC_CAPACITY_FACTOR * n // N_EXPERTS)
        u, v, qd, kd, vd, qm, km, vm = outs[:8]
        cache = _cache_prep(l, cache_diff_k, cache_diff_v, cache_mla_ckv, cache_mla_kpe, w_k_p[l], w_v_p[l], place) if latent else None
        cn, sn = dft[n]
        yfour = _fourier(u, v, cn, sn, _block_diag(w_four[l]).astype(BF))
        x = _attention(latent, l, x, modv, yfour, qd, kd, vd, qm, km, vm, cache, diff_lambda[l], g_subln[l], w_out_p[l])
        h2, aff = _router(x, modv, g_ffn[l], w_router_t[l], packed=latent)
        sel = _select(aff, cap)
        if latent:
            gate, row = _slots(sel, aff, cap)
            xs = _sc_gather(h2.reshape(b * n, h2.shape[2]), row[..., 0].reshape(-1))
        else:
            xs, gate = _gather(sel, aff, h2, cap)
        e = gate.shape[0]
        return x, jnp.swapaxes(sel, 1, 2), cap, xs.reshape(e, b * cap, -1), gate.reshape(e, b * cap, HEAD_W)

    new = tuple(jnp.zeros((b_ctx, depth, n_ctx) + tail, F32)
                for tail in ((DIFF_HEADS, HEAD_W), (DIFF_HEADS, HEAD_W), (MLA_KV_RANK,), (MLA_ROPE,)))
    x_l, mod_l, pre_l = pre(True, 0, x_sample)
    x_c, mod_c, pre_c = pre(False, 0, x_prompt, new)
    for l in range(depth):
        new = tuple(pre_c[8:])
        x_l, selt_l, cap_l, xs_l, gate_l = mix_and_route(True, l, x_l, mod_l, pre_l)
        x_c, selt_c, cap_c, xs_c, gate_c = mix_and_route(False, l, x_c, mod_c, pre_c)
        ye_c = _experts(l, xs_c, gate_c, w_e_gate, w_e_up, w_e_down).reshape(-1, b_ctx, cap_c, d)
        if l + 1 < depth:
            x_c, mod_c, pre_c = pre(False, l + 1, x_c, new, (mod_c, selt_c, ye_c, cap_c))
        else:
            x_c = _scatter(x_c, mod_c, selt_c, ye_c, cap_c, g_final)
        ye_l = _experts(l, xs_l, gate_l, w_e_gate, w_e_up, w_e_down).reshape(-1, b_lat, cap_l, d)
        if l + 1 < depth:
            x_l, mod_l, pre_l = pre(True, l + 1, x_l, None, (mod_l, selt_l, ye_l, cap_l))
        else:
            x_l = _scatter(x_l, mod_l, selt_l, ye_l, cap_l, g_final)
    y_prompt, y_sample = x_c, x_l
    new_diff_k, new_diff_v, new_mla_ckv, new_mla_kpe = new
    return (y_prompt, y_sample, new_diff_k, new_diff_v, new_mla_ckv, new_mla_kpe)
```

```python
import functools
import math

import jax
import jax.numpy as jnp
import numpy as np
from jax import lax
from jax.experimental import pallas as pl
from jax.experimental.pallas import tpu as pltpu
from jax.experimental.pallas import tpu_sc as plsc

BF = jnp.bfloat16
F32 = jnp.float32

GRID_W = 64
ROPE_BASE = 10000.0
NORM_EPS = 1e-6
F_GROUPS, F_CH = 4, 64
D_FOURIER = F_GROUPS * F_CH
DIFF_HEADS, DIFF_DH = 4, 64
D_DIFF = DIFF_HEADS * 2 * DIFF_DH
MLA_HEADS, MLA_Q_RANK, MLA_KV_RANK = 4, 256, 128
MLA_NOPE, MLA_ROPE, MLA_V = 64, 32, 64
HEAD_W = 128
D_MLA_P = MLA_HEADS * HEAD_W
D_MLA = MLA_HEADS * MLA_V
D_MIX = D_FOURIER + D_DIFF + D_MLA
KEY_CHUNK = 256
SCORE_BUFS = 2
BF_ROWS = 16
SELECT_ROWS = 128
EXPERT_ROWS = 512
PRE_ROWS = 512
SC_CHUNK = 64
SUB_ROWS = 256
N_EXPERTS = 16
EC_CAPACITY_FACTOR = 2
N_MOD = 6
IN_COLS_P = D_FOURIER + 3 * D_DIFF + MLA_Q_RANK + MLA_KV_RANK + HEAD_W
LOG2E = 1.4426950408889634
VMEM_LIMIT = 56 * 1024 * 1024


def _cparams(n_axes, vmem=VMEM_LIMIT):
    return pltpu.CompilerParams(dimension_semantics=("arbitrary",) * n_axes, vmem_limit_bytes=vmem)


def _dot(a, b):
    return jnp.dot(a, b, preferred_element_type=F32)


def _dot_nt(a, b):
    return lax.dot_general(a, b, (((1,), (1,)), ((), ())), preferred_element_type=F32)


def _rms(x, g):
    return x * lax.rsqrt(jnp.mean(x * x, axis=-1, keepdims=True) + NORM_EPS) * g


def _pack_halves(hb):
    w = hb.shape[1] // 2
    bits = pltpu.bitcast(hb.astype(F32), jnp.uint32)
    packed = (bits[:, :w] >> 16) | (bits[:, w:] & jnp.uint32(0xFFFF0000))
    return pltpu.bitcast(packed, jnp.int32)


def _unpack_halves(xi):
    bits = pltpu.bitcast(xi, jnp.uint32)
    lo = pltpu.bitcast(bits << 16, F32).astype(BF)
    hi = pltpu.bitcast(bits & jnp.uint32(0xFFFF0000), F32).astype(BF)
    return jnp.concatenate([lo, hi], axis=1)


def _rope(z, cos, sin_signed, half, group, lo):
    w = z.shape[1]
    reps = w // cos.shape[1]
    cos_w = jnp.concatenate([cos] * reps, axis=1) if reps > 1 else cos
    sin_w = jnp.concatenate([sin_signed] * reps, axis=1) if reps > 1 else sin_signed
    from_right = pltpu.roll(z, w - half, 1)
    from_left = pltpu.roll(z, half, 1)
    lane = lax.broadcasted_iota(jnp.int32, z.shape, 1) % group
    first = (lane >= lo) & (lane < lo + half)
    partner = jnp.where(first, from_right, from_left)
    return z * cos_w + partner * sin_w


def _mod_kernel(c_ref, w_ref, b_ref, o_ref):
    c = c_ref[...]
    a = (c * (1.0 / (1.0 + jnp.exp(-c)))).astype(BF)
    o_ref[0] = _dot(a, w_ref[0].astype(BF)) + b_ref[0]


def _modulation(cond, w_ada, b_ada):
    depth, d, n6 = w_ada.shape
    r = cond.shape[0]
    tn = 1536
    return pl.pallas_call(
        _mod_kernel,
        name="modulation",
        grid=(depth, n6 // tn),
        in_specs=[
            pl.BlockSpec((r, d), lambda l, j: (0, 0)),
            pl.BlockSpec((1, d, tn), lambda l, j: (l, 0, j)),
            pl.BlockSpec((1, 1, tn), lambda l, j: (l, 0, j)),
        ],
        out_specs=pl.BlockSpec((1, r, tn), lambda l, j: (l, 0, j)),
        out_shape=jax.ShapeDtypeStruct((depth, r, n6), F32),
        compiler_params=_cparams(2),
    )(cond, w_ada, b_ada.reshape(depth, 1, n6))


def _pre_kernel(latent, d, *refs):
    if latent:
        (x_ref, mod_ref, g_ref, win_ref, gq_ref, wuq_ref, gkv_ref, wk_ref, wv_ref, cc_ref, sc_ref,
         cd_ref, sd_ref, cm_ref, sm_ref,
         u_ref, v_ref, qd_ref, kd_ref, vd_ref, qm_ref, km_ref, vm_ref) = refs
    else:
        (x_ref, mod_ref, g_ref, win_ref, gq_ref, wuq_ref, gkv_ref, wk_ref, wv_ref, cc_ref, sc_ref,
         u_ref, v_ref, qd_ref, kd_ref, vd_ref, qm_ref, km_ref, vm_ref,
         k32_ref, v32_ref, ckv32_ref, kpe32_ref) = refs[:11] + refs[-12:]
    sh = mod_ref[0, :, 0:d]
    sc = mod_ref[0, :, d:2 * d]
    tm = x_ref.shape[1]
    sub = min(tm, SUB_ROWS)
    def project(r0):
        h = _rms(x_ref[0, r0:r0 + sub, :], g_ref[...]) * (1.0 + sc) + sh
        return _dot(h.astype(BF), win_ref[...])

    starts = list(range(0, tm, sub))
    z_next = project(starts[0])
    for i, r0 in enumerate(starts):
        rows = slice(r0, r0 + sub)
        z = z_next
        z_next = project(starts[i + 1]) if i + 1 < len(starts) else None
        o = 0
        zf = z[:, o:o + D_FOURIER]; o += D_FOURIER
        zq = z[:, o:o + D_DIFF]; o += D_DIFF
        zk = z[:, o:o + D_DIFF]; o += D_DIFF
        zv = z[:, o:o + D_DIFF]; o += D_DIFF
        zcq = z[:, o:o + MLA_Q_RANK]; o += MLA_Q_RANK
        zckv = z[:, o:o + MLA_KV_RANK]; o += MLA_KV_RANK
        kpe = z[:, o:o + HEAD_W]

        zf_b = zf.astype(BF)
        u_ref[0, rows, :] = _dot(zf_b, cc_ref[...]).astype(BF)
        v_ref[0, rows, :] = _dot(zf_b, sc_ref[...]).astype(BF)
        vd_ref[0, :, rows] = zv.T.astype(BF)
        cq = _rms(zcq, gq_ref[...])
        qm = _dot(cq.astype(BF), wuq_ref[...])
        ckv = _rms(zckv, gkv_ref[...])
        ckv_b = ckv.astype(BF)
        k_nope = _dot(ckv_b, wk_ref[...])
        vm_ref[0, :, rows] = _dot(ckv_b, wv_ref[...]).T.astype(BF)
        if latent:
            cd, sd, cm, sm = cd_ref[rows, :], sd_ref[rows, :], cm_ref[rows, :], sm_ref[rows, :]
            zq_r = _rope(zq, cd, sd, DIFF_DH // 2, DIFF_DH, 0)
            zk_r = _rope(zk, cd, sd, DIFF_DH // 2, DIFF_DH, 0)
            qm = _rope(qm, cm, sm, MLA_ROPE // 2, HEAD_W, MLA_NOPE)
            kpe_r = _rope(kpe, cm, sm, MLA_ROPE // 2, HEAD_W, MLA_NOPE)
        else:
            zq_r, zk_r, kpe_r = zq, zk, kpe
            for hd in range(DIFF_HEADS):
                k32_ref[0, 0, rows, hd, :] = zk[:, hd * HEAD_W:(hd + 1) * HEAD_W]
                v32_ref[0, 0, rows, hd, :] = zv[:, hd * HEAD_W:(hd + 1) * HEAD_W]
            ckv32_ref[0, 0, rows, :] = ckv
            kpe32_ref[0, 0, rows, :] = kpe[:, MLA_NOPE:MLA_NOPE + MLA_ROPE]
            if i == 0:
                for ref in (k32_ref, v32_ref, ckv32_ref, kpe32_ref):
                    if ref.shape[1] > 1:
                        ref[0, 1:] = jnp.zeros((ref.shape[1] - 1,) + ref.shape[2:], F32)
        qd_ref[0, rows, :] = (zq_r * (DIFF_DH ** -0.5 * LOG2E)).astype(BF)
        kd_ref[0, rows, :] = zk_r.astype(BF)
        qm_ref[0, rows, :] = (qm * ((MLA_NOPE + MLA_ROPE) ** -0.5 * LOG2E)).astype(BF)
        km_ref[0, rows, :] = (k_nope + jnp.concatenate([kpe_r] * MLA_HEADS, axis=1)).astype(BF)


def _combine_pre_kernel(latent, d, cap, n_pre_in, *refs):
    x_ref, modp_ref, selt_ref, ye_ref = refs[:4]
    pre_in = refs[4:4 + n_pre_in]
    xo_ref = refs[4 + n_pre_in]
    pre_out = refs[5 + n_pre_in:-1]
    _scatter_kernel(cap, d, False, x_ref, modp_ref, selt_ref, ye_ref, xo_ref, refs[-1])
    _pre_kernel(latent, d, xo_ref, *pre_in, *pre_out)


def _pre(latent, x, modv, g_attn, w_in_p, g_q, w_uq_p, g_kv, w_k_p, w_v_p, cc_bd, sc_bd, ropes, l=0, depth=1, prev=None,
         combine=None):
    b, n, d = x.shape
    tm = min(PRE_ROWS, n)
    bm = modv.shape[0]
    mod_idx = (lambda i, j: (i, 0, 0)) if bm > 1 else (lambda i, j: (0, 0, 0))
    const = lambda i, j: (0, 0)
    tok = lambda i, j: (i, j, 0)
    in_specs = [
        pl.BlockSpec((1, tm, d), tok),
        pl.BlockSpec((1, 1, modv.shape[2]), mod_idx),
        pl.BlockSpec((1, d), const),
        pl.BlockSpec(w_in_p.shape, const),
        pl.BlockSpec((1, MLA_Q_RANK), const),
        pl.BlockSpec(w_uq_p.shape, const),
        pl.BlockSpec((1, MLA_KV_RANK), const),
        pl.BlockSpec(w_k_p.shape, const),
        pl.BlockSpec(w_v_p.shape, const),
        pl.BlockSpec(cc_bd.shape, const),
        pl.BlockSpec(sc_bd.shape, const),
    ]
    args = [x, modv, g_attn.reshape(1, d), w_in_p, g_q.reshape(1, -1), w_uq_p, g_kv.reshape(1, -1), w_k_p, w_v_p, cc_bd, sc_bd]
    if latent:
        in_specs += [pl.BlockSpec((tm, HEAD_W), lambda i, j: (j, 0))] * 4
        args += list(ropes)
    feat = lambda i, j: (i, 0, j)
    out_shape, out_specs = [], []
    for w, token_major in ((D_FOURIER, True), (D_FOURIER, True), (D_DIFF, True), (D_DIFF, True), (D_DIFF, False),
                           (D_MLA_P, True), (D_MLA_P, True), (D_MLA, False)):
        out_shape.append(jax.ShapeDtypeStruct((b, n, w) if token_major else (b, w, n), BF))
        out_specs.append(pl.BlockSpec((1, tm, w), tok) if token_major else pl.BlockSpec((1, w, tm), feat))
    aliases = {}
    if not latent:
        span = depth if prev is None else 1
        for tail in ((DIFF_HEADS, HEAD_W), (DIFF_HEADS, HEAD_W), (MLA_KV_RANK,), (MLA_ROPE,)):
            out_shape.append(jax.ShapeDtypeStruct((b, depth, n) + tail, F32))
            out_specs.append(pl.BlockSpec((1, span, tm) + tail, lambda i, j, t=len(tail): (i, l, j) + (0,) * t))
        if prev is not None:
            for k, buf in enumerate(prev):
                aliases[len(args)] = len(out_shape) - len(prev) + k
                in_specs.append(pl.BlockSpec(memory_space=pl.ANY))
                args.append(buf)
    body, scratch, tag = functools.partial(_pre_kernel, latent, d), [], "pre"
    if combine is not None:
        modp, sel_t, ye, cap = combine
        e = ye.shape[0]
        modp_idx = (lambda i, j: (i, 0, 0)) if modp.shape[0] > 1 else (lambda i, j: (0, 0, 0))
        extra_specs = [pl.BlockSpec((1, 1, modp.shape[2]), modp_idx), pl.BlockSpec((1, tm, e), tok),
                       pl.BlockSpec((e, 1, cap, d), lambda i, j: (0, i, 0, 0))]
        n_extra = len(extra_specs)
        body = functools.partial(_combine_pre_kernel, latent, d, cap, len(args) - 1)
        in_specs = in_specs[:1] + extra_specs + in_specs[1:]
        args = args[:1] + [modp, sel_t, ye] + args[1:]
        out_specs = [pl.BlockSpec((1, tm, d), tok)] + out_specs
        out_shape = [jax.ShapeDtypeStruct((b, n, d), F32)] + out_shape
        aliases = {k + n_extra: v + 1 for k, v in aliases.items()}
        scratch, tag = [pltpu.VMEM((tm, e * cap), BF)], "combine_pre"
    return pl.pallas_call(
        body,
        name=tag + ("_lat" if latent else "_ctx"),
        grid=(b, n // tm),
        in_specs=in_specs,
        out_specs=out_specs,
        out_shape=out_shape,
        input_output_aliases=aliases,
        scratch_shapes=scratch,
        compiler_params=_cparams(2),
    )(*args)


def _cache_kernel(dk_ref, dv_ref, ckv_ref, kpe_ref, wk_ref, wv_ref, place_ref, kd_ref, vd_ref, km_ref, vm_ref):
    for hd in range(DIFF_HEADS):
        sl = slice(hd * HEAD_W, (hd + 1) * HEAD_W)
        kd_ref[0, :, sl] = dk_ref[0, 0, :, hd, :].astype(BF)
        vd_ref[0, sl, :] = dv_ref[0, 0, :, hd, :].T.astype(BF)
    ckv_b = ckv_ref[0, 0].astype(BF)
    kpe_w = _dot(kpe_ref[0, 0].astype(BF), place_ref[...])
    km_ref[0] = (_dot(ckv_b, wk_ref[...]) + kpe_w).astype(BF)
    vm_ref[0] = _dot(ckv_b, wv_ref[...]).T.astype(BF)


def _cache_prep(l, cache_dk, cache_dv, cache_ckv, cache_kpe, w_k_p, w_v_p, place):
    b, _, p = cache_dk.shape[:3]
    at_l = lambda i: (i, l, 0, 0)
    at_l5 = lambda i: (i, l, 0, 0, 0)
    const = lambda i: (0, 0)
    shapes = [(p, D_DIFF), (D_DIFF, p), (p, D_MLA_P), (D_MLA, p)]
    return pl.pallas_call(
        _cache_kernel,
        name="cache_prep",
        grid=(b,),
        in_specs=[
            pl.BlockSpec((1, 1, p, DIFF_HEADS, HEAD_W), at_l5),
            pl.BlockSpec((1, 1, p, DIFF_HEADS, HEAD_W), at_l5),
            pl.BlockSpec((1, 1, p, MLA_KV_RANK), at_l),
            pl.BlockSpec((1, 1, p, MLA_ROPE), at_l),
            pl.BlockSpec(w_k_p.shape, const),
            pl.BlockSpec(w_v_p.shape, const),
            pl.BlockSpec(place.shape, const),
        ],
        out_specs=[pl.BlockSpec((1,) + s, lambda i: (i, 0, 0)) for s in shapes],
        out_shape=[jax.ShapeDtypeStruct((b,) + s, BF) for s in shapes],
        compiler_params=_cparams(1),
    )(cache_dk, cache_dv, cache_ckv, cache_kpe, w_k_p, w_v_p, place)


def _fourier_kernel(scale, u_ref, v_ref, cn_ref, sn_ref, w_ref, o_ref):
    f = (_dot(cn_ref[...], u_ref[0]) - _dot(sn_ref[...], v_ref[0])) * scale
    o_ref[0] = _dot(f.astype(BF), w_ref[...]).astype(BF)


def _fourier(u, v, cn, sn, w_bd):
    b, n, w = u.shape
    tm = min(512, n)
    return pl.pallas_call(
        functools.partial(_fourier_kernel, 1.0 / math.sqrt(n * F_CH)),
        name="fourier_%d" % n,
        grid=(n // tm, b),
        in_specs=[
            pl.BlockSpec((1, n, w), lambda i, j: (j, 0, 0)),
            pl.BlockSpec((1, n, w), lambda i, j: (j, 0, 0)),
            pl.BlockSpec((tm, n), lambda i, j: (i, 0)),
            pl.BlockSpec((tm, n), lambda i, j: (i, 0)),
            pl.BlockSpec((w, w), lambda i, j: (0, 0)),
        ],
        out_specs=pl.BlockSpec((1, tm, w), lambda i, j: (j, i, 0)),
        out_shape=jax.ShapeDtypeStruct((b, n, w), BF),
        compiler_params=_cparams(2),
    )(u, v, cn, sn, w_bd)


def _scores(q, k_parts, s_ref):
    off, m, bounds = 0, None, []
    for k in k_parts:
        s = _dot_nt(k(), q)
        n_i = s.shape[0]
        s_ref[off:off + n_i, :] = s
        mi = jnp.max(s, axis=0, keepdims=True)
        m = mi if m is None else jnp.maximum(m, mi)
        bounds.append((off, n_i))
        off += n_i
    return m, bounds


def _probs(m, total, s_ref, p_ref):
    for r in range(0, total, KEY_CHUNK):
        rows = min(KEY_CHUNK, total - r)
        p_ref[r:r + rows, :] = jnp.exp2(s_ref[r:r + rows, :] - m).astype(BF)


def _weighted(vt_parts, bounds, p_ref):
    o = None
    for vt, (r, n_i) in zip(vt_parts, bounds):
        v = vt()
        ones = jnp.where(lax.broadcasted_iota(jnp.int32, (BF_ROWS, n_i), 0) == 0, 1.0, 0.0).astype(BF)
        oi = _dot(jnp.concatenate([v, ones], axis=0), p_ref[r:r + n_i, :])
        o = oi if o is None else o + oi
    dv = o.shape[0] - BF_ROWS
    return o[:dv] * (1.0 / o[dv:dv + 1])


def _attn_kernel(latent, lam_init, d, *refs):
    if latent:
        (x_ref, mod_ref, yf_ref, qd_ref, kd_ref, vd_ref, kdc_ref, vdc_ref, qm_ref, km_ref, vm_ref, kmc_ref, vmc_ref,
         lam_ref, gs_ref, wo_ref, o_ref, mix_ref, s_ref, p_ref) = refs
    else:
        (x_ref, mod_ref, yf_ref, qd_ref, kd_ref, vd_ref, qm_ref, km_ref, vm_ref,
         lam_ref, gs_ref, wo_ref, o_ref, mix_ref, s_ref, p_ref) = refs
        kdc_ref = vdc_ref = kmc_ref = vmc_ref = None
    lf = lam_ref[...]
    lam = (jnp.exp(jnp.sum(lf[0:1] * lf[1:2], axis=-1, keepdims=True))
           - jnp.exp(jnp.sum(lf[2:3] * lf[3:4], axis=-1, keepdims=True)) + lam_init)
    mix_ref[:, 0:D_FOURIER] = yf_ref[0]
    gs = gs_ref[...]

    def keys(ref, cache_ref, sl):
        parts = [lambda: ref[0, :, sl]]
        if latent:
            parts.append(lambda: cache_ref[0, :, sl])
        return parts

    def values_t(ref, cache_ref, sl):
        parts = [lambda: ref[0, sl, :]]
        if latent:
            parts.append(lambda: cache_ref[0, sl, :])
        return parts

    def diff_query(sl, comp):
        def load():
            qh = qd_ref[0, :, sl]
            lane = lax.broadcasted_iota(jnp.int32, qh.shape, 1)
            keep = (lane >= DIFF_DH) if comp else (lane < DIFF_DH)
            return jnp.where(keep, qh, jnp.zeros_like(qh))
        return load

    items = []
    for hd in range(DIFF_HEADS):
        sl = slice(hd * HEAD_W, (hd + 1) * HEAD_W)
        for comp in range(2):
            items.append((diff_query(sl, comp), keys(kd_ref, kdc_ref, sl), values_t(vd_ref, vdc_ref, sl)))
    for hd in range(MLA_HEADS):
        sl = slice(hd * HEAD_W, (hd + 1) * HEAD_W)
        vsl = slice(hd * MLA_V, (hd + 1) * MLA_V)
        items.append(((lambda sl=sl: qm_ref[0, :, sl]), keys(km_ref, kmc_ref, sl), values_t(vm_ref, vmc_ref, vsl)))

    def start(i):
        return _scores(items[i][0](), items[i][1], s_ref.at[i % SCORE_BUFS])

    total = s_ref.shape[1]
    outs = []
    state = start(0)
    for i in range(len(items)):
        nxt = start(i + 1) if i + 1 < len(items) else None
        m, bounds = state
        buf = i % SCORE_BUFS
        _probs(m, total, s_ref.at[buf], p_ref.at[buf])
        outs.append(_weighted(items[i][2], bounds, p_ref.at[buf]))
        state = nxt

    for hd in range(DIFF_HEADS):
        o = outs[2 * hd] - lam * outs[2 * hd + 1]
        o = o * lax.rsqrt(jnp.mean(o * o, axis=0, keepdims=True) + NORM_EPS) * gs * (1.0 - lam_init)
        mix_ref[:, D_FOURIER + hd * HEAD_W:D_FOURIER + (hd + 1) * HEAD_W] = o.T.astype(BF)
    mix_ref[:, D_FOURIER + D_DIFF:D_MIX] = jnp.concatenate(outs[2 * DIFF_HEADS:], axis=0).T.astype(BF)
    y = _dot(mix_ref[...], wo_ref[...])
    gt = mod_ref[0, :, 2 * d:3 * d]
    o_ref[0] = x_ref[0] + gt * y


def _attention(latent, l, x, modv, yfour, qd, kd, vd, qm, km, vm, cache, diff_lambda_l, g_subln_l, w_out_b):
    b, n, d = x.shape
    tq = min(256, n)
    bm = modv.shape[0]
    mod_idx = (lambda i, j: (i, 0, 0)) if bm > 1 else (lambda i, j: (0, 0, 0))
    const = lambda i, j: (0, 0)
    tok = lambda i, j: (i, j, 0)
    full = lambda i, j: (i, 0, 0)
    p = cache[0].shape[1] if latent else 0
    in_specs = [
        pl.BlockSpec((1, tq, d), tok),
        pl.BlockSpec((1, 1, modv.shape[2]), mod_idx),
        pl.BlockSpec((1, tq, D_FOURIER), tok),
        pl.BlockSpec((1, tq, D_DIFF), tok),
        pl.BlockSpec((1, n, D_DIFF), full),
        pl.BlockSpec((1, D_DIFF, n), full),
    ]
    args = [x, modv, yfour, qd, kd, vd]
    if latent:
        kdc, vdc, kmc, vmc = cache
        in_specs += [pl.BlockSpec((1, p, D_DIFF), full), pl.BlockSpec((1, D_DIFF, p), full)]
        args += [kdc, vdc]
    in_specs += [pl.BlockSpec((1, tq, D_MLA_P), tok), pl.BlockSpec((1, n, D_MLA_P), full), pl.BlockSpec((1, D_MLA, n), full)]
    args += [qm, km, vm]
    if latent:
        in_specs += [pl.BlockSpec((1, p, D_MLA_P), full), pl.BlockSpec((1, D_MLA, p), full)]
        args += [kmc, vmc]
    in_specs += [pl.BlockSpec((4, DIFF_DH), const), pl.BlockSpec((HEAD_W, 1), const), pl.BlockSpec(w_out_b.shape, const)]
    args += [diff_lambda_l, g_subln_l.reshape(HEAD_W, 1), w_out_b]
    lam_init = 0.8 - 0.6 * math.exp(-0.3 * l)
    return pl.pallas_call(
        functools.partial(_attn_kernel, latent, lam_init, d),
        name="attn_lat" if latent else "attn_ctx",
        grid=(b, n // tq),
        in_specs=in_specs,
        out_specs=pl.BlockSpec((1, tq, d), tok),
        out_shape=jax.ShapeDtypeStruct((b, n, d), F32),
        scratch_shapes=[pltpu.VMEM((tq, D_MIX), BF), pltpu.VMEM((SCORE_BUFS, n + p, tq), F32),
                        pltpu.VMEM((SCORE_BUFS, n + p, tq), BF)],
        compiler_params=_cparams(2),
    )(*args)


def _router_kernel(d, packed, x_ref, mod_ref, g_ref, wr_ref, h_ref, aff_ref):
    sh = mod_ref[0, :, 3 * d:4 * d]
    sc = mod_ref[0, :, 4 * d:5 * d]
    tm = x_ref.shape[1]
    sub = min(tm, SUB_ROWS)
    for r0 in range(0, tm, sub):
        rows = slice(r0, r0 + sub)
        hb = (_rms(x_ref[0, rows, :], g_ref[...]) * (1.0 + sc) + sh).astype(BF)
        h_ref[0, rows, :] = _pack_halves(hb) if packed else hb
        logits = _dot_nt(wr_ref[...], hb)
        e = jnp.exp(logits - jnp.max(logits, axis=0, keepdims=True))
        aff_ref[0, :, rows] = e / jnp.sum(e, axis=0, keepdims=True)


def _router(x, modv, g_ffn_l, w_router_t, packed):
    b, n, d = x.shape
    hw, hdt = (d // 2, jnp.int32) if packed else (d, BF)
    tm = min(PRE_ROWS, n)
    bm = modv.shape[0]
    mod_idx = (lambda i, j: (i, 0, 0)) if bm > 1 else (lambda i, j: (0, 0, 0))
    const = lambda i, j: (0, 0)
    return pl.pallas_call(
        functools.partial(_router_kernel, d, packed),
        name="router_%d" % n,
        grid=(b, n // tm),
        in_specs=[
            pl.BlockSpec((1, tm, d), lambda i, j: (i, j, 0)),
            pl.BlockSpec((1, 1, modv.shape[2]), mod_idx),
            pl.BlockSpec((1, d), const),
            pl.BlockSpec(w_router_t.shape, const),
        ],
        out_specs=[pl.BlockSpec((1, tm, hw), lambda i, j: (i, j, 0)), pl.BlockSpec((1, N_EXPERTS, tm), lambda i, j: (i, 0, j))],
        out_shape=[jax.ShapeDtypeStruct((b, n, hw), hdt), jax.ShapeDtypeStruct((b, N_EXPERTS, n), F32)],
        compiler_params=_cparams(2),
    )(x, modv, g_ffn_l.reshape(1, d), w_router_t)


def _select_kernel(cap, aff_ref, sel_ref):
    a = aff_ref[...]
    e, n = a.shape
    bits = pltpu.bitcast(a, jnp.int32)
    capf = float(cap)

    def body(_, carry):
        lo, hi = carry
        mid = lo + ((hi - lo) >> 1)
        cnt = jnp.sum(jnp.where(bits >= mid, 1.0, 0.0), axis=1, keepdims=True)
        up = cnt >= capf
        return jnp.where(up, mid, lo), jnp.where(up, hi, mid)

    lo0 = jnp.zeros((e, 1), jnp.int32)
    hi0 = jnp.full((e, 1), 0x7F800000, jnp.int32)
    thr, _ = lax.fori_loop(0, 31, body, (lo0, hi0))
    gt = bits > thr
    eq = bits == thr
    need = capf - jnp.sum(jnp.where(gt, 1.0, 0.0), axis=1, keepdims=True)
    both = jnp.concatenate([jnp.where(gt, 1.0, 0.0), jnp.where(eq, 1.0, 0.0)], axis=0).astype(BF)
    ck = min(256, n)
    tri = jnp.where(lax.broadcasted_iota(jnp.int32, (ck, ck), 0) < lax.broadcasted_iota(jnp.int32, (ck, ck), 1), 1.0, 0.0).astype(BF)
    off = jnp.zeros((2 * e, 1), F32)
    pieces = []
    for k in range(n // ck):
        blk = both[:, k * ck:(k + 1) * ck]
        pieces.append(_dot(blk, tri) + off)
        off = off + jnp.sum(blk.astype(F32), axis=1, keepdims=True)
    cum = jnp.concatenate(pieces, axis=1) if len(pieces) > 1 else pieces[0]
    cum_gt, cum_eq = cum[:e], cum[e:]
    chosen = gt | (eq & (cum_eq < need))
    pos = cum_gt + jnp.minimum(cum_eq, need)
    sel_ref[...] = jnp.where(chosen, pos, -1.0).astype(jnp.int32)


def _select(aff, cap):
    b, e, n = aff.shape
    rows = b * e
    rb = min(rows, SELECT_ROWS)
    sel = pl.pallas_call(
        functools.partial(_select_kernel, cap),
        name="select_%d" % n,
        grid=(rows // rb,),
        in_specs=[pl.BlockSpec((rb, n), lambda i: (i, 0))],
        out_specs=pl.BlockSpec((rb, n), lambda i: (i, 0)),
        out_shape=jax.ShapeDtypeStruct((rows, n), jnp.int32),
        compiler_params=_cparams(1),
    )(aff.reshape(rows, n))
    return sel.reshape(b, e, n)


def _gather_kernel(cap, sel_ref, aff_ref, h_ref, xs_ref, gate_ref):
    h = h_ref[0]
    n = h.shape[0]
    e_tot = sel_ref.shape[1]

    def one(e_idx):
        sel_e = sel_ref[0, pl.ds(e_idx, 1), :]
        aff_e = aff_ref[0, pl.ds(e_idx, 1), :]
        hit = sel_e == lax.broadcasted_iota(jnp.int32, (cap, n), 0)
        g = jnp.sum(jnp.where(hit, aff_e, 0.0), axis=1, keepdims=True)
        return jnp.where(hit, 1.0, 0.0).astype(BF), jnp.broadcast_to(g, (cap, HEAD_W))

    if cap >= 128:
        def body(e_idx, carry):
            p, g = one(e_idx)
            xs_ref[e_idx, 0] = _dot(p, h).astype(BF)
            gate_ref[e_idx, 0] = g
            return carry

        lax.fori_loop(0, e_tot, body, 0)
    else:
        ps, gs = zip(*[one(e_idx) for e_idx in range(e_tot)])
        xs = _dot(jnp.concatenate(ps, axis=0), h).astype(BF)
        for e_idx in range(e_tot):
            xs_ref[e_idx, 0] = xs[e_idx * cap:(e_idx + 1) * cap]
            gate_ref[e_idx, 0] = gs[e_idx]


def _gather(sel, aff, h2, cap):
    b, e, n = sel.shape
    d = h2.shape[2]
    return pl.pallas_call(
        functools.partial(_gather_kernel, cap),
        name="gather_%d" % n,
        grid=(b,),
        in_specs=[
            pl.BlockSpec((1, e, n), lambda i: (i, 0, 0)),
            pl.BlockSpec((1, e, n), lambda i: (i, 0, 0)),
            pl.BlockSpec((1, n, d), lambda i: (i, 0, 0)),
        ],
        out_specs=[pl.BlockSpec((e, 1, cap, d), lambda i: (0, i, 0, 0)), pl.BlockSpec((e, 1, cap, HEAD_W), lambda i: (0, i, 0, 0))],
        out_shape=[jax.ShapeDtypeStruct((e, b, cap, d), BF), jax.ShapeDtypeStruct((e, b, cap, HEAD_W), F32)],
        compiler_params=_cparams(1),
    )(sel, aff, h2)


def _sc_dispatch(table, sel, aff, cap, n_experts):
    info = plsc.get_sparse_core_info()
    n_workers = info.num_cores * info.num_subcores
    lanes = info.num_lanes
    pairs, n = sel.shape
    b_tot = pairs // n_experts
    w = table.shape[1]
    per_w = pairs // n_workers
    ch = min(SC_CHUNK, cap)
    n_ch = cap // ch
    assert pairs % n_workers == 0 and cap % ch == 0 and ch % 8 == 0 and n % lanes == 0
    mesh = plsc.VectorSubcoreMesh(core_axis_name="c", subcore_axis_name="s")

    @functools.partial(
        pl.kernel, mesh=mesh, name="sc_dispatch",
        compiler_params=pltpu.CompilerParams(needs_layout_passes=False),
        out_type=[jax.ShapeDtypeStruct((pairs * cap, w), jnp.int32), jax.ShapeDtypeStruct((pairs * cap,), F32)],
        scratch_types=[pltpu.VMEM((n,), jnp.int32), pltpu.VMEM((n,), F32), pltpu.VMEM((n_ch, ch), jnp.int32),
                       pltpu.VMEM((cap,), F32), pltpu.VMEM((2, ch, w), jnp.int32), pltpu.SemaphoreType.DMA((2,))],
    )
    def k(table_hbm, sel_hbm, aff_hbm, xs_hbm, gate_hbm, sel_v, aff_v, tok_v, gate_v, rows_v, sem):
        wid = lax.axis_index("s") * info.num_cores + lax.axis_index("c")

        @pl.loop(0, per_w)
        def _(i):
            p = wid * per_w + i
            b = p // n_experts
            e = p - b * n_experts
            out_off = pl.multiple_of((e * b_tot + b) * cap, 8)
            pltpu.sync_copy(sel_hbm.at[p], sel_v)
            pltpu.sync_copy(aff_hbm.at[p], aff_v)
            lane = lax.iota(jnp.int32, lanes)

            @plsc.parallel_loop(0, n // lanes, unroll=4)
            def _(j):
                t0 = pl.multiple_of(j * lanes, lanes)
                s = sel_v[pl.ds(t0, lanes)]
                chosen = s >= 0
                slot = jnp.maximum(s, 0)
                plsc.store_scatter(tok_v, [slot // ch, slot % ch], lane + (t0 + b * n), mask=chosen)
                plsc.store_scatter(gate_v, [slot], aff_v[pl.ds(t0, lanes)], mask=chosen)

            prev = None
            for c in range(n_ch + 1):
                cur = None
                if c < n_ch:
                    cur = pltpu.async_copy(table_hbm.at[tok_v.at[c]], rows_v.at[c % 2], sem.at[c % 2])
                if prev is not None:
                    prev.wait()
                    pltpu.sync_copy(rows_v.at[(c - 1) % 2], xs_hbm.at[pl.ds(out_off + (c - 1) * ch, ch)])
                prev = cur
            pltpu.sync_copy(gate_v, gate_hbm.at[pl.ds(out_off, cap)])

    return k(table, sel, aff)


def _expert_kernel(packed, x_ref, g_ref, wg_ref, wu_ref, wd_ref, y_ref, wg_s, wu_s, wd_s):
    @pl.when(pl.program_id(1) == 0)
    def _():
        wg_s[...] = wg_ref[0, 0].astype(BF)
        wu_s[...] = wu_ref[0, 0].astype(BF)
        wd_s[...] = wd_ref[0, 0].astype(BF)

    x = _unpack_halves(x_ref[0]) if packed else x_ref[0]
    a = _dot(x, wg_s[...])
    u = _dot(x, wu_s[...])
    mid = (a * (1.0 / (1.0 + jnp.exp(-a))) * u).astype(BF)
    y_ref[0] = (_dot(mid, wd_s[...]) * g_ref[0][:, 0:1]).astype(BF)


def _experts(l, xs, gate, w_e_gate, w_e_up, w_e_down):
    e, rows, w = xs.shape
    d, ff = w_e_gate.shape[2:]
    rb = math.gcd(rows, EXPERT_ROWS)
    w_idx = lambda i, j: (l, i, 0, 0)
    blk = lambda i, j: (i, j, 0)
    return pl.pallas_call(
        functools.partial(_expert_kernel, xs.dtype == jnp.int32),
        name="experts_%d" % rows,
        grid=(e, rows // rb),
        in_specs=[
            pl.BlockSpec((1, rb, w), blk),
            pl.BlockSpec((1, rb, HEAD_W), blk),
            pl.BlockSpec((1, 1, d, ff), w_idx),
            pl.BlockSpec((1, 1, d, ff), w_idx),
            pl.BlockSpec((1, 1, ff, d), w_idx),
        ],
        out_specs=pl.BlockSpec((1, rb, d), blk),
        out_shape=jax.ShapeDtypeStruct((e, rows, d), BF),
        scratch_shapes=[pltpu.VMEM((d, ff), BF), pltpu.VMEM((d, ff), BF), pltpu.VMEM((ff, d), BF)],
        compiler_params=_cparams(2),
    )(xs, gate, w_e_gate, w_e_up, w_e_down)


def _scatter_kernel(cap, d, final, *refs):
    if final:
        x_ref, mod_ref, selt_ref, ye_ref, gf_ref, o_ref, pt_ref = refs
    else:
        x_ref, mod_ref, selt_ref, ye_ref, o_ref, pt_ref = refs
    e_tot = ye_ref.shape[0]
    tn = x_ref.shape[1]
    selt = selt_ref[0]
    if cap % 128 == 0:
        lane = lax.broadcasted_iota(jnp.int32, (tn, cap), 1)
        for e_idx in range(e_tot):
            pt_ref[:, e_idx * cap:(e_idx + 1) * cap] = jnp.where(selt[:, e_idx:e_idx + 1] == lane, 1.0, 0.0).astype(BF)
    else:
        lane = lax.broadcasted_iota(jnp.int32, (tn, e_tot * cap), 1)
        acc = jnp.zeros((tn, e_tot * cap), F32)
        for e_idx in range(e_tot):
            s = selt[:, e_idx:e_idx + 1]
            acc = acc + jnp.where((s >= 0) & (s + e_idx * cap == lane), 1.0, 0.0)
        pt_ref[...] = acc.astype(BF)
    y = _dot(pt_ref[...], ye_ref[...].reshape(e_tot * cap, d))
    gt = mod_ref[0, :, 5 * d:6 * d]
    out = x_ref[0] + gt * y
    if final:
        out = _rms(out, gf_ref[...])
    o_ref[0] = out


def _scatter(x, modv, sel_t, ye, cap, g_final):
    b, n, d = x.shape
    e = ye.shape[0]
    tn = min(PRE_ROWS, n)
    bm = modv.shape[0]
    mod_idx = (lambda i, j: (i, 0, 0)) if bm > 1 else (lambda i, j: (0, 0, 0))
    final = g_final is not None
    in_specs = [
        pl.BlockSpec((1, tn, d), lambda i, j: (i, j, 0)),
        pl.BlockSpec((1, 1, modv.shape[2]), mod_idx),
        pl.BlockSpec((1, tn, e), lambda i, j: (i, j, 0)),
        pl.BlockSpec((e, 1, cap, d), lambda i, j: (0, i, 0, 0)),
    ]
    args = [x, modv, sel_t, ye]
    if final:
        in_specs.append(pl.BlockSpec((1, d), lambda i, j: (0, 0)))
        args.append(g_final.reshape(1, d))
    return pl.pallas_call(
        functools.partial(_scatter_kernel, cap, d, final),
        name="scatter_%d" % n,
        grid=(b, n // tn),
        in_specs=in_specs,
        out_specs=pl.BlockSpec((1, tn, d), lambda i, j: (i, j, 0)),
        out_shape=jax.ShapeDtypeStruct((b, n, d), F32),
        scratch_shapes=[pltpu.VMEM((tn, e * cap), BF)],
        compiler_params=_cparams(2),
    )(*args)


def _dft_tables(n):
    j = np.arange(n, dtype=np.int64)
    ang = ((j[:, None] * j[None, :]) % n).astype(np.float64) * (2.0 * math.pi / n)
    return jnp.asarray(np.cos(ang), dtype=BF), jnp.asarray(np.sin(ang), dtype=BF)


def _block_diag(blocks):
    g, r, c = blocks.shape
    out = jnp.zeros((g * r, g * c), blocks.dtype)
    for i in range(g):
        out = out.at[i * r:(i + 1) * r, i * c:(i + 1) * c].set(blocks[i])
    return out


def _rope_tables(n):
    rows = n // GRID_W
    row = jnp.repeat(jnp.arange(rows, dtype=F32), GRID_W)
    col = jnp.tile(jnp.arange(GRID_W, dtype=F32), rows)

    def ang(dim):
        nf = dim // 4
        inv = ROPE_BASE ** (-jnp.arange(nf, dtype=F32) / nf)
        return jnp.concatenate([row[:, None] * inv, col[:, None] * inv], axis=-1)

    a = ang(DIFF_DH)
    cos_d = jnp.tile(jnp.cos(a), (1, 4))
    sin_d = jnp.tile(jnp.concatenate([-jnp.sin(a), jnp.sin(a)], axis=1), (1, 2))
    a = ang(MLA_ROPE)
    ones = jnp.ones((n, MLA_NOPE), F32)
    pad = HEAD_W - MLA_NOPE - MLA_ROPE
    cos_m = jnp.concatenate([ones, jnp.cos(a), jnp.cos(a), jnp.ones((n, pad), F32)], axis=1)
    sin_m = jnp.concatenate([0 * ones, -jnp.sin(a), jnp.sin(a), jnp.zeros((n, pad), F32)], axis=1)
    return cos_d, sin_d, cos_m, sin_m


def _pad_heads(w, heads, lo, hi):
    k = w.shape[0]
    w3 = w.reshape(k, heads, -1)[:, :, lo:hi]
    return jnp.pad(w3, ((0, 0), (0, 0), (0, HEAD_W - (hi - lo)))).reshape(k, heads * HEAD_W)


def kernel(x_prompt, x_sample, cache_diff_k, cache_diff_v, cache_mla_ckv, cache_mla_kpe, c, c_ctx, w_ada, b_ada, g_attn, g_ffn, w_in, w_four, diff_lambda, g_subln, g_mla_q, w_mla_uq, g_mla_kv, w_mla_ukv, w_out, w_router, w_e_gate, w_e_up, w_e_down, g_final):
    depth, d = g_attn.shape
    b_ctx, n_ctx, _ = x_prompt.shape
    b_lat, n_lat, _ = x_sample.shape

    r = b_lat + 1
    r_pad = -(-r // 8) * 8
    cond = jnp.concatenate([c, c_ctx[None, :], jnp.zeros((r_pad - r, d), F32)], axis=0)
    mod = _modulation(cond, w_ada, b_ada)

    n_main = w_in.shape[2] - MLA_ROPE
    w_in_p = jnp.concatenate(
        [w_in[:, :, :n_main], jnp.zeros((depth, d, MLA_NOPE), F32), w_in[:, :, n_main:],
         jnp.zeros((depth, d, HEAD_W - MLA_NOPE - MLA_ROPE), F32)], axis=2).astype(BF)
    w_uq_p = jnp.stack([_pad_heads(w_mla_uq[l], MLA_HEADS, 0, MLA_NOPE + MLA_ROPE) for l in range(depth)]).astype(BF)
    w_k_p = jnp.stack([_pad_heads(w_mla_ukv[l], MLA_HEADS, 0, MLA_NOPE) for l in range(depth)]).astype(BF)
    w_v_p = w_mla_ukv.reshape(depth, MLA_KV_RANK, MLA_HEADS, MLA_NOPE + MLA_V)[..., MLA_NOPE:].reshape(depth, MLA_KV_RANK, D_MLA).astype(BF)
    w_out_p = w_out.astype(BF)
    w_router_t = jnp.swapaxes(w_router, 1, 2).astype(BF)
    place = jnp.zeros((MLA_ROPE, HEAD_W), F32).at[jnp.arange(MLA_ROPE), MLA_NOPE + jnp.arange(MLA_ROPE)].set(1.0)
    place = jnp.tile(place, (1, MLA_HEADS)).astype(BF)

    jc = jnp.arange(F_CH, dtype=jnp.int32)
    ang_c = ((jc[:, None] * jc[None, :]) % F_CH).astype(F32) * (2.0 * math.pi / F_CH)
    cc_bd = _block_diag(jnp.broadcast_to(jnp.cos(ang_c), (F_GROUPS, F_CH, F_CH))).astype(BF)
    sc_bd = _block_diag(jnp.broadcast_to(jnp.sin(ang_c), (F_GROUPS, F_CH, F_CH))).astype(BF)
    dft = {n: _dft_tables(n) for n in {n_ctx, n_lat}}
    ropes = _rope_tables(n_lat)

    def pre(latent, l, x, prev=None, combine=None):
        modv = (mod[l, :b_lat] if latent else mod[l, b_lat:b_lat + 1]).reshape(-1, 1, N_MOD * d)
        outs = _pre(latent, x, modv, g_attn[l], w_in_p[l], g_mla_q[l], w_uq_p[l], g_mla_kv[l], w_k_p[l], w_v_p[l],
                    cc_bd, sc_bd, ropes, l, depth, prev, combine)
        if combine is not None:
            x, outs = outs[0], outs[1:]
        return x, modv, outs

    def mix_and_route(latent, l, x, modv, outs):
        b, n, _ = x.shape
        cap = max(1, EC_CAPACITY_FACTOR * n // N_EXPERTS)
        u, v, qd, kd, vd, qm, km, vm = outs[:8]
        cache = _cache_prep(l, cache_diff_k, cache_diff_v, cache_mla_ckv, cache_mla_kpe, w_k_p[l], w_v_p[l], place) if latent else None
        cn, sn = dft[n]
        yfour = _fourier(u, v, cn, sn, _block_diag(w_four[l]).astype(BF))
        x = _attention(latent, l, x, modv, yfour, qd, kd, vd, qm, km, vm, cache, diff_lambda[l], g_subln[l], w_out_p[l])
        h2, aff = _router(x, modv, g_ffn[l], w_router_t[l], packed=latent)
        sel = _select(aff, cap)
        e = sel.shape[1]
        if latent:
            xs, gate = _sc_dispatch(h2.reshape(b * n, h2.shape[2]), sel.reshape(b * e, n), aff.reshape(b * e, n), cap, e)
            gate = jnp.broadcast_to(gate.reshape(e, b * cap, 1), (e, b * cap, HEAD_W))
        else:
            xs, gate = _gather(sel, aff, h2, cap)
            gate = gate.reshape(e, b * cap, HEAD_W)
        return x, jnp.swapaxes(sel, 1, 2), cap, xs.reshape(e, b * cap, -1), gate

    x_l, mod_l, pre_l = pre(True, 0, x_sample)
    x_c, mod_c, pre_c = pre(False, 0, x_prompt)
    for l in range(depth):
        new = tuple(pre_c[8:])
        x_l, selt_l, cap_l, xs_l, gate_l = mix_and_route(True, l, x_l, mod_l, pre_l)
        x_c, selt_c, cap_c, xs_c, gate_c = mix_and_route(False, l, x_c, mod_c, pre_c)
        ye_c = _experts(l, xs_c, gate_c, w_e_gate, w_e_up, w_e_down).reshape(-1, b_ctx, cap_c, d)
        if l + 1 < depth:
            x_c, mod_c, pre_c = pre(False, l + 1, x_c, new, (mod_c, selt_c, ye_c, cap_c))
        else:
            x_c = _scatter(x_c, mod_c, selt_c, ye_c, cap_c, g_final)
        ye_l = _experts(l, xs_l, gate_l, w_e_gate, w_e_up, w_e_down).reshape(-1, b_lat, cap_l, d)
        if l + 1 < depth:
            x_l, mod_l, pre_l = pre(True, l + 1, x_l, None, (mod_l, selt_l, ye_l, cap_l))
        else:
            x_l = _scatter(x_l, mod_l, selt_l, ye_l, cap_l, g_final)
    y_prompt, y_sample = x_c, x_l
    new_diff_k, new_diff_v, new_mla_ckv, new_mla_kpe = new
    return (y_prompt, y_sample, new_diff_k, new_diff_v, new_mla_ckv, new_mla_kpe)
```

```python
import functools
import math

import jax
import jax.numpy as jnp
import numpy as np
from jax import lax
from jax.experimental import pallas as pl
from jax.experimental.pallas import tpu as pltpu
from jax.experimental.pallas import tpu_sc as plsc

BF = jnp.bfloat16
F32 = jnp.float32

GRID_W = 64
ROPE_BASE = 10000.0
NORM_EPS = 1e-6
F_GROUPS, F_CH = 4, 64
D_FOURIER = F_GROUPS * F_CH
DIFF_HEADS, DIFF_DH = 4, 64
D_DIFF = DIFF_HEADS * 2 * DIFF_DH
MLA_HEADS, MLA_Q_RANK, MLA_KV_RANK = 4, 256, 128
MLA_NOPE, MLA_ROPE, MLA_V = 64, 32, 64
HEAD_W = 128
D_MLA_P = MLA_HEADS * HEAD_W
D_MLA = MLA_HEADS * MLA_V
D_MIX = D_FOURIER + D_DIFF + D_MLA
KEY_CHUNK = 256
SCORE_BUFS = 2
BF_ROWS = 16
SELECT_ROWS = 128
EXPERT_ROWS = 1024
EXPERT_SUB_ROWS = 512
PRE_ROWS = 512
SC_CHUNK = 64
SUB_ROWS = 256
N_EXPERTS = 16
EC_CAPACITY_FACTOR = 2
N_MOD = 6
IN_COLS_P = D_FOURIER + 3 * D_DIFF + MLA_Q_RANK + MLA_KV_RANK + HEAD_W
LOG2E = 1.4426950408889634
VMEM_LIMIT = 56 * 1024 * 1024


def _cparams(n_axes, vmem=VMEM_LIMIT):
    return pltpu.CompilerParams(dimension_semantics=("arbitrary",) * n_axes, vmem_limit_bytes=vmem)


def _dot(a, b):
    return jnp.dot(a, b, preferred_element_type=F32)


def _dot_nt(a, b):
    return lax.dot_general(a, b, (((1,), (1,)), ((), ())), preferred_element_type=F32)


def _rms(x, g):
    return x * lax.rsqrt(jnp.mean(x * x, axis=-1, keepdims=True) + NORM_EPS) * g


def _pack_halves(hb):
    w = hb.shape[1] // 2
    bits = pltpu.bitcast(hb.astype(F32), jnp.uint32)
    packed = (bits[:, :w] >> 16) | (bits[:, w:] & jnp.uint32(0xFFFF0000))
    return pltpu.bitcast(packed, jnp.int32)


def _unpack_halves(xi):
    bits = pltpu.bitcast(xi, jnp.uint32)
    lo = pltpu.bitcast(bits << 16, F32).astype(BF)
    hi = pltpu.bitcast(bits & jnp.uint32(0xFFFF0000), F32).astype(BF)
    return jnp.concatenate([lo, hi], axis=1)


def _rope(z, cos, sin_signed, half, group, lo):
    w = z.shape[1]
    reps = w // cos.shape[1]
    cos_w = jnp.concatenate([cos] * reps, axis=1) if reps > 1 else cos
    sin_w = jnp.concatenate([sin_signed] * reps, axis=1) if reps > 1 else sin_signed
    from_right = pltpu.roll(z, w - half, 1)
    from_left = pltpu.roll(z, half, 1)
    lane = lax.broadcasted_iota(jnp.int32, z.shape, 1) % group
    first = (lane >= lo) & (lane < lo + half)
    partner = jnp.where(first, from_right, from_left)
    return z * cos_w + partner * sin_w


def _mod_kernel(c_ref, w_ref, b_ref, o_ref):
    c = c_ref[...]
    a = (c * (1.0 / (1.0 + jnp.exp(-c)))).astype(BF)
    o_ref[0] = _dot(a, w_ref[0].astype(BF)) + b_ref[0]


def _modulation(cond, w_ada, b_ada):
    depth, d, n6 = w_ada.shape
    r = cond.shape[0]
    tn = 1536
    return pl.pallas_call(
        _mod_kernel,
        name="modulation",
        grid=(depth, n6 // tn),
        in_specs=[
            pl.BlockSpec((r, d), lambda l, j: (0, 0)),
            pl.BlockSpec((1, d, tn), lambda l, j: (l, 0, j)),
            pl.BlockSpec((1, 1, tn), lambda l, j: (l, 0, j)),
        ],
        out_specs=pl.BlockSpec((1, r, tn), lambda l, j: (l, 0, j)),
        out_shape=jax.ShapeDtypeStruct((depth, r, n6), F32),
        compiler_params=_cparams(2),
    )(cond, w_ada, b_ada.reshape(depth, 1, n6))


def _pre_kernel(latent, d, *refs):
    if latent:
        (x_ref, mod_ref, g_ref, win_ref, gq_ref, wuq_ref, gkv_ref, wk_ref, wv_ref, cc_ref, sc_ref,
         cd_ref, sd_ref, cm_ref, sm_ref,
         u_ref, v_ref, qd_ref, kd_ref, vd_ref, qm_ref, km_ref, vm_ref) = refs
    else:
        (x_ref, mod_ref, g_ref, win_ref, gq_ref, wuq_ref, gkv_ref, wk_ref, wv_ref, cc_ref, sc_ref,
         u_ref, v_ref, qd_ref, kd_ref, vd_ref, qm_ref, km_ref, vm_ref,
         k32_ref, v32_ref, ckv32_ref, kpe32_ref) = refs[:11] + refs[-12:]
    sh = mod_ref[0, :, 0:d]
    sc = mod_ref[0, :, d:2 * d]
    tm = x_ref.shape[1]
    sub = SUB_ROWS if tm > SUB_ROWS else max(tm // 2, BF_ROWS)
    def project(r0):
        h = _rms(x_ref[0, r0:r0 + sub, :], g_ref[...]) * (1.0 + sc) + sh
        return _dot(h.astype(BF), win_ref[...])

    starts = list(range(0, tm, sub))
    z_next = project(starts[0])
    for i, r0 in enumerate(starts):
        rows = slice(r0, r0 + sub)
        z = z_next
        z_next = project(starts[i + 1]) if i + 1 < len(starts) else None
        o = 0
        zf = z[:, o:o + D_FOURIER]; o += D_FOURIER
        zq = z[:, o:o + D_DIFF]; o += D_DIFF
        zk = z[:, o:o + D_DIFF]; o += D_DIFF
        zv = z[:, o:o + D_DIFF]; o += D_DIFF
        zcq = z[:, o:o + MLA_Q_RANK]; o += MLA_Q_RANK
        zckv = z[:, o:o + MLA_KV_RANK]; o += MLA_KV_RANK
        kpe = z[:, o:o + HEAD_W]

        zf_b = zf.astype(BF)
        u_ref[0, rows, :] = _dot(zf_b, cc_ref[...]).astype(BF)
        v_ref[0, rows, :] = _dot(zf_b, sc_ref[...]).astype(BF)
        vd_ref[0, :, rows] = zv.T.astype(BF)
        cq = _rms(zcq, gq_ref[...])
        qm = _dot(cq.astype(BF), wuq_ref[...])
        ckv = _rms(zckv, gkv_ref[...])
        ckv_b = ckv.astype(BF)
        k_nope = _dot(ckv_b, wk_ref[...])
        vm_ref[0, :, rows] = _dot(ckv_b, wv_ref[...]).T.astype(BF)
        if latent:
            cd, sd, cm, sm = cd_ref[rows, :], sd_ref[rows, :], cm_ref[rows, :], sm_ref[rows, :]
            zq_r = _rope(zq, cd, sd, DIFF_DH // 2, DIFF_DH, 0)
            zk_r = _rope(zk, cd, sd, DIFF_DH // 2, DIFF_DH, 0)
            qm = _rope(qm, cm, sm, MLA_ROPE // 2, HEAD_W, MLA_NOPE)
            kpe_r = _rope(kpe, cm, sm, MLA_ROPE // 2, HEAD_W, MLA_NOPE)
        else:
            zq_r, zk_r, kpe_r = zq, zk, kpe
            for hd in range(DIFF_HEADS):
                k32_ref[0, 0, rows, hd, :] = zk[:, hd * HEAD_W:(hd + 1) * HEAD_W]
                v32_ref[0, 0, rows, hd, :] = zv[:, hd * HEAD_W:(hd + 1) * HEAD_W]
            ckv32_ref[0, 0, rows, :] = ckv
            kpe32_ref[0, 0, rows, :] = kpe[:, MLA_NOPE:MLA_NOPE + MLA_ROPE]
            if i == 0:
                for ref in (k32_ref, v32_ref, ckv32_ref, kpe32_ref):
                    if ref.shape[1] > 1:
                        ref[0, 1:] = jnp.zeros((ref.shape[1] - 1,) + ref.shape[2:], F32)
        qd_ref[0, rows, :] = (zq_r * (DIFF_DH ** -0.5 * LOG2E)).astype(BF)
        kd_ref[0, rows, :] = zk_r.astype(BF)
        qm_ref[0, rows, :] = (qm * ((MLA_NOPE + MLA_ROPE) ** -0.5 * LOG2E)).astype(BF)
        km_ref[0, rows, :] = (k_nope + jnp.concatenate([kpe_r] * MLA_HEADS, axis=1)).astype(BF)


def _combine_pre_kernel(latent, d, cap, n_pre_in, *refs):
    x_ref, modp_ref, selt_ref, ye_ref = refs[:4]
    pre_in = refs[4:4 + n_pre_in]
    xo_ref = refs[4 + n_pre_in]
    pre_out = refs[5 + n_pre_in:-1]
    xo_ref[0] = _combine_rows(cap, d, x_ref, modp_ref, selt_ref, ye_ref, refs[-1], slice(None))
    _pre_kernel(latent, d, xo_ref, *pre_in, *pre_out)


def _pre(latent, x, modv, g_attn, w_in_p, g_q, w_uq_p, g_kv, w_k_p, w_v_p, cc_bd, sc_bd, ropes, l=0, depth=1, prev=None,
         combine=None):
    b, n, d = x.shape
    tm = min(PRE_ROWS, n)
    bm = modv.shape[0]
    mod_idx = (lambda i, j: (i, 0, 0)) if bm > 1 else (lambda i, j: (0, 0, 0))
    const = lambda i, j: (0, 0)
    tok = lambda i, j: (i, j, 0)
    in_specs = [
        pl.BlockSpec((1, tm, d), tok),
        pl.BlockSpec((1, 1, modv.shape[2]), mod_idx),
        pl.BlockSpec((1, d), const),
        pl.BlockSpec(w_in_p.shape, const),
        pl.BlockSpec((1, MLA_Q_RANK), const),
        pl.BlockSpec(w_uq_p.shape, const),
        pl.BlockSpec((1, MLA_KV_RANK), const),
        pl.BlockSpec(w_k_p.shape, const),
        pl.BlockSpec(w_v_p.shape, const),
        pl.BlockSpec(cc_bd.shape, const),
        pl.BlockSpec(sc_bd.shape, const),
    ]
    args = [x, modv, g_attn.reshape(1, d), w_in_p, g_q.reshape(1, -1), w_uq_p, g_kv.reshape(1, -1), w_k_p, w_v_p, cc_bd, sc_bd]
    if latent:
        in_specs += [pl.BlockSpec((tm, HEAD_W), lambda i, j: (j, 0))] * 4
        args += list(ropes)
    feat = lambda i, j: (i, 0, j)
    out_shape, out_specs = [], []
    for w, token_major in ((D_FOURIER, True), (D_FOURIER, True), (D_DIFF, True), (D_DIFF, True), (D_DIFF, False),
                           (D_MLA_P, True), (D_MLA_P, True), (D_MLA, False)):
        out_shape.append(jax.ShapeDtypeStruct((b, n, w) if token_major else (b, w, n), BF))
        out_specs.append(pl.BlockSpec((1, tm, w), tok) if token_major else pl.BlockSpec((1, w, tm), feat))
    aliases = {}
    if not latent:
        span = depth if prev is None else 1
        for tail in ((DIFF_HEADS, HEAD_W), (DIFF_HEADS, HEAD_W), (MLA_KV_RANK,), (MLA_ROPE,)):
            out_shape.append(jax.ShapeDtypeStruct((b, depth, n) + tail, F32))
            out_specs.append(pl.BlockSpec((1, span, tm) + tail, lambda i, j, t=len(tail): (i, l, j) + (0,) * t))
        if prev is not None:
            for k, buf in enumerate(prev):
                aliases[len(args)] = len(out_shape) - len(prev) + k
                in_specs.append(pl.BlockSpec(memory_space=pl.ANY))
                args.append(buf)
    body, scratch, tag = functools.partial(_pre_kernel, latent, d), [], "pre"
    if combine is not None:
        modp, sel_t, ye, cap = combine
        e = ye.shape[0]
        modp_idx = (lambda i, j: (i, 0, 0)) if modp.shape[0] > 1 else (lambda i, j: (0, 0, 0))
        extra_specs = [pl.BlockSpec((1, 1, modp.shape[2]), modp_idx), pl.BlockSpec((1, tm, e), tok),
                       pl.BlockSpec((e, 1, cap, d), lambda i, j: (0, i, 0, 0))]
        n_extra = len(extra_specs)
        body = functools.partial(_combine_pre_kernel, latent, d, cap, len(args) - 1)
        in_specs = in_specs[:1] + extra_specs + in_specs[1:]
        args = args[:1] + [modp, sel_t, ye] + args[1:]
        out_specs = [pl.BlockSpec((1, tm, d), tok)] + out_specs
        out_shape = [jax.ShapeDtypeStruct((b, n, d), F32)] + out_shape
        aliases = {k + n_extra: v + 1 for k, v in aliases.items()}
        scratch, tag = [pltpu.VMEM((tm, e * cap), BF)], "combine_pre"
    return pl.pallas_call(
        body,
        name=tag + ("_lat" if latent else "_ctx"),
        grid=(b, n // tm),
        in_specs=in_specs,
        out_specs=out_specs,
        out_shape=out_shape,
        input_output_aliases=aliases,
        scratch_shapes=scratch,
        compiler_params=_cparams(2),
    )(*args)


def _cache_kernel(dk_ref, dv_ref, ckv_ref, kpe_ref, wk_ref, wv_ref, place_ref, kd_ref, vd_ref, km_ref, vm_ref):
    for hd in range(DIFF_HEADS):
        sl = slice(hd * HEAD_W, (hd + 1) * HEAD_W)
        kd_ref[0, :, sl] = dk_ref[0, 0, :, hd, :].astype(BF)
        vd_ref[0, sl, :] = dv_ref[0, 0, :, hd, :].T.astype(BF)
    ckv_b = ckv_ref[0, 0].astype(BF)
    kpe_w = _dot(kpe_ref[0, 0].astype(BF), place_ref[...])
    km_ref[0] = (_dot(ckv_b, wk_ref[...]) + kpe_w).astype(BF)
    vm_ref[0] = _dot(ckv_b, wv_ref[...]).T.astype(BF)


def _cache_prep(l, cache_dk, cache_dv, cache_ckv, cache_kpe, w_k_p, w_v_p, place):
    b, _, p = cache_dk.shape[:3]
    at_l = lambda i: (i, l, 0, 0)
    at_l5 = lambda i: (i, l, 0, 0, 0)
    const = lambda i: (0, 0)
    shapes = [(p, D_DIFF), (D_DIFF, p), (p, D_MLA_P), (D_MLA, p)]
    return pl.pallas_call(
        _cache_kernel,
        name="cache_prep",
        grid=(b,),
        in_specs=[
            pl.BlockSpec((1, 1, p, DIFF_HEADS, HEAD_W), at_l5),
            pl.BlockSpec((1, 1, p, DIFF_HEADS, HEAD_W), at_l5),
            pl.BlockSpec((1, 1, p, MLA_KV_RANK), at_l),
            pl.BlockSpec((1, 1, p, MLA_ROPE), at_l),
            pl.BlockSpec(w_k_p.shape, const),
            pl.BlockSpec(w_v_p.shape, const),
            pl.BlockSpec(place.shape, const),
        ],
        out_specs=[pl.BlockSpec((1,) + s, lambda i: (i, 0, 0)) for s in shapes],
        out_shape=[jax.ShapeDtypeStruct((b,) + s, BF) for s in shapes],
        compiler_params=_cparams(1),
    )(cache_dk, cache_dv, cache_ckv, cache_kpe, w_k_p, w_v_p, place)


def _fourier_kernel(scale, u_ref, v_ref, cn_ref, sn_ref, w_ref, o_ref):
    f = (_dot(cn_ref[...], u_ref[0]) - _dot(sn_ref[...], v_ref[0])) * scale
    o_ref[0] = _dot(f.astype(BF), w_ref[...]).astype(BF)


def _fourier(u, v, cn, sn, w_bd):
    b, n, w = u.shape
    tm = min(512, n)
    return pl.pallas_call(
        functools.partial(_fourier_kernel, 1.0 / math.sqrt(n * F_CH)),
        name="fourier_%d" % n,
        grid=(n // tm, b),
        in_specs=[
            pl.BlockSpec((1, n, w), lambda i, j: (j, 0, 0)),
            pl.BlockSpec((1, n, w), lambda i, j: (j, 0, 0)),
            pl.BlockSpec((tm, n), lambda i, j: (i, 0)),
            pl.BlockSpec((tm, n), lambda i, j: (i, 0)),
            pl.BlockSpec((w, w), lambda i, j: (0, 0)),
        ],
        out_specs=pl.BlockSpec((1, tm, w), lambda i, j: (j, i, 0)),
        out_shape=jax.ShapeDtypeStruct((b, n, w), BF),
        compiler_params=_cparams(2),
    )(u, v, cn, sn, w_bd)


def _scores(q, k_parts, s_ref):
    off, m, bounds = 0, None, []
    for k in k_parts:
        s = _dot_nt(k(), q)
        n_i = s.shape[0]
        s_ref[off:off + n_i, :] = s
        mi = jnp.max(s, axis=0, keepdims=True)
        m = mi if m is None else jnp.maximum(m, mi)
        bounds.append((off, n_i))
        off += n_i
    return m, bounds


def _probs(m, total, s_ref, p_ref):
    for r in range(0, total, KEY_CHUNK):
        rows = min(KEY_CHUNK, total - r)
        p_ref[r:r + rows, :] = jnp.exp2(s_ref[r:r + rows, :] - m).astype(BF)


def _weighted(vt_parts, bounds, p_ref):
    o = None
    for vt, (r, n_i) in zip(vt_parts, bounds):
        v = vt()
        ones = jnp.where(lax.broadcasted_iota(jnp.int32, (BF_ROWS, n_i), 0) == 0, 1.0, 0.0).astype(BF)
        oi = _dot(jnp.concatenate([v, ones], axis=0), p_ref[r:r + n_i, :])
        o = oi if o is None else o + oi
    dv = o.shape[0] - BF_ROWS
    return o[:dv] * (1.0 / o[dv:dv + 1])


def _attn_kernel(latent, lam_init, d, *refs):
    if latent:
        (x_ref, mod_ref, yf_ref, qd_ref, kd_ref, vd_ref, kdc_ref, vdc_ref, qm_ref, km_ref, vm_ref, kmc_ref, vmc_ref,
         lam_ref, gs_ref, wo_ref, o_ref, mix_ref, s_ref, p_ref) = refs
    else:
        (x_ref, mod_ref, yf_ref, qd_ref, kd_ref, vd_ref, qm_ref, km_ref, vm_ref,
         lam_ref, gs_ref, wo_ref, o_ref, mix_ref, s_ref, p_ref) = refs
        kdc_ref = vdc_ref = kmc_ref = vmc_ref = None
    lf = lam_ref[...]
    lam = (jnp.exp(jnp.sum(lf[0:1] * lf[1:2], axis=-1, keepdims=True))
           - jnp.exp(jnp.sum(lf[2:3] * lf[3:4], axis=-1, keepdims=True)) + lam_init)
    mix_ref[:, 0:D_FOURIER] = yf_ref[0]
    gs = gs_ref[...]

    def keys(ref, cache_ref, sl):
        parts = [lambda: ref[0, :, sl]]
        if latent:
            parts.append(lambda: cache_ref[0, :, sl])
        return parts

    def values_t(ref, cache_ref, sl):
        parts = [lambda: ref[0, sl, :]]
        if latent:
            parts.append(lambda: cache_ref[0, sl, :])
        return parts

    def diff_query(sl, comp):
        def load():
            qh = qd_ref[0, :, sl]
            lane = lax.broadcasted_iota(jnp.int32, qh.shape, 1)
            keep = (lane >= DIFF_DH) if comp else (lane < DIFF_DH)
            return jnp.where(keep, qh, jnp.zeros_like(qh))
        return load

    items = []
    for hd in range(DIFF_HEADS):
        sl = slice(hd * HEAD_W, (hd + 1) * HEAD_W)
        for comp in range(2):
            items.append((diff_query(sl, comp), keys(kd_ref, kdc_ref, sl), values_t(vd_ref, vdc_ref, sl)))
    for hd in range(MLA_HEADS):
        sl = slice(hd * HEAD_W, (hd + 1) * HEAD_W)
        vsl = slice(hd * MLA_V, (hd + 1) * MLA_V)
        items.append(((lambda sl=sl: qm_ref[0, :, sl]), keys(km_ref, kmc_ref, sl), values_t(vm_ref, vmc_ref, vsl)))

    def start(i):
        return _scores(items[i][0](), items[i][1], s_ref.at[i % SCORE_BUFS])

    total = s_ref.shape[1]
    outs = []
    state = start(0)
    for i in range(len(items)):
        nxt = start(i + 1) if i + 1 < len(items) else None
        m, bounds = state
        buf = i % SCORE_BUFS
        _probs(m, total, s_ref.at[buf], p_ref.at[buf])
        outs.append(_weighted(items[i][2], bounds, p_ref.at[buf]))
        state = nxt

    for hd in range(DIFF_HEADS):
        o = outs[2 * hd] - lam * outs[2 * hd + 1]
        o = o * lax.rsqrt(jnp.mean(o * o, axis=0, keepdims=True) + NORM_EPS) * gs * (1.0 - lam_init)
        mix_ref[:, D_FOURIER + hd * HEAD_W:D_FOURIER + (hd + 1) * HEAD_W] = o.T.astype(BF)
    mix_ref[:, D_FOURIER + D_DIFF:D_MIX] = jnp.concatenate(outs[2 * DIFF_HEADS:], axis=0).T.astype(BF)
    y = _dot(mix_ref[...], wo_ref[...])
    gt = mod_ref[0, :, 2 * d:3 * d]
    o_ref[0] = x_ref[0] + gt * y


def _attention(latent, l, x, modv, yfour, qd, kd, vd, qm, km, vm, cache, diff_lambda_l, g_subln_l, w_out_b):
    b, n, d = x.shape
    tq = min(256, n)
    bm = modv.shape[0]
    mod_idx = (lambda i, j: (i, 0, 0)) if bm > 1 else (lambda i, j: (0, 0, 0))
    const = lambda i, j: (0, 0)
    tok = lambda i, j: (i, j, 0)
    full = lambda i, j: (i, 0, 0)
    p = cache[0].shape[1] if latent else 0
    in_specs = [
        pl.BlockSpec((1, tq, d), tok),
        pl.BlockSpec((1, 1, modv.shape[2]), mod_idx),
        pl.BlockSpec((1, tq, D_FOURIER), tok),
        pl.BlockSpec((1, tq, D_DIFF), tok),
        pl.BlockSpec((1, n, D_DIFF), full),
        pl.BlockSpec((1, D_DIFF, n), full),
    ]
    args = [x, modv, yfour, qd, kd, vd]
    if latent:
        kdc, vdc, kmc, vmc = cache
        in_specs += [pl.BlockSpec((1, p, D_DIFF), full), pl.BlockSpec((1, D_DIFF, p), full)]
        args += [kdc, vdc]
    in_specs += [pl.BlockSpec((1, tq, D_MLA_P), tok), pl.BlockSpec((1, n, D_MLA_P), full), pl.BlockSpec((1, D_MLA, n), full)]
    args += [qm, km, vm]
    if latent:
        in_specs += [pl.BlockSpec((1, p, D_MLA_P), full), pl.BlockSpec((1, D_MLA, p), full)]
        args += [kmc, vmc]
    in_specs += [pl.BlockSpec((4, DIFF_DH), const), pl.BlockSpec((HEAD_W, 1), const), pl.BlockSpec(w_out_b.shape, const)]
    args += [diff_lambda_l, g_subln_l.reshape(HEAD_W, 1), w_out_b]
    lam_init = 0.8 - 0.6 * math.exp(-0.3 * l)
    return pl.pallas_call(
        functools.partial(_attn_kernel, latent, lam_init, d),
        name="attn_lat" if latent else "attn_ctx",
        grid=(b, n // tq),
        in_specs=in_specs,
        out_specs=pl.BlockSpec((1, tq, d), tok),
        out_shape=jax.ShapeDtypeStruct((b, n, d), F32),
        scratch_shapes=[pltpu.VMEM((tq, D_MIX), BF), pltpu.VMEM((SCORE_BUFS, n + p, tq), F32),
                        pltpu.VMEM((SCORE_BUFS, n + p, tq), BF)],
        compiler_params=_cparams(2),
    )(*args)


def _router_kernel(d, packed, x_ref, mod_ref, g_ref, wr_ref, h_ref, aff_ref):
    sh = mod_ref[0, :, 3 * d:4 * d]
    sc = mod_ref[0, :, 4 * d:5 * d]
    tm = x_ref.shape[1]
    sub = SUB_ROWS if tm > SUB_ROWS else max(tm // 2, BF_ROWS)
    for r0 in range(0, tm, sub):
        rows = slice(r0, r0 + sub)
        hb = (_rms(x_ref[0, rows, :], g_ref[...]) * (1.0 + sc) + sh).astype(BF)
        h_ref[0, rows, :] = _pack_halves(hb) if packed else hb
        logits = _dot_nt(wr_ref[...], hb)
        e = jnp.exp(logits - jnp.max(logits, axis=0, keepdims=True))
        aff_ref[0, :, rows] = e / jnp.sum(e, axis=0, keepdims=True)


def _router(x, modv, g_ffn_l, w_router_t, packed):
    b, n, d = x.shape
    hw, hdt = (d // 2, jnp.int32) if packed else (d, BF)
    tm = min(PRE_ROWS, n)
    bm = modv.shape[0]
    mod_idx = (lambda i, j: (i, 0, 0)) if bm > 1 else (lambda i, j: (0, 0, 0))
    const = lambda i, j: (0, 0)
    return pl.pallas_call(
        functools.partial(_router_kernel, d, packed),
        name="router_%d" % n,
        grid=(b, n // tm),
        in_specs=[
            pl.BlockSpec((1, tm, d), lambda i, j: (i, j, 0)),
            pl.BlockSpec((1, 1, modv.shape[2]), mod_idx),
            pl.BlockSpec((1, d), const),
            pl.BlockSpec(w_router_t.shape, const),
        ],
        out_specs=[pl.BlockSpec((1, tm, hw), lambda i, j: (i, j, 0)), pl.BlockSpec((1, N_EXPERTS, tm), lambda i, j: (i, 0, j))],
        out_shape=[jax.ShapeDtypeStruct((b, n, hw), hdt), jax.ShapeDtypeStruct((b, N_EXPERTS, n), F32)],
        compiler_params=_cparams(2),
    )(x, modv, g_ffn_l.reshape(1, d), w_router_t)


def _select_kernel(cap, aff_ref, sel_ref):
    a = aff_ref[...]
    e, n = a.shape
    bits = pltpu.bitcast(a, jnp.int32)
    capf = float(cap)

    def body(_, carry):
        lo, hi = carry
        mid = lo + ((hi - lo) >> 1)
        cnt = jnp.sum(jnp.where(bits >= mid, 1.0, 0.0), axis=1, keepdims=True)
        up = cnt >= capf
        return jnp.where(up, mid, lo), jnp.where(up, hi, mid)

    lo0 = jnp.zeros((e, 1), jnp.int32)
    hi0 = jnp.full((e, 1), 0x7F800000, jnp.int32)
    thr, _ = lax.fori_loop(0, 31, body, (lo0, hi0))
    gt = bits > thr
    eq = bits == thr
    need = capf - jnp.sum(jnp.where(gt, 1.0, 0.0), axis=1, keepdims=True)
    both = jnp.concatenate([jnp.where(gt, 1.0, 0.0), jnp.where(eq, 1.0, 0.0)], axis=0).astype(BF)
    ck = min(256, n)
    tri = jnp.where(lax.broadcasted_iota(jnp.int32, (ck, ck), 0) < lax.broadcasted_iota(jnp.int32, (ck, ck), 1), 1.0, 0.0).astype(BF)
    off = jnp.zeros((2 * e, 1), F32)
    pieces = []
    for k in range(n // ck):
        blk = both[:, k * ck:(k + 1) * ck]
        pieces.append(_dot(blk, tri) + off)
        off = off + jnp.sum(blk.astype(F32), axis=1, keepdims=True)
    cum = jnp.concatenate(pieces, axis=1) if len(pieces) > 1 else pieces[0]
    cum_gt, cum_eq = cum[:e], cum[e:]
    chosen = gt | (eq & (cum_eq < need))
    pos = cum_gt + jnp.minimum(cum_eq, need)
    sel_ref[...] = jnp.where(chosen, pos, -1.0).astype(jnp.int32)


def _select(aff, cap):
    b, e, n = aff.shape
    rows = b * e
    rb = min(rows, SELECT_ROWS)
    sel = pl.pallas_call(
        functools.partial(_select_kernel, cap),
        name="select_%d" % n,
        grid=(rows // rb,),
        in_specs=[pl.BlockSpec((rb, n), lambda i: (i, 0))],
        out_specs=pl.BlockSpec((rb, n), lambda i: (i, 0)),
        out_shape=jax.ShapeDtypeStruct((rows, n), jnp.int32),
        compiler_params=_cparams(1),
    )(aff.reshape(rows, n))
    return sel.reshape(b, e, n)


def _gather_kernel(cap, sel_ref, aff_ref, h_ref, xs_ref, gate_ref):
    h = h_ref[0]
    n = h.shape[0]
    e_tot = sel_ref.shape[1]

    def one(e_idx):
        sel_e = sel_ref[0, pl.ds(e_idx, 1), :]
        aff_e = aff_ref[0, pl.ds(e_idx, 1), :]
        hit = sel_e == lax.broadcasted_iota(jnp.int32, (cap, n), 0)
        g = jnp.sum(jnp.where(hit, aff_e, 0.0), axis=1, keepdims=True)
        return jnp.where(hit, 1.0, 0.0).astype(BF), jnp.broadcast_to(g, (cap, HEAD_W))

    if cap >= 128:
        def body(e_idx, carry):
            p, g = one(e_idx)
            xs_ref[e_idx, 0] = _dot(p, h).astype(BF)
            gate_ref[e_idx, 0] = g
            return carry

        lax.fori_loop(0, e_tot, body, 0)
    else:
        ps, gs = zip(*[one(e_idx) for e_idx in range(e_tot)])
        xs = _dot(jnp.concatenate(ps, axis=0), h).astype(BF)
        for e_idx in range(e_tot):
            xs_ref[e_idx, 0] = xs[e_idx * cap:(e_idx + 1) * cap]
            gate_ref[e_idx, 0] = gs[e_idx]


def _gather(sel, aff, h2, cap):
    b, e, n = sel.shape
    d = h2.shape[2]
    return pl.pallas_call(
        functools.partial(_gather_kernel, cap),
        name="gather_%d" % n,
        grid=(b,),
        in_specs=[
            pl.BlockSpec((1, e, n), lambda i: (i, 0, 0)),
            pl.BlockSpec((1, e, n), lambda i: (i, 0, 0)),
            pl.BlockSpec((1, n, d), lambda i: (i, 0, 0)),
        ],
        out_specs=[pl.BlockSpec((e, 1, cap, d), lambda i: (0, i, 0, 0)), pl.BlockSpec((e, 1, cap, HEAD_W), lambda i: (0, i, 0, 0))],
        out_shape=[jax.ShapeDtypeStruct((e, b, cap, d), BF), jax.ShapeDtypeStruct((e, b, cap, HEAD_W), F32)],
        compiler_params=_cparams(1),
    )(sel, aff, h2)


def _sc_dispatch(table, sel, aff, cap, n_experts):
    info = plsc.get_sparse_core_info()
    n_workers = info.num_cores * info.num_subcores
    lanes = info.num_lanes
    pairs, n = sel.shape
    b_tot = pairs // n_experts
    w = table.shape[1]
    per_w = pairs // n_workers
    ch = min(SC_CHUNK, cap)
    n_ch = cap // ch
    assert pairs % n_workers == 0 and cap % ch == 0 and ch % 8 == 0 and n % lanes == 0
    mesh = plsc.VectorSubcoreMesh(core_axis_name="c", subcore_axis_name="s")

    @functools.partial(
        pl.kernel, mesh=mesh, name="sc_dispatch",
        compiler_params=pltpu.CompilerParams(needs_layout_passes=False),
        out_type=[jax.ShapeDtypeStruct((pairs * cap, w), jnp.int32), jax.ShapeDtypeStruct((pairs * cap,), F32)],
        scratch_types=[pltpu.VMEM((n,), jnp.int32), pltpu.VMEM((n,), F32), pltpu.VMEM((n_ch, ch), jnp.int32),
                       pltpu.VMEM((cap,), F32), pltpu.VMEM((2, ch, w), jnp.int32), pltpu.SemaphoreType.DMA((2,))],
    )
    def k(table_hbm, sel_hbm, aff_hbm, xs_hbm, gate_hbm, sel_v, aff_v, tok_v, gate_v, rows_v, sem):
        wid = lax.axis_index("s") * info.num_cores + lax.axis_index("c")

        @pl.loop(0, per_w)
        def _(i):
            p = wid * per_w + i
            b = p // n_experts
            e = p - b * n_experts
            out_off = pl.multiple_of((e * b_tot + b) * cap, 8)
            pltpu.sync_copy(sel_hbm.at[p], sel_v)
            pltpu.sync_copy(aff_hbm.at[p], aff_v)
            lane = lax.iota(jnp.int32, lanes)

            @plsc.parallel_loop(0, n // lanes, unroll=4)
            def _(j):
                t0 = pl.multiple_of(j * lanes, lanes)
                s = sel_v[pl.ds(t0, lanes)]
                chosen = s >= 0
                slot = jnp.maximum(s, 0)
                plsc.store_scatter(tok_v, [slot // ch, slot % ch], lane + (t0 + b * n), mask=chosen)
                plsc.store_scatter(gate_v, [slot], aff_v[pl.ds(t0, lanes)], mask=chosen)

            prev = None
            for c in range(n_ch + 1):
                cur = None
                if c < n_ch:
                    cur = pltpu.async_copy(table_hbm.at[tok_v.at[c]], rows_v.at[c % 2], sem.at[c % 2])
                if prev is not None:
                    prev.wait()
                    pltpu.sync_copy(rows_v.at[(c - 1) % 2], xs_hbm.at[pl.ds(out_off + (c - 1) * ch, ch)])
                prev = cur
            pltpu.sync_copy(gate_v, gate_hbm.at[pl.ds(out_off, cap)])

    return k(table, sel, aff)


def _expert_kernel(packed, x_ref, g_ref, wg_ref, wu_ref, wd_ref, y_ref, wg_s, wu_s, wd_s):
    @pl.when(pl.program_id(1) == 0)
    def _():
        wg_s[...] = wg_ref[0, 0].astype(BF)
        wu_s[...] = wu_ref[0, 0].astype(BF)
        wd_s[...] = wd_ref[0, 0].astype(BF)

    rb = x_ref.shape[1]
    sub = min(rb, EXPERT_SUB_ROWS)
    for r0 in range(0, rb, sub):
        rows = slice(r0, r0 + sub)
        x = _unpack_halves(x_ref[0, rows, :]) if packed else x_ref[0, rows, :]
        a = _dot(x, wg_s[...])
        u = _dot(x, wu_s[...])
        mid = (a * (1.0 / (1.0 + jnp.exp(-a))) * u).astype(BF)
        y_ref[0, rows, :] = (_dot(mid, wd_s[...]) * g_ref[0, rows, 0:1]).astype(BF)


def _experts(l, xs, gate, w_e_gate, w_e_up, w_e_down):
    e, rows, w = xs.shape
    d, ff = w_e_gate.shape[2:]
    rb = math.gcd(rows, EXPERT_ROWS)
    w_idx = lambda i, j: (l, i, 0, 0)
    blk = lambda i, j: (i, j, 0)
    return pl.pallas_call(
        functools.partial(_expert_kernel, xs.dtype == jnp.int32),
        name="experts_%d" % rows,
        grid=(e, rows // rb),
        in_specs=[
            pl.BlockSpec((1, rb, w), blk),
            pl.BlockSpec((1, rb, HEAD_W), blk),
            pl.BlockSpec((1, 1, d, ff), w_idx),
            pl.BlockSpec((1, 1, d, ff), w_idx),
            pl.BlockSpec((1, 1, ff, d), w_idx),
        ],
        out_specs=pl.BlockSpec((1, rb, d), blk),
        out_shape=jax.ShapeDtypeStruct((e, rows, d), BF),
        scratch_shapes=[pltpu.VMEM((d, ff), BF), pltpu.VMEM((d, ff), BF), pltpu.VMEM((ff, d), BF)],
        compiler_params=_cparams(2),
    )(xs, gate, w_e_gate, w_e_up, w_e_down)


def _combine_rows(cap, d, x_ref, mod_ref, selt_ref, ye_ref, pt_ref, rows):
    e_tot = ye_ref.shape[0]
    selt = selt_ref[0, rows, :]
    r = selt.shape[0]
    if cap % 128 == 0:
        lane = lax.broadcasted_iota(jnp.int32, (r, cap), 1)
        for e_idx in range(e_tot):
            pt_ref[rows, e_idx * cap:(e_idx + 1) * cap] = jnp.where(selt[:, e_idx:e_idx + 1] == lane, 1.0, 0.0).astype(BF)
    else:
        assert cap & (cap - 1) == 0
        shift = cap.bit_length() - 1
        col = lax.broadcasted_iota(jnp.int32, (e_tot, e_tot * cap), 1)
        spread = jnp.where((col >> shift) == lax.broadcasted_iota(jnp.int32, col.shape, 0), 1.0, 0.0).astype(BF)
        slot_of_col = _dot(selt.astype(F32).astype(BF), spread)
        want = (lax.broadcasted_iota(jnp.int32, (r, e_tot * cap), 1) & (cap - 1)).astype(F32)
        pt_ref[rows, :] = jnp.where(slot_of_col == want, 1.0, 0.0).astype(BF)
    y = _dot(pt_ref[rows, :], ye_ref[...].reshape(e_tot * cap, d))
    return x_ref[0, rows, :] + mod_ref[0, :, 5 * d:6 * d] * y


def _scatter_kernel(cap, d, x_ref, mod_ref, selt_ref, ye_ref, gf_ref, o_ref, pt_ref):
    o_ref[0] = _rms(_combine_rows(cap, d, x_ref, mod_ref, selt_ref, ye_ref, pt_ref, slice(None)), gf_ref[...])


def _scatter(x, modv, sel_t, ye, cap, g_final):
    b, n, d = x.shape
    e = ye.shape[0]
    tn = min(PRE_ROWS, n)
    bm = modv.shape[0]
    mod_idx = (lambda i, j: (i, 0, 0)) if bm > 1 else (lambda i, j: (0, 0, 0))
    return pl.pallas_call(
        functools.partial(_scatter_kernel, cap, d),
        name="scatter_%d" % n,
        grid=(b, n // tn),
        in_specs=[
            pl.BlockSpec((1, tn, d), lambda i, j: (i, j, 0)),
            pl.BlockSpec((1, 1, modv.shape[2]), mod_idx),
            pl.BlockSpec((1, tn, e), lambda i, j: (i, j, 0)),
            pl.BlockSpec((e, 1, cap, d), lambda i, j: (0, i, 0, 0)),
            pl.BlockSpec((1, d), lambda i, j: (0, 0)),
        ],
        out_specs=pl.BlockSpec((1, tn, d), lambda i, j: (i, j, 0)),
        out_shape=jax.ShapeDtypeStruct((b, n, d), F32),
        scratch_shapes=[pltpu.VMEM((tn, e * cap), BF)],
        compiler_params=_cparams(2),
    )(x, modv, sel_t, ye, g_final.reshape(1, d))


def _dft_tables(n):
    j = np.arange(n, dtype=np.int64)
    ang = ((j[:, None] * j[None, :]) % n).astype(np.float64) * (2.0 * math.pi / n)
    return jnp.asarray(np.cos(ang), dtype=BF), jnp.asarray(np.sin(ang), dtype=BF)


def _block_diag(blocks):
    g, r, c = blocks.shape
    out = jnp.zeros((g * r, g * c), blocks.dtype)
    for i in range(g):
        out = out.at[i * r:(i + 1) * r, i * c:(i + 1) * c].set(blocks[i])
    return out


def _rope_tables(n):
    rows = n // GRID_W
    row = jnp.repeat(jnp.arange(rows, dtype=F32), GRID_W)
    col = jnp.tile(jnp.arange(GRID_W, dtype=F32), rows)

    def ang(dim):
        nf = dim // 4
        inv = ROPE_BASE ** (-jnp.arange(nf, dtype=F32) / nf)
        return jnp.concatenate([row[:, None] * inv, col[:, None] * inv], axis=-1)

    a = ang(DIFF_DH)
    cos_d = jnp.tile(jnp.cos(a), (1, 4))
    sin_d = jnp.tile(jnp.concatenate([-jnp.sin(a), jnp.sin(a)], axis=1), (1, 2))
    a = ang(MLA_ROPE)
    ones = jnp.ones((n, MLA_NOPE), F32)
    pad = HEAD_W - MLA_NOPE - MLA_ROPE
    cos_m = jnp.concatenate([ones, jnp.cos(a), jnp.cos(a), jnp.ones((n, pad), F32)], axis=1)
    sin_m = jnp.concatenate([0 * ones, -jnp.sin(a), jnp.sin(a), jnp.zeros((n, pad), F32)], axis=1)
    return cos_d, sin_d, cos_m, sin_m


def _pad_heads(w, heads, lo, hi):
    k = w.shape[0]
    w3 = w.reshape(k, heads, -1)[:, :, lo:hi]
    return jnp.pad(w3, ((0, 0), (0, 0), (0, HEAD_W - (hi - lo)))).reshape(k, heads * HEAD_W)


def kernel(x_prompt, x_sample, cache_diff_k, cache_diff_v, cache_mla_ckv, cache_mla_kpe, c, c_ctx, w_ada, b_ada, g_attn, g_ffn, w_in, w_four, diff_lambda, g_subln, g_mla_q, w_mla_uq, g_mla_kv, w_mla_ukv, w_out, w_router, w_e_gate, w_e_up, w_e_down, g_final):
    depth, d = g_attn.shape
    b_ctx, n_ctx, _ = x_prompt.shape
    b_lat, n_lat, _ = x_sample.shape

    r = b_lat + 1
    r_pad = -(-r // 8) * 8
    cond = jnp.concatenate([c, c_ctx[None, :], jnp.zeros((r_pad - r, d), F32)], axis=0)
    mod = _modulation(cond, w_ada, b_ada)

    n_main = w_in.shape[2] - MLA_ROPE
    w_in_p = jnp.concatenate(
        [w_in[:, :, :n_main], jnp.zeros((depth, d, MLA_NOPE), F32), w_in[:, :, n_main:],
         jnp.zeros((depth, d, HEAD_W - MLA_NOPE - MLA_ROPE), F32)], axis=2).astype(BF)
    w_uq_p = jnp.stack([_pad_heads(w_mla_uq[l], MLA_HEADS, 0, MLA_NOPE + MLA_ROPE) for l in range(depth)]).astype(BF)
    w_k_p = jnp.stack([_pad_heads(w_mla_ukv[l], MLA_HEADS, 0, MLA_NOPE) for l in range(depth)]).astype(BF)
    w_v_p = w_mla_ukv.reshape(depth, MLA_KV_RANK, MLA_HEADS, MLA_NOPE + MLA_V)[..., MLA_NOPE:].reshape(depth, MLA_KV_RANK, D_MLA).astype(BF)
    w_out_p = w_out.astype(BF)
    w_router_t = jnp.swapaxes(w_router, 1, 2).astype(BF)
    place = jnp.zeros((MLA_ROPE, HEAD_W), F32).at[jnp.arange(MLA_ROPE), MLA_NOPE + jnp.arange(MLA_ROPE)].set(1.0)
    place = jnp.tile(place, (1, MLA_HEADS)).astype(BF)

    jc = jnp.arange(F_CH, dtype=jnp.int32)
    ang_c = ((jc[:, None] * jc[None, :]) % F_CH).astype(F32) * (2.0 * math.pi / F_CH)
    cc_bd = _block_diag(jnp.broadcast_to(jnp.cos(ang_c), (F_GROUPS, F_CH, F_CH))).astype(BF)
    sc_bd = _block_diag(jnp.broadcast_to(jnp.sin(ang_c), (F_GROUPS, F_CH, F_CH))).astype(BF)
    dft = {n: _dft_tables(n) for n in {n_ctx, n_lat}}
    ropes = _rope_tables(n_lat)

    def pre(latent, l, x, prev=None, combine=None):
        modv = (mod[l, :b_lat] if latent else mod[l, b_lat:b_lat + 1]).reshape(-1, 1, N_MOD * d)
        outs = _pre(latent, x, modv, g_attn[l], w_in_p[l], g_mla_q[l], w_uq_p[l], g_mla_kv[l], w_k_p[l], w_v_p[l],
                    cc_bd, sc_bd, ropes, l, depth, prev, combine)
        if combine is not None:
            x, outs = outs[0], outs[1:]
        return x, modv, outs

    def mix_and_route(latent, l, x, modv, outs):
        b, n, _ = x.shape
        cap = max(1, EC_CAPACITY_FACTOR * n // N_EXPERTS)
        u, v, qd, kd, vd, qm, km, vm = outs[:8]
        cache = _cache_prep(l, cache_diff_k, cache_diff_v, cache_mla_ckv, cache_mla_kpe, w_k_p[l], w_v_p[l], place) if latent else None
        cn, sn = dft[n]
        yfour = _fourier(u, v, cn, sn, _block_diag(w_four[l]).astype(BF))
        x = _attention(latent, l, x, modv, yfour, qd, kd, vd, qm, km, vm, cache, diff_lambda[l], g_subln[l], w_out_p[l])
        h2, aff = _router(x, modv, g_ffn[l], w_router_t[l], packed=latent)
        sel = _select(aff, cap)
        e = sel.shape[1]
        if latent:
            xs, gate = _sc_dispatch(h2.reshape(b * n, h2.shape[2]), sel.reshape(b * e, n), aff.reshape(b * e, n), cap, e)
            gate = jnp.broadcast_to(gate.reshape(e, b * cap, 1), (e, b * cap, HEAD_W))
        else:
            xs, gate = _gather(sel, aff, h2, cap)
            gate = gate.reshape(e, b * cap, HEAD_W)
        return x, jnp.swapaxes(sel, 1, 2), cap, xs.reshape(e, b * cap, -1), gate

    x_l, mod_l, pre_l = pre(True, 0, x_sample)
    x_c, mod_c, pre_c = pre(False, 0, x_prompt)
    for l in range(depth):
        new = tuple(pre_c[8:])
        x_l, selt_l, cap_l, xs_l, gate_l = mix_and_route(True, l, x_l, mod_l, pre_l)
        x_c, selt_c, cap_c, xs_c, gate_c = mix_and_route(False, l, x_c, mod_c, pre_c)
        ye_c = _experts(l, xs_c, gate_c, w_e_gate, w_e_up, w_e_down).reshape(-1, b_ctx, cap_c, d)
        if l + 1 < depth:
            x_c, mod_c, pre_c = pre(False, l + 1, x_c, new, (mod_c, selt_c, ye_c, cap_c))
        else:
            x_c = _scatter(x_c, mod_c, selt_c, ye_c, cap_c, g_final)
        ye_l = _experts(l, xs_l, gate_l, w_e_gate, w_e_up, w_e_down).reshape(-1, b_lat, cap_l, d)
        if l + 1 < depth:
            x_l, mod_l, pre_l = pre(True, l + 1, x_l, None, (mod_l, selt_l, ye_l, cap_l))
        else:
            x_l = _scatter(x_l, mod_l, selt_l, ye_l, cap_l, g_final)
    y_prompt, y_sample = x_c, x_l
    new_diff_k, new_diff_v, new_mla_ckv, new_mla_kpe = new
    return (y_prompt, y_sample, new_diff_k, new_diff_v, new_mla_ckv, new_mla_kpe)
```

```python
import functools
import math

import jax
import jax.numpy as jnp
import numpy as np
from jax import lax
from jax.experimental import pallas as pl
from jax.experimental.pallas import tpu as pltpu
from jax.experimental.pallas import tpu_sc as plsc

BF = jnp.bfloat16
F32 = jnp.float32

GRID_W = 64
ROPE_BASE = 10000.0
NORM_EPS = 1e-6
F_GROUPS, F_CH = 4, 64
D_FOURIER = F_GROUPS * F_CH
DIFF_HEADS, DIFF_DH = 4, 64
D_DIFF = DIFF_HEADS * 2 * DIFF_DH
MLA_HEADS, MLA_Q_RANK, MLA_KV_RANK = 4, 256, 128
MLA_NOPE, MLA_ROPE, MLA_V = 64, 32, 64
HEAD_W = 128
D_MLA_P = MLA_HEADS * HEAD_W
D_MLA = MLA_HEADS * MLA_V
D_MIX = D_FOURIER + D_DIFF + D_MLA
ATTN_ROWS = 512
KEY_CHUNK = 256
SCORE_BUFS = 2
BF_ROWS = 16
SELECT_ROWS = 128
EXPERT_ROWS = 1024
EXPERT_SUB_ROWS = 512
PRE_ROWS = 512
SC_CHUNK = 64
SUB_ROWS = 256
N_EXPERTS = 16
EC_CAPACITY_FACTOR = 2
N_MOD = 6
IN_COLS_P = D_FOURIER + 3 * D_DIFF + MLA_Q_RANK + MLA_KV_RANK + HEAD_W
LOG2E = 1.4426950408889634
VMEM_LIMIT = 56 * 1024 * 1024


def _cparams(n_axes, vmem=VMEM_LIMIT):
    return pltpu.CompilerParams(dimension_semantics=("arbitrary",) * n_axes, vmem_limit_bytes=vmem)


def _dot(a, b):
    return jnp.dot(a, b, preferred_element_type=F32)


def _dot_nt(a, b):
    return lax.dot_general(a, b, (((1,), (1,)), ((), ())), preferred_element_type=F32)


def _rms(x, g):
    return x * lax.rsqrt(jnp.mean(x * x, axis=-1, keepdims=True) + NORM_EPS) * g


def _pack_halves(hb):
    w = hb.shape[1] // 2
    bits = pltpu.bitcast(hb.astype(F32), jnp.uint32)
    packed = (bits[:, :w] >> 16) | (bits[:, w:] & jnp.uint32(0xFFFF0000))
    return pltpu.bitcast(packed, jnp.int32)


def _unpack_halves(xi):
    bits = pltpu.bitcast(xi, jnp.uint32)
    lo = pltpu.bitcast(bits << 16, F32).astype(BF)
    hi = pltpu.bitcast(bits & jnp.uint32(0xFFFF0000), F32).astype(BF)
    return jnp.concatenate([lo, hi], axis=1)


def _rope(z, cos, sin_signed, half, group, lo):
    w = z.shape[1]
    reps = w // cos.shape[1]
    cos_w = jnp.concatenate([cos] * reps, axis=1) if reps > 1 else cos
    sin_w = jnp.concatenate([sin_signed] * reps, axis=1) if reps > 1 else sin_signed
    from_right = pltpu.roll(z, w - half, 1)
    from_left = pltpu.roll(z, half, 1)
    lane = lax.broadcasted_iota(jnp.int32, z.shape, 1) % group
    first = (lane >= lo) & (lane < lo + half)
    partner = jnp.where(first, from_right, from_left)
    return z * cos_w + partner * sin_w


def _mod_kernel(c_ref, w_ref, b_ref, o_ref):
    c = c_ref[...]
    a = (c * (1.0 / (1.0 + jnp.exp(-c)))).astype(BF)
    o_ref[0] = _dot(a, w_ref[0].astype(BF)) + b_ref[0]


def _modulation(cond, w_ada, b_ada):
    depth, d, n6 = w_ada.shape
    r = cond.shape[0]
    tn = 1536
    return pl.pallas_call(
        _mod_kernel,
        name="modulation",
        grid=(depth, n6 // tn),
        in_specs=[
            pl.BlockSpec((r, d), lambda l, j: (0, 0)),
            pl.BlockSpec((1, d, tn), lambda l, j: (l, 0, j)),
            pl.BlockSpec((1, 1, tn), lambda l, j: (l, 0, j)),
        ],
        out_specs=pl.BlockSpec((1, r, tn), lambda l, j: (l, 0, j)),
        out_shape=jax.ShapeDtypeStruct((depth, r, n6), F32),
        compiler_params=_cparams(2),
    )(cond, w_ada, b_ada.reshape(depth, 1, n6))


def _pre_kernel(latent, d, *refs):
    if latent:
        (x_ref, mod_ref, g_ref, win_ref, gq_ref, wuq_ref, gkv_ref, wk_ref, wv_ref, cc_ref, sc_ref,
         cd_ref, sd_ref, cm_ref, sm_ref,
         u_ref, v_ref, qd_ref, kd_ref, vd_ref, qm_ref, km_ref, vm_ref) = refs
    else:
        (x_ref, mod_ref, g_ref, win_ref, gq_ref, wuq_ref, gkv_ref, wk_ref, wv_ref, cc_ref, sc_ref,
         u_ref, v_ref, qd_ref, kd_ref, vd_ref, qm_ref, km_ref, vm_ref,
         k32_ref, v32_ref, ckv32_ref, kpe32_ref) = refs[:11] + refs[-12:]
    sh = mod_ref[0, :, 0:d]
    sc = mod_ref[0, :, d:2 * d]
    tm = x_ref.shape[1]
    sub = SUB_ROWS if tm > SUB_ROWS else max(tm // 2, BF_ROWS)
    def project(r0):
        h = _rms(x_ref[0, r0:r0 + sub, :], g_ref[...]) * (1.0 + sc) + sh
        return _dot(h.astype(BF), win_ref[...])

    starts = list(range(0, tm, sub))
    z_next = project(starts[0])
    for i, r0 in enumerate(starts):
        rows = slice(r0, r0 + sub)
        z = z_next
        z_next = project(starts[i + 1]) if i + 1 < len(starts) else None
        o = 0
        zf = z[:, o:o + D_FOURIER]; o += D_FOURIER
        zq = z[:, o:o + D_DIFF]; o += D_DIFF
        zk = z[:, o:o + D_DIFF]; o += D_DIFF
        zv = z[:, o:o + D_DIFF]; o += D_DIFF
        zcq = z[:, o:o + MLA_Q_RANK]; o += MLA_Q_RANK
        zckv = z[:, o:o + MLA_KV_RANK]; o += MLA_KV_RANK
        kpe = z[:, o:o + HEAD_W]

        zf_b = zf.astype(BF)
        u_ref[0, rows, :] = _dot(zf_b, cc_ref[...]).astype(BF)
        v_ref[0, rows, :] = _dot(zf_b, sc_ref[...]).astype(BF)
        vd_ref[0, :, rows] = zv.T.astype(BF)
        cq = _rms(zcq, gq_ref[...])
        qm = _dot(cq.astype(BF), wuq_ref[...])
        ckv = _rms(zckv, gkv_ref[...])
        ckv_b = ckv.astype(BF)
        k_nope = _dot(ckv_b, wk_ref[...])
        vm_ref[0, :, rows] = _dot(ckv_b, wv_ref[...]).T.astype(BF)
        if latent:
            cd, sd, cm, sm = cd_ref[rows, :], sd_ref[rows, :], cm_ref[rows, :], sm_ref[rows, :]
            zq_r = _rope(zq, cd, sd, DIFF_DH // 2, DIFF_DH, 0)
            zk_r = _rope(zk, cd, sd, DIFF_DH // 2, DIFF_DH, 0)
            qm = _rope(qm, cm, sm, MLA_ROPE // 2, HEAD_W, MLA_NOPE)
            kpe_r = _rope(kpe, cm, sm, MLA_ROPE // 2, HEAD_W, MLA_NOPE)
        else:
            zq_r, zk_r, kpe_r = zq, zk, kpe
            for hd in range(DIFF_HEADS):
                k32_ref[0, 0, rows, hd, :] = zk[:, hd * HEAD_W:(hd + 1) * HEAD_W]
                v32_ref[0, 0, rows, hd, :] = zv[:, hd * HEAD_W:(hd + 1) * HEAD_W]
            ckv32_ref[0, 0, rows, :] = ckv
            kpe32_ref[0, 0, rows, :] = kpe[:, MLA_NOPE:MLA_NOPE + MLA_ROPE]
            if i == 0:
                for ref in (k32_ref, v32_ref, ckv32_ref, kpe32_ref):
                    if ref.shape[1] > 1:
                        ref[0, 1:] = jnp.zeros((ref.shape[1] - 1,) + ref.shape[2:], F32)
        qd_ref[0, rows, :] = (zq_r * (DIFF_DH ** -0.5 * LOG2E)).astype(BF)
        kd_ref[0, rows, :] = zk_r.astype(BF)
        qm_ref[0, rows, :] = (qm * ((MLA_NOPE + MLA_ROPE) ** -0.5 * LOG2E)).astype(BF)
        km_ref[0, rows, :] = (k_nope + jnp.concatenate([kpe_r] * MLA_HEADS, axis=1)).astype(BF)


def _combine_pre_kernel(latent, d, cap, n_pre_in, *refs):
    x_ref, modp_ref, selt_ref, ye_ref = refs[:4]
    pre_in = refs[4:4 + n_pre_in]
    xo_ref = refs[4 + n_pre_in]
    pre_out = refs[5 + n_pre_in:-1]
    xo_ref[0] = _combine_rows(cap, d, x_ref, modp_ref, selt_ref, ye_ref, refs[-1], slice(None))
    _pre_kernel(latent, d, xo_ref, *pre_in, *pre_out)


def _pre(latent, x, modv, g_attn, w_in_p, g_q, w_uq_p, g_kv, w_k_p, w_v_p, cc_bd, sc_bd, ropes, l=0, depth=1, prev=None,
         combine=None):
    b, n, d = x.shape
    tm = min(PRE_ROWS, n)
    bm = modv.shape[0]
    mod_idx = (lambda i, j: (i, 0, 0)) if bm > 1 else (lambda i, j: (0, 0, 0))
    const = lambda i, j: (0, 0)
    tok = lambda i, j: (i, j, 0)
    in_specs = [
        pl.BlockSpec((1, tm, d), tok),
        pl.BlockSpec((1, 1, modv.shape[2]), mod_idx),
        pl.BlockSpec((1, d), const),
        pl.BlockSpec(w_in_p.shape, const),
        pl.BlockSpec((1, MLA_Q_RANK), const),
        pl.BlockSpec(w_uq_p.shape, const),
        pl.BlockSpec((1, MLA_KV_RANK), const),
        pl.BlockSpec(w_k_p.shape, const),
        pl.BlockSpec(w_v_p.shape, const),
        pl.BlockSpec(cc_bd.shape, const),
        pl.BlockSpec(sc_bd.shape, const),
    ]
    args = [x, modv, g_attn.reshape(1, d), w_in_p, g_q.reshape(1, -1), w_uq_p, g_kv.reshape(1, -1), w_k_p, w_v_p, cc_bd, sc_bd]
    if latent:
        in_specs += [pl.BlockSpec((tm, HEAD_W), lambda i, j: (j, 0))] * 4
        args += list(ropes)
    feat = lambda i, j: (i, 0, j)
    out_shape, out_specs = [], []
    for w, token_major in ((D_FOURIER, True), (D_FOURIER, True), (D_DIFF, True), (D_DIFF, True), (D_DIFF, False),
                           (D_MLA_P, True), (D_MLA_P, True), (D_MLA, False)):
        out_shape.append(jax.ShapeDtypeStruct((b, n, w) if token_major else (b, w, n), BF))
        out_specs.append(pl.BlockSpec((1, tm, w), tok) if token_major else pl.BlockSpec((1, w, tm), feat))
    aliases = {}
    if not latent:
        span = depth if prev is None else 1
        for tail in ((DIFF_HEADS, HEAD_W), (DIFF_HEADS, HEAD_W), (MLA_KV_RANK,), (MLA_ROPE,)):
            out_shape.append(jax.ShapeDtypeStruct((b, depth, n) + tail, F32))
            out_specs.append(pl.BlockSpec((1, span, tm) + tail, lambda i, j, t=len(tail): (i, l, j) + (0,) * t))
        if prev is not None:
            for k, buf in enumerate(prev):
                aliases[len(args)] = len(out_shape) - len(prev) + k
                in_specs.append(pl.BlockSpec(memory_space=pl.ANY))
                args.append(buf)
    body, scratch, tag = functools.partial(_pre_kernel, latent, d), [], "pre"
    if combine is not None:
        modp, sel_t, ye, cap = combine
        e = ye.shape[0]
        modp_idx = (lambda i, j: (i, 0, 0)) if modp.shape[0] > 1 else (lambda i, j: (0, 0, 0))
        extra_specs = [pl.BlockSpec((1, 1, modp.shape[2]), modp_idx), pl.BlockSpec((1, tm, e), tok),
                       pl.BlockSpec((e, 1, cap, d), lambda i, j: (0, i, 0, 0))]
        n_extra = len(extra_specs)
        body = functools.partial(_combine_pre_kernel, latent, d, cap, len(args) - 1)
        in_specs = in_specs[:1] + extra_specs + in_specs[1:]
        args = args[:1] + [modp, sel_t, ye] + args[1:]
        out_specs = [pl.BlockSpec((1, tm, d), tok)] + out_specs
        out_shape = [jax.ShapeDtypeStruct((b, n, d), F32)] + out_shape
        aliases = {k + n_extra: v + 1 for k, v in aliases.items()}
        scratch, tag = [pltpu.VMEM((tm, e * cap), BF)], "combine_pre"
    return pl.pallas_call(
        body,
        name=tag + ("_lat" if latent else "_ctx"),
        grid=(b, n // tm),
        in_specs=in_specs,
        out_specs=out_specs,
        out_shape=out_shape,
        input_output_aliases=aliases,
        scratch_shapes=scratch,
        compiler_params=_cparams(2),
    )(*args)


def _cache_kernel(dk_ref, dv_ref, ckv_ref, kpe_ref, wk_ref, wv_ref, place_ref, kd_ref, vd_ref, km_ref, vm_ref):
    for hd in range(DIFF_HEADS):
        sl = slice(hd * HEAD_W, (hd + 1) * HEAD_W)
        kd_ref[0, :, sl] = dk_ref[0, 0, :, hd, :].astype(BF)
        vd_ref[0, sl, :] = dv_ref[0, 0, :, hd, :].T.astype(BF)
    ckv_b = ckv_ref[0, 0].astype(BF)
    kpe_w = _dot(kpe_ref[0, 0].astype(BF), place_ref[...])
    km_ref[0] = (_dot(ckv_b, wk_ref[...]) + kpe_w).astype(BF)
    vm_ref[0] = _dot(ckv_b, wv_ref[...]).T.astype(BF)


def _cache_prep(l, cache_dk, cache_dv, cache_ckv, cache_kpe, w_k_p, w_v_p, place):
    b, _, p = cache_dk.shape[:3]
    at_l = lambda i: (i, l, 0, 0)
    at_l5 = lambda i: (i, l, 0, 0, 0)
    const = lambda i: (0, 0)
    shapes = [(p, D_DIFF), (D_DIFF, p), (p, D_MLA_P), (D_MLA, p)]
    return pl.pallas_call(
        _cache_kernel,
        name="cache_prep",
        grid=(b,),
        in_specs=[
            pl.BlockSpec((1, 1, p, DIFF_HEADS, HEAD_W), at_l5),
            pl.BlockSpec((1, 1, p, DIFF_HEADS, HEAD_W), at_l5),
            pl.BlockSpec((1, 1, p, MLA_KV_RANK), at_l),
            pl.BlockSpec((1, 1, p, MLA_ROPE), at_l),
            pl.BlockSpec(w_k_p.shape, const),
            pl.BlockSpec(w_v_p.shape, const),
            pl.BlockSpec(place.shape, const),
        ],
        out_specs=[pl.BlockSpec((1,) + s, lambda i: (i, 0, 0)) for s in shapes],
        out_shape=[jax.ShapeDtypeStruct((b,) + s, BF) for s in shapes],
        compiler_params=_cparams(1),
    )(cache_dk, cache_dv, cache_ckv, cache_kpe, w_k_p, w_v_p, place)


def _fourier_kernel(scale, u_ref, v_ref, cn_ref, sn_ref, w_ref, o_ref):
    f = (_dot(cn_ref[...], u_ref[0]) - _dot(sn_ref[...], v_ref[0])) * scale
    o_ref[0] = _dot(f.astype(BF), w_ref[...]).astype(BF)


def _fourier(u, v, cn, sn, w_bd):
    b, n, w = u.shape
    tm = min(512, n)
    return pl.pallas_call(
        functools.partial(_fourier_kernel, 1.0 / math.sqrt(n * F_CH)),
        name="fourier_%d" % n,
        grid=(n // tm, b),
        in_specs=[
            pl.BlockSpec((1, n, w), lambda i, j: (j, 0, 0)),
            pl.BlockSpec((1, n, w), lambda i, j: (j, 0, 0)),
            pl.BlockSpec((tm, n), lambda i, j: (i, 0)),
            pl.BlockSpec((tm, n), lambda i, j: (i, 0)),
            pl.BlockSpec((w, w), lambda i, j: (0, 0)),
        ],
        out_specs=pl.BlockSpec((1, tm, w), lambda i, j: (j, i, 0)),
        out_shape=jax.ShapeDtypeStruct((b, n, w), BF),
        compiler_params=_cparams(2),
    )(u, v, cn, sn, w_bd)


def _scores(q, k_parts, s_ref):
    off, m, bounds = 0, None, []
    for k in k_parts:
        s = _dot_nt(k(), q)
        n_i = s.shape[0]
        s_ref[off:off + n_i, :] = s
        mi = jnp.max(s, axis=0, keepdims=True)
        m = mi if m is None else jnp.maximum(m, mi)
        bounds.append((off, n_i))
        off += n_i
    return m, bounds


def _probs(m, total, s_ref, p_ref):
    for r in range(0, total, KEY_CHUNK):
        rows = min(KEY_CHUNK, total - r)
        p_ref[r:r + rows, :] = jnp.exp2(s_ref[r:r + rows, :] - m).astype(BF)


def _weighted(vt_parts, bounds, p_ref):
    o = None
    for vt, (r, n_i) in zip(vt_parts, bounds):
        v = vt()
        ones = jnp.where(lax.broadcasted_iota(jnp.int32, (BF_ROWS, n_i), 0) == 0, 1.0, 0.0).astype(BF)
        oi = _dot(jnp.concatenate([v, ones], axis=0), p_ref[r:r + n_i, :])
        o = oi if o is None else o + oi
    dv = o.shape[0] - BF_ROWS
    return o[:dv] * (1.0 / o[dv:dv + 1])


def _attn_kernel(latent, lam_init, d, *refs):
    if latent:
        (x_ref, mod_ref, yf_ref, qd_ref, kd_ref, vd_ref, kdc_ref, vdc_ref, qm_ref, km_ref, vm_ref, kmc_ref, vmc_ref,
         lam_ref, gs_ref, wo_ref, o_ref, mix_ref, s_ref, p_ref) = refs
    else:
        (x_ref, mod_ref, yf_ref, qd_ref, kd_ref, vd_ref, qm_ref, km_ref, vm_ref,
         lam_ref, gs_ref, wo_ref, o_ref, mix_ref, s_ref, p_ref) = refs
        kdc_ref = vdc_ref = kmc_ref = vmc_ref = None
    lf = lam_ref[...]
    lam = (jnp.exp(jnp.sum(lf[0:1] * lf[1:2], axis=-1, keepdims=True))
           - jnp.exp(jnp.sum(lf[2:3] * lf[3:4], axis=-1, keepdims=True)) + lam_init)
    mix_ref[:, 0:D_FOURIER] = yf_ref[0]
    gs = gs_ref[...]

    def keys(ref, cache_ref, sl):
        parts = [lambda: ref[0, :, sl]]
        if latent:
            parts.append(lambda: cache_ref[0, :, sl])
        return parts

    def values_t(ref, cache_ref, sl):
        parts = [lambda: ref[0, sl, :]]
        if latent:
            parts.append(lambda: cache_ref[0, sl, :])
        return parts

    def diff_query(sl, comp):
        def load():
            qh = qd_ref[0, :, sl]
            lane = lax.broadcasted_iota(jnp.int32, qh.shape, 1)
            keep = (lane >= DIFF_DH) if comp else (lane < DIFF_DH)
            return jnp.where(keep, qh, jnp.zeros_like(qh))
        return load

    items = []
    for hd in range(DIFF_HEADS):
        sl = slice(hd * HEAD_W, (hd + 1) * HEAD_W)
        for comp in range(2):
            items.append((diff_query(sl, comp), keys(kd_ref, kdc_ref, sl), values_t(vd_ref, vdc_ref, sl)))
    for hd in range(MLA_HEADS):
        sl = slice(hd * HEAD_W, (hd + 1) * HEAD_W)
        vsl = slice(hd * MLA_V, (hd + 1) * MLA_V)
        items.append(((lambda sl=sl: qm_ref[0, :, sl]), keys(km_ref, kmc_ref, sl), values_t(vm_ref, vmc_ref, vsl)))

    def start(i):
        return _scores(items[i][0](), items[i][1], s_ref.at[i % SCORE_BUFS])

    total = s_ref.shape[1]
    outs = []
    state = start(0)
    for i in range(len(items)):
        nxt = start(i + 1) if i + 1 < len(items) else None
        m, bounds = state
        buf = i % SCORE_BUFS
        _probs(m, total, s_ref.at[buf], p_ref.at[buf])
        outs.append(_weighted(items[i][2], bounds, p_ref.at[buf]))
        state = nxt
        if i < 2 * DIFF_HEADS and i % 2 == 1:
            hd = i // 2
            o = outs[i - 1] - lam * outs[i]
            o = o * lax.rsqrt(jnp.mean(o * o, axis=0, keepdims=True) + NORM_EPS) * gs * (1.0 - lam_init)
            mix_ref[:, D_FOURIER + hd * HEAD_W:D_FOURIER + (hd + 1) * HEAD_W] = o.T.astype(BF)

    mix_ref[:, D_FOURIER + D_DIFF:D_MIX] = jnp.concatenate(outs[2 * DIFF_HEADS:], axis=0).T.astype(BF)
    y = _dot(mix_ref[...], wo_ref[...])
    gt = mod_ref[0, :, 2 * d:3 * d]
    o_ref[0] = x_ref[0] + gt * y


def _attention(latent, l, x, modv, yfour, qd, kd, vd, qm, km, vm, cache, diff_lambda_l, g_subln_l, w_out_b):
    b, n, d = x.shape
    tq = min(ATTN_ROWS, n)
    bm = modv.shape[0]
    mod_idx = (lambda i, j: (i, 0, 0)) if bm > 1 else (lambda i, j: (0, 0, 0))
    const = lambda i, j: (0, 0)
    tok = lambda i, j: (i, j, 0)
    full = lambda i, j: (i, 0, 0)
    p = cache[0].shape[1] if latent else 0
    in_specs = [
        pl.BlockSpec((1, tq, d), tok),
        pl.BlockSpec((1, 1, modv.shape[2]), mod_idx),
        pl.BlockSpec((1, tq, D_FOURIER), tok),
        pl.BlockSpec((1, tq, D_DIFF), tok),
        pl.BlockSpec((1, n, D_DIFF), full),
        pl.BlockSpec((1, D_DIFF, n), full),
    ]
    args = [x, modv, yfour, qd, kd, vd]
    if latent:
        kdc, vdc, kmc, vmc = cache
        in_specs += [pl.BlockSpec((1, p, D_DIFF), full), pl.BlockSpec((1, D_DIFF, p), full)]
        args += [kdc, vdc]
    in_specs += [pl.BlockSpec((1, tq, D_MLA_P), tok), pl.BlockSpec((1, n, D_MLA_P), full), pl.BlockSpec((1, D_MLA, n), full)]
    args += [qm, km, vm]
    if latent:
        in_specs += [pl.BlockSpec((1, p, D_MLA_P), full), pl.BlockSpec((1, D_MLA, p), full)]
        args += [kmc, vmc]
    in_specs += [pl.BlockSpec((4, DIFF_DH), const), pl.BlockSpec((HEAD_W, 1), const), pl.BlockSpec(w_out_b.shape, const)]
    args += [diff_lambda_l, g_subln_l.reshape(HEAD_W, 1), w_out_b]
    lam_init = 0.8 - 0.6 * math.exp(-0.3 * l)
    return pl.pallas_call(
        functools.partial(_attn_kernel, latent, lam_init, d),
        name="attn_lat" if latent else "attn_ctx",
        grid=(b, n // tq),
        in_specs=in_specs,
        out_specs=pl.BlockSpec((1, tq, d), tok),
        out_shape=jax.ShapeDtypeStruct((b, n, d), F32),
        scratch_shapes=[pltpu.VMEM((tq, D_MIX), BF), pltpu.VMEM((SCORE_BUFS, n + p, tq), F32),
                        pltpu.VMEM((SCORE_BUFS, n + p, tq), BF)],
        compiler_params=_cparams(2),
    )(*args)


def _router_kernel(d, packed, x_ref, mod_ref, g_ref, wr_ref, h_ref, aff_ref):
    sh = mod_ref[0, :, 3 * d:4 * d]
    sc = mod_ref[0, :, 4 * d:5 * d]
    tm = x_ref.shape[1]
    sub = SUB_ROWS if tm > SUB_ROWS else max(tm // 2, BF_ROWS)
    for r0 in range(0, tm, sub):
        rows = slice(r0, r0 + sub)
        hb = (_rms(x_ref[0, rows, :], g_ref[...]) * (1.0 + sc) + sh).astype(BF)
        h_ref[0, rows, :] = _pack_halves(hb) if packed else hb
        logits = _dot_nt(wr_ref[...], hb)
        e = jnp.exp(logits - jnp.max(logits, axis=0, keepdims=True))
        aff_ref[0, :, rows] = e / jnp.sum(e, axis=0, keepdims=True)


def _router(x, modv, g_ffn_l, w_router_t, packed):
    b, n, d = x.shape
    hw, hdt = (d // 2, jnp.int32) if packed else (d, BF)
    tm = min(PRE_ROWS, n)
    bm = modv.shape[0]
    mod_idx = (lambda i, j: (i, 0, 0)) if bm > 1 else (lambda i, j: (0, 0, 0))
    const = lambda i, j: (0, 0)
    return pl.pallas_call(
        functools.partial(_router_kernel, d, packed),
        name="router_%d" % n,
        grid=(b, n // tm),
        in_specs=[
            pl.BlockSpec((1, tm, d), lambda i, j: (i, j, 0)),
            pl.BlockSpec((1, 1, modv.shape[2]), mod_idx),
            pl.BlockSpec((1, d), const),
            pl.BlockSpec(w_router_t.shape, const),
        ],
        out_specs=[pl.BlockSpec((1, tm, hw), lambda i, j: (i, j, 0)), pl.BlockSpec((1, N_EXPERTS, tm), lambda i, j: (i, 0, j))],
        out_shape=[jax.ShapeDtypeStruct((b, n, hw), hdt), jax.ShapeDtypeStruct((b, N_EXPERTS, n), F32)],
        compiler_params=_cparams(2),
    )(x, modv, g_ffn_l.reshape(1, d), w_router_t)


def _select_kernel(cap, aff_ref, sel_ref):
    a = aff_ref[...]
    e, n = a.shape
    bits = pltpu.bitcast(a, jnp.int32)
    capf = float(cap)

    def body(_, carry):
        lo, hi = carry
        mid = lo + ((hi - lo) >> 1)
        cnt = jnp.sum(jnp.where(bits >= mid, 1.0, 0.0), axis=1, keepdims=True)
        up = cnt >= capf
        return jnp.where(up, mid, lo), jnp.where(up, hi, mid)

    lo0 = jnp.zeros((e, 1), jnp.int32)
    hi0 = jnp.full((e, 1), 0x7F800000, jnp.int32)
    thr, _ = lax.fori_loop(0, 31, body, (lo0, hi0))
    gt = bits > thr
    eq = bits == thr
    need = capf - jnp.sum(jnp.where(gt, 1.0, 0.0), axis=1, keepdims=True)
    both = jnp.concatenate([jnp.where(gt, 1.0, 0.0), jnp.where(eq, 1.0, 0.0)], axis=0).astype(BF)
    ck = min(256, n)
    tri = jnp.where(lax.broadcasted_iota(jnp.int32, (ck, ck), 0) < lax.broadcasted_iota(jnp.int32, (ck, ck), 1), 1.0, 0.0).astype(BF)
    off = jnp.zeros((2 * e, 1), F32)
    pieces = []
    for k in range(n // ck):
        blk = both[:, k * ck:(k + 1) * ck]
        pieces.append(_dot(blk, tri) + off)
        off = off + jnp.sum(blk.astype(F32), axis=1, keepdims=True)
    cum = jnp.concatenate(pieces, axis=1) if len(pieces) > 1 else pieces[0]
    cum_gt, cum_eq = cum[:e], cum[e:]
    chosen = gt | (eq & (cum_eq < need))
    pos = cum_gt + jnp.minimum(cum_eq, need)
    sel_ref[...] = jnp.where(chosen, pos, -1.0).astype(jnp.int32)


def _select(aff, cap):
    b, e, n = aff.shape
    rows = b * e
    rb = min(rows, SELECT_ROWS)
    sel = pl.pallas_call(
        functools.partial(_select_kernel, cap),
        name="select_%d" % n,
        grid=(rows // rb,),
        in_specs=[pl.BlockSpec((rb, n), lambda i: (i, 0))],
        out_specs=pl.BlockSpec((rb, n), lambda i: (i, 0)),
        out_shape=jax.ShapeDtypeStruct((rows, n), jnp.int32),
        compiler_params=_cparams(1),
    )(aff.reshape(rows, n))
    return sel.reshape(b, e, n)


def _gather_kernel(cap, sel_ref, aff_ref, h_ref, xs_ref, gate_ref):
    h = h_ref[0]
    n = h.shape[0]
    e_tot = sel_ref.shape[1]

    def one(e_idx):
        sel_e = sel_ref[0, pl.ds(e_idx, 1), :]
        aff_e = aff_ref[0, pl.ds(e_idx, 1), :]
        hit = sel_e == lax.broadcasted_iota(jnp.int32, (cap, n), 0)
        g = jnp.sum(jnp.where(hit, aff_e, 0.0), axis=1, keepdims=True)
        return jnp.where(hit, 1.0, 0.0).astype(BF), jnp.broadcast_to(g, (cap, HEAD_W))

    if cap >= 128:
        def body(e_idx, carry):
            p, g = one(e_idx)
            xs_ref[e_idx, 0] = _dot(p, h).astype(BF)
            gate_ref[e_idx, 0] = g
            return carry

        lax.fori_loop(0, e_tot, body, 0)
    else:
        ps, gs = zip(*[one(e_idx) for e_idx in range(e_tot)])
        xs = _dot(jnp.concatenate(ps, axis=0), h).astype(BF)
        for e_idx in range(e_tot):
            xs_ref[e_idx, 0] = xs[e_idx * cap:(e_idx + 1) * cap]
            gate_ref[e_idx, 0] = gs[e_idx]


def _gather(sel, aff, h2, cap):
    b, e, n = sel.shape
    d = h2.shape[2]
    return pl.pallas_call(
        functools.partial(_gather_kernel, cap),
        name="gather_%d" % n,
        grid=(b,),
        in_specs=[
            pl.BlockSpec((1, e, n), lambda i: (i, 0, 0)),
            pl.BlockSpec((1, e, n), lambda i: (i, 0, 0)),
            pl.BlockSpec((1, n, d), lambda i: (i, 0, 0)),
        ],
        out_specs=[pl.BlockSpec((e, 1, cap, d), lambda i: (0, i, 0, 0)), pl.BlockSpec((e, 1, cap, HEAD_W), lambda i: (0, i, 0, 0))],
        out_shape=[jax.ShapeDtypeStruct((e, b, cap, d), BF), jax.ShapeDtypeStruct((e, b, cap, HEAD_W), F32)],
        compiler_params=_cparams(1),
    )(sel, aff, h2)


def _sc_dispatch(table, sel, aff, cap, n_experts):
    info = plsc.get_sparse_core_info()
    n_workers = info.num_cores * info.num_subcores
    lanes = info.num_lanes
    pairs, n = sel.shape
    b_tot = pairs // n_experts
    w = table.shape[1]
    per_w = pairs // n_workers
    ch = min(SC_CHUNK, cap)
    n_ch = cap // ch
    assert pairs % n_workers == 0 and cap % ch == 0 and ch % 8 == 0 and n % lanes == 0
    mesh = plsc.VectorSubcoreMesh(core_axis_name="c", subcore_axis_name="s")

    @functools.partial(
        pl.kernel, mesh=mesh, name="sc_dispatch",
        compiler_params=pltpu.CompilerParams(needs_layout_passes=False),
        out_type=[jax.ShapeDtypeStruct((pairs * cap, w), jnp.int32), jax.ShapeDtypeStruct((pairs * cap,), F32)],
        scratch_types=[pltpu.VMEM((n,), jnp.int32), pltpu.VMEM((n,), F32), pltpu.VMEM((n_ch, ch), jnp.int32),
                       pltpu.VMEM((cap,), F32), pltpu.VMEM((2, ch, w), jnp.int32), pltpu.SemaphoreType.DMA((2,))],
    )
    def k(table_hbm, sel_hbm, aff_hbm, xs_hbm, gate_hbm, sel_v, aff_v, tok_v, gate_v, rows_v, sem):
        wid = lax.axis_index("s") * info.num_cores + lax.axis_index("c")

        @pl.loop(0, per_w)
        def _(i):
            p = wid * per_w + i
            b = p // n_experts
            e = p - b * n_experts
            out_off = pl.multiple_of((e * b_tot + b) * cap, 8)
            pltpu.sync_copy(sel_hbm.at[p], sel_v)
            pltpu.sync_copy(aff_hbm.at[p], aff_v)
            lane = lax.iota(jnp.int32, lanes)

            @plsc.parallel_loop(0, n // lanes, unroll=4)
            def _(j):
                t0 = pl.multiple_of(j * lanes, lanes)
                s = sel_v[pl.ds(t0, lanes)]
                chosen = s >= 0
                slot = jnp.maximum(s, 0)
                plsc.store_scatter(tok_v, [slot // ch, slot % ch], lane + (t0 + b * n), mask=chosen)
                plsc.store_scatter(gate_v, [slot], aff_v[pl.ds(t0, lanes)], mask=chosen)

            prev = None
            for c in range(n_ch + 1):
                cur = None
                if c < n_ch:
                    cur = pltpu.async_copy(table_hbm.at[tok_v.at[c]], rows_v.at[c % 2], sem.at[c % 2])
                if prev is not None:
                    prev.wait()
                    pltpu.sync_copy(rows_v.at[(c - 1) % 2], xs_hbm.at[pl.ds(out_off + (c - 1) * ch, ch)])
                prev = cur
            pltpu.sync_copy(gate_v, gate_hbm.at[pl.ds(out_off, cap)])

    return k(table, sel, aff)


def _expert_kernel(packed, x_ref, g_ref, wg_ref, wu_ref, wd_ref, y_ref, wg_s, wu_s, wd_s):
    @pl.when(pl.program_id(1) == 0)
    def _():
        wg_s[...] = wg_ref[0, 0].astype(BF)
        wu_s[...] = wu_ref[0, 0].astype(BF)
        wd_s[...] = wd_ref[0, 0].astype(BF)

    rb = x_ref.shape[1]
    sub = min(rb, EXPERT_SUB_ROWS)
    for r0 in range(0, rb, sub):
        rows = slice(r0, r0 + sub)
        x = _unpack_halves(x_ref[0, rows, :]) if packed else x_ref[0, rows, :]
        a = _dot(x, wg_s[...])
        u = _dot(x, wu_s[...])
        mid = (a * (1.0 / (1.0 + jnp.exp(-a))) * u).astype(BF)
        y_ref[0, rows, :] = (_dot(mid, wd_s[...]) * g_ref[0, rows, 0:1]).astype(BF)


def _experts(l, xs, gate, w_e_gate, w_e_up, w_e_down):
    e, rows, w = xs.shape
    d, ff = w_e_gate.shape[2:]
    rb = math.gcd(rows, EXPERT_ROWS)
    w_idx = lambda i, j: (l, i, 0, 0)
    blk = lambda i, j: (i, j, 0)
    return pl.pallas_call(
        functools.partial(_expert_kernel, xs.dtype == jnp.int32),
        name="experts_%d" % rows,
        grid=(e, rows // rb),
        in_specs=[
            pl.BlockSpec((1, rb, w), blk),
            pl.BlockSpec((1, rb, HEAD_W), blk),
            pl.BlockSpec((1, 1, d, ff), w_idx),
            pl.BlockSpec((1, 1, d, ff), w_idx),
            pl.BlockSpec((1, 1, ff, d), w_idx),
        ],
        out_specs=pl.BlockSpec((1, rb, d), blk),
        out_shape=jax.ShapeDtypeStruct((e, rows, d), BF),
        scratch_shapes=[pltpu.VMEM((d, ff), BF), pltpu.VMEM((d, ff), BF), pltpu.VMEM((ff, d), BF)],
        compiler_params=_cparams(2),
    )(xs, gate, w_e_gate, w_e_up, w_e_down)


def _combine_rows(cap, d, x_ref, mod_ref, selt_ref, ye_ref, pt_ref, rows):
    e_tot = ye_ref.shape[0]
    selt = selt_ref[0, rows, :]
    r = selt.shape[0]
    if cap % 128 == 0:
        lane = lax.broadcasted_iota(jnp.int32, (r, cap), 1)
        for e_idx in range(e_tot):
            pt_ref[rows, e_idx * cap:(e_idx + 1) * cap] = jnp.where(selt[:, e_idx:e_idx + 1] == lane, 1.0, 0.0).astype(BF)
    else:
        assert cap & (cap - 1) == 0
        shift = cap.bit_length() - 1
        col = lax.broadcasted_iota(jnp.int32, (e_tot, e_tot * cap), 1)
        spread = jnp.where((col >> shift) == lax.broadcasted_iota(jnp.int32, col.shape, 0), 1.0, 0.0).astype(BF)
        slot_of_col = _dot(selt.astype(F32).astype(BF), spread)
        want = (lax.broadcasted_iota(jnp.int32, (r, e_tot * cap), 1) & (cap - 1)).astype(F32)
        pt_ref[rows, :] = jnp.where(slot_of_col == want, 1.0, 0.0).astype(BF)
    y = _dot(pt_ref[rows, :], ye_ref[...].reshape(e_tot * cap, d))
    return x_ref[0, rows, :] + mod_ref[0, :, 5 * d:6 * d] * y


def _scatter_kernel(cap, d, x_ref, mod_ref, selt_ref, ye_ref, gf_ref, o_ref, pt_ref):
    o_ref[0] = _rms(_combine_rows(cap, d, x_ref, mod_ref, selt_ref, ye_ref, pt_ref, slice(None)), gf_ref[...])


def _scatter(x, modv, sel_t, ye, cap, g_final):
    b, n, d = x.shape
    e = ye.shape[0]
    tn = min(PRE_ROWS, n)
    bm = modv.shape[0]
    mod_idx = (lambda i, j: (i, 0, 0)) if bm > 1 else (lambda i, j: (0, 0, 0))
    return pl.pallas_call(
        functools.partial(_scatter_kernel, cap, d),
        name="scatter_%d" % n,
        grid=(b, n // tn),
        in_specs=[
            pl.BlockSpec((1, tn, d), lambda i, j: (i, j, 0)),
            pl.BlockSpec((1, 1, modv.shape[2]), mod_idx),
            pl.BlockSpec((1, tn, e), lambda i, j: (i, j, 0)),
            pl.BlockSpec((e, 1, cap, d), lambda i, j: (0, i, 0, 0)),
            pl.BlockSpec((1, d), lambda i, j: (0, 0)),
        ],
        out_specs=pl.BlockSpec((1, tn, d), lambda i, j: (i, j, 0)),
        out_shape=jax.ShapeDtypeStruct((b, n, d), F32),
        scratch_shapes=[pltpu.VMEM((tn, e * cap), BF)],
        compiler_params=_cparams(2),
    )(x, modv, sel_t, ye, g_final.reshape(1, d))


def _dft_tables(n):
    j = np.arange(n, dtype=np.int64)
    ang = ((j[:, None] * j[None, :]) % n).astype(np.float64) * (2.0 * math.pi / n)
    return jnp.asarray(np.cos(ang), dtype=BF), jnp.asarray(np.sin(ang), dtype=BF)


def _block_diag(blocks):
    g, r, c = blocks.shape
    out = jnp.zeros((g * r, g * c), blocks.dtype)
    for i in range(g):
        out = out.at[i * r:(i + 1) * r, i * c:(i + 1) * c].set(blocks[i])
    return out


def _rope_tables(n):
    rows = n // GRID_W
    row = jnp.repeat(jnp.arange(rows, dtype=F32), GRID_W)
    col = jnp.tile(jnp.arange(GRID_W, dtype=F32), rows)

    def ang(dim):
        nf = dim // 4
        inv = ROPE_BASE ** (-jnp.arange(nf, dtype=F32) / nf)
        return jnp.concatenate([row[:, None] * inv, col[:, None] * inv], axis=-1)

    a = ang(DIFF_DH)
    cos_d = jnp.tile(jnp.cos(a), (1, 4))
    sin_d = jnp.tile(jnp.concatenate([-jnp.sin(a), jnp.sin(a)], axis=1), (1, 2))
    a = ang(MLA_ROPE)
    ones = jnp.ones((n, MLA_NOPE), F32)
    pad = HEAD_W - MLA_NOPE - MLA_ROPE
    cos_m = jnp.concatenate([ones, jnp.cos(a), jnp.cos(a), jnp.ones((n, pad), F32)], axis=1)
    sin_m = jnp.concatenate([0 * ones, -jnp.sin(a), jnp.sin(a), jnp.zeros((n, pad), F32)], axis=1)
    return cos_d, sin_d, cos_m, sin_m


def _pad_heads(w, heads, lo, hi):
    k = w.shape[0]
    w3 = w.reshape(k, heads, -1)[:, :, lo:hi]
    return jnp.pad(w3, ((0, 0), (0, 0), (0, HEAD_W - (hi - lo)))).reshape(k, heads * HEAD_W)


def kernel(x_prompt, x_sample, cache_diff_k, cache_diff_v, cache_mla_ckv, cache_mla_kpe, c, c_ctx, w_ada, b_ada, g_attn, g_ffn, w_in, w_four, diff_lambda, g_subln, g_mla_q, w_mla_uq, g_mla_kv, w_mla_ukv, w_out, w_router, w_e_gate, w_e_up, w_e_down, g_final):
    depth, d = g_attn.shape
    b_ctx, n_ctx, _ = x_prompt.shape
    b_lat, n_lat, _ = x_sample.shape

    r = b_lat + 1
    r_pad = -(-r // 8) * 8
    cond = jnp.concatenate([c, c_ctx[None, :], jnp.zeros((r_pad - r, d), F32)], axis=0)
    mod = _modulation(cond, w_ada, b_ada)

    n_main = w_in.shape[2] - MLA_ROPE
    w_in_p = jnp.concatenate(
        [w_in[:, :, :n_main], jnp.zeros((depth, d, MLA_NOPE), F32), w_in[:, :, n_main:],
         jnp.zeros((depth, d, HEAD_W - MLA_NOPE - MLA_ROPE), F32)], axis=2).astype(BF)
    w_uq_p = jnp.stack([_pad_heads(w_mla_uq[l], MLA_HEADS, 0, MLA_NOPE + MLA_ROPE) for l in range(depth)]).astype(BF)
    w_k_p = jnp.stack([_pad_heads(w_mla_ukv[l], MLA_HEADS, 0, MLA_NOPE) for l in range(depth)]).astype(BF)
    w_v_p = w_mla_ukv.reshape(depth, MLA_KV_RANK, MLA_HEADS, MLA_NOPE + MLA_V)[..., MLA_NOPE:].reshape(depth, MLA_KV_RANK, D_MLA).astype(BF)
    w_out_p = w_out.astype(BF)
    w_router_t = jnp.swapaxes(w_router, 1, 2).astype(BF)
    place = jnp.zeros((MLA_ROPE, HEAD_W), F32).at[jnp.arange(MLA_ROPE), MLA_NOPE + jnp.arange(MLA_ROPE)].set(1.0)
    place = jnp.tile(place, (1, MLA_HEADS)).astype(BF)

    jc = jnp.arange(F_CH, dtype=jnp.int32)
    ang_c = ((jc[:, None] * jc[None, :]) % F_CH).astype(F32) * (2.0 * math.pi / F_CH)
    cc_bd = _block_diag(jnp.broadcast_to(jnp.cos(ang_c), (F_GROUPS, F_CH, F_CH))).astype(BF)
    sc_bd = _block_diag(jnp.broadcast_to(jnp.sin(ang_c), (F_GROUPS, F_CH, F_CH))).astype(BF)
    dft = {n: _dft_tables(n) for n in {n_ctx, n_lat}}
    ropes = _rope_tables(n_lat)

    def pre(latent, l, x, prev=None, combine=None):
        modv = (mod[l, :b_lat] if latent else mod[l, b_lat:b_lat + 1]).reshape(-1, 1, N_MOD * d)
        outs = _pre(latent, x, modv, g_attn[l], w_in_p[l], g_mla_q[l], w_uq_p[l], g_mla_kv[l], w_k_p[l], w_v_p[l],
                    cc_bd, sc_bd, ropes, l, depth, prev, combine)
        if combine is not None:
            x, outs = outs[0], outs[1:]
        return x, modv, outs

    def mix_and_route(latent, l, x, modv, outs):
        b, n, _ = x.shape
        cap = max(1, EC_CAPACITY_FACTOR * n // N_EXPERTS)
        u, v, qd, kd, vd, qm, km, vm = outs[:8]
        cache = _cache_prep(l, cache_diff_k, cache_diff_v, cache_mla_ckv, cache_mla_kpe, w_k_p[l], w_v_p[l], place) if latent else None
        cn, sn = dft[n]
        yfour = _fourier(u, v, cn, sn, _block_diag(w_four[l]).astype(BF))
        x = _attention(latent, l, x, modv, yfour, qd, kd, vd, qm, km, vm, cache, diff_lambda[l], g_subln[l], w_out_p[l])
        h2, aff = _router(x, modv, g_ffn[l], w_router_t[l], packed=latent)
        sel = _select(aff, cap)
        e = sel.shape[1]
        if latent:
            xs, gate = _sc_dispatch(h2.reshape(b * n, h2.shape[2]), sel.reshape(b * e, n), aff.reshape(b * e, n), cap, e)
            gate = jnp.broadcast_to(gate.reshape(e, b * cap, 1), (e, b * cap, HEAD_W))
        else:
            xs, gate = _gather(sel, aff, h2, cap)
            gate = gate.reshape(e, b * cap, HEAD_W)
        return x, jnp.swapaxes(sel, 1, 2), cap, xs.reshape(e, b * cap, -1), gate

    x_l, mod_l, pre_l = pre(True, 0, x_sample)
    x_c, mod_c, pre_c = pre(False, 0, x_prompt)
    for l in range(depth):
        new = tuple(pre_c[8:])
        x_l, selt_l, cap_l, xs_l, gate_l = mix_and_route(True, l, x_l, mod_l, pre_l)
        x_c, selt_c, cap_c, xs_c, gate_c = mix_and_route(False, l, x_c, mod_c, pre_c)
        ye_c = _experts(l, xs_c, gate_c, w_e_gate, w_e_up, w_e_down).reshape(-1, b_ctx, cap_c, d)
        if l + 1 < depth:
            x_c, mod_c, pre_c = pre(False, l + 1, x_c, new, (mod_c, selt_c, ye_c, cap_c))
        else:
            x_c = _scatter(x_c, mod_c, selt_c, ye_c, cap_c, g_final)
        ye_l = _experts(l, xs_l, gate_l, w_e_gate, w_e_up, w_e_down).reshape(-1, b_lat, cap_l, d)
        if l + 1 < depth:
            x_l, mod_l, pre_l = pre(True, l + 1, x_l, None, (mod_l, selt_l, ye_l, cap_l))
        else:
            x_l = _scatter(x_l, mod_l, selt_l, ye_l, cap_l, g_final)
    y_prompt, y_sample = x_c, x_l
    new_diff_k, new_diff_v, new_mla_ckv, new_mla_kpe = new
    return (y_prompt, y_sample, new_diff_k, new_diff_v, new_mla_ckv, new_mla_kpe)
```

```python
import functools
import math

import jax
import jax.numpy as jnp
import numpy as np
from jax import lax
from jax.experimental import pallas as pl
from jax.experimental.pallas import tpu as pltpu
from jax.experimental.pallas import tpu_sc as plsc

BF = jnp.bfloat16
F32 = jnp.float32

GRID_W = 64
ROPE_BASE = 10000.0
NORM_EPS = 1e-6
F_GROUPS, F_CH = 4, 64
D_FOURIER = F_GROUPS * F_CH
DIFF_HEADS, DIFF_DH = 4, 64
D_DIFF = DIFF_HEADS * 2 * DIFF_DH
MLA_HEADS, MLA_Q_RANK, MLA_KV_RANK = 4, 256, 128
MLA_NOPE, MLA_ROPE, MLA_V = 64, 32, 64
HEAD_W = 128
D_MLA_P = MLA_HEADS * HEAD_W
D_MLA = MLA_HEADS * MLA_V
D_MIX = D_FOURIER + D_DIFF + D_MLA
ATTN_ROWS = 512
KEY_CHUNK = 256
SCORE_BUFS = 2
BF_ROWS = 16
SELECT_ROWS = 128
EXPERT_ROWS = 1024
EXPERT_SUB_ROWS = 512
PRE_ROWS = 512
SC_CHUNK = 64
SUB_ROWS = 256
N_EXPERTS = 16
EC_CAPACITY_FACTOR = 2
N_MOD = 6
IN_COLS_P = D_FOURIER + 3 * D_DIFF + MLA_Q_RANK + MLA_KV_RANK + HEAD_W
LOG2E = 1.4426950408889634
VMEM_LIMIT = 56 * 1024 * 1024


def _cparams(n_axes, vmem=VMEM_LIMIT):
    return pltpu.CompilerParams(dimension_semantics=("arbitrary",) * n_axes, vmem_limit_bytes=vmem)


def _dot(a, b):
    return jnp.dot(a, b, preferred_element_type=F32)


def _dot_nt(a, b):
    return lax.dot_general(a, b, (((1,), (1,)), ((), ())), preferred_element_type=F32)


def _rms(x, g):
    return x * lax.rsqrt(jnp.mean(x * x, axis=-1, keepdims=True) + NORM_EPS) * g


def _pack_halves(hb):
    w = hb.shape[1] // 2
    bits = pltpu.bitcast(hb.astype(F32), jnp.uint32)
    packed = (bits[:, :w] >> 16) | (bits[:, w:] & jnp.uint32(0xFFFF0000))
    return pltpu.bitcast(packed, jnp.int32)


def _unpack_halves(xi):
    bits = pltpu.bitcast(xi, jnp.uint32)
    lo = pltpu.bitcast(bits << 16, F32).astype(BF)
    hi = pltpu.bitcast(bits & jnp.uint32(0xFFFF0000), F32).astype(BF)
    return jnp.concatenate([lo, hi], axis=1)


def _rope(z, cos, sin_signed, half, group, lo):
    w = z.shape[1]
    reps = w // cos.shape[1]
    cos_w = jnp.concatenate([cos] * reps, axis=1) if reps > 1 else cos
    sin_w = jnp.concatenate([sin_signed] * reps, axis=1) if reps > 1 else sin_signed
    from_right = pltpu.roll(z, w - half, 1)
    from_left = pltpu.roll(z, half, 1)
    lane = lax.broadcasted_iota(jnp.int32, z.shape, 1) % group
    first = (lane >= lo) & (lane < lo + half)
    partner = jnp.where(first, from_right, from_left)
    return z * cos_w + partner * sin_w


def _mod_kernel(c_ref, w_ref, b_ref, o_ref):
    c = c_ref[...]
    a = (c * (1.0 / (1.0 + jnp.exp(-c)))).astype(BF)
    o_ref[0] = _dot(a, w_ref[0].astype(BF)) + b_ref[0]


def _modulation(cond, w_ada, b_ada):
    depth, d, n6 = w_ada.shape
    r = cond.shape[0]
    tn = 1536
    return pl.pallas_call(
        _mod_kernel,
        name="modulation",
        grid=(depth, n6 // tn),
        in_specs=[
            pl.BlockSpec((r, d), lambda l, j: (0, 0)),
            pl.BlockSpec((1, d, tn), lambda l, j: (l, 0, j)),
            pl.BlockSpec((1, 1, tn), lambda l, j: (l, 0, j)),
        ],
        out_specs=pl.BlockSpec((1, r, tn), lambda l, j: (l, 0, j)),
        out_shape=jax.ShapeDtypeStruct((depth, r, n6), F32),
        compiler_params=_cparams(2),
    )(cond, w_ada, b_ada.reshape(depth, 1, n6))


def _pre_kernel(latent, d, *refs):
    if latent:
        (x_ref, mod_ref, g_ref, win_ref, gq_ref, wuq_ref, gkv_ref, wk_ref, wv_ref, cc_ref, sc_ref,
         cd_ref, sd_ref, cm_ref, sm_ref,
         u_ref, v_ref, qd_ref, kd_ref, vd_ref, qm_ref, km_ref, vm_ref) = refs
    else:
        (x_ref, mod_ref, g_ref, win_ref, gq_ref, wuq_ref, gkv_ref, wk_ref, wv_ref, cc_ref, sc_ref,
         u_ref, v_ref, qd_ref, kd_ref, vd_ref, qm_ref, km_ref, vm_ref,
         k32_ref, v32_ref, ckv32_ref, kpe32_ref) = refs[:11] + refs[-12:]
    sh = mod_ref[0, :, 0:d]
    sc = mod_ref[0, :, d:2 * d]
    tm = x_ref.shape[1]
    sub = SUB_ROWS if tm > SUB_ROWS else max(tm // 2, BF_ROWS)
    def project(r0):
        h = _rms(x_ref[0, r0:r0 + sub, :], g_ref[...]) * (1.0 + sc) + sh
        return _dot(h.astype(BF), win_ref[...])

    starts = list(range(0, tm, sub))
    z_next = project(starts[0])
    for i, r0 in enumerate(starts):
        rows = slice(r0, r0 + sub)
        z = z_next
        z_next = project(starts[i + 1]) if i + 1 < len(starts) else None
        o = 0
        zf = z[:, o:o + D_FOURIER]; o += D_FOURIER
        zq = z[:, o:o + D_DIFF]; o += D_DIFF
        zk = z[:, o:o + D_DIFF]; o += D_DIFF
        zv = z[:, o:o + D_DIFF]; o += D_DIFF
        zcq = z[:, o:o + MLA_Q_RANK]; o += MLA_Q_RANK
        zckv = z[:, o:o + MLA_KV_RANK]; o += MLA_KV_RANK
        kpe = z[:, o:o + HEAD_W]

        zf_b = zf.astype(BF)
        u_ref[0, rows, :] = _dot(zf_b, cc_ref[...]).astype(BF)
        v_ref[0, rows, :] = _dot(zf_b, sc_ref[...]).astype(BF)
        vd_ref[0, :, rows] = zv.T.astype(BF)
        cq = _rms(zcq, gq_ref[...])
        qm = _dot(cq.astype(BF), wuq_ref[...])
        ckv = _rms(zckv, gkv_ref[...])
        ckv_b = ckv.astype(BF)
        k_nope = _dot(ckv_b, wk_ref[...])
        vm_ref[0, :, rows] = _dot(ckv_b, wv_ref[...]).T.astype(BF)
        if latent:
            cd, sd, cm, sm = cd_ref[rows, :], sd_ref[rows, :], cm_ref[rows, :], sm_ref[rows, :]
            zq_r = _rope(zq, cd, sd, DIFF_DH // 2, DIFF_DH, 0)
            zk_r = _rope(zk, cd, sd, DIFF_DH // 2, DIFF_DH, 0)
            qm = _rope(qm, cm, sm, MLA_ROPE // 2, HEAD_W, MLA_NOPE)
            kpe_r = _rope(kpe, cm, sm, MLA_ROPE // 2, HEAD_W, MLA_NOPE)
        else:
            zq_r, zk_r, kpe_r = zq, zk, kpe
            for hd in range(DIFF_HEADS):
                k32_ref[0, 0, rows, hd, :] = zk[:, hd * HEAD_W:(hd + 1) * HEAD_W]
                v32_ref[0, 0, rows, hd, :] = zv[:, hd * HEAD_W:(hd + 1) * HEAD_W]
            ckv32_ref[0, 0, rows, :] = ckv
            kpe32_ref[0, 0, rows, :] = kpe[:, MLA_NOPE:MLA_NOPE + MLA_ROPE]
            if i == 0:
                for ref in (k32_ref, v32_ref, ckv32_ref, kpe32_ref):
                    if ref.shape[1] > 1:
                        ref[0, 1:] = jnp.zeros((ref.shape[1] - 1,) + ref.shape[2:], F32)
        qd_ref[0, rows, :] = (zq_r * (DIFF_DH ** -0.5 * LOG2E)).astype(BF)
        kd_ref[0, rows, :] = zk_r.astype(BF)
        qm_ref[0, rows, :] = (qm * ((MLA_NOPE + MLA_ROPE) ** -0.5 * LOG2E)).astype(BF)
        km_ref[0, rows, :] = (k_nope + jnp.concatenate([kpe_r] * MLA_HEADS, axis=1)).astype(BF)


def _combine_pre_kernel(latent, d, cap, n_pre_in, *refs):
    x_ref, modp_ref, selt_ref, ye_ref = refs[:4]
    pre_in = refs[4:4 + n_pre_in]
    xo_ref = refs[4 + n_pre_in]
    pre_out = refs[5 + n_pre_in:-1]
    xo_ref[0] = _combine_rows(cap, d, x_ref, modp_ref, selt_ref, ye_ref, refs[-1], slice(None))
    _pre_kernel(latent, d, xo_ref, *pre_in, *pre_out)


def _pre(latent, x, modv, g_attn, w_in_p, g_q, w_uq_p, g_kv, w_k_p, w_v_p, cc_bd, sc_bd, ropes, l=0, depth=1, prev=None,
         combine=None):
    b, n, d = x.shape
    tm = min(PRE_ROWS, n)
    bm = modv.shape[0]
    mod_idx = (lambda i, j: (i, 0, 0)) if bm > 1 else (lambda i, j: (0, 0, 0))
    const = lambda i, j: (0, 0)
    tok = lambda i, j: (i, j, 0)
    in_specs = [
        pl.BlockSpec((1, tm, d), tok),
        pl.BlockSpec((1, 1, modv.shape[2]), mod_idx),
        pl.BlockSpec((1, d), const),
        pl.BlockSpec(w_in_p.shape, const),
        pl.BlockSpec((1, MLA_Q_RANK), const),
        pl.BlockSpec(w_uq_p.shape, const),
        pl.BlockSpec((1, MLA_KV_RANK), const),
        pl.BlockSpec(w_k_p.shape, const),
        pl.BlockSpec(w_v_p.shape, const),
        pl.BlockSpec(cc_bd.shape, const),
        pl.BlockSpec(sc_bd.shape, const),
    ]
    args = [x, modv, g_attn.reshape(1, d), w_in_p, g_q.reshape(1, -1), w_uq_p, g_kv.reshape(1, -1), w_k_p, w_v_p, cc_bd, sc_bd]
    if latent:
        in_specs += [pl.BlockSpec((tm, HEAD_W), lambda i, j: (j, 0))] * 4
        args += list(ropes)
    feat = lambda i, j: (i, 0, j)
    out_shape, out_specs = [], []
    for w, token_major in ((D_FOURIER, True), (D_FOURIER, True), (D_DIFF, True), (D_DIFF, True), (D_DIFF, False),
                           (D_MLA_P, True), (D_MLA_P, True), (D_MLA, False)):
        out_shape.append(jax.ShapeDtypeStruct((b, n, w) if token_major else (b, w, n), BF))
        out_specs.append(pl.BlockSpec((1, tm, w), tok) if token_major else pl.BlockSpec((1, w, tm), feat))
    aliases = {}
    if not latent:
        span = depth if prev is None else 1
        for tail in ((DIFF_HEADS, HEAD_W), (DIFF_HEADS, HEAD_W), (MLA_KV_RANK,), (MLA_ROPE,)):
            out_shape.append(jax.ShapeDtypeStruct((b, depth, n) + tail, F32))
            out_specs.append(pl.BlockSpec((1, span, tm) + tail, lambda i, j, t=len(tail): (i, l, j) + (0,) * t))
        if prev is not None:
            for k, buf in enumerate(prev):
                aliases[len(args)] = len(out_shape) - len(prev) + k
                in_specs.append(pl.BlockSpec(memory_space=pl.ANY))
                args.append(buf)
    body, scratch, tag = functools.partial(_pre_kernel, latent, d), [], "pre"
    if combine is not None:
        modp, sel_t, ye, cap = combine
        e = ye.shape[0]
        modp_idx = (lambda i, j: (i, 0, 0)) if modp.shape[0] > 1 else (lambda i, j: (0, 0, 0))
        extra_specs = [pl.BlockSpec((1, 1, modp.shape[2]), modp_idx), pl.BlockSpec((1, tm, e), tok),
                       pl.BlockSpec((e, 1, cap, d), lambda i, j: (0, i, 0, 0))]
        n_extra = len(extra_specs)
        body = functools.partial(_combine_pre_kernel, latent, d, cap, len(args) - 1)
        in_specs = in_specs[:1] + extra_specs + in_specs[1:]
        args = args[:1] + [modp, sel_t, ye] + args[1:]
        out_specs = [pl.BlockSpec((1, tm, d), tok)] + out_specs
        out_shape = [jax.ShapeDtypeStruct((b, n, d), F32)] + out_shape
        aliases = {k + n_extra: v + 1 for k, v in aliases.items()}
        scratch, tag = [pltpu.VMEM((tm, e * cap), BF)], "combine_pre"
    return pl.pallas_call(
        body,
        name=tag + ("_lat" if latent else "_ctx"),
        grid=(b, n // tm),
        in_specs=in_specs,
        out_specs=out_specs,
        out_shape=out_shape,
        input_output_aliases=aliases,
        scratch_shapes=scratch,
        compiler_params=_cparams(2),
    )(*args)


def _cache_kernel(dk_ref, dv_ref, ckv_ref, kpe_ref, wk_ref, wv_ref, place_ref, kd_ref, vd_ref, km_ref, vm_ref):
    for hd in range(DIFF_HEADS):
        sl = slice(hd * HEAD_W, (hd + 1) * HEAD_W)
        kd_ref[0, :, sl] = dk_ref[0, 0, :, hd, :].astype(BF)
        vd_ref[0, sl, :] = dv_ref[0, 0, :, hd, :].T.astype(BF)
    ckv_b = ckv_ref[0, 0].astype(BF)
    kpe_w = _dot(kpe_ref[0, 0].astype(BF), place_ref[...])
    km_ref[0] = (_dot(ckv_b, wk_ref[...]) + kpe_w).astype(BF)
    vm_ref[0] = _dot(ckv_b, wv_ref[...]).T.astype(BF)


def _cache_prep(l, cache_dk, cache_dv, cache_ckv, cache_kpe, w_k_p, w_v_p, place):
    b, _, p = cache_dk.shape[:3]
    at_l = lambda i: (i, l, 0, 0)
    at_l5 = lambda i: (i, l, 0, 0, 0)
    const = lambda i: (0, 0)
    shapes = [(p, D_DIFF), (D_DIFF, p), (p, D_MLA_P), (D_MLA, p)]
    return pl.pallas_call(
        _cache_kernel,
        name="cache_prep",
        grid=(b,),
        in_specs=[
            pl.BlockSpec((1, 1, p, DIFF_HEADS, HEAD_W), at_l5),
            pl.BlockSpec((1, 1, p, DIFF_HEADS, HEAD_W), at_l5),
            pl.BlockSpec((1, 1, p, MLA_KV_RANK), at_l),
            pl.BlockSpec((1, 1, p, MLA_ROPE), at_l),
            pl.BlockSpec(w_k_p.shape, const),
            pl.BlockSpec(w_v_p.shape, const),
            pl.BlockSpec(place.shape, const),
        ],
        out_specs=[pl.BlockSpec((1,) + s, lambda i: (i, 0, 0)) for s in shapes],
        out_shape=[jax.ShapeDtypeStruct((b,) + s, BF) for s in shapes],
        compiler_params=_cparams(1),
    )(cache_dk, cache_dv, cache_ckv, cache_kpe, w_k_p, w_v_p, place)


def _fourier_kernel(scale, u_ref, v_ref, cn_ref, sn_ref, w_ref, o_ref):
    f = (_dot(cn_ref[...], u_ref[0]) - _dot(sn_ref[...], v_ref[0])) * scale
    o_ref[0] = _dot(f.astype(BF), w_ref[...]).astype(BF)


def _fourier(u, v, cn, sn, w_bd):
    b, n, w = u.shape
    tm = min(512, n)
    return pl.pallas_call(
        functools.partial(_fourier_kernel, 1.0 / math.sqrt(n * F_CH)),
        name="fourier_%d" % n,
        grid=(n // tm, b),
        in_specs=[
            pl.BlockSpec((1, n, w), lambda i, j: (j, 0, 0)),
            pl.BlockSpec((1, n, w), lambda i, j: (j, 0, 0)),
            pl.BlockSpec((tm, n), lambda i, j: (i, 0)),
            pl.BlockSpec((tm, n), lambda i, j: (i, 0)),
            pl.BlockSpec((w, w), lambda i, j: (0, 0)),
        ],
        out_specs=pl.BlockSpec((1, tm, w), lambda i, j: (j, i, 0)),
        out_shape=jax.ShapeDtypeStruct((b, n, w), BF),
        compiler_params=_cparams(2),
    )(u, v, cn, sn, w_bd)


def _scores(q, k_parts, s_ref):
    off, m, bounds = 0, None, []
    for k in k_parts:
        s = _dot_nt(k(), q)
        n_i = s.shape[0]
        s_ref[off:off + n_i, :] = s
        mi = jnp.max(s, axis=0, keepdims=True)
        m = mi if m is None else jnp.maximum(m, mi)
        bounds.append((off, n_i))
        off += n_i
    return m, bounds


def _weighted(m, vt_parts, bounds, s_ref):
    o = None
    for vt, (r, n_i) in zip(vt_parts, bounds):
        v = vt()
        ones = jnp.where(lax.broadcasted_iota(jnp.int32, (BF_ROWS, n_i), 0) == 0, 1.0, 0.0).astype(BF)
        v_ext = jnp.concatenate([v, ones], axis=0)
        for c in range(0, n_i, KEY_CHUNK):
            rows = min(KEY_CHUNK, n_i - c)
            p = jnp.exp2(s_ref[r + c:r + c + rows, :] - m).astype(BF)
            oi = _dot(v_ext[:, c:c + rows], p)
            o = oi if o is None else o + oi
    dv = o.shape[0] - BF_ROWS
    return o[:dv] * (1.0 / o[dv:dv + 1])


def _attn_kernel(latent, lam_init, d, *refs):
    if latent:
        (x_ref, mod_ref, yf_ref, qd_ref, kd_ref, vd_ref, kdc_ref, vdc_ref, qm_ref, km_ref, vm_ref, kmc_ref, vmc_ref,
         lam_ref, gs_ref, wo_ref, o_ref, mix_ref, s_ref) = refs
    else:
        (x_ref, mod_ref, yf_ref, qd_ref, kd_ref, vd_ref, qm_ref, km_ref, vm_ref,
         lam_ref, gs_ref, wo_ref, o_ref, mix_ref, s_ref) = refs
        kdc_ref = vdc_ref = kmc_ref = vmc_ref = None
    lf = lam_ref[...]
    lam = (jnp.exp(jnp.sum(lf[0:1] * lf[1:2], axis=-1, keepdims=True))
           - jnp.exp(jnp.sum(lf[2:3] * lf[3:4], axis=-1, keepdims=True)) + lam_init)
    mix_ref[:, 0:D_FOURIER] = yf_ref[0]
    gs = gs_ref[...]

    def keys(ref, cache_ref, sl):
        parts = [lambda: ref[0, :, sl]]
        if latent:
            parts.append(lambda: cache_ref[0, :, sl])
        return parts

    def values_t(ref, cache_ref, sl):
        parts = [lambda: ref[0, sl, :]]
        if latent:
            parts.append(lambda: cache_ref[0, sl, :])
        return parts

    def diff_query(sl, comp):
        def load():
            qh = qd_ref[0, :, sl]
            lane = lax.broadcasted_iota(jnp.int32, qh.shape, 1)
            keep = (lane >= DIFF_DH) if comp else (lane < DIFF_DH)
            return jnp.where(keep, qh, jnp.zeros_like(qh))
        return load

    items = []
    for hd in range(DIFF_HEADS):
        sl = slice(hd * HEAD_W, (hd + 1) * HEAD_W)
        for comp in range(2):
            items.append((diff_query(sl, comp), keys(kd_ref, kdc_ref, sl), values_t(vd_ref, vdc_ref, sl)))
    for hd in range(MLA_HEADS):
        sl = slice(hd * HEAD_W, (hd + 1) * HEAD_W)
        vsl = slice(hd * MLA_V, (hd + 1) * MLA_V)
        items.append(((lambda sl=sl: qm_ref[0, :, sl]), keys(km_ref, kmc_ref, sl), values_t(vm_ref, vmc_ref, vsl)))

    def start(i):
        return _scores(items[i][0](), items[i][1], s_ref.at[i % SCORE_BUFS])

    outs = []
    state = start(0)
    for i in range(len(items)):
        nxt = start(i + 1) if i + 1 < len(items) else None
        m, bounds = state
        buf = i % SCORE_BUFS
        outs.append(_weighted(m, items[i][2], bounds, s_ref.at[buf]))
        state = nxt
        if i < 2 * DIFF_HEADS and i % 2 == 1:
            hd = i // 2
            o = outs[i - 1] - lam * outs[i]
            o = o * lax.rsqrt(jnp.mean(o * o, axis=0, keepdims=True) + NORM_EPS) * gs * (1.0 - lam_init)
            mix_ref[:, D_FOURIER + hd * HEAD_W:D_FOURIER + (hd + 1) * HEAD_W] = o.T.astype(BF)

    mix_ref[:, D_FOURIER + D_DIFF:D_MIX] = jnp.concatenate(outs[2 * DIFF_HEADS:], axis=0).T.astype(BF)
    y = _dot(mix_ref[...], wo_ref[...])
    gt = mod_ref[0, :, 2 * d:3 * d]
    o_ref[0] = x_ref[0] + gt * y


def _attention(latent, l, x, modv, yfour, qd, kd, vd, qm, km, vm, cache, diff_lambda_l, g_subln_l, w_out_b):
    b, n, d = x.shape
    tq = min(ATTN_ROWS, n)
    bm = modv.shape[0]
    mod_idx = (lambda i, j: (i, 0, 0)) if bm > 1 else (lambda i, j: (0, 0, 0))
    const = lambda i, j: (0, 0)
    tok = lambda i, j: (i, j, 0)
    full = lambda i, j: (i, 0, 0)
    p = cache[0].shape[1] if latent else 0
    in_specs = [
        pl.BlockSpec((1, tq, d), tok),
        pl.BlockSpec((1, 1, modv.shape[2]), mod_idx),
        pl.BlockSpec((1, tq, D_FOURIER), tok),
        pl.BlockSpec((1, tq, D_DIFF), tok),
        pl.BlockSpec((1, n, D_DIFF), full),
        pl.BlockSpec((1, D_DIFF, n), full),
    ]
    args = [x, modv, yfour, qd, kd, vd]
    if latent:
        kdc, vdc, kmc, vmc = cache
        in_specs += [pl.BlockSpec((1, p, D_DIFF), full), pl.BlockSpec((1, D_DIFF, p), full)]
        args += [kdc, vdc]
    in_specs += [pl.BlockSpec((1, tq, D_MLA_P), tok), pl.BlockSpec((1, n, D_MLA_P), full), pl.BlockSpec((1, D_MLA, n), full)]
    args += [qm, km, vm]
    if latent:
        in_specs += [pl.BlockSpec((1, p, D_MLA_P), full), pl.BlockSpec((1, D_MLA, p), full)]
        args += [kmc, vmc]
    in_specs += [pl.BlockSpec((4, DIFF_DH), const), pl.BlockSpec((HEAD_W, 1), const), pl.BlockSpec(w_out_b.shape, const)]
    args += [diff_lambda_l, g_subln_l.reshape(HEAD_W, 1), w_out_b]
    lam_init = 0.8 - 0.6 * math.exp(-0.3 * l)
    return pl.pallas_call(
        functools.partial(_attn_kernel, latent, lam_init, d),
        name="attn_lat" if latent else "attn_ctx",
        grid=(b, n // tq),
        in_specs=in_specs,
        out_specs=pl.BlockSpec((1, tq, d), tok),
        out_shape=jax.ShapeDtypeStruct((b, n, d), F32),
        scratch_shapes=[pltpu.VMEM((tq, D_MIX), BF), pltpu.VMEM((SCORE_BUFS, n + p, tq), F32)],
        compiler_params=_cparams(2),
    )(*args)


def _router_kernel(d, packed, x_ref, mod_ref, g_ref, wr_ref, h_ref, aff_ref):
    sh = mod_ref[0, :, 3 * d:4 * d]
    sc = mod_ref[0, :, 4 * d:5 * d]
    tm = x_ref.shape[1]
    sub = SUB_ROWS if tm > SUB_ROWS else max(tm // 2, BF_ROWS)
    for r0 in range(0, tm, sub):
        rows = slice(r0, r0 + sub)
        hb = (_rms(x_ref[0, rows, :], g_ref[...]) * (1.0 + sc) + sh).astype(BF)
        h_ref[0, rows, :] = _pack_halves(hb) if packed else hb
        logits = _dot_nt(wr_ref[...], hb)
        e = jnp.exp(logits - jnp.max(logits, axis=0, keepdims=True))
        aff_ref[0, :, rows] = e / jnp.sum(e, axis=0, keepdims=True)


def _router(x, modv, g_ffn_l, w_router_t, packed):
    b, n, d = x.shape
    hw, hdt = (d // 2, jnp.int32) if packed else (d, BF)
    tm = min(PRE_ROWS, n)
    bm = modv.shape[0]
    mod_idx = (lambda i, j: (i, 0, 0)) if bm > 1 else (lambda i, j: (0, 0, 0))
    const = lambda i, j: (0, 0)
    return pl.pallas_call(
        functools.partial(_router_kernel, d, packed),
        name="router_%d" % n,
        grid=(b, n // tm),
        in_specs=[
            pl.BlockSpec((1, tm, d), lambda i, j: (i, j, 0)),
            pl.BlockSpec((1, 1, modv.shape[2]), mod_idx),
            pl.BlockSpec((1, d), const),
            pl.BlockSpec(w_router_t.shape, const),
        ],
        out_specs=[pl.BlockSpec((1, tm, hw), lambda i, j: (i, j, 0)), pl.BlockSpec((1, N_EXPERTS, tm), lambda i, j: (i, 0, j))],
        out_shape=[jax.ShapeDtypeStruct((b, n, hw), hdt), jax.ShapeDtypeStruct((b, N_EXPERTS, n), F32)],
        compiler_params=_cparams(2),
    )(x, modv, g_ffn_l.reshape(1, d), w_router_t)


def _select_kernel(cap, aff_ref, sel_ref):
    a = aff_ref[...]
    e, n = a.shape
    bits = pltpu.bitcast(a, jnp.int32)
    capf = float(cap)

    def body(_, carry):
        lo, hi = carry
        mid = lo + ((hi - lo) >> 1)
        cnt = jnp.sum(jnp.where(bits >= mid, 1.0, 0.0), axis=1, keepdims=True)
        up = cnt >= capf
        return jnp.where(up, mid, lo), jnp.where(up, hi, mid)

    lo0 = jnp.zeros((e, 1), jnp.int32)
    hi0 = jnp.full((e, 1), 0x7F800000, jnp.int32)
    thr, _ = lax.fori_loop(0, 31, body, (lo0, hi0))
    gt = bits > thr
    eq = bits == thr
    need = capf - jnp.sum(jnp.where(gt, 1.0, 0.0), axis=1, keepdims=True)
    both = jnp.concatenate([jnp.where(gt, 1.0, 0.0), jnp.where(eq, 1.0, 0.0)], axis=0).astype(BF)
    ck = min(256, n)
    tri = jnp.where(lax.broadcasted_iota(jnp.int32, (ck, ck), 0) < lax.broadcasted_iota(jnp.int32, (ck, ck), 1), 1.0, 0.0).astype(BF)
    off = jnp.zeros((2 * e, 1), F32)
    pieces = []
    for k in range(n // ck):
        blk = both[:, k * ck:(k + 1) * ck]
        pieces.append(_dot(blk, tri) + off)
        off = off + jnp.sum(blk.astype(F32), axis=1, keepdims=True)
    cum = jnp.concatenate(pieces, axis=1) if len(pieces) > 1 else pieces[0]
    cum_gt, cum_eq = cum[:e], cum[e:]
    chosen = gt | (eq & (cum_eq < need))
    pos = cum_gt + jnp.minimum(cum_eq, need)
    sel_ref[...] = jnp.where(chosen, pos, -1.0).astype(jnp.int32)


def _select(aff, cap):
    b, e, n = aff.shape
    rows = b * e
    rb = min(rows, SELECT_ROWS)
    sel = pl.pallas_call(
        functools.partial(_select_kernel, cap),
        name="select_%d" % n,
        grid=(rows // rb,),
        in_specs=[pl.BlockSpec((rb, n), lambda i: (i, 0))],
        out_specs=pl.BlockSpec((rb, n), lambda i: (i, 0)),
        out_shape=jax.ShapeDtypeStruct((rows, n), jnp.int32),
        compiler_params=_cparams(1),
    )(aff.reshape(rows, n))
    return sel.reshape(b, e, n)


def _gather_kernel(cap, sel_ref, aff_ref, h_ref, xs_ref, gate_ref):
    h = h_ref[0]
    n = h.shape[0]
    e_tot = sel_ref.shape[1]

    def one(e_idx):
        sel_e = sel_ref[0, pl.ds(e_idx, 1), :]
        aff_e = aff_ref[0, pl.ds(e_idx, 1), :]
        hit = sel_e == lax.broadcasted_iota(jnp.int32, (cap, n), 0)
        g = jnp.sum(jnp.where(hit, aff_e, 0.0), axis=1, keepdims=True)
        return jnp.where(hit, 1.0, 0.0).astype(BF), jnp.broadcast_to(g, (cap, HEAD_W))

    if cap >= 128:
        def body(e_idx, carry):
            p, g = one(e_idx)
            xs_ref[e_idx, 0] = _dot(p, h).astype(BF)
            gate_ref[e_idx, 0] = g
            return carry

        lax.fori_loop(0, e_tot, body, 0)
    else:
        ps, gs = zip(*[one(e_idx) for e_idx in range(e_tot)])
        xs = _dot(jnp.concatenate(ps, axis=0), h).astype(BF)
        for e_idx in range(e_tot):
            xs_ref[e_idx, 0] = xs[e_idx * cap:(e_idx + 1) * cap]
            gate_ref[e_idx, 0] = gs[e_idx]


def _gather(sel, aff, h2, cap):
    b, e, n = sel.shape
    d = h2.shape[2]
    return pl.pallas_call(
        functools.partial(_gather_kernel, cap),
        name="gather_%d" % n,
        grid=(b,),
        in_specs=[
            pl.BlockSpec((1, e, n), lambda i: (i, 0, 0)),
            pl.BlockSpec((1, e, n), lambda i: (i, 0, 0)),
            pl.BlockSpec((1, n, d), lambda i: (i, 0, 0)),
        ],
        out_specs=[pl.BlockSpec((e, 1, cap, d), lambda i: (0, i, 0, 0)), pl.BlockSpec((e, 1, cap, HEAD_W), lambda i: (0, i, 0, 0))],
        out_shape=[jax.ShapeDtypeStruct((e, b, cap, d), BF), jax.ShapeDtypeStruct((e, b, cap, HEAD_W), F32)],
        compiler_params=_cparams(1),
    )(sel, aff, h2)


def _sc_dispatch(table, sel, aff, cap, n_experts):
    info = plsc.get_sparse_core_info()
    n_workers = info.num_cores * info.num_subcores
    lanes = info.num_lanes
    pairs, n = sel.shape
    b_tot = pairs // n_experts
    w = table.shape[1]
    per_w = pairs // n_workers
    ch = min(SC_CHUNK, cap)
    n_ch = cap // ch
    assert pairs % n_workers == 0 and cap % ch == 0 and ch % 8 == 0 and n % lanes == 0
    mesh = plsc.VectorSubcoreMesh(core_axis_name="c", subcore_axis_name="s")

    @functools.partial(
        pl.kernel, mesh=mesh, name="sc_dispatch",
        compiler_params=pltpu.CompilerParams(needs_layout_passes=False),
        out_type=[jax.ShapeDtypeStruct((pairs * cap, w), jnp.int32), jax.ShapeDtypeStruct((pairs * cap,), F32)],
        scratch_types=[pltpu.VMEM((n,), jnp.int32), pltpu.VMEM((n,), F32), pltpu.VMEM((n_ch, ch), jnp.int32),
                       pltpu.VMEM((cap,), F32), pltpu.VMEM((2, ch, w), jnp.int32), pltpu.SemaphoreType.DMA((2,))],
    )
    def k(table_hbm, sel_hbm, aff_hbm, xs_hbm, gate_hbm, sel_v, aff_v, tok_v, gate_v, rows_v, sem):
        wid = lax.axis_index("s") * info.num_cores + lax.axis_index("c")

        @pl.loop(0, per_w)
        def _(i):
            p = wid * per_w + i
            b = p // n_experts
            e = p - b * n_experts
            out_off = pl.multiple_of((e * b_tot + b) * cap, 8)
            pltpu.sync_copy(sel_hbm.at[p], sel_v)
            pltpu.sync_copy(aff_hbm.at[p], aff_v)
            lane = lax.iota(jnp.int32, lanes)

            @plsc.parallel_loop(0, n // lanes, unroll=4)
            def _(j):
                t0 = pl.multiple_of(j * lanes, lanes)
                s = sel_v[pl.ds(t0, lanes)]
                chosen = s >= 0
                slot = jnp.maximum(s, 0)
                plsc.store_scatter(tok_v, [slot // ch, slot % ch], lane + (t0 + b * n), mask=chosen)
                plsc.store_scatter(gate_v, [slot], aff_v[pl.ds(t0, lanes)], mask=chosen)

            prev = None
            for c in range(n_ch + 1):
                cur = None
                if c < n_ch:
                    cur = pltpu.async_copy(table_hbm.at[tok_v.at[c]], rows_v.at[c % 2], sem.at[c % 2])
                if prev is not None:
                    prev.wait()
                    pltpu.sync_copy(rows_v.at[(c - 1) % 2], xs_hbm.at[pl.ds(out_off + (c - 1) * ch, ch)])
                prev = cur
            pltpu.sync_copy(gate_v, gate_hbm.at[pl.ds(out_off, cap)])

    return k(table, sel, aff)


def _expert_kernel(steps_a, xa_ref, ga_ref, xb_ref, gb_ref, wg_ref, wu_ref, wd_ref, ya_ref, yb_ref, wg_s, wu_s, wd_s):
    j = pl.program_id(1)

    @pl.when(j == 0)
    def _():
        wg_s[...] = wg_ref[0, 0].astype(BF)
        wu_s[...] = wu_ref[0, 0].astype(BF)
        wd_s[...] = wd_ref[0, 0].astype(BF)

    def ffn(x_ref, g_ref, y_ref):
        rb = x_ref.shape[1]
        sub = min(rb, EXPERT_SUB_ROWS)
        for r0 in range(0, rb, sub):
            rows = slice(r0, r0 + sub)
            x = _unpack_halves(x_ref[0, rows, :]) if x_ref.dtype == jnp.int32 else x_ref[0, rows, :]
            a = _dot(x, wg_s[...])
            u = _dot(x, wu_s[...])
            mid = (a * (1.0 / (1.0 + jnp.exp(-a))) * u).astype(BF)
            y_ref[0, rows, :] = (_dot(mid, wd_s[...]) * g_ref[0, rows, 0:1]).astype(BF)

    @pl.when(j < steps_a)
    def _():
        ffn(xa_ref, ga_ref, ya_ref)

    @pl.when(j >= steps_a)
    def _():
        ffn(xb_ref, gb_ref, yb_ref)


def _experts(l, xs_a, gate_a, xs_b, gate_b, w_e_gate, w_e_up, w_e_down):
    e, rows_a, wa = xs_a.shape
    rows_b, wb = xs_b.shape[1:]
    d, ff = w_e_gate.shape[2:]
    rb_a, rb_b = math.gcd(rows_a, EXPERT_ROWS), math.gcd(rows_b, EXPERT_ROWS)
    steps_a, steps_b = rows_a // rb_a, rows_b // rb_b
    w_idx = lambda i, j: (l, i, 0, 0)
    idx_a = lambda i, j: (i, jnp.minimum(j, steps_a - 1), 0)
    idx_b = lambda i, j: (i, jnp.maximum(j - steps_a, 0), 0)
    return pl.pallas_call(
        functools.partial(_expert_kernel, steps_a),
        name="experts",
        grid=(e, steps_a + steps_b),
        in_specs=[
            pl.BlockSpec((1, rb_a, wa), idx_a),
            pl.BlockSpec((1, rb_a, HEAD_W), idx_a),
            pl.BlockSpec((1, rb_b, wb), idx_b),
            pl.BlockSpec((1, rb_b, HEAD_W), idx_b),
            pl.BlockSpec((1, 1, d, ff), w_idx),
            pl.BlockSpec((1, 1, d, ff), w_idx),
            pl.BlockSpec((1, 1, ff, d), w_idx),
        ],
        out_specs=[pl.BlockSpec((1, rb_a, d), idx_a), pl.BlockSpec((1, rb_b, d), idx_b)],
        out_shape=[jax.ShapeDtypeStruct((e, rows_a, d), BF), jax.ShapeDtypeStruct((e, rows_b, d), BF)],
        scratch_shapes=[pltpu.VMEM((d, ff), BF), pltpu.VMEM((d, ff), BF), pltpu.VMEM((ff, d), BF)],
        compiler_params=_cparams(2),
    )(xs_a, gate_a, xs_b, gate_b, w_e_gate, w_e_up, w_e_down)


def _combine_rows(cap, d, x_ref, mod_ref, selt_ref, ye_ref, pt_ref, rows):
    e_tot = ye_ref.shape[0]
    selt = selt_ref[0, rows, :]
    r = selt.shape[0]
    if cap % 128 == 0:
        lane = lax.broadcasted_iota(jnp.int32, (r, cap), 1)
        for e_idx in range(e_tot):
            pt_ref[rows, e_idx * cap:(e_idx + 1) * cap] = jnp.where(selt[:, e_idx:e_idx + 1] == lane, 1.0, 0.0).astype(BF)
    else:
        assert cap & (cap - 1) == 0
        shift = cap.bit_length() - 1
        col = lax.broadcasted_iota(jnp.int32, (e_tot, e_tot * cap), 1)
        spread = jnp.where((col >> shift) == lax.broadcasted_iota(jnp.int32, col.shape, 0), 1.0, 0.0).astype(BF)
        slot_of_col = _dot(selt.astype(F32).astype(BF), spread)
        want = (lax.broadcasted_iota(jnp.int32, (r, e_tot * cap), 1) & (cap - 1)).astype(F32)
        pt_ref[rows, :] = jnp.where(slot_of_col == want, 1.0, 0.0).astype(BF)
    y = _dot(pt_ref[rows, :], ye_ref[...].reshape(e_tot * cap, d))
    return x_ref[0, rows, :] + mod_ref[0, :, 5 * d:6 * d] * y


def _scatter_kernel(cap, d, x_ref, mod_ref, selt_ref, ye_ref, gf_ref, o_ref, pt_ref):
    o_ref[0] = _rms(_combine_rows(cap, d, x_ref, mod_ref, selt_ref, ye_ref, pt_ref, slice(None)), gf_ref[...])


def _scatter(x, modv, sel_t, ye, cap, g_final):
    b, n, d = x.shape
    e = ye.shape[0]
    tn = min(PRE_ROWS, n)
    bm = modv.shape[0]
    mod_idx = (lambda i, j: (i, 0, 0)) if bm > 1 else (lambda i, j: (0, 0, 0))
    return pl.pallas_call(
        functools.partial(_scatter_kernel, cap, d),
        name="scatter_%d" % n,
        grid=(b, n // tn),
        in_specs=[
            pl.BlockSpec((1, tn, d), lambda i, j: (i, j, 0)),
            pl.BlockSpec((1, 1, modv.shape[2]), mod_idx),
            pl.BlockSpec((1, tn, e), lambda i, j: (i, j, 0)),
            pl.BlockSpec((e, 1, cap, d), lambda i, j: (0, i, 0, 0)),
            pl.BlockSpec((1, d), lambda i, j: (0, 0)),
        ],
        out_specs=pl.BlockSpec((1, tn, d), lambda i, j: (i, j, 0)),
        out_shape=jax.ShapeDtypeStruct((b, n, d), F32),
        scratch_shapes=[pltpu.VMEM((tn, e * cap), BF)],
        compiler_params=_cparams(2),
    )(x, modv, sel_t, ye, g_final.reshape(1, d))


def _dft_tables(n):
    j = np.arange(n, dtype=np.int64)
    ang = ((j[:, None] * j[None, :]) % n).astype(np.float64) * (2.0 * math.pi / n)
    return jnp.asarray(np.cos(ang), dtype=BF), jnp.asarray(np.sin(ang), dtype=BF)


def _block_diag(blocks):
    g, r, c = blocks.shape
    out = jnp.zeros((g * r, g * c), blocks.dtype)
    for i in range(g):
        out = out.at[i * r:(i + 1) * r, i * c:(i + 1) * c].set(blocks[i])
    return out


def _rope_tables(n):
    rows = n // GRID_W
    row = jnp.repeat(jnp.arange(rows, dtype=F32), GRID_W)
    col = jnp.tile(jnp.arange(GRID_W, dtype=F32), rows)

    def ang(dim):
        nf = dim // 4
        inv = ROPE_BASE ** (-jnp.arange(nf, dtype=F32) / nf)
        return jnp.concatenate([row[:, None] * inv, col[:, None] * inv], axis=-1)

    a = ang(DIFF_DH)
    cos_d = jnp.tile(jnp.cos(a), (1, 4))
    sin_d = jnp.tile(jnp.concatenate([-jnp.sin(a), jnp.sin(a)], axis=1), (1, 2))
    a = ang(MLA_ROPE)
    ones = jnp.ones((n, MLA_NOPE), F32)
    pad = HEAD_W - MLA_NOPE - MLA_ROPE
    cos_m = jnp.concatenate([ones, jnp.cos(a), jnp.cos(a), jnp.ones((n, pad), F32)], axis=1)
    sin_m = jnp.concatenate([0 * ones, -jnp.sin(a), jnp.sin(a), jnp.zeros((n, pad), F32)], axis=1)
    return cos_d, sin_d, cos_m, sin_m


def _pad_heads(w, heads, lo, hi):
    k = w.shape[0]
    w3 = w.reshape(k, heads, -1)[:, :, lo:hi]
    return jnp.pad(w3, ((0, 0), (0, 0), (0, HEAD_W - (hi - lo)))).reshape(k, heads * HEAD_W)


def kernel(x_prompt, x_sample, cache_diff_k, cache_diff_v, cache_mla_ckv, cache_mla_kpe, c, c_ctx, w_ada, b_ada, g_attn, g_ffn, w_in, w_four, diff_lambda, g_subln, g_mla_q, w_mla_uq, g_mla_kv, w_mla_ukv, w_out, w_router, w_e_gate, w_e_up, w_e_down, g_final):
    depth, d = g_attn.shape
    b_ctx, n_ctx, _ = x_prompt.shape
    b_lat, n_lat, _ = x_sample.shape

    r = b_lat + 1
    r_pad = -(-r // 8) * 8
    cond = jnp.concatenate([c, c_ctx[None, :], jnp.zeros((r_pad - r, d), F32)], axis=0)
    mod = _modulation(cond, w_ada, b_ada)

    n_main = w_in.shape[2] - MLA_ROPE
    w_in_p = jnp.concatenate(
        [w_in[:, :, :n_main], jnp.zeros((depth, d, MLA_NOPE), F32), w_in[:, :, n_main:],
         jnp.zeros((depth, d, HEAD_W - MLA_NOPE - MLA_ROPE), F32)], axis=2).astype(BF)
    w_uq_p = jnp.stack([_pad_heads(w_mla_uq[l], MLA_HEADS, 0, MLA_NOPE + MLA_ROPE) for l in range(depth)]).astype(BF)
    w_k_p = jnp.stack([_pad_heads(w_mla_ukv[l], MLA_HEADS, 0, MLA_NOPE) for l in range(depth)]).astype(BF)
    w_v_p = w_mla_ukv.reshape(depth, MLA_KV_RANK, MLA_HEADS, MLA_NOPE + MLA_V)[..., MLA_NOPE:].reshape(depth, MLA_KV_RANK, D_MLA).astype(BF)
    w_out_p = w_out.astype(BF)
    w_router_t = jnp.swapaxes(w_router, 1, 2).astype(BF)
    place = jnp.zeros((MLA_ROPE, HEAD_W), F32).at[jnp.arange(MLA_ROPE), MLA_NOPE + jnp.arange(MLA_ROPE)].set(1.0)
    place = jnp.tile(place, (1, MLA_HEADS)).astype(BF)

    jc = jnp.arange(F_CH, dtype=jnp.int32)
    ang_c = ((jc[:, None] * jc[None, :]) % F_CH).astype(F32) * (2.0 * math.pi / F_CH)
    cc_bd = _block_diag(jnp.broadcast_to(jnp.cos(ang_c), (F_GROUPS, F_CH, F_CH))).astype(BF)
    sc_bd = _block_diag(jnp.broadcast_to(jnp.sin(ang_c), (F_GROUPS, F_CH, F_CH))).astype(BF)
    dft = {n: _dft_tables(n) for n in {n_ctx, n_lat}}
    ropes = _rope_tables(n_lat)

    def pre(latent, l, x, prev=None, combine=None):
        modv = (mod[l, :b_lat] if latent else mod[l, b_lat:b_lat + 1]).reshape(-1, 1, N_MOD * d)
        outs = _pre(latent, x, modv, g_attn[l], w_in_p[l], g_mla_q[l], w_uq_p[l], g_mla_kv[l], w_k_p[l], w_v_p[l],
                    cc_bd, sc_bd, ropes, l, depth, prev, combine)
        if combine is not None:
            x, outs = outs[0], outs[1:]
        return x, modv, outs

    def cache_prep(l):
        return _cache_prep(l, cache_diff_k, cache_diff_v, cache_mla_ckv, cache_mla_kpe, w_k_p[l], w_v_p[l], place)

    def mix_and_route(latent, l, x, modv, outs, cache=None):
        b, n, _ = x.shape
        cap = max(1, EC_CAPACITY_FACTOR * n // N_EXPERTS)
        u, v, qd, kd, vd, qm, km, vm = outs[:8]
        cn, sn = dft[n]
        yfour = _fourier(u, v, cn, sn, _block_diag(w_four[l]).astype(BF))
        x = _attention(latent, l, x, modv, yfour, qd, kd, vd, qm, km, vm, cache, diff_lambda[l], g_subln[l], w_out_p[l])
        h2, aff = _router(x, modv, g_ffn[l], w_router_t[l], packed=latent)
        sel = _select(aff, cap)
        e = sel.shape[1]
        if latent:
            xs, gate = _sc_dispatch(h2.reshape(b * n, h2.shape[2]), sel.reshape(b * e, n), aff.reshape(b * e, n), cap, e)
            gate = jnp.broadcast_to(gate.reshape(e, b * cap, 1), (e, b * cap, HEAD_W))
        else:
            xs, gate = _gather(sel, aff, h2, cap)
            gate = gate.reshape(e, b * cap, HEAD_W)
        return x, jnp.swapaxes(sel, 1, 2), cap, xs.reshape(e, b * cap, -1), gate

    x_l, mod_l, pre_l = pre(True, 0, x_sample)
    x_c, mod_c, pre_c = pre(False, 0, x_prompt)
    cache = cache_prep(0)
    for l in range(depth):
        new = tuple(pre_c[8:])
        x_l, selt_l, cap_l, xs_l, gate_l = mix_and_route(True, l, x_l, mod_l, pre_l, cache)
        x_c, selt_c, cap_c, xs_c, gate_c = mix_and_route(False, l, x_c, mod_c, pre_c)
        if l + 1 < depth:
            cache = cache_prep(l + 1)
        ye_c, ye_l = _experts(l, xs_c, gate_c, xs_l, gate_l, w_e_gate, w_e_up, w_e_down)
        ye_c = ye_c.reshape(-1, b_ctx, cap_c, d)
        ye_l = ye_l.reshape(-1, b_lat, cap_l, d)
        if l + 1 < depth:
            x_c, mod_c, pre_c = pre(False, l + 1, x_c, new, (mod_c, selt_c, ye_c, cap_c))
            x_l, mod_l, pre_l = pre(True, l + 1, x_l, None, (mod_l, selt_l, ye_l, cap_l))
        else:
            x_c = _scatter(x_c, mod_c, selt_c, ye_c, cap_c, g_final)
            x_l = _scatter(x_l, mod_l, selt_l, ye_l, cap_l, g_final)
    y_prompt, y_sample = x_c, x_l
    new_diff_k, new_diff_v, new_mla_ckv, new_mla_kpe = new
    return (y_prompt, y_sample, new_diff_k, new_diff_v, new_mla_ckv, new_mla_kpe)
```

```python
import functools
import math

import jax
import jax.numpy as jnp
import numpy as np
from jax import lax
from jax.experimental import pallas as pl
from jax.experimental.pallas import tpu as pltpu
from jax.experimental.pallas import tpu_sc as plsc

BF = jnp.bfloat16
F32 = jnp.float32

GRID_W = 64
ROPE_BASE = 10000.0
NORM_EPS = 1e-6
F_GROUPS, F_CH = 4, 64
D_FOURIER = F_GROUPS * F_CH
DIFF_HEADS, DIFF_DH = 4, 64
D_DIFF = DIFF_HEADS * 2 * DIFF_DH
MLA_HEADS, MLA_Q_RANK, MLA_KV_RANK = 4, 256, 128
MLA_NOPE, MLA_ROPE, MLA_V = 64, 32, 64
HEAD_W = 128
D_MLA_P = MLA_HEADS * HEAD_W
D_MLA = MLA_HEADS * MLA_V
D_MIX = D_FOURIER + D_DIFF + D_MLA
ATTN_ROWS = 512
ATTN_GROUPS = 2
KEY_CHUNK = 256
SCORE_BUFS = 2
BF_ROWS = 16
SELECT_ROWS = 128
EXPERT_ROWS = 1024
EXPERT_SUB_ROWS = 512
PRE_ROWS = 512
SC_CHUNK = 64
SUB_ROWS = 256
N_EXPERTS = 16
EC_CAPACITY_FACTOR = 2
N_MOD = 6
IN_COLS_P = D_FOURIER + 3 * D_DIFF + MLA_Q_RANK + MLA_KV_RANK + HEAD_W
LOG2E = 1.4426950408889634
VMEM_LIMIT = 56 * 1024 * 1024


def _cparams(n_axes, vmem=VMEM_LIMIT):
    return pltpu.CompilerParams(dimension_semantics=("arbitrary",) * n_axes, vmem_limit_bytes=vmem)


def _dot(a, b):
    return jnp.dot(a, b, preferred_element_type=F32)


def _dot_nt(a, b):
    return lax.dot_general(a, b, (((1,), (1,)), ((), ())), preferred_element_type=F32)


def _rms(x, g):
    return x * lax.rsqrt(jnp.mean(x * x, axis=-1, keepdims=True) + NORM_EPS) * g


def _pack_halves(hb):
    w = hb.shape[1] // 2
    bits = pltpu.bitcast(hb.astype(F32), jnp.uint32)
    packed = (bits[:, :w] >> 16) | (bits[:, w:] & jnp.uint32(0xFFFF0000))
    return pltpu.bitcast(packed, jnp.int32)


def _unpack_halves(xi):
    bits = pltpu.bitcast(xi, jnp.uint32)
    lo = pltpu.bitcast(bits << 16, F32).astype(BF)
    hi = pltpu.bitcast(bits & jnp.uint32(0xFFFF0000), F32).astype(BF)
    return jnp.concatenate([lo, hi], axis=1)


def _rope(z, cos, sin_signed, half, group, lo):
    w = z.shape[1]
    reps = w // cos.shape[1]
    cos_w = jnp.concatenate([cos] * reps, axis=1) if reps > 1 else cos
    sin_w = jnp.concatenate([sin_signed] * reps, axis=1) if reps > 1 else sin_signed
    from_right = pltpu.roll(z, w - half, 1)
    from_left = pltpu.roll(z, half, 1)
    lane = lax.broadcasted_iota(jnp.int32, z.shape, 1) % group
    first = (lane >= lo) & (lane < lo + half)
    partner = jnp.where(first, from_right, from_left)
    return z * cos_w + partner * sin_w


def _mod_kernel(c_ref, w_ref, b_ref, o_ref):
    c = c_ref[...]
    a = (c * (1.0 / (1.0 + jnp.exp(-c)))).astype(BF)
    o_ref[0] = _dot(a, w_ref[0].astype(BF)) + b_ref[0]


def _modulation(cond, w_ada, b_ada):
    depth, d, n6 = w_ada.shape
    r = cond.shape[0]
    tn = 1536
    return pl.pallas_call(
        _mod_kernel,
        name="modulation",
        grid=(depth, n6 // tn),
        in_specs=[
            pl.BlockSpec((r, d), lambda l, j: (0, 0)),
            pl.BlockSpec((1, d, tn), lambda l, j: (l, 0, j)),
            pl.BlockSpec((1, 1, tn), lambda l, j: (l, 0, j)),
        ],
        out_specs=pl.BlockSpec((1, r, tn), lambda l, j: (l, 0, j)),
        out_shape=jax.ShapeDtypeStruct((depth, r, n6), F32),
        compiler_params=_cparams(2),
    )(cond, w_ada, b_ada.reshape(depth, 1, n6))


def _pre_kernel(latent, d, *refs):
    if latent:
        (x_ref, mod_ref, g_ref, win_ref, gq_ref, wuq_ref, gkv_ref, wk_ref, wv_ref, cc_ref, sc_ref,
         cd_ref, sd_ref, cm_ref, sm_ref,
         u_ref, v_ref, qd_ref, kd_ref, vd_ref, qm_ref, km_ref, vm_ref) = refs
    else:
        (x_ref, mod_ref, g_ref, win_ref, gq_ref, wuq_ref, gkv_ref, wk_ref, wv_ref, cc_ref, sc_ref,
         u_ref, v_ref, qd_ref, kd_ref, vd_ref, qm_ref, km_ref, vm_ref,
         k32_ref, v32_ref, ckv32_ref, kpe32_ref) = refs[:11] + refs[-12:]
    sh = mod_ref[0, :, 0:d]
    sc = mod_ref[0, :, d:2 * d]
    tm = x_ref.shape[1]
    sub = SUB_ROWS if tm > SUB_ROWS else max(tm // 2, BF_ROWS)
    def project(r0):
        h = _rms(x_ref[0, r0:r0 + sub, :], g_ref[...]) * (1.0 + sc) + sh
        return _dot(h.astype(BF), win_ref[...])

    starts = list(range(0, tm, sub))
    z_next = project(starts[0])
    for i, r0 in enumerate(starts):
        rows = slice(r0, r0 + sub)
        z = z_next
        z_next = project(starts[i + 1]) if i + 1 < len(starts) else None
        o = 0
        zf = z[:, o:o + D_FOURIER]; o += D_FOURIER
        zq = z[:, o:o + D_DIFF]; o += D_DIFF
        zk = z[:, o:o + D_DIFF]; o += D_DIFF
        zv = z[:, o:o + D_DIFF]; o += D_DIFF
        zcq = z[:, o:o + MLA_Q_RANK]; o += MLA_Q_RANK
        zckv = z[:, o:o + MLA_KV_RANK]; o += MLA_KV_RANK
        kpe = z[:, o:o + HEAD_W]

        zf_b = zf.astype(BF)
        u_ref[0, rows, :] = _dot(zf_b, cc_ref[...]).astype(BF)
        v_ref[0, rows, :] = _dot(zf_b, sc_ref[...]).astype(BF)
        vd_ref[0, :, rows] = zv.T.astype(BF)
        cq = _rms(zcq, gq_ref[...])
        qm = _dot(cq.astype(BF), wuq_ref[...])
        ckv = _rms(zckv, gkv_ref[...])
        ckv_b = ckv.astype(BF)
        k_nope = _dot(ckv_b, wk_ref[...])
        vm_ref[0, :, rows] = _dot(ckv_b, wv_ref[...]).T.astype(BF)
        if latent:
            cd, sd, cm, sm = cd_ref[rows, :], sd_ref[rows, :], cm_ref[rows, :], sm_ref[rows, :]
            zq_r = _rope(zq, cd, sd, DIFF_DH // 2, DIFF_DH, 0)
            zk_r = _rope(zk, cd, sd, DIFF_DH // 2, DIFF_DH, 0)
            qm = _rope(qm, cm, sm, MLA_ROPE // 2, HEAD_W, MLA_NOPE)
            kpe_r = _rope(kpe, cm, sm, MLA_ROPE // 2, HEAD_W, MLA_NOPE)
        else:
            zq_r, zk_r, kpe_r = zq, zk, kpe
            for hd in range(DIFF_HEADS):
                k32_ref[0, 0, rows, hd, :] = zk[:, hd * HEAD_W:(hd + 1) * HEAD_W]
                v32_ref[0, 0, rows, hd, :] = zv[:, hd * HEAD_W:(hd + 1) * HEAD_W]
            ckv32_ref[0, 0, rows, :] = ckv
            kpe32_ref[0, 0, rows, :] = kpe[:, MLA_NOPE:MLA_NOPE + MLA_ROPE]
            if i == 0:
                for ref in (k32_ref, v32_ref, ckv32_ref, kpe32_ref):
                    if ref.shape[1] > 1:
                        ref[0, 1:] = jnp.zeros((ref.shape[1] - 1,) + ref.shape[2:], F32)
        qd_ref[0, rows, :] = (zq_r * (DIFF_DH ** -0.5 * LOG2E)).astype(BF)
        kd_ref[0, rows, :] = zk_r.astype(BF)
        qm_ref[0, rows, :] = (qm * ((MLA_NOPE + MLA_ROPE) ** -0.5 * LOG2E)).astype(BF)
        km_ref[0, rows, :] = (k_nope + jnp.concatenate([kpe_r] * MLA_HEADS, axis=1)).astype(BF)


def _combine_pre_kernel(latent, d, cap, n_pre_in, *refs):
    x_ref, modp_ref, selt_ref, ye_ref = refs[:4]
    pre_in = refs[4:4 + n_pre_in]
    xo_ref = refs[4 + n_pre_in]
    pre_out = refs[5 + n_pre_in:-1]
    xo_ref[0] = _combine_rows(cap, d, x_ref, modp_ref, selt_ref, ye_ref, refs[-1], slice(None))
    _pre_kernel(latent, d, xo_ref, *pre_in, *pre_out)


def _pre(latent, x, modv, g_attn, w_in_p, g_q, w_uq_p, g_kv, w_k_p, w_v_p, cc_bd, sc_bd, ropes, l=0, depth=1, prev=None,
         combine=None):
    b, n, d = x.shape
    tm = min(PRE_ROWS, n)
    bm = modv.shape[0]
    mod_idx = (lambda i, j: (i, 0, 0)) if bm > 1 else (lambda i, j: (0, 0, 0))
    const = lambda i, j: (0, 0)
    tok = lambda i, j: (i, j, 0)
    in_specs = [
        pl.BlockSpec((1, tm, d), tok),
        pl.BlockSpec((1, 1, modv.shape[2]), mod_idx),
        pl.BlockSpec((1, d), const),
        pl.BlockSpec(w_in_p.shape, const),
        pl.BlockSpec((1, MLA_Q_RANK), const),
        pl.BlockSpec(w_uq_p.shape, const),
        pl.BlockSpec((1, MLA_KV_RANK), const),
        pl.BlockSpec(w_k_p.shape, const),
        pl.BlockSpec(w_v_p.shape, const),
        pl.BlockSpec(cc_bd.shape, const),
        pl.BlockSpec(sc_bd.shape, const),
    ]
    args = [x, modv, g_attn.reshape(1, d), w_in_p, g_q.reshape(1, -1), w_uq_p, g_kv.reshape(1, -1), w_k_p, w_v_p, cc_bd, sc_bd]
    if latent:
        in_specs += [pl.BlockSpec((tm, HEAD_W), lambda i, j: (j, 0))] * 4
        args += list(ropes)
    feat = lambda i, j: (i, 0, j)
    out_shape, out_specs = [], []
    for w, token_major in ((D_FOURIER, True), (D_FOURIER, True), (D_DIFF, True), (D_DIFF, True), (D_DIFF, False),
                           (D_MLA_P, True), (D_MLA_P, True), (D_MLA, False)):
        out_shape.append(jax.ShapeDtypeStruct((b, n, w) if token_major else (b, w, n), BF))
        out_specs.append(pl.BlockSpec((1, tm, w), tok) if token_major else pl.BlockSpec((1, w, tm), feat))
    aliases = {}
    if not latent:
        span = depth if prev is None else 1
        for tail in ((DIFF_HEADS, HEAD_W), (DIFF_HEADS, HEAD_W), (MLA_KV_RANK,), (MLA_ROPE,)):
            out_shape.append(jax.ShapeDtypeStruct((b, depth, n) + tail, F32))
            out_specs.append(pl.BlockSpec((1, span, tm) + tail, lambda i, j, t=len(tail): (i, l, j) + (0,) * t))
        if prev is not None:
            for k, buf in enumerate(prev):
                aliases[len(args)] = len(out_shape) - len(prev) + k
                in_specs.append(pl.BlockSpec(memory_space=pl.ANY))
                args.append(buf)
    body, scratch, tag = functools.partial(_pre_kernel, latent, d), [], "pre"
    if combine is not None:
        modp, sel_t, ye, cap = combine
        e = ye.shape[0]
        modp_idx = (lambda i, j: (i, 0, 0)) if modp.shape[0] > 1 else (lambda i, j: (0, 0, 0))
        extra_specs = [pl.BlockSpec((1, 1, modp.shape[2]), modp_idx), pl.BlockSpec((1, tm, e), tok),
                       pl.BlockSpec((e, 1, cap, d), lambda i, j: (0, i, 0, 0))]
        n_extra = len(extra_specs)
        body = functools.partial(_combine_pre_kernel, latent, d, cap, len(args) - 1)
        in_specs = in_specs[:1] + extra_specs + in_specs[1:]
        args = args[:1] + [modp, sel_t, ye] + args[1:]
        out_specs = [pl.BlockSpec((1, tm, d), tok)] + out_specs
        out_shape = [jax.ShapeDtypeStruct((b, n, d), F32)] + out_shape
        aliases = {k + n_extra: v + 1 for k, v in aliases.items()}
        scratch, tag = [pltpu.VMEM((tm, e * cap), BF)], "combine_pre"
    return pl.pallas_call(
        body,
        name=tag + ("_lat" if latent else "_ctx"),
        grid=(b, n // tm),
        in_specs=in_specs,
        out_specs=out_specs,
        out_shape=out_shape,
        input_output_aliases=aliases,
        scratch_shapes=scratch,
        compiler_params=_cparams(2),
    )(*args)


def _cache_kernel(dk_ref, dv_ref, ckv_ref, kpe_ref, wk_ref, wv_ref, place_ref, kd_ref, vd_ref, km_ref, vm_ref):
    for hd in range(DIFF_HEADS):
        sl = slice(hd * HEAD_W, (hd + 1) * HEAD_W)
        kd_ref[0, :, sl] = dk_ref[0, 0, :, hd, :].astype(BF)
        vd_ref[0, sl, :] = dv_ref[0, 0, :, hd, :].T.astype(BF)
    ckv_b = ckv_ref[0, 0].astype(BF)
    kpe_w = _dot(kpe_ref[0, 0].astype(BF), place_ref[...])
    km_ref[0] = (_dot(ckv_b, wk_ref[...]) + kpe_w).astype(BF)
    vm_ref[0] = _dot(ckv_b, wv_ref[...]).T.astype(BF)


def _cache_prep(l, cache_dk, cache_dv, cache_ckv, cache_kpe, w_k_p, w_v_p, place):
    b, _, p = cache_dk.shape[:3]
    at_l = lambda i: (i, l, 0, 0)
    at_l5 = lambda i: (i, l, 0, 0, 0)
    const = lambda i: (0, 0)
    shapes = [(p, D_DIFF), (D_DIFF, p), (p, D_MLA_P), (D_MLA, p)]
    return pl.pallas_call(
        _cache_kernel,
        name="cache_prep",
        grid=(b,),
        in_specs=[
            pl.BlockSpec((1, 1, p, DIFF_HEADS, HEAD_W), at_l5),
            pl.BlockSpec((1, 1, p, DIFF_HEADS, HEAD_W), at_l5),
            pl.BlockSpec((1, 1, p, MLA_KV_RANK), at_l),
            pl.BlockSpec((1, 1, p, MLA_ROPE), at_l),
            pl.BlockSpec(w_k_p.shape, const),
            pl.BlockSpec(w_v_p.shape, const),
            pl.BlockSpec(place.shape, const),
        ],
        out_specs=[pl.BlockSpec((1,) + s, lambda i: (i, 0, 0)) for s in shapes],
        out_shape=[jax.ShapeDtypeStruct((b,) + s, BF) for s in shapes],
        compiler_params=_cparams(1),
    )(cache_dk, cache_dv, cache_ckv, cache_kpe, w_k_p, w_v_p, place)


def _fourier_kernel(scale, u_ref, v_ref, cn_ref, sn_ref, w_ref, o_ref):
    f = (_dot(cn_ref[...], u_ref[0]) - _dot(sn_ref[...], v_ref[0])) * scale
    o_ref[0] = _dot(f.astype(BF), w_ref[...]).astype(BF)


def _fourier(u, v, cn, sn, w_bd):
    b, n, w = u.shape
    tm = min(512, n)
    return pl.pallas_call(
        functools.partial(_fourier_kernel, 1.0 / math.sqrt(n * F_CH)),
        name="fourier_%d" % n,
        grid=(n // tm, b),
        in_specs=[
            pl.BlockSpec((1, n, w), lambda i, j: (j, 0, 0)),
            pl.BlockSpec((1, n, w), lambda i, j: (j, 0, 0)),
            pl.BlockSpec((tm, n), lambda i, j: (i, 0)),
            pl.BlockSpec((tm, n), lambda i, j: (i, 0)),
            pl.BlockSpec((w, w), lambda i, j: (0, 0)),
        ],
        out_specs=pl.BlockSpec((1, tm, w), lambda i, j: (j, i, 0)),
        out_shape=jax.ShapeDtypeStruct((b, n, w), BF),
        compiler_params=_cparams(2),
    )(u, v, cn, sn, w_bd)


def _scores(q, k_parts, s_ref):
    off, m, bounds = 0, None, []
    for k in k_parts:
        s = _dot_nt(k(), q)
        n_i = s.shape[0]
        s_ref[off:off + n_i, :] = s
        mi = jnp.max(s, axis=0, keepdims=True)
        m = mi if m is None else jnp.maximum(m, mi)
        bounds.append((off, n_i))
        off += n_i
    return m, bounds


def _weighted(m, vt_parts, bounds, s_ref):
    o = None
    for vt, (r, n_i) in zip(vt_parts, bounds):
        v = vt()
        ones = jnp.where(lax.broadcasted_iota(jnp.int32, (BF_ROWS, n_i), 0) == 0, 1.0, 0.0).astype(BF)
        v_ext = jnp.concatenate([v, ones], axis=0)
        for c in range(0, n_i, KEY_CHUNK):
            rows = min(KEY_CHUNK, n_i - c)
            p = jnp.exp2(s_ref[r + c:r + c + rows, :] - m).astype(BF)
            oi = _dot(v_ext[:, c:c + rows], p)
            o = oi if o is None else o + oi
    dv = o.shape[0] - BF_ROWS
    return o[:dv] * (1.0 / o[dv:dv + 1])


def _attn_kernel(latent, lam_init, d, groups, *refs):
    if latent:
        (x_ref, mod_ref, yf_ref, qd_ref, kd_ref, vd_ref, kdc_ref, vdc_ref, qm_ref, km_ref, vm_ref, kmc_ref, vmc_ref,
         lam_ref, gs_ref, wo_ref, o_ref, mix_ref, s_ref) = refs
    else:
        (x_ref, mod_ref, yf_ref, qd_ref, kd_ref, vd_ref, qm_ref, km_ref, vm_ref,
         lam_ref, gs_ref, wo_ref, o_ref, mix_ref, s_ref) = refs
        kdc_ref = vdc_ref = kmc_ref = vmc_ref = None
    tq = s_ref.shape[2]
    lf = lam_ref[...]
    lam = (jnp.exp(jnp.sum(lf[0:1] * lf[1:2], axis=-1, keepdims=True))
           - jnp.exp(jnp.sum(lf[2:3] * lf[3:4], axis=-1, keepdims=True)) + lam_init)
    gs = gs_ref[...]
    gt = mod_ref[0, :, 2 * d:3 * d]

    def keys(ref, cache_ref, ri, sl):
        parts = [lambda: ref[ri, :, sl]]
        if latent:
            parts.append(lambda: cache_ref[ri, :, sl])
        return parts

    def values_t(ref, cache_ref, ri, sl):
        parts = [lambda: ref[ri, sl, :]]
        if latent:
            parts.append(lambda: cache_ref[ri, sl, :])
        return parts

    def diff_query(ri, rows, sl, comp):
        def load():
            qh = qd_ref[ri, rows, sl]
            lane = lax.broadcasted_iota(jnp.int32, qh.shape, 1)
            keep = (lane >= DIFF_DH) if comp else (lane < DIFF_DH)
            return jnp.where(keep, qh, jnp.zeros_like(qh))
        return load

    items = []
    for ri, r0 in groups:
        rows = slice(r0, r0 + tq)
        for hd in range(DIFF_HEADS):
            sl = slice(hd * HEAD_W, (hd + 1) * HEAD_W)
            for comp in range(2):
                items.append((diff_query(ri, rows, sl, comp), keys(kd_ref, kdc_ref, ri, sl), values_t(vd_ref, vdc_ref, ri, sl)))
        for hd in range(MLA_HEADS):
            sl = slice(hd * HEAD_W, (hd + 1) * HEAD_W)
            vsl = slice(hd * MLA_V, (hd + 1) * MLA_V)
            items.append(((lambda ri=ri, rows=rows, sl=sl: qm_ref[ri, rows, sl]), keys(km_ref, kmc_ref, ri, sl),
                          values_t(vm_ref, vmc_ref, ri, vsl)))
    per_group = 2 * DIFF_HEADS + MLA_HEADS

    def start(i):
        return _scores(items[i][0](), items[i][1], s_ref.at[i % SCORE_BUFS])

    outs = []
    state = start(0)
    for i in range(len(items)):
        nxt = start(i + 1) if i + 1 < len(items) else None
        m, bounds = state
        outs.append(_weighted(m, items[i][2], bounds, s_ref.at[i % SCORE_BUFS]))
        state = nxt
        gi, k = divmod(i, per_group)
        if k < 2 * DIFF_HEADS and k % 2 == 1:
            hd = k // 2
            o = outs[i - 1] - lam * outs[i]
            o = o * lax.rsqrt(jnp.mean(o * o, axis=0, keepdims=True) + NORM_EPS) * gs * (1.0 - lam_init)
            mix_ref[gi, :, D_FOURIER + hd * HEAD_W:D_FOURIER + (hd + 1) * HEAD_W] = o.T.astype(BF)
        if k == per_group - 1:
            ri, r0 = groups[gi]
            rows = slice(r0, r0 + tq)
            mix_ref[gi, :, 0:D_FOURIER] = yf_ref[ri, rows, :]
            mix_ref[gi, :, D_FOURIER + D_DIFF:D_MIX] = jnp.concatenate(outs[i + 1 - MLA_HEADS:i + 1], axis=0).T.astype(BF)
            o_ref[ri, rows, :] = x_ref[ri, rows, :] + gt * _dot(mix_ref[gi], wo_ref[...])


def _attention(latent, l, x, modv, yfour, qd, kd, vd, qm, km, vm, cache, diff_lambda_l, g_subln_l, w_out_b):
    b, n, d = x.shape
    tq = min(ATTN_ROWS, n)
    if (n // tq) % ATTN_GROUPS == 0:
        bb, rows_blk = 1, tq * ATTN_GROUPS
        groups = tuple((0, g * tq) for g in range(ATTN_GROUPS))
    else:
        assert n == tq and b % ATTN_GROUPS == 0 and modv.shape[0] == 1
        bb, rows_blk = ATTN_GROUPS, tq
        groups = tuple((g, 0) for g in range(ATTN_GROUPS))
    bm = modv.shape[0]
    mod_idx = (lambda i, j: (i, 0, 0)) if bm > 1 else (lambda i, j: (0, 0, 0))
    const = lambda i, j: (0, 0)
    tok = lambda i, j: (i, j, 0)
    full = lambda i, j: (i, 0, 0)
    p = cache[0].shape[1] if latent else 0
    in_specs = [
        pl.BlockSpec((bb, rows_blk, d), tok),
        pl.BlockSpec((1, 1, modv.shape[2]), mod_idx),
        pl.BlockSpec((bb, rows_blk, D_FOURIER), tok),
        pl.BlockSpec((bb, rows_blk, D_DIFF), tok),
        pl.BlockSpec((bb, n, D_DIFF), full),
        pl.BlockSpec((bb, D_DIFF, n), full),
    ]
    args = [x, modv, yfour, qd, kd, vd]
    if latent:
        kdc, vdc, kmc, vmc = cache
        in_specs += [pl.BlockSpec((bb, p, D_DIFF), full), pl.BlockSpec((bb, D_DIFF, p), full)]
        args += [kdc, vdc]
    in_specs += [pl.BlockSpec((bb, rows_blk, D_MLA_P), tok), pl.BlockSpec((bb, n, D_MLA_P), full), pl.BlockSpec((bb, D_MLA, n), full)]
    args += [qm, km, vm]
    if latent:
        in_specs += [pl.BlockSpec((bb, p, D_MLA_P), full), pl.BlockSpec((bb, D_MLA, p), full)]
        args += [kmc, vmc]
    in_specs += [pl.BlockSpec((4, DIFF_DH), const), pl.BlockSpec((HEAD_W, 1), const), pl.BlockSpec(w_out_b.shape, const)]
    args += [diff_lambda_l, g_subln_l.reshape(HEAD_W, 1), w_out_b]
    lam_init = 0.8 - 0.6 * math.exp(-0.3 * l)
    return pl.pallas_call(
        functools.partial(_attn_kernel, latent, lam_init, d, groups),
        name="attn_lat" if latent else "attn_ctx",
        grid=(b // bb, n // rows_blk),
        in_specs=in_specs,
        out_specs=pl.BlockSpec((bb, rows_blk, d), tok),
        out_shape=jax.ShapeDtypeStruct((b, n, d), F32),
        scratch_shapes=[pltpu.VMEM((len(groups), tq, D_MIX), BF), pltpu.VMEM((SCORE_BUFS, n + p, tq), F32)],
        compiler_params=_cparams(2),
    )(*args)


def _router_kernel(d, packed, x_ref, mod_ref, g_ref, wr_ref, h_ref, aff_ref):
    sh = mod_ref[0, :, 3 * d:4 * d]
    sc = mod_ref[0, :, 4 * d:5 * d]
    tm = x_ref.shape[1]
    sub = SUB_ROWS if tm > SUB_ROWS else max(tm // 2, BF_ROWS)
    for r0 in range(0, tm, sub):
        rows = slice(r0, r0 + sub)
        hb = (_rms(x_ref[0, rows, :], g_ref[...]) * (1.0 + sc) + sh).astype(BF)
        h_ref[0, rows, :] = _pack_halves(hb) if packed else hb
        logits = _dot_nt(wr_ref[...], hb)
        e = jnp.exp(logits - jnp.max(logits, axis=0, keepdims=True))
        aff_ref[0, :, rows] = e / jnp.sum(e, axis=0, keepdims=True)


def _router(x, modv, g_ffn_l, w_router_t, packed):
    b, n, d = x.shape
    hw, hdt = (d // 2, jnp.int32) if packed else (d, BF)
    tm = min(PRE_ROWS, n)
    bm = modv.shape[0]
    mod_idx = (lambda i, j: (i, 0, 0)) if bm > 1 else (lambda i, j: (0, 0, 0))
    const = lambda i, j: (0, 0)
    return pl.pallas_call(
        functools.partial(_router_kernel, d, packed),
        name="router_%d" % n,
        grid=(b, n // tm),
        in_specs=[
            pl.BlockSpec((1, tm, d), lambda i, j: (i, j, 0)),
            pl.BlockSpec((1, 1, modv.shape[2]), mod_idx),
            pl.BlockSpec((1, d), const),
            pl.BlockSpec(w_router_t.shape, const),
        ],
        out_specs=[pl.BlockSpec((1, tm, hw), lambda i, j: (i, j, 0)), pl.BlockSpec((1, N_EXPERTS, tm), lambda i, j: (i, 0, j))],
        out_shape=[jax.ShapeDtypeStruct((b, n, hw), hdt), jax.ShapeDtypeStruct((b, N_EXPERTS, n), F32)],
        compiler_params=_cparams(2),
    )(x, modv, g_ffn_l.reshape(1, d), w_router_t)


def _select_kernel(cap, aff_ref, sel_ref):
    a = aff_ref[...]
    e, n = a.shape
    bits = pltpu.bitcast(a, jnp.int32)
    capf = float(cap)

    def body(_, carry):
        lo, hi = carry
        mid = lo + ((hi - lo) >> 1)
        cnt = jnp.sum(jnp.where(bits >= mid, 1.0, 0.0), axis=1, keepdims=True)
        up = cnt >= capf
        return jnp.where(up, mid, lo), jnp.where(up, hi, mid)

    lo0 = jnp.zeros((e, 1), jnp.int32)
    hi0 = jnp.full((e, 1), 0x7F800000, jnp.int32)
    thr, _ = lax.fori_loop(0, 31, body, (lo0, hi0))
    gt = bits > thr
    eq = bits == thr
    need = capf - jnp.sum(jnp.where(gt, 1.0, 0.0), axis=1, keepdims=True)
    both = jnp.concatenate([jnp.where(gt, 1.0, 0.0), jnp.where(eq, 1.0, 0.0)], axis=0).astype(BF)
    ck = min(256, n)
    tri = jnp.where(lax.broadcasted_iota(jnp.int32, (ck, ck), 0) < lax.broadcasted_iota(jnp.int32, (ck, ck), 1), 1.0, 0.0).astype(BF)
    off = jnp.zeros((2 * e, 1), F32)
    pieces = []
    for k in range(n // ck):
        blk = both[:, k * ck:(k + 1) * ck]
        pieces.append(_dot(blk, tri) + off)
        off = off + jnp.sum(blk.astype(F32), axis=1, keepdims=True)
    cum = jnp.concatenate(pieces, axis=1) if len(pieces) > 1 else pieces[0]
    cum_gt, cum_eq = cum[:e], cum[e:]
    chosen = gt | (eq & (cum_eq < need))
    pos = cum_gt + jnp.minimum(cum_eq, need)
    sel_ref[...] = jnp.where(chosen, pos, -1.0).astype(jnp.int32)


def _select(aff, cap):
    b, e, n = aff.shape
    rows = b * e
    rb = min(rows, SELECT_ROWS)
    sel = pl.pallas_call(
        functools.partial(_select_kernel, cap),
        name="select_%d" % n,
        grid=(rows // rb,),
        in_specs=[pl.BlockSpec((rb, n), lambda i: (i, 0))],
        out_specs=pl.BlockSpec((rb, n), lambda i: (i, 0)),
        out_shape=jax.ShapeDtypeStruct((rows, n), jnp.int32),
        compiler_params=_cparams(1),
    )(aff.reshape(rows, n))
    return sel.reshape(b, e, n)


def _gather_kernel(cap, sel_ref, aff_ref, h_ref, xs_ref, gate_ref):
    h = h_ref[0]
    n = h.shape[0]
    e_tot = sel_ref.shape[1]

    def one(e_idx):
        sel_e = sel_ref[0, pl.ds(e_idx, 1), :]
        aff_e = aff_ref[0, pl.ds(e_idx, 1), :]
        hit = sel_e == lax.broadcasted_iota(jnp.int32, (cap, n), 0)
        g = jnp.sum(jnp.where(hit, aff_e, 0.0), axis=1, keepdims=True)
        return jnp.where(hit, 1.0, 0.0).astype(BF), jnp.broadcast_to(g, (cap, HEAD_W))

    if cap >= 128:
        def body(e_idx, carry):
            p, g = one(e_idx)
            xs_ref[e_idx, 0] = _dot(p, h).astype(BF)
            gate_ref[e_idx, 0] = g
            return carry

        lax.fori_loop(0, e_tot, body, 0)
    else:
        ps, gs = zip(*[one(e_idx) for e_idx in range(e_tot)])
        xs = _dot(jnp.concatenate(ps, axis=0), h).astype(BF)
        for e_idx in range(e_tot):
            xs_ref[e_idx, 0] = xs[e_idx * cap:(e_idx + 1) * cap]
            gate_ref[e_idx, 0] = gs[e_idx]


def _gather(sel, aff, h2, cap):
    b, e, n = sel.shape
    d = h2.shape[2]
    return pl.pallas_call(
        functools.partial(_gather_kernel, cap),
        name="gather_%d" % n,
        grid=(b,),
        in_specs=[
            pl.BlockSpec((1, e, n), lambda i: (i, 0, 0)),
            pl.BlockSpec((1, e, n), lambda i: (i, 0, 0)),
            pl.BlockSpec((1, n, d), lambda i: (i, 0, 0)),
        ],
        out_specs=[pl.BlockSpec((e, 1, cap, d), lambda i: (0, i, 0, 0)), pl.BlockSpec((e, 1, cap, HEAD_W), lambda i: (0, i, 0, 0))],
        out_shape=[jax.ShapeDtypeStruct((e, b, cap, d), BF), jax.ShapeDtypeStruct((e, b, cap, HEAD_W), F32)],
        compiler_params=_cparams(1),
    )(sel, aff, h2)


def _sc_dispatch(table, sel, aff, cap, n_experts):
    info = plsc.get_sparse_core_info()
    n_workers = info.num_cores * info.num_subcores
    lanes = info.num_lanes
    pairs, n = sel.shape
    b_tot = pairs // n_experts
    w = table.shape[1]
    per_w = pairs // n_workers
    ch = min(SC_CHUNK, cap)
    n_ch = cap // ch
    assert pairs % n_workers == 0 and cap % ch == 0 and ch % 8 == 0 and n % lanes == 0
    mesh = plsc.VectorSubcoreMesh(core_axis_name="c", subcore_axis_name="s")

    @functools.partial(
        pl.kernel, mesh=mesh, name="sc_dispatch",
        compiler_params=pltpu.CompilerParams(needs_layout_passes=False),
        out_type=[jax.ShapeDtypeStruct((pairs * cap, w), jnp.int32), jax.ShapeDtypeStruct((pairs * cap,), F32)],
        scratch_types=[pltpu.VMEM((n,), jnp.int32), pltpu.VMEM((n,), F32), pltpu.VMEM((n_ch, ch), jnp.int32),
                       pltpu.VMEM((cap,), F32), pltpu.VMEM((2, ch, w), jnp.int32), pltpu.SemaphoreType.DMA((2,))],
    )
    def k(table_hbm, sel_hbm, aff_hbm, xs_hbm, gate_hbm, sel_v, aff_v, tok_v, gate_v, rows_v, sem):
        wid = lax.axis_index("s") * info.num_cores + lax.axis_index("c")

        @pl.loop(0, per_w)
        def _(i):
            p = wid * per_w + i
            b = p // n_experts
            e = p - b * n_experts
            out_off = pl.multiple_of((e * b_tot + b) * cap, 8)
            pltpu.sync_copy(sel_hbm.at[p], sel_v)
            pltpu.sync_copy(aff_hbm.at[p], aff_v)
            lane = lax.iota(jnp.int32, lanes)

            @plsc.parallel_loop(0, n // lanes, unroll=4)
            def _(j):
                t0 = pl.multiple_of(j * lanes, lanes)
                s = sel_v[pl.ds(t0, lanes)]
                chosen = s >= 0
                slot = jnp.maximum(s, 0)
                plsc.store_scatter(tok_v, [slot // ch, slot % ch], lane + (t0 + b * n), mask=chosen)
                plsc.store_scatter(gate_v, [slot], aff_v[pl.ds(t0, lanes)], mask=chosen)

            prev = None
            for c in range(n_ch + 1):
                cur = None
                if c < n_ch:
                    cur = pltpu.async_copy(table_hbm.at[tok_v.at[c]], rows_v.at[c % 2], sem.at[c % 2])
                if prev is not None:
                    prev.wait()
                    pltpu.sync_copy(rows_v.at[(c - 1) % 2], xs_hbm.at[pl.ds(out_off + (c - 1) * ch, ch)])
                prev = cur
            pltpu.sync_copy(gate_v, gate_hbm.at[pl.ds(out_off, cap)])

    return k(table, sel, aff)


def _expert_kernel(steps_a, xa_ref, ga_ref, xb_ref, gb_ref, wg_ref, wu_ref, wd_ref, ya_ref, yb_ref, wg_s, wu_s, wd_s):
    j = pl.program_id(1)

    @pl.when(j == 0)
    def _():
        wg_s[...] = wg_ref[0, 0].astype(BF)
        wu_s[...] = wu_ref[0, 0].astype(BF)
        wd_s[...] = wd_ref[0, 0].astype(BF)

    def ffn(x_ref, g_ref, y_ref):
        rb = x_ref.shape[1]
        sub = min(rb, EXPERT_SUB_ROWS)
        for r0 in range(0, rb, sub):
            rows = slice(r0, r0 + sub)
            x = _unpack_halves(x_ref[0, rows, :]) if x_ref.dtype == jnp.int32 else x_ref[0, rows, :]
            a = _dot(x, wg_s[...])
            u = _dot(x, wu_s[...])
            mid = (a * (1.0 / (1.0 + jnp.exp(-a))) * u).astype(BF)
            y_ref[0, rows, :] = (_dot(mid, wd_s[...]) * g_ref[0, rows, 0:1]).astype(BF)

    @pl.when(j < steps_a)
    def _():
        ffn(xa_ref, ga_ref, ya_ref)

    @pl.when(j >= steps_a)
    def _():
        ffn(xb_ref, gb_ref, yb_ref)


def _experts(l, xs_a, gate_a, xs_b, gate_b, w_e_gate, w_e_up, w_e_down):
    e, rows_a, wa = xs_a.shape
    rows_b, wb = xs_b.shape[1:]
    d, ff = w_e_gate.shape[2:]
    rb_a, rb_b = math.gcd(rows_a, EXPERT_ROWS), math.gcd(rows_b, EXPERT_ROWS)
    steps_a, steps_b = rows_a // rb_a, rows_b // rb_b
    w_idx = lambda i, j: (l, i, 0, 0)
    idx_a = lambda i, j: (i, jnp.minimum(j, steps_a - 1), 0)
    idx_b = lambda i, j: (i, jnp.maximum(j - steps_a, 0), 0)
    return pl.pallas_call(
        functools.partial(_expert_kernel, steps_a),
        name="experts",
        grid=(e, steps_a + steps_b),
        in_specs=[
            pl.BlockSpec((1, rb_a, wa), idx_a),
            pl.BlockSpec((1, rb_a, HEAD_W), idx_a),
            pl.BlockSpec((1, rb_b, wb), idx_b),
            pl.BlockSpec((1, rb_b, HEAD_W), idx_b),
            pl.BlockSpec((1, 1, d, ff), w_idx),
            pl.BlockSpec((1, 1, d, ff), w_idx),
            pl.BlockSpec((1, 1, ff, d), w_idx),
        ],
        out_specs=[pl.BlockSpec((1, rb_a, d), idx_a), pl.BlockSpec((1, rb_b, d), idx_b)],
        out_shape=[jax.ShapeDtypeStruct((e, rows_a, d), BF), jax.ShapeDtypeStruct((e, rows_b, d), BF)],
        scratch_shapes=[pltpu.VMEM((d, ff), BF), pltpu.VMEM((d, ff), BF), pltpu.VMEM((ff, d), BF)],
        compiler_params=_cparams(2),
    )(xs_a, gate_a, xs_b, gate_b, w_e_gate, w_e_up, w_e_down)


def _combine_rows(cap, d, x_ref, mod_ref, selt_ref, ye_ref, pt_ref, rows):
    e_tot = ye_ref.shape[0]
    selt = selt_ref[0, rows, :]
    r = selt.shape[0]
    if cap % 128 == 0:
        lane = lax.broadcasted_iota(jnp.int32, (r, cap), 1)
        for e_idx in range(e_tot):
            pt_ref[rows, e_idx * cap:(e_idx + 1) * cap] = jnp.where(selt[:, e_idx:e_idx + 1] == lane, 1.0, 0.0).astype(BF)
    else:
        assert cap & (cap - 1) == 0
        shift = cap.bit_length() - 1
        col = lax.broadcasted_iota(jnp.int32, (e_tot, e_tot * cap), 1)
        spread = jnp.where((col >> shift) == lax.broadcasted_iota(jnp.int32, col.shape, 0), 1.0, 0.0).astype(BF)
        slot_of_col = _dot(selt.astype(F32).astype(BF), spread)
        want = (lax.broadcasted_iota(jnp.int32, (r, e_tot * cap), 1) & (cap - 1)).astype(F32)
        pt_ref[rows, :] = jnp.where(slot_of_col == want, 1.0, 0.0).astype(BF)
    y = _dot(pt_ref[rows, :], ye_ref[...].reshape(e_tot * cap, d))
    return x_ref[0, rows, :] + mod_ref[0, :, 5 * d:6 * d] * y


def _scatter_kernel(cap, d, x_ref, mod_ref, selt_ref, ye_ref, gf_ref, o_ref, pt_ref):
    o_ref[0] = _rms(_combine_rows(cap, d, x_ref, mod_ref, selt_ref, ye_ref, pt_ref, slice(None)), gf_ref[...])


def _scatter(x, modv, sel_t, ye, cap, g_final):
    b, n, d = x.shape
    e = ye.shape[0]
    tn = min(PRE_ROWS, n)
    bm = modv.shape[0]
    mod_idx = (lambda i, j: (i, 0, 0)) if bm > 1 else (lambda i, j: (0, 0, 0))
    return pl.pallas_call(
        functools.partial(_scatter_kernel, cap, d),
        name="scatter_%d" % n,
        grid=(b, n // tn),
        in_specs=[
            pl.BlockSpec((1, tn, d), lambda i, j: (i, j, 0)),
            pl.BlockSpec((1, 1, modv.shape[2]), mod_idx),
            pl.BlockSpec((1, tn, e), lambda i, j: (i, j, 0)),
            pl.BlockSpec((e, 1, cap, d), lambda i, j: (0, i, 0, 0)),
            pl.BlockSpec((1, d), lambda i, j: (0, 0)),
        ],
        out_specs=pl.BlockSpec((1, tn, d), lambda i, j: (i, j, 0)),
        out_shape=jax.ShapeDtypeStruct((b, n, d), F32),
        scratch_shapes=[pltpu.VMEM((tn, e * cap), BF)],
        compiler_params=_cparams(2),
    )(x, modv, sel_t, ye, g_final.reshape(1, d))


def _dft_tables(n):
    j = np.arange(n, dtype=np.int64)
    ang = ((j[:, None] * j[None, :]) % n).astype(np.float64) * (2.0 * math.pi / n)
    return jnp.asarray(np.cos(ang), dtype=BF), jnp.asarray(np.sin(ang), dtype=BF)


def _block_diag(blocks):
    g, r, c = blocks.shape
    out = jnp.zeros((g * r, g * c), blocks.dtype)
    for i in range(g):
        out = out.at[i * r:(i + 1) * r, i * c:(i + 1) * c].set(blocks[i])
    return out


def _rope_tables(n):
    rows = n // GRID_W
    row = jnp.repeat(jnp.arange(rows, dtype=F32), GRID_W)
    col = jnp.tile(jnp.arange(GRID_W, dtype=F32), rows)

    def ang(dim):
        nf = dim // 4
        inv = ROPE_BASE ** (-jnp.arange(nf, dtype=F32) / nf)
        return jnp.concatenate([row[:, None] * inv, col[:, None] * inv], axis=-1)

    a = ang(DIFF_DH)
    cos_d = jnp.tile(jnp.cos(a), (1, 4))
    sin_d = jnp.tile(jnp.concatenate([-jnp.sin(a), jnp.sin(a)], axis=1), (1, 2))
    a = ang(MLA_ROPE)
    ones = jnp.ones((n, MLA_NOPE), F32)
    pad = HEAD_W - MLA_NOPE - MLA_ROPE
    cos_m = jnp.concatenate([ones, jnp.cos(a), jnp.cos(a), jnp.ones((n, pad), F32)], axis=1)
    sin_m = jnp.concatenate([0 * ones, -jnp.sin(a), jnp.sin(a), jnp.zeros((n, pad), F32)], axis=1)
    return cos_d, sin_d, cos_m, sin_m


def _pad_heads(w, heads, lo, hi):
    k = w.shape[0]
    w3 = w.reshape(k, heads, -1)[:, :, lo:hi]
    return jnp.pad(w3, ((0, 0), (0, 0), (0, HEAD_W - (hi - lo)))).reshape(k, heads * HEAD_W)


def kernel(x_prompt, x_sample, cache_diff_k, cache_diff_v, cache_mla_ckv, cache_mla_kpe, c, c_ctx, w_ada, b_ada, g_attn, g_ffn, w_in, w_four, diff_lambda, g_subln, g_mla_q, w_mla_uq, g_mla_kv, w_mla_ukv, w_out, w_router, w_e_gate, w_e_up, w_e_down, g_final):
    depth, d = g_attn.shape
    b_ctx, n_ctx, _ = x_prompt.shape
    b_lat, n_lat, _ = x_sample.shape

    r = b_lat + 1
    r_pad = -(-r // 8) * 8
    cond = jnp.concatenate([c, c_ctx[None, :], jnp.zeros((r_pad - r, d), F32)], axis=0)
    mod = _modulation(cond, w_ada, b_ada)

    n_main = w_in.shape[2] - MLA_ROPE
    w_in_p = jnp.concatenate(
        [w_in[:, :, :n_main], jnp.zeros((depth, d, MLA_NOPE), F32), w_in[:, :, n_main:],
         jnp.zeros((depth, d, HEAD_W - MLA_NOPE - MLA_ROPE), F32)], axis=2).astype(BF)
    w_uq_p = jnp.stack([_pad_heads(w_mla_uq[l], MLA_HEADS, 0, MLA_NOPE + MLA_ROPE) for l in range(depth)]).astype(BF)
    w_k_p = jnp.stack([_pad_heads(w_mla_ukv[l], MLA_HEADS, 0, MLA_NOPE) for l in range(depth)]).astype(BF)
    w_v_p = w_mla_ukv.reshape(depth, MLA_KV_RANK, MLA_HEADS, MLA_NOPE + MLA_V)[..., MLA_NOPE:].reshape(depth, MLA_KV_RANK, D_MLA).astype(BF)
    w_out_p = w_out.astype(BF)
    w_router_t = jnp.swapaxes(w_router, 1, 2).astype(BF)
    place = jnp.zeros((MLA_ROPE, HEAD_W), F32).at[jnp.arange(MLA_ROPE), MLA_NOPE + jnp.arange(MLA_ROPE)].set(1.0)
    place = jnp.tile(place, (1, MLA_HEADS)).astype(BF)

    jc = jnp.arange(F_CH, dtype=jnp.int32)
    ang_c = ((jc[:, None] * jc[None, :]) % F_CH).astype(F32) * (2.0 * math.pi / F_CH)
    cc_bd = _block_diag(jnp.broadcast_to(jnp.cos(ang_c), (F_GROUPS, F_CH, F_CH))).astype(BF)
    sc_bd = _block_diag(jnp.broadcast_to(jnp.sin(ang_c), (F_GROUPS, F_CH, F_CH))).astype(BF)
    dft = {n: _dft_tables(n) for n in {n_ctx, n_lat}}
    ropes = _rope_tables(n_lat)

    def pre(latent, l, x, prev=None, combine=None):
        modv = (mod[l, :b_lat] if latent else mod[l, b_lat:b_lat + 1]).reshape(-1, 1, N_MOD * d)
        outs = _pre(latent, x, modv, g_attn[l], w_in_p[l], g_mla_q[l], w_uq_p[l], g_mla_kv[l], w_k_p[l], w_v_p[l],
                    cc_bd, sc_bd, ropes, l, depth, prev, combine)
        if combine is not None:
            x, outs = outs[0], outs[1:]
        return x, modv, outs

    def cache_prep(l):
        return _cache_prep(l, cache_diff_k, cache_diff_v, cache_mla_ckv, cache_mla_kpe, w_k_p[l], w_v_p[l], place)

    def mix_and_route(latent, l, x, modv, outs, cache=None):
        b, n, _ = x.shape
        cap = max(1, EC_CAPACITY_FACTOR * n // N_EXPERTS)
        u, v, qd, kd, vd, qm, km, vm = outs[:8]
        cn, sn = dft[n]
        yfour = _fourier(u, v, cn, sn, _block_diag(w_four[l]).astype(BF))
        x = _attention(latent, l, x, modv, yfour, qd, kd, vd, qm, km, vm, cache, diff_lambda[l], g_subln[l], w_out_p[l])
        h2, aff = _router(x, modv, g_ffn[l], w_router_t[l], packed=latent)
        sel = _select(aff, cap)
        e = sel.shape[1]
        if latent:
            xs, gate = _sc_dispatch(h2.reshape(b * n, h2.shape[2]), sel.reshape(b * e, n), aff.reshape(b * e, n), cap, e)
            gate = jnp.broadcast_to(gate.reshape(e, b * cap, 1), (e, b * cap, HEAD_W))
        else:
            xs, gate = _gather(sel, aff, h2, cap)
            gate = gate.reshape(e, b * cap, HEAD_W)
        return x, jnp.swapaxes(sel, 1, 2), cap, xs.reshape(e, b * cap, -1), gate

    x_l, mod_l, pre_l = pre(True, 0, x_sample)
    x_c, mod_c, pre_c = pre(False, 0, x_prompt)
    cache = cache_prep(0)
    for l in range(depth):
        new = tuple(pre_c[8:])
        x_l, selt_l, cap_l, xs_l, gate_l = mix_and_route(True, l, x_l, mod_l, pre_l, cache)
        x_c, selt_c, cap_c, xs_c, gate_c = mix_and_route(False, l, x_c, mod_c, pre_c)
        if l + 1 < depth:
            cache = cache_prep(l + 1)
        ye_c, ye_l = _experts(l, xs_c, gate_c, xs_l, gate_l, w_e_gate, w_e_up, w_e_down)
        ye_c = ye_c.reshape(-1, b_ctx, cap_c, d)
        ye_l = ye_l.reshape(-1, b_lat, cap_l, d)
        if l + 1 < depth:
            x_c, mod_c, pre_c = pre(False, l + 1, x_c, new, (mod_c, selt_c, ye_c, cap_c))
            x_l, mod_l, pre_l = pre(True, l + 1, x_l, None, (mod_l, selt_l, ye_l, cap_l))
        else:
            x_c = _scatter(x_c, mod_c, selt_c, ye_c, cap_c, g_final)
            x_l = _scatter(x_l, mod_l, selt_l, ye_l, cap_l, g_final)
    y_prompt, y_sample = x_c, x_l
    new_diff_k, new_diff_v, new_mla_ckv, new_mla_kpe = new
    return (y_prompt, y_sample, new_diff_k, new_diff_v, new_mla_ckv, new_mla_kpe)
```

```python
import functools
import math

import jax
import jax.numpy as jnp
import numpy as np
from jax import lax
from jax.experimental import pallas as pl
from jax.experimental.pallas import tpu as pltpu
from jax.experimental.pallas import tpu_sc as plsc

BF = jnp.bfloat16
F32 = jnp.float32

GRID_W = 64
ROPE_BASE = 10000.0
NORM_EPS = 1e-6
F_GROUPS, F_CH = 4, 64
D_FOURIER = F_GROUPS * F_CH
DIFF_HEADS, DIFF_DH = 4, 64
D_DIFF = DIFF_HEADS * 2 * DIFF_DH
MLA_HEADS, MLA_Q_RANK, MLA_KV_RANK = 4, 256, 128
MLA_NOPE, MLA_ROPE, MLA_V = 64, 32, 64
HEAD_W = 128
D_MLA_P = MLA_HEADS * HEAD_W
D_MLA = MLA_HEADS * MLA_V
D_MIX = D_FOURIER + D_DIFF + D_MLA
ATTN_ROWS = 512
KEY_CHUNK = 256
SCORE_BUFS = 2
BF_ROWS = 16
SELECT_ROWS = 128
EXPERT_ROWS = 1024
EXPERT_SUB_ROWS = 512
PRE_ROWS = 512
STREAM_ROWS = 1024
SC_CHUNK = 64
SUB_ROWS = 256
N_EXPERTS = 16
EC_CAPACITY_FACTOR = 2
N_MOD = 6
IN_COLS_P = D_FOURIER + 3 * D_DIFF + MLA_Q_RANK + MLA_KV_RANK + HEAD_W
LOG2E = 1.4426950408889634
VMEM_LIMIT = 56 * 1024 * 1024


def _cparams(n_axes, vmem=VMEM_LIMIT):
    return pltpu.CompilerParams(dimension_semantics=("arbitrary",) * n_axes, vmem_limit_bytes=vmem)


def _dot(a, b):
    return jnp.dot(a, b, preferred_element_type=F32)


def _dot_nt(a, b):
    return lax.dot_general(a, b, (((1,), (1,)), ((), ())), preferred_element_type=F32)


def _rms(x, g):
    return x * lax.rsqrt(jnp.mean(x * x, axis=-1, keepdims=True) + NORM_EPS) * g


def _pack_halves(hb):
    w = hb.shape[1] // 2
    bits = pltpu.bitcast(hb.astype(F32), jnp.uint32)
    packed = (bits[:, :w] >> 16) | (bits[:, w:] & jnp.uint32(0xFFFF0000))
    return pltpu.bitcast(packed, jnp.int32)


def _unpack_halves(xi):
    bits = pltpu.bitcast(xi, jnp.uint32)
    lo = pltpu.bitcast(bits << 16, F32).astype(BF)
    hi = pltpu.bitcast(bits & jnp.uint32(0xFFFF0000), F32).astype(BF)
    return jnp.concatenate([lo, hi], axis=1)


def _rope(z, cos, sin_signed, half, group, lo):
    w = z.shape[1]
    reps = w // cos.shape[1]
    cos_w = jnp.concatenate([cos] * reps, axis=1) if reps > 1 else cos
    sin_w = jnp.concatenate([sin_signed] * reps, axis=1) if reps > 1 else sin_signed
    from_right = pltpu.roll(z, w - half, 1)
    from_left = pltpu.roll(z, half, 1)
    lane = lax.broadcasted_iota(jnp.int32, z.shape, 1) % group
    first = (lane >= lo) & (lane < lo + half)
    partner = jnp.where(first, from_right, from_left)
    return z * cos_w + partner * sin_w


def _mod_kernel(c_ref, w_ref, b_ref, o_ref):
    c = c_ref[...]
    a = (c * (1.0 / (1.0 + jnp.exp(-c)))).astype(BF)
    o_ref[0] = _dot(a, w_ref[0].astype(BF)) + b_ref[0]


def _modulation(cond, w_ada, b_ada):
    depth, d, n6 = w_ada.shape
    r = cond.shape[0]
    tn = 1536
    return pl.pallas_call(
        _mod_kernel,
        name="modulation",
        grid=(depth, n6 // tn),
        in_specs=[
            pl.BlockSpec((r, d), lambda l, j: (0, 0)),
            pl.BlockSpec((1, d, tn), lambda l, j: (l, 0, j)),
            pl.BlockSpec((1, 1, tn), lambda l, j: (l, 0, j)),
        ],
        out_specs=pl.BlockSpec((1, r, tn), lambda l, j: (l, 0, j)),
        out_shape=jax.ShapeDtypeStruct((depth, r, n6), F32),
        compiler_params=_cparams(2),
    )(cond, w_ada, b_ada.reshape(depth, 1, n6))


def _pre_kernel(latent, d, *refs):
    if latent:
        (x_ref, mod_ref, g_ref, win_ref, gq_ref, wuq_ref, gkv_ref, wk_ref, wv_ref, cc_ref, sc_ref,
         cd_ref, sd_ref, cm_ref, sm_ref,
         u_ref, v_ref, qd_ref, kd_ref, vd_ref, qm_ref, km_ref, vm_ref) = refs
    else:
        (x_ref, mod_ref, g_ref, win_ref, gq_ref, wuq_ref, gkv_ref, wk_ref, wv_ref, cc_ref, sc_ref,
         u_ref, v_ref, qd_ref, kd_ref, vd_ref, qm_ref, km_ref, vm_ref,
         k32_ref, v32_ref, ckv32_ref, kpe32_ref) = refs[:11] + refs[-12:]
    sh = mod_ref[0, :, 0:d]
    sc = mod_ref[0, :, d:2 * d]
    tm = x_ref.shape[1]
    sub = SUB_ROWS if tm > SUB_ROWS else max(tm // 2, BF_ROWS)
    def project(r0):
        h = _rms(x_ref[0, r0:r0 + sub, :], g_ref[...]) * (1.0 + sc) + sh
        return _dot(h.astype(BF), win_ref[...])

    starts = list(range(0, tm, sub))
    z_next = project(starts[0])
    for i, r0 in enumerate(starts):
        rows = slice(r0, r0 + sub)
        z = z_next
        z_next = project(starts[i + 1]) if i + 1 < len(starts) else None
        o = 0
        zf = z[:, o:o + D_FOURIER]; o += D_FOURIER
        zq = z[:, o:o + D_DIFF]; o += D_DIFF
        zk = z[:, o:o + D_DIFF]; o += D_DIFF
        zv = z[:, o:o + D_DIFF]; o += D_DIFF
        zcq = z[:, o:o + MLA_Q_RANK]; o += MLA_Q_RANK
        zckv = z[:, o:o + MLA_KV_RANK]; o += MLA_KV_RANK
        kpe = z[:, o:o + HEAD_W]

        zf_b = zf.astype(BF)
        u_ref[0, rows, :] = _dot(zf_b, cc_ref[...]).astype(BF)
        v_ref[0, rows, :] = _dot(zf_b, sc_ref[...]).astype(BF)
        vd_ref[0, :, rows] = zv.T.astype(BF)
        cq = _rms(zcq, gq_ref[...])
        qm = _dot(cq.astype(BF), wuq_ref[...])
        ckv = _rms(zckv, gkv_ref[...])
        ckv_b = ckv.astype(BF)
        k_nope = _dot(ckv_b, wk_ref[...])
        vm_ref[0, :, rows] = _dot(ckv_b, wv_ref[...]).T.astype(BF)
        if latent:
            cd, sd, cm, sm = cd_ref[rows, :], sd_ref[rows, :], cm_ref[rows, :], sm_ref[rows, :]
            zq_r = _rope(zq, cd, sd, DIFF_DH // 2, DIFF_DH, 0)
            zk_r = _rope(zk, cd, sd, DIFF_DH // 2, DIFF_DH, 0)
            qm = _rope(qm, cm, sm, MLA_ROPE // 2, HEAD_W, MLA_NOPE)
            kpe_r = _rope(kpe, cm, sm, MLA_ROPE // 2, HEAD_W, MLA_NOPE)
        else:
            zq_r, zk_r, kpe_r = zq, zk, kpe
            for hd in range(DIFF_HEADS):
                k32_ref[0, 0, rows, hd, :] = zk[:, hd * HEAD_W:(hd + 1) * HEAD_W]
                v32_ref[0, 0, rows, hd, :] = zv[:, hd * HEAD_W:(hd + 1) * HEAD_W]
            ckv32_ref[0, 0, rows, :] = ckv
            kpe32_ref[0, 0, rows, :] = kpe[:, MLA_NOPE:MLA_NOPE + MLA_ROPE]
            if i == 0:
                for ref in (k32_ref, v32_ref, ckv32_ref, kpe32_ref):
                    if ref.shape[1] > 1:
                        ref[0, 1:] = jnp.zeros((ref.shape[1] - 1,) + ref.shape[2:], F32)
        qd_ref[0, rows, :] = (zq_r * (DIFF_DH ** -0.5 * LOG2E)).astype(BF)
        kd_ref[0, rows, :] = zk_r.astype(BF)
        qm_ref[0, rows, :] = (qm * ((MLA_NOPE + MLA_ROPE) ** -0.5 * LOG2E)).astype(BF)
        km_ref[0, rows, :] = (k_nope + jnp.concatenate([kpe_r] * MLA_HEADS, axis=1)).astype(BF)


def _combine_pre_kernel(latent, d, cap, n_pre_in, *refs):
    x_ref, modp_ref, selt_ref, ye_ref = refs[:4]
    pre_in = refs[4:4 + n_pre_in]
    xo_ref = refs[4 + n_pre_in]
    pre_out = refs[5 + n_pre_in:-1]
    xo_ref[0] = _combine_rows(cap, d, x_ref, modp_ref, selt_ref, ye_ref, refs[-1], slice(None))
    _pre_kernel(latent, d, xo_ref, *pre_in, *pre_out)


def _pre(latent, x, modv, g_attn, w_in_p, g_q, w_uq_p, g_kv, w_k_p, w_v_p, cc_bd, sc_bd, ropes, l=0, depth=1, prev=None,
         combine=None):
    b, n, d = x.shape
    tm = min(PRE_ROWS, n)
    bm = modv.shape[0]
    mod_idx = (lambda i, j: (i, 0, 0)) if bm > 1 else (lambda i, j: (0, 0, 0))
    const = lambda i, j: (0, 0)
    tok = lambda i, j: (i, j, 0)
    in_specs = [
        pl.BlockSpec((1, tm, d), tok),
        pl.BlockSpec((1, 1, modv.shape[2]), mod_idx),
        pl.BlockSpec((1, d), const),
        pl.BlockSpec(w_in_p.shape, const),
        pl.BlockSpec((1, MLA_Q_RANK), const),
        pl.BlockSpec(w_uq_p.shape, const),
        pl.BlockSpec((1, MLA_KV_RANK), const),
        pl.BlockSpec(w_k_p.shape, const),
        pl.BlockSpec(w_v_p.shape, const),
        pl.BlockSpec(cc_bd.shape, const),
        pl.BlockSpec(sc_bd.shape, const),
    ]
    args = [x, modv, g_attn.reshape(1, d), w_in_p, g_q.reshape(1, -1), w_uq_p, g_kv.reshape(1, -1), w_k_p, w_v_p, cc_bd, sc_bd]
    if latent:
        in_specs += [pl.BlockSpec((tm, HEAD_W), lambda i, j: (j, 0))] * 4
        args += list(ropes)
    feat = lambda i, j: (i, 0, j)
    out_shape, out_specs = [], []
    for w, token_major in ((D_FOURIER, True), (D_FOURIER, True), (D_DIFF, True), (D_DIFF, True), (D_DIFF, False),
                           (D_MLA_P, True), (D_MLA_P, True), (D_MLA, False)):
        out_shape.append(jax.ShapeDtypeStruct((b, n, w) if token_major else (b, w, n), BF))
        out_specs.append(pl.BlockSpec((1, tm, w), tok) if token_major else pl.BlockSpec((1, w, tm), feat))
    aliases = {}
    if not latent:
        span = depth if prev is None else 1
        for tail in ((DIFF_HEADS, HEAD_W), (DIFF_HEADS, HEAD_W), (MLA_KV_RANK,), (MLA_ROPE,)):
            out_shape.append(jax.ShapeDtypeStruct((b, depth, n) + tail, F32))
            out_specs.append(pl.BlockSpec((1, span, tm) + tail, lambda i, j, t=len(tail): (i, l, j) + (0,) * t))
        if prev is not None:
            for k, buf in enumerate(prev):
                aliases[len(args)] = len(out_shape) - len(prev) + k
                in_specs.append(pl.BlockSpec(memory_space=pl.ANY))
                args.append(buf)
    body, scratch, tag = functools.partial(_pre_kernel, latent, d), [], "pre"
    if combine is not None:
        modp, sel_t, ye, cap = combine
        e = ye.shape[0]
        modp_idx = (lambda i, j: (i, 0, 0)) if modp.shape[0] > 1 else (lambda i, j: (0, 0, 0))
        extra_specs = [pl.BlockSpec((1, 1, modp.shape[2]), modp_idx), pl.BlockSpec((1, tm, e), tok),
                       pl.BlockSpec((e, 1, cap, d), lambda i, j: (0, i, 0, 0))]
        n_extra = len(extra_specs)
        body = functools.partial(_combine_pre_kernel, latent, d, cap, len(args) - 1)
        in_specs = in_specs[:1] + extra_specs + in_specs[1:]
        args = args[:1] + [modp, sel_t, ye] + args[1:]
        out_specs = [pl.BlockSpec((1, tm, d), tok)] + out_specs
        out_shape = [jax.ShapeDtypeStruct((b, n, d), F32)] + out_shape
        aliases = {k + n_extra: v + 1 for k, v in aliases.items()}
        scratch, tag = [pltpu.VMEM((tm, e * cap), BF)], "combine_pre"
    return pl.pallas_call(
        body,
        name=tag + ("_lat" if latent else "_ctx"),
        grid=(b, n // tm),
        in_specs=in_specs,
        out_specs=out_specs,
        out_shape=out_shape,
        input_output_aliases=aliases,
        scratch_shapes=scratch,
        compiler_params=_cparams(2),
    )(*args)


def _cache_kernel(dk_ref, dv_ref, ckv_ref, kpe_ref, wk_ref, wv_ref, place_ref, kd_ref, vd_ref, km_ref, vm_ref):
    for hd in range(DIFF_HEADS):
        sl = slice(hd * HEAD_W, (hd + 1) * HEAD_W)
        kd_ref[0, :, sl] = dk_ref[0, 0, :, hd, :].astype(BF)
        vd_ref[0, sl, :] = dv_ref[0, 0, :, hd, :].T.astype(BF)
    ckv_b = ckv_ref[0, 0].astype(BF)
    kpe_w = _dot(kpe_ref[0, 0].astype(BF), place_ref[...])
    km_ref[0] = (_dot(ckv_b, wk_ref[...]) + kpe_w).astype(BF)
    vm_ref[0] = _dot(ckv_b, wv_ref[...]).T.astype(BF)


def _cache_prep(l, cache_dk, cache_dv, cache_ckv, cache_kpe, w_k_p, w_v_p, place):
    b, _, p = cache_dk.shape[:3]
    at_l = lambda i: (i, l, 0, 0)
    at_l5 = lambda i: (i, l, 0, 0, 0)
    const = lambda i: (0, 0)
    shapes = [(p, D_DIFF), (D_DIFF, p), (p, D_MLA_P), (D_MLA, p)]
    return pl.pallas_call(
        _cache_kernel,
        name="cache_prep",
        grid=(b,),
        in_specs=[
            pl.BlockSpec((1, 1, p, DIFF_HEADS, HEAD_W), at_l5),
            pl.BlockSpec((1, 1, p, DIFF_HEADS, HEAD_W), at_l5),
            pl.BlockSpec((1, 1, p, MLA_KV_RANK), at_l),
            pl.BlockSpec((1, 1, p, MLA_ROPE), at_l),
            pl.BlockSpec(w_k_p.shape, const),
            pl.BlockSpec(w_v_p.shape, const),
            pl.BlockSpec(place.shape, const),
        ],
        out_specs=[pl.BlockSpec((1,) + s, lambda i: (i, 0, 0)) for s in shapes],
        out_shape=[jax.ShapeDtypeStruct((b,) + s, BF) for s in shapes],
        compiler_params=_cparams(1),
    )(cache_dk, cache_dv, cache_ckv, cache_kpe, w_k_p, w_v_p, place)


def _fourier_kernel(scale, u_ref, v_ref, cn_ref, sn_ref, w_ref, o_ref):
    f = (_dot(cn_ref[...], u_ref[0]) - _dot(sn_ref[...], v_ref[0])) * scale
    o_ref[0] = _dot(f.astype(BF), w_ref[...]).astype(BF)


def _fourier(u, v, cn, sn, w_bd):
    b, n, w = u.shape
    tm = min(STREAM_ROWS, n)
    return pl.pallas_call(
        functools.partial(_fourier_kernel, 1.0 / math.sqrt(n * F_CH)),
        name="fourier_%d" % n,
        grid=(n // tm, b),
        in_specs=[
            pl.BlockSpec((1, n, w), lambda i, j: (j, 0, 0)),
            pl.BlockSpec((1, n, w), lambda i, j: (j, 0, 0)),
            pl.BlockSpec((tm, n), lambda i, j: (i, 0)),
            pl.BlockSpec((tm, n), lambda i, j: (i, 0)),
            pl.BlockSpec((w, w), lambda i, j: (0, 0)),
        ],
        out_specs=pl.BlockSpec((1, tm, w), lambda i, j: (j, i, 0)),
        out_shape=jax.ShapeDtypeStruct((b, n, w), BF),
        compiler_params=_cparams(2),
    )(u, v, cn, sn, w_bd)


def _scores(q, k_parts, s_ref):
    off, m, bounds = 0, None, []
    for k in k_parts:
        s = _dot_nt(k(), q)
        n_i = s.shape[0]
        s_ref[off:off + n_i, :] = s
        mi = jnp.max(s, axis=0, keepdims=True)
        m = mi if m is None else jnp.maximum(m, mi)
        bounds.append((off, n_i))
        off += n_i
    return m, bounds


def _weighted(m, vt_parts, bounds, s_ref):
    o = None
    for vt, (r, n_i) in zip(vt_parts, bounds):
        v = vt()
        ones = jnp.where(lax.broadcasted_iota(jnp.int32, (BF_ROWS, n_i), 0) == 0, 1.0, 0.0).astype(BF)
        v_ext = jnp.concatenate([v, ones], axis=0)
        for c in range(0, n_i, KEY_CHUNK):
            rows = min(KEY_CHUNK, n_i - c)
            p = jnp.exp2(s_ref[r + c:r + c + rows, :] - m).astype(BF)
            oi = _dot(v_ext[:, c:c + rows], p)
            o = oi if o is None else o + oi
    dv = o.shape[0] - BF_ROWS
    return o[:dv] * (1.0 / o[dv:dv + 1])


def _attn_kernel(latent, lam_init, d, *refs):
    if latent:
        (x_ref, mod_ref, yf_ref, qd_ref, kd_ref, vd_ref, kdc_ref, vdc_ref, qm_ref, km_ref, vm_ref, kmc_ref, vmc_ref,
         lam_ref, gs_ref, wo_ref, o_ref, mix_ref, s_ref) = refs
    else:
        (x_ref, mod_ref, yf_ref, qd_ref, kd_ref, vd_ref, qm_ref, km_ref, vm_ref,
         lam_ref, gs_ref, wo_ref, o_ref, mix_ref, s_ref) = refs
        kdc_ref = vdc_ref = kmc_ref = vmc_ref = None
    lf = lam_ref[...]
    lam = (jnp.exp(jnp.sum(lf[0:1] * lf[1:2], axis=-1, keepdims=True))
           - jnp.exp(jnp.sum(lf[2:3] * lf[3:4], axis=-1, keepdims=True)) + lam_init)
    mix_ref[:, 0:D_FOURIER] = yf_ref[0]
    gs = gs_ref[...]

    def keys(ref, cache_ref, sl):
        parts = [lambda: ref[0, :, sl]]
        if latent:
            parts.append(lambda: cache_ref[0, :, sl])
        return parts

    def values_t(ref, cache_ref, sl):
        parts = [lambda: ref[0, sl, :]]
        if latent:
            parts.append(lambda: cache_ref[0, sl, :])
        return parts

    def diff_query(sl, comp):
        def load():
            qh = qd_ref[0, :, sl]
            lane = lax.broadcasted_iota(jnp.int32, qh.shape, 1)
            keep = (lane >= DIFF_DH) if comp else (lane < DIFF_DH)
            return jnp.where(keep, qh, jnp.zeros_like(qh))
        return load

    items = []
    for hd in range(DIFF_HEADS):
        sl = slice(hd * HEAD_W, (hd + 1) * HEAD_W)
        for comp in range(2):
            items.append((diff_query(sl, comp), keys(kd_ref, kdc_ref, sl), values_t(vd_ref, vdc_ref, sl)))
    for hd in range(MLA_HEADS):
        sl = slice(hd * HEAD_W, (hd + 1) * HEAD_W)
        vsl = slice(hd * MLA_V, (hd + 1) * MLA_V)
        items.append(((lambda sl=sl: qm_ref[0, :, sl]), keys(km_ref, kmc_ref, sl), values_t(vm_ref, vmc_ref, vsl)))

    def start(i):
        return _scores(items[i][0](), items[i][1], s_ref.at[i % SCORE_BUFS])

    outs = []
    state = start(0)
    for i in range(len(items)):
        nxt = start(i + 1) if i + 1 < len(items) else None
        m, bounds = state
        buf = i % SCORE_BUFS
        outs.append(_weighted(m, items[i][2], bounds, s_ref.at[buf]))
        state = nxt
        if i < 2 * DIFF_HEADS and i % 2 == 1:
            hd = i // 2
            o = outs[i - 1] - lam * outs[i]
            o = o * lax.rsqrt(jnp.mean(o * o, axis=0, keepdims=True) + NORM_EPS) * gs * (1.0 - lam_init)
            mix_ref[:, D_FOURIER + hd * HEAD_W:D_FOURIER + (hd + 1) * HEAD_W] = o.T.astype(BF)

    mix_ref[:, D_FOURIER + D_DIFF:D_MIX] = jnp.concatenate(outs[2 * DIFF_HEADS:], axis=0).T.astype(BF)
    y = _dot(mix_ref[...], wo_ref[...])
    gt = mod_ref[0, :, 2 * d:3 * d]
    o_ref[0] = x_ref[0] + gt * y


def _attention(latent, l, x, modv, yfour, qd, kd, vd, qm, km, vm, cache, diff_lambda_l, g_subln_l, w_out_b):
    b, n, d = x.shape
    tq = min(ATTN_ROWS, n)
    bm = modv.shape[0]
    mod_idx = (lambda i, j: (i, 0, 0)) if bm > 1 else (lambda i, j: (0, 0, 0))
    const = lambda i, j: (0, 0)
    tok = lambda i, j: (i, j, 0)
    full = lambda i, j: (i, 0, 0)
    p = cache[0].shape[1] if latent else 0
    in_specs = [
        pl.BlockSpec((1, tq, d), tok),
        pl.BlockSpec((1, 1, modv.shape[2]), mod_idx),
        pl.BlockSpec((1, tq, D_FOURIER), tok),
        pl.BlockSpec((1, tq, D_DIFF), tok),
        pl.BlockSpec((1, n, D_DIFF), full),
        pl.BlockSpec((1, D_DIFF, n), full),
    ]
    args = [x, modv, yfour, qd, kd, vd]
    if latent:
        kdc, vdc, kmc, vmc = cache
        in_specs += [pl.BlockSpec((1, p, D_DIFF), full), pl.BlockSpec((1, D_DIFF, p), full)]
        args += [kdc, vdc]
    in_specs += [pl.BlockSpec((1, tq, D_MLA_P), tok), pl.BlockSpec((1, n, D_MLA_P), full), pl.BlockSpec((1, D_MLA, n), full)]
    args += [qm, km, vm]
    if latent:
        in_specs += [pl.BlockSpec((1, p, D_MLA_P), full), pl.BlockSpec((1, D_MLA, p), full)]
        args += [kmc, vmc]
    in_specs += [pl.BlockSpec((4, DIFF_DH), const), pl.BlockSpec((HEAD_W, 1), const), pl.BlockSpec(w_out_b.shape, const)]
    args += [diff_lambda_l, g_subln_l.reshape(HEAD_W, 1), w_out_b]
    lam_init = 0.8 - 0.6 * math.exp(-0.3 * l)
    return pl.pallas_call(
        functools.partial(_attn_kernel, latent, lam_init, d),
        name="attn_lat" if latent else "attn_ctx",
        grid=(b, n // tq),
        in_specs=in_specs,
        out_specs=pl.BlockSpec((1, tq, d), tok),
        out_shape=jax.ShapeDtypeStruct((b, n, d), F32),
        scratch_shapes=[pltpu.VMEM((tq, D_MIX), BF), pltpu.VMEM((SCORE_BUFS, n + p, tq), F32)],
        compiler_params=_cparams(2),
    )(*args)


def _router_kernel(d, packed, x_ref, mod_ref, g_ref, wr_ref, h_ref, aff_ref):
    sh = mod_ref[0, :, 3 * d:4 * d]
    sc = mod_ref[0, :, 4 * d:5 * d]
    tm = x_ref.shape[1]
    sub = SUB_ROWS if tm > SUB_ROWS else max(tm // 2, BF_ROWS)
    for r0 in range(0, tm, sub):
        rows = slice(r0, r0 + sub)
        hb = (_rms(x_ref[0, rows, :], g_ref[...]) * (1.0 + sc) + sh).astype(BF)
        h_ref[0, rows, :] = _pack_halves(hb) if packed else hb
        logits = _dot_nt(wr_ref[...], hb)
        e = jnp.exp(logits - jnp.max(logits, axis=0, keepdims=True))
        aff_ref[0, :, rows] = e / jnp.sum(e, axis=0, keepdims=True)


def _router(x, modv, g_ffn_l, w_router_t, packed):
    b, n, d = x.shape
    hw, hdt = (d // 2, jnp.int32) if packed else (d, BF)
    tm = min(STREAM_ROWS, n)
    bm = modv.shape[0]
    mod_idx = (lambda i, j: (i, 0, 0)) if bm > 1 else (lambda i, j: (0, 0, 0))
    const = lambda i, j: (0, 0)
    return pl.pallas_call(
        functools.partial(_router_kernel, d, packed),
        name="router_%d" % n,
        grid=(b, n // tm),
        in_specs=[
            pl.BlockSpec((1, tm, d), lambda i, j: (i, j, 0)),
            pl.BlockSpec((1, 1, modv.shape[2]), mod_idx),
            pl.BlockSpec((1, d), const),
            pl.BlockSpec(w_router_t.shape, const),
        ],
        out_specs=[pl.BlockSpec((1, tm, hw), lambda i, j: (i, j, 0)), pl.BlockSpec((1, N_EXPERTS, tm), lambda i, j: (i, 0, j))],
        out_shape=[jax.ShapeDtypeStruct((b, n, hw), hdt), jax.ShapeDtypeStruct((b, N_EXPERTS, n), F32)],
        compiler_params=_cparams(2),
    )(x, modv, g_ffn_l.reshape(1, d), w_router_t)


def _select_kernel(cap, aff_ref, sel_ref):
    a = aff_ref[...]
    e, n = a.shape
    bits = pltpu.bitcast(a, jnp.int32)
    capf = float(cap)

    def body(_, carry):
        lo, hi = carry
        mid = lo + ((hi - lo) >> 1)
        cnt = jnp.sum(jnp.where(bits >= mid, 1.0, 0.0), axis=1, keepdims=True)
        up = cnt >= capf
        return jnp.where(up, mid, lo), jnp.where(up, hi, mid)

    lo0 = jnp.zeros((e, 1), jnp.int32)
    hi0 = jnp.full((e, 1), 0x7F800000, jnp.int32)
    thr, _ = lax.fori_loop(0, 31, body, (lo0, hi0))
    gt = bits > thr
    eq = bits == thr
    need = capf - jnp.sum(jnp.where(gt, 1.0, 0.0), axis=1, keepdims=True)
    both = jnp.concatenate([jnp.where(gt, 1.0, 0.0), jnp.where(eq, 1.0, 0.0)], axis=0).astype(BF)
    ck = min(256, n)
    tri = jnp.where(lax.broadcasted_iota(jnp.int32, (ck, ck), 0) < lax.broadcasted_iota(jnp.int32, (ck, ck), 1), 1.0, 0.0).astype(BF)
    off = jnp.zeros((2 * e, 1), F32)
    pieces = []
    for k in range(n // ck):
        blk = both[:, k * ck:(k + 1) * ck]
        pieces.append(_dot(blk, tri) + off)
        off = off + jnp.sum(blk.astype(F32), axis=1, keepdims=True)
    cum = jnp.concatenate(pieces, axis=1) if len(pieces) > 1 else pieces[0]
    cum_gt, cum_eq = cum[:e], cum[e:]
    chosen = gt | (eq & (cum_eq < need))
    pos = cum_gt + jnp.minimum(cum_eq, need)
    sel_ref[...] = jnp.where(chosen, pos, -1.0).astype(jnp.int32)


def _select(aff, cap):
    b, e, n = aff.shape
    rows = b * e
    rb = min(rows, SELECT_ROWS)
    sel = pl.pallas_call(
        functools.partial(_select_kernel, cap),
        name="select_%d" % n,
        grid=(rows // rb,),
        in_specs=[pl.BlockSpec((rb, n), lambda i: (i, 0))],
        out_specs=pl.BlockSpec((rb, n), lambda i: (i, 0)),
        out_shape=jax.ShapeDtypeStruct((rows, n), jnp.int32),
        compiler_params=_cparams(1),
    )(aff.reshape(rows, n))
    return sel.reshape(b, e, n)


def _gather_kernel(cap, sel_ref, aff_ref, h_ref, xs_ref, gate_ref):
    h = h_ref[0]
    n = h.shape[0]
    e_tot = sel_ref.shape[1]

    def one(e_idx):
        sel_e = sel_ref[0, pl.ds(e_idx, 1), :]
        aff_e = aff_ref[0, pl.ds(e_idx, 1), :]
        hit = sel_e == lax.broadcasted_iota(jnp.int32, (cap, n), 0)
        g = jnp.sum(jnp.where(hit, aff_e, 0.0), axis=1, keepdims=True)
        return jnp.where(hit, 1.0, 0.0).astype(BF), jnp.broadcast_to(g, (cap, HEAD_W))

    if cap >= 128:
        def body(e_idx, carry):
            p, g = one(e_idx)
            xs_ref[e_idx, 0] = _dot(p, h).astype(BF)
            gate_ref[e_idx, 0] = g
            return carry

        lax.fori_loop(0, e_tot, body, 0)
    else:
        ps, gs = zip(*[one(e_idx) for e_idx in range(e_tot)])
        xs = _dot(jnp.concatenate(ps, axis=0), h).astype(BF)
        for e_idx in range(e_tot):
            xs_ref[e_idx, 0] = xs[e_idx * cap:(e_idx + 1) * cap]
            gate_ref[e_idx, 0] = gs[e_idx]


def _gather(sel, aff, h2, cap):
    b, e, n = sel.shape
    d = h2.shape[2]
    return pl.pallas_call(
        functools.partial(_gather_kernel, cap),
        name="gather_%d" % n,
        grid=(b,),
        in_specs=[
            pl.BlockSpec((1, e, n), lambda i: (i, 0, 0)),
            pl.BlockSpec((1, e, n), lambda i: (i, 0, 0)),
            pl.BlockSpec((1, n, d), lambda i: (i, 0, 0)),
        ],
        out_specs=[pl.BlockSpec((e, 1, cap, d), lambda i: (0, i, 0, 0)), pl.BlockSpec((e, 1, cap, HEAD_W), lambda i: (0, i, 0, 0))],
        out_shape=[jax.ShapeDtypeStruct((e, b, cap, d), BF), jax.ShapeDtypeStruct((e, b, cap, HEAD_W), F32)],
        compiler_params=_cparams(1),
    )(sel, aff, h2)


def _sc_dispatch(table, sel, aff, cap, n_experts):
    info = plsc.get_sparse_core_info()
    n_workers = info.num_cores * info.num_subcores
    lanes = info.num_lanes
    pairs, n = sel.shape
    b_tot = pairs // n_experts
    w = table.shape[1]
    per_w = pairs // n_workers
    ch = min(SC_CHUNK, cap)
    n_ch = cap // ch
    assert pairs % n_workers == 0 and cap % ch == 0 and ch % 8 == 0 and n % lanes == 0
    mesh = plsc.VectorSubcoreMesh(core_axis_name="c", subcore_axis_name="s")

    @functools.partial(
        pl.kernel, mesh=mesh, name="sc_dispatch",
        compiler_params=pltpu.CompilerParams(needs_layout_passes=False),
        out_type=[jax.ShapeDtypeStruct((pairs * cap, w), jnp.int32), jax.ShapeDtypeStruct((pairs * cap,), F32)],
        scratch_types=[pltpu.VMEM((n,), jnp.int32), pltpu.VMEM((n,), F32), pltpu.VMEM((n_ch, ch), jnp.int32),
                       pltpu.VMEM((cap,), F32), pltpu.VMEM((2, ch, w), jnp.int32), pltpu.SemaphoreType.DMA((2,))],
    )
    def k(table_hbm, sel_hbm, aff_hbm, xs_hbm, gate_hbm, sel_v, aff_v, tok_v, gate_v, rows_v, sem):
        wid = lax.axis_index("s") * info.num_cores + lax.axis_index("c")

        @pl.loop(0, per_w)
        def _(i):
            p = wid * per_w + i
            b = p // n_experts
            e = p - b * n_experts
            out_off = pl.multiple_of((e * b_tot + b) * cap, 8)
            pltpu.sync_copy(sel_hbm.at[p], sel_v)
            pltpu.sync_copy(aff_hbm.at[p], aff_v)
            lane = lax.iota(jnp.int32, lanes)

            @plsc.parallel_loop(0, n // lanes, unroll=4)
            def _(j):
                t0 = pl.multiple_of(j * lanes, lanes)
                s = sel_v[pl.ds(t0, lanes)]
                chosen = s >= 0
                slot = jnp.maximum(s, 0)
                plsc.store_scatter(tok_v, [slot // ch, slot % ch], lane + (t0 + b * n), mask=chosen)
                plsc.store_scatter(gate_v, [slot], aff_v[pl.ds(t0, lanes)], mask=chosen)

            prev = None
            for c in range(n_ch + 1):
                cur = None
                if c < n_ch:
                    cur = pltpu.async_copy(table_hbm.at[tok_v.at[c]], rows_v.at[c % 2], sem.at[c % 2])
                if prev is not None:
                    prev.wait()
                    pltpu.sync_copy(rows_v.at[(c - 1) % 2], xs_hbm.at[pl.ds(out_off + (c - 1) * ch, ch)])
                prev = cur
            pltpu.sync_copy(gate_v, gate_hbm.at[pl.ds(out_off, cap)])

    return k(table, sel, aff)


def _expert_kernel(steps_a, xa_ref, ga_ref, xb_ref, gb_ref, wg_ref, wu_ref, wd_ref, ya_ref, yb_ref, wg_s, wu_s, wd_s):
    j = pl.program_id(1)

    @pl.when(j == 0)
    def _():
        wg_s[...] = wg_ref[0, 0].astype(BF)
        wu_s[...] = wu_ref[0, 0].astype(BF)
        wd_s[...] = wd_ref[0, 0].astype(BF)

    def ffn(x_ref, g_ref, y_ref):
        rb = x_ref.shape[1]
        sub = min(rb, EXPERT_SUB_ROWS)
        for r0 in range(0, rb, sub):
            rows = slice(r0, r0 + sub)
            x = _unpack_halves(x_ref[0, rows, :]) if x_ref.dtype == jnp.int32 else x_ref[0, rows, :]
            a = _dot(x, wg_s[...])
            u = _dot(x, wu_s[...])
            mid = (a * (1.0 / (1.0 + jnp.exp(-a))) * u).astype(BF)
            y_ref[0, rows, :] = (_dot(mid, wd_s[...]) * g_ref[0, rows, 0:1]).astype(BF)

    @pl.when(j < steps_a)
    def _():
        ffn(xa_ref, ga_ref, ya_ref)

    @pl.when(j >= steps_a)
    def _():
        ffn(xb_ref, gb_ref, yb_ref)


def _experts(l, xs_a, gate_a, xs_b, gate_b, w_e_gate, w_e_up, w_e_down):
    e, rows_a, wa = xs_a.shape
    rows_b, wb = xs_b.shape[1:]
    d, ff = w_e_gate.shape[2:]
    rb_a, rb_b = math.gcd(rows_a, EXPERT_ROWS), math.gcd(rows_b, EXPERT_ROWS)
    steps_a, steps_b = rows_a // rb_a, rows_b // rb_b
    w_idx = lambda i, j: (l, i, 0, 0)
    idx_a = lambda i, j: (i, jnp.minimum(j, steps_a - 1), 0)
    idx_b = lambda i, j: (i, jnp.maximum(j - steps_a, 0), 0)
    return pl.pallas_call(
        functools.partial(_expert_kernel, steps_a),
        name="experts",
        grid=(e, steps_a + steps_b),
        in_specs=[
            pl.BlockSpec((1, rb_a, wa), idx_a),
            pl.BlockSpec((1, rb_a, HEAD_W), idx_a),
            pl.BlockSpec((1, rb_b, wb), idx_b),
            pl.BlockSpec((1, rb_b, HEAD_W), idx_b),
            pl.BlockSpec((1, 1, d, ff), w_idx),
            pl.BlockSpec((1, 1, d, ff), w_idx),
            pl.BlockSpec((1, 1, ff, d), w_idx),
        ],
        out_specs=[pl.BlockSpec((1, rb_a, d), idx_a), pl.BlockSpec((1, rb_b, d), idx_b)],
        out_shape=[jax.ShapeDtypeStruct((e, rows_a, d), BF), jax.ShapeDtypeStruct((e, rows_b, d), BF)],
        scratch_shapes=[pltpu.VMEM((d, ff), BF), pltpu.VMEM((d, ff), BF), pltpu.VMEM((ff, d), BF)],
        compiler_params=_cparams(2),
    )(xs_a, gate_a, xs_b, gate_b, w_e_gate, w_e_up, w_e_down)


def _combine_rows(cap, d, x_ref, mod_ref, selt_ref, ye_ref, pt_ref, rows):
    e_tot = ye_ref.shape[0]
    selt = selt_ref[0, rows, :]
    r = selt.shape[0]
    if cap % 128 == 0:
        lane = lax.broadcasted_iota(jnp.int32, (r, cap), 1)
        for e_idx in range(e_tot):
            pt_ref[rows, e_idx * cap:(e_idx + 1) * cap] = jnp.where(selt[:, e_idx:e_idx + 1] == lane, 1.0, 0.0).astype(BF)
    else:
        assert cap & (cap - 1) == 0
        shift = cap.bit_length() - 1
        col = lax.broadcasted_iota(jnp.int32, (e_tot, e_tot * cap), 1)
        spread = jnp.where((col >> shift) == lax.broadcasted_iota(jnp.int32, col.shape, 0), 1.0, 0.0).astype(BF)
        slot_of_col = _dot(selt.astype(F32).astype(BF), spread)
        want = (lax.broadcasted_iota(jnp.int32, (r, e_tot * cap), 1) & (cap - 1)).astype(F32)
        pt_ref[rows, :] = jnp.where(slot_of_col == want, 1.0, 0.0).astype(BF)
    y = _dot(pt_ref[rows, :], ye_ref[...].reshape(e_tot * cap, d))
    return x_ref[0, rows, :] + mod_ref[0, :, 5 * d:6 * d] * y


def _scatter_kernel(cap, d, x_ref, mod_ref, selt_ref, ye_ref, gf_ref, o_ref, pt_ref):
    o_ref[0] = _rms(_combine_rows(cap, d, x_ref, mod_ref, selt_ref, ye_ref, pt_ref, slice(None)), gf_ref[...])


def _scatter(x, modv, sel_t, ye, cap, g_final):
    b, n, d = x.shape
    e = ye.shape[0]
    tn = min(PRE_ROWS, n)
    bm = modv.shape[0]
    mod_idx = (lambda i, j: (i, 0, 0)) if bm > 1 else (lambda i, j: (0, 0, 0))
    return pl.pallas_call(
        functools.partial(_scatter_kernel, cap, d),
        name="scatter_%d" % n,
        grid=(b, n // tn),
        in_specs=[
            pl.BlockSpec((1, tn, d), lambda i, j: (i, j, 0)),
            pl.BlockSpec((1, 1, modv.shape[2]), mod_idx),
            pl.BlockSpec((1, tn, e), lambda i, j: (i, j, 0)),
            pl.BlockSpec((e, 1, cap, d), lambda i, j: (0, i, 0, 0)),
            pl.BlockSpec((1, d), lambda i, j: (0, 0)),
        ],
        out_specs=pl.BlockSpec((1, tn, d), lambda i, j: (i, j, 0)),
        out_shape=jax.ShapeDtypeStruct((b, n, d), F32),
        scratch_shapes=[pltpu.VMEM((tn, e * cap), BF)],
        compiler_params=_cparams(2),
    )(x, modv, sel_t, ye, g_final.reshape(1, d))


def _dft_tables(n):
    j = np.arange(n, dtype=np.int64)
    ang = ((j[:, None] * j[None, :]) % n).astype(np.float64) * (2.0 * math.pi / n)
    return jnp.asarray(np.cos(ang), dtype=BF), jnp.asarray(np.sin(ang), dtype=BF)


def _block_diag(blocks):
    g, r, c = blocks.shape
    out = jnp.zeros((g * r, g * c), blocks.dtype)
    for i in range(g):
        out = out.at[i * r:(i + 1) * r, i * c:(i + 1) * c].set(blocks[i])
    return out


def _rope_tables(n):
    rows = n // GRID_W
    row = jnp.repeat(jnp.arange(rows, dtype=F32), GRID_W)
    col = jnp.tile(jnp.arange(GRID_W, dtype=F32), rows)

    def ang(dim):
        nf = dim // 4
        inv = ROPE_BASE ** (-jnp.arange(nf, dtype=F32) / nf)
        return jnp.concatenate([row[:, None] * inv, col[:, None] * inv], axis=-1)

    a = ang(DIFF_DH)
    cos_d = jnp.tile(jnp.cos(a), (1, 4))
    sin_d = jnp.tile(jnp.concatenate([-jnp.sin(a), jnp.sin(a)], axis=1), (1, 2))
    a = ang(MLA_ROPE)
    ones = jnp.ones((n, MLA_NOPE), F32)
    pad = HEAD_W - MLA_NOPE - MLA_ROPE
    cos_m = jnp.concatenate([ones, jnp.cos(a), jnp.cos(a), jnp.ones((n, pad), F32)], axis=1)
    sin_m = jnp.concatenate([0 * ones, -jnp.sin(a), jnp.sin(a), jnp.zeros((n, pad), F32)], axis=1)
    return cos_d, sin_d, cos_m, sin_m


def _pad_heads(w, heads, lo, hi):
    k = w.shape[0]
    w3 = w.reshape(k, heads, -1)[:, :, lo:hi]
    return jnp.pad(w3, ((0, 0), (0, 0), (0, HEAD_W - (hi - lo)))).reshape(k, heads * HEAD_W)


def kernel(x_prompt, x_sample, cache_diff_k, cache_diff_v, cache_mla_ckv, cache_mla_kpe, c, c_ctx, w_ada, b_ada, g_attn, g_ffn, w_in, w_four, diff_lambda, g_subln, g_mla_q, w_mla_uq, g_mla_kv, w_mla_ukv, w_out, w_router, w_e_gate, w_e_up, w_e_down, g_final):
    depth, d = g_attn.shape
    b_ctx, n_ctx, _ = x_prompt.shape
    b_lat, n_lat, _ = x_sample.shape

    r = b_lat + 1
    r_pad = -(-r // 8) * 8
    cond = jnp.concatenate([c, c_ctx[None, :], jnp.zeros((r_pad - r, d), F32)], axis=0)
    mod = _modulation(cond, w_ada, b_ada)

    n_main = w_in.shape[2] - MLA_ROPE
    w_in_p = jnp.concatenate(
        [w_in[:, :, :n_main], jnp.zeros((depth, d, MLA_NOPE), F32), w_in[:, :, n_main:],
         jnp.zeros((depth, d, HEAD_W - MLA_NOPE - MLA_ROPE), F32)], axis=2).astype(BF)
    w_uq_p = jnp.stack([_pad_heads(w_mla_uq[l], MLA_HEADS, 0, MLA_NOPE + MLA_ROPE) for l in range(depth)]).astype(BF)
    w_k_p = jnp.stack([_pad_heads(w_mla_ukv[l], MLA_HEADS, 0, MLA_NOPE) for l in range(depth)]).astype(BF)
    w_v_p = w_mla_ukv.reshape(depth, MLA_KV_RANK, MLA_HEADS, MLA_NOPE + MLA_V)[..., MLA_NOPE:].reshape(depth, MLA_KV_RANK, D_MLA).astype(BF)
    w_out_p = w_out.astype(BF)
    w_router_t = jnp.swapaxes(w_router, 1, 2).astype(BF)
    place = jnp.zeros((MLA_ROPE, HEAD_W), F32).at[jnp.arange(MLA_ROPE), MLA_NOPE + jnp.arange(MLA_ROPE)].set(1.0)
    place = jnp.tile(place, (1, MLA_HEADS)).astype(BF)

    jc = jnp.arange(F_CH, dtype=jnp.int32)
    ang_c = ((jc[:, None] * jc[None, :]) % F_CH).astype(F32) * (2.0 * math.pi / F_CH)
    cc_bd = _block_diag(jnp.broadcast_to(jnp.cos(ang_c), (F_GROUPS, F_CH, F_CH))).astype(BF)
    sc_bd = _block_diag(jnp.broadcast_to(jnp.sin(ang_c), (F_GROUPS, F_CH, F_CH))).astype(BF)
    dft = {n: _dft_tables(n) for n in {n_ctx, n_lat}}
    ropes = _rope_tables(n_lat)

    def pre(latent, l, x, prev=None, combine=None):
        modv = (mod[l, :b_lat] if latent else mod[l, b_lat:b_lat + 1]).reshape(-1, 1, N_MOD * d)
        outs = _pre(latent, x, modv, g_attn[l], w_in_p[l], g_mla_q[l], w_uq_p[l], g_mla_kv[l], w_k_p[l], w_v_p[l],
                    cc_bd, sc_bd, ropes, l, depth, prev, combine)
        if combine is not None:
            x, outs = outs[0], outs[1:]
        return x, modv, outs

    def cache_prep(l):
        return _cache_prep(l, cache_diff_k, cache_diff_v, cache_mla_ckv, cache_mla_kpe, w_k_p[l], w_v_p[l], place)

    def mix_and_route(latent, l, x, modv, outs, cache=None):
        b, n, _ = x.shape
        cap = max(1, EC_CAPACITY_FACTOR * n // N_EXPERTS)
        u, v, qd, kd, vd, qm, km, vm = outs[:8]
        cn, sn = dft[n]
        yfour = _fourier(u, v, cn, sn, _block_diag(w_four[l]).astype(BF))
        x = _attention(latent, l, x, modv, yfour, qd, kd, vd, qm, km, vm, cache, diff_lambda[l], g_subln[l], w_out_p[l])
        h2, aff = _router(x, modv, g_ffn[l], w_router_t[l], packed=latent)
        sel = _select(aff, cap)
        e = sel.shape[1]
        if latent:
            xs, gate = _sc_dispatch(h2.reshape(b * n, h2.shape[2]), sel.reshape(b * e, n), aff.reshape(b * e, n), cap, e)
            gate = jnp.broadcast_to(gate.reshape(e, b * cap, 1), (e, b * cap, HEAD_W))
        else:
            xs, gate = _gather(sel, aff, h2, cap)
            gate = gate.reshape(e, b * cap, HEAD_W)
        return x, jnp.swapaxes(sel, 1, 2), cap, xs.reshape(e, b * cap, -1), gate

    x_l, mod_l, pre_l = pre(True, 0, x_sample)
    x_c, mod_c, pre_c = pre(False, 0, x_prompt)
    cache = cache_prep(0)
    for l in range(depth):
        new = tuple(pre_c[8:])
        x_l, selt_l, cap_l, xs_l, gate_l = mix_and_route(True, l, x_l, mod_l, pre_l, cache)
        x_c, selt_c, cap_c, xs_c, gate_c = mix_and_route(False, l, x_c, mod_c, pre_c)
        if l + 1 < depth:
            cache = cache_prep(l + 1)
        ye_c, ye_l = _experts(l, xs_c, gate_c, xs_l, gate_l, w_e_gate, w_e_up, w_e_down)
        ye_c = ye_c.reshape(-1, b_ctx, cap_c, d)
        ye_l = ye_l.reshape(-1, b_lat, cap_l, d)
        if l + 1 < depth:
            x_c, mod_c, pre_c = pre(False, l + 1, x_c, new, (mod_c, selt_c, ye_c, cap_c))
            x_l, mod_l, pre_l = pre(True, l + 1, x_l, None, (mod_l, selt_l, ye_l, cap_l))
        else:
            x_c = _scatter(x_c, mod_c, selt_c, ye_c, cap_c, g_final)
            x_l = _scatter(x_l, mod_l, selt_l, ye_l, cap_l, g_final)
    y_prompt, y_sample = x_c, x_l
    new_diff_k, new_diff_v, new_mla_ckv, new_mla_kpe = new
    return (y_prompt, y_sample, new_diff_k, new_diff_v, new_mla_ckv, new_mla_kpe)
```

```python
import functools
import math

import jax
import jax.numpy as jnp
import numpy as np
from jax import lax
from jax.experimental import pallas as pl
from jax.experimental.pallas import tpu as pltpu
from jax.experimental.pallas import tpu_sc as plsc

BF = jnp.bfloat16
F32 = jnp.float32

GRID_W = 64
ROPE_BASE = 10000.0
NORM_EPS = 1e-6
F_GROUPS, F_CH = 4, 64
D_FOURIER = F_GROUPS * F_CH
DIFF_HEADS, DIFF_DH = 4, 64
D_DIFF = DIFF_HEADS * 2 * DIFF_DH
MLA_HEADS, MLA_Q_RANK, MLA_KV_RANK = 4, 256, 128
MLA_NOPE, MLA_ROPE, MLA_V = 64, 32, 64
HEAD_W = 128
D_MLA_P = MLA_HEADS * HEAD_W
D_MLA = MLA_HEADS * MLA_V
D_MIX = D_FOURIER + D_DIFF + D_MLA
ATTN_ROWS = 512
KEY_CHUNK = 256
SCORE_BUFS = 2
BF_ROWS = 16
SELECT_ROWS = 128
EXPERT_ROWS = 1024
EXPERT_SUB_ROWS = 512
PRE_ROWS = 512
STREAM_ROWS = 1024
SC_CHUNK = 64
SUB_ROWS = 256
N_EXPERTS = 16
EC_CAPACITY_FACTOR = 2
N_MOD = 6
IN_COLS_P = D_FOURIER + 3 * D_DIFF + MLA_Q_RANK + MLA_KV_RANK + HEAD_W
LOG2E = 1.4426950408889634
VMEM_LIMIT = 56 * 1024 * 1024


def _cparams(n_axes, vmem=VMEM_LIMIT):
    return pltpu.CompilerParams(dimension_semantics=("arbitrary",) * n_axes, vmem_limit_bytes=vmem)


def _dot(a, b):
    return jnp.dot(a, b, preferred_element_type=F32)


def _dot_nt(a, b):
    return lax.dot_general(a, b, (((1,), (1,)), ((), ())), preferred_element_type=F32)


def _rms(x, g):
    return x * lax.rsqrt(jnp.mean(x * x, axis=-1, keepdims=True) + NORM_EPS) * g


def _pack_halves(hb):
    w = hb.shape[1] // 2
    bits = pltpu.bitcast(hb.astype(F32), jnp.uint32)
    packed = (bits[:, :w] >> 16) | (bits[:, w:] & jnp.uint32(0xFFFF0000))
    return pltpu.bitcast(packed, jnp.int32)


def _unpack_halves(xi):
    bits = pltpu.bitcast(xi, jnp.uint32)
    lo = pltpu.bitcast(bits << 16, F32).astype(BF)
    hi = pltpu.bitcast(bits & jnp.uint32(0xFFFF0000), F32).astype(BF)
    return jnp.concatenate([lo, hi], axis=1)


def _rope(z, cos, sin_signed, half, group, lo):
    w = z.shape[1]
    reps = w // cos.shape[1]
    cos_w = jnp.concatenate([cos] * reps, axis=1) if reps > 1 else cos
    sin_w = jnp.concatenate([sin_signed] * reps, axis=1) if reps > 1 else sin_signed
    from_right = pltpu.roll(z, w - half, 1)
    from_left = pltpu.roll(z, half, 1)
    lane = lax.broadcasted_iota(jnp.int32, z.shape, 1) % group
    first = (lane >= lo) & (lane < lo + half)
    partner = jnp.where(first, from_right, from_left)
    return z * cos_w + partner * sin_w


def _mod_kernel(c_ref, w_ref, b_ref, o_ref):
    c = c_ref[...]
    a = (c * (1.0 / (1.0 + jnp.exp(-c)))).astype(BF)
    o_ref[0] = _dot(a, w_ref[0].astype(BF)) + b_ref[0]


def _modulation(cond, w_ada, b_ada):
    depth, d, n6 = w_ada.shape
    r = cond.shape[0]
    tn = 1536
    return pl.pallas_call(
        _mod_kernel,
        name="modulation",
        grid=(depth, n6 // tn),
        in_specs=[
            pl.BlockSpec((r, d), lambda l, j: (0, 0)),
            pl.BlockSpec((1, d, tn), lambda l, j: (l, 0, j)),
            pl.BlockSpec((1, 1, tn), lambda l, j: (l, 0, j)),
        ],
        out_specs=pl.BlockSpec((1, r, tn), lambda l, j: (l, 0, j)),
        out_shape=jax.ShapeDtypeStruct((depth, r, n6), F32),
        compiler_params=_cparams(2),
    )(cond, w_ada, b_ada.reshape(depth, 1, n6))


def _pre_kernel(latent, d, *refs):
    if latent:
        (x_ref, mod_ref, g_ref, win_ref, gq_ref, wuq_ref, gkv_ref, wk_ref, wv_ref, cc_ref, sc_ref,
         cd_ref, sd_ref, cm_ref, sm_ref,
         u_ref, v_ref, qd_ref, kd_ref, vd_ref, qm_ref, km_ref, vm_ref) = refs
    else:
        (x_ref, mod_ref, g_ref, win_ref, gq_ref, wuq_ref, gkv_ref, wk_ref, wv_ref, cc_ref, sc_ref,
         u_ref, v_ref, qd_ref, kd_ref, vd_ref, qm_ref, km_ref, vm_ref,
         k32_ref, v32_ref, ckv32_ref, kpe32_ref) = refs[:11] + refs[-12:]
    sh = mod_ref[0, :, 0:d]
    sc = mod_ref[0, :, d:2 * d]
    tm = x_ref.shape[1]
    sub = SUB_ROWS if tm > SUB_ROWS else max(tm // 2, BF_ROWS)
    def project(r0):
        h = _rms(x_ref[0, r0:r0 + sub, :], g_ref[...]) * (1.0 + sc) + sh
        return _dot(h.astype(BF), win_ref[...])

    starts = list(range(0, tm, sub))
    z_next = project(starts[0])
    for i, r0 in enumerate(starts):
        rows = slice(r0, r0 + sub)
        z = z_next
        z_next = project(starts[i + 1]) if i + 1 < len(starts) else None
        o = 0
        zf = z[:, o:o + D_FOURIER]; o += D_FOURIER
        zq = z[:, o:o + D_DIFF]; o += D_DIFF
        zk = z[:, o:o + D_DIFF]; o += D_DIFF
        zv = z[:, o:o + D_DIFF]; o += D_DIFF
        zcq = z[:, o:o + MLA_Q_RANK]; o += MLA_Q_RANK
        zckv = z[:, o:o + MLA_KV_RANK]; o += MLA_KV_RANK
        kpe = z[:, o:o + HEAD_W]

        zf_b = zf.astype(BF)
        u_ref[0, rows, :] = _dot(zf_b, cc_ref[...]).astype(BF)
        v_ref[0, rows, :] = _dot(zf_b, sc_ref[...]).astype(BF)
        vd_ref[0, :, rows] = zv.T.astype(BF)
        cq = _rms(zcq, gq_ref[...])
        qm = _dot(cq.astype(BF), wuq_ref[...])
        ckv = _rms(zckv, gkv_ref[...])
        ckv_b = ckv.astype(BF)
        k_nope = _dot(ckv_b, wk_ref[...])
        vm_ref[0, :, rows] = _dot(ckv_b, wv_ref[...]).T.astype(BF)
        if latent:
            cd, sd, cm, sm = cd_ref[rows, :], sd_ref[rows, :], cm_ref[rows, :], sm_ref[rows, :]
            zq_r = _rope(zq, cd, sd, DIFF_DH // 2, DIFF_DH, 0)
            zk_r = _rope(zk, cd, sd, DIFF_DH // 2, DIFF_DH, 0)
            qm = _rope(qm, cm, sm, MLA_ROPE // 2, HEAD_W, MLA_NOPE)
            kpe_r = _rope(kpe, cm, sm, MLA_ROPE // 2, HEAD_W, MLA_NOPE)
        else:
            zq_r, zk_r, kpe_r = zq, zk, kpe
            for hd in range(DIFF_HEADS):
                k32_ref[0, 0, rows, hd, :] = zk[:, hd * HEAD_W:(hd + 1) * HEAD_W]
                v32_ref[0, 0, rows, hd, :] = zv[:, hd * HEAD_W:(hd + 1) * HEAD_W]
            ckv32_ref[0, 0, rows, :] = ckv
            kpe32_ref[0, 0, rows, :] = kpe[:, MLA_NOPE:MLA_NOPE + MLA_ROPE]
            if i == 0:
                for ref in (k32_ref, v32_ref, ckv32_ref, kpe32_ref):
                    if ref.shape[1] > 1:
                        ref[0, 1:] = jnp.zeros((ref.shape[1] - 1,) + ref.shape[2:], F32)
        qd_ref[0, rows, :] = (zq_r * (DIFF_DH ** -0.5 * LOG2E)).astype(BF)
        kd_ref[0, rows, :] = zk_r.astype(BF)
        qm_ref[0, rows, :] = (qm * ((MLA_NOPE + MLA_ROPE) ** -0.5 * LOG2E)).astype(BF)
        km_ref[0, rows, :] = (k_nope + jnp.concatenate([kpe_r] * MLA_HEADS, axis=1)).astype(BF)


def _combine_pre_kernel(latent, d, cap, n_pre_in, *refs):
    x_ref, modp_ref, selt_ref, ye_ref = refs[:4]
    pre_in = refs[4:4 + n_pre_in]
    xo_ref = refs[4 + n_pre_in]
    pre_out = refs[5 + n_pre_in:-1]
    xo_ref[0] = _combine_rows(cap, d, x_ref, modp_ref, selt_ref, ye_ref, refs[-1], slice(None))
    _pre_kernel(latent, d, xo_ref, *pre_in, *pre_out)


def _pre(latent, x, modv, g_attn, w_in_p, g_q, w_uq_p, g_kv, w_k_p, w_v_p, cc_bd, sc_bd, ropes, l=0, depth=1, prev=None,
         combine=None):
    b, n, d = x.shape
    tm = min(PRE_ROWS, n)
    bm = modv.shape[0]
    mod_idx = (lambda i, j: (i, 0, 0)) if bm > 1 else (lambda i, j: (0, 0, 0))
    const = lambda i, j: (0, 0)
    tok = lambda i, j: (i, j, 0)
    in_specs = [
        pl.BlockSpec((1, tm, d), tok),
        pl.BlockSpec((1, 1, modv.shape[2]), mod_idx),
        pl.BlockSpec((1, d), const),
        pl.BlockSpec(w_in_p.shape, const),
        pl.BlockSpec((1, MLA_Q_RANK), const),
        pl.BlockSpec(w_uq_p.shape, const),
        pl.BlockSpec((1, MLA_KV_RANK), const),
        pl.BlockSpec(w_k_p.shape, const),
        pl.BlockSpec(w_v_p.shape, const),
        pl.BlockSpec(cc_bd.shape, const),
        pl.BlockSpec(sc_bd.shape, const),
    ]
    args = [x, modv, g_attn.reshape(1, d), w_in_p, g_q.reshape(1, -1), w_uq_p, g_kv.reshape(1, -1), w_k_p, w_v_p, cc_bd, sc_bd]
    if latent:
        in_specs += [pl.BlockSpec((tm, HEAD_W), lambda i, j: (j, 0))] * 4
        args += list(ropes)
    feat = lambda i, j: (i, 0, j)
    out_shape, out_specs = [], []
    for w, token_major in ((D_FOURIER, True), (D_FOURIER, True), (D_DIFF, True), (D_DIFF, True), (D_DIFF, False),
                           (D_MLA_P, True), (D_MLA_P, True), (D_MLA, False)):
        out_shape.append(jax.ShapeDtypeStruct((b, n, w) if token_major else (b, w, n), BF))
        out_specs.append(pl.BlockSpec((1, tm, w), tok) if token_major else pl.BlockSpec((1, w, tm), feat))
    aliases = {}
    if not latent:
        span = depth if prev is None else 1
        for tail in ((DIFF_HEADS, HEAD_W), (DIFF_HEADS, HEAD_W), (MLA_KV_RANK,), (MLA_ROPE,)):
            out_shape.append(jax.ShapeDtypeStruct((b, depth, n) + tail, F32))
            out_specs.append(pl.BlockSpec((1, span, tm) + tail, lambda i, j, t=len(tail): (i, l, j) + (0,) * t))
        if prev is not None:
            for k, buf in enumerate(prev):
                aliases[len(args)] = len(out_shape) - len(prev) + k
                in_specs.append(pl.BlockSpec(memory_space=pl.ANY))
                args.append(buf)
    body, scratch, tag = functools.partial(_pre_kernel, latent, d), [], "pre"
    if combine is not None:
        modp, sel_t, ye, cap = combine
        e = ye.shape[0]
        modp_idx = (lambda i, j: (i, 0, 0)) if modp.shape[0] > 1 else (lambda i, j: (0, 0, 0))
        extra_specs = [pl.BlockSpec((1, 1, modp.shape[2]), modp_idx), pl.BlockSpec((1, tm, e), tok),
                       pl.BlockSpec((e, 1, cap, d), lambda i, j: (0, i, 0, 0))]
        n_extra = len(extra_specs)
        body = functools.partial(_combine_pre_kernel, latent, d, cap, len(args) - 1)
        in_specs = in_specs[:1] + extra_specs + in_specs[1:]
        args = args[:1] + [modp, sel_t, ye] + args[1:]
        out_specs = [pl.BlockSpec((1, tm, d), tok)] + out_specs
        out_shape = [jax.ShapeDtypeStruct((b, n, d), F32)] + out_shape
        aliases = {k + n_extra: v + 1 for k, v in aliases.items()}
        scratch, tag = [pltpu.VMEM((tm, e * cap), BF)], "combine_pre"
    return pl.pallas_call(
        body,
        name=tag + ("_lat" if latent else "_ctx"),
        grid=(b, n // tm),
        in_specs=in_specs,
        out_specs=out_specs,
        out_shape=out_shape,
        input_output_aliases=aliases,
        scratch_shapes=scratch,
        compiler_params=_cparams(2),
    )(*args)


def _cache_kernel(dk_ref, dv_ref, ckv_ref, kpe_ref, wk_ref, wv_ref, place_ref, kd_ref, vd_ref, km_ref, vm_ref):
    for hd in range(DIFF_HEADS):
        sl = slice(hd * HEAD_W, (hd + 1) * HEAD_W)
        kd_ref[0, :, sl] = dk_ref[0, 0, :, hd, :].astype(BF)
        vd_ref[0, sl, :] = dv_ref[0, 0, :, hd, :].T.astype(BF)
    ckv_b = ckv_ref[0, 0].astype(BF)
    kpe_w = _dot(kpe_ref[0, 0].astype(BF), place_ref[...])
    km_ref[0] = (_dot(ckv_b, wk_ref[...]) + kpe_w).astype(BF)
    vm_ref[0] = _dot(ckv_b, wv_ref[...]).T.astype(BF)


def _cache_prep(l, cache_dk, cache_dv, cache_ckv, cache_kpe, w_k_p, w_v_p, place):
    b, _, p = cache_dk.shape[:3]
    at_l = lambda i: (i, l, 0, 0)
    at_l5 = lambda i: (i, l, 0, 0, 0)
    const = lambda i: (0, 0)
    shapes = [(p, D_DIFF), (D_DIFF, p), (p, D_MLA_P), (D_MLA, p)]
    return pl.pallas_call(
        _cache_kernel,
        name="cache_prep",
        grid=(b,),
        in_specs=[
            pl.BlockSpec((1, 1, p, DIFF_HEADS, HEAD_W), at_l5),
            pl.BlockSpec((1, 1, p, DIFF_HEADS, HEAD_W), at_l5),
            pl.BlockSpec((1, 1, p, MLA_KV_RANK), at_l),
            pl.BlockSpec((1, 1, p, MLA_ROPE), at_l),
            pl.BlockSpec(w_k_p.shape, const),
            pl.BlockSpec(w_v_p.shape, const),
            pl.BlockSpec(place.shape, const),
        ],
        out_specs=[pl.BlockSpec((1,) + s, lambda i: (i, 0, 0)) for s in shapes],
        out_shape=[jax.ShapeDtypeStruct((b,) + s, BF) for s in shapes],
        compiler_params=_cparams(1),
    )(cache_dk, cache_dv, cache_ckv, cache_kpe, w_k_p, w_v_p, place)


def _requests_per_step(b, n, tm):
    bb = 1
    if n == tm:
        while bb * 2 * n <= STREAM_ROWS and b % (bb * 2) == 0:
            bb *= 2
    return bb


def _fourier_kernel(scale, u_ref, v_ref, cn_ref, sn_ref, w_ref, o_ref):
    for bi in range(u_ref.shape[0]):
        f = (_dot(cn_ref[...], u_ref[bi]) - _dot(sn_ref[...], v_ref[bi])) * scale
        o_ref[bi] = _dot(f.astype(BF), w_ref[...]).astype(BF)


def _fourier(u, v, cn, sn, w_bd):
    b, n, w = u.shape
    tm = min(STREAM_ROWS, n)
    bb = _requests_per_step(b, n, tm)
    return pl.pallas_call(
        functools.partial(_fourier_kernel, 1.0 / math.sqrt(n * F_CH)),
        name="fourier_%d" % n,
        grid=(n // tm, b // bb),
        in_specs=[
            pl.BlockSpec((bb, n, w), lambda i, j: (j, 0, 0)),
            pl.BlockSpec((bb, n, w), lambda i, j: (j, 0, 0)),
            pl.BlockSpec((tm, n), lambda i, j: (i, 0)),
            pl.BlockSpec((tm, n), lambda i, j: (i, 0)),
            pl.BlockSpec((w, w), lambda i, j: (0, 0)),
        ],
        out_specs=pl.BlockSpec((bb, tm, w), lambda i, j: (j, i, 0)),
        out_shape=jax.ShapeDtypeStruct((b, n, w), BF),
        compiler_params=_cparams(2),
    )(u, v, cn, sn, w_bd)


def _scores(q, k_parts, s_ref):
    off, m, bounds = 0, None, []
    for k in k_parts:
        s = _dot_nt(k(), q)
        n_i = s.shape[0]
        s_ref[off:off + n_i, :] = s
        mi = jnp.max(s, axis=0, keepdims=True)
        m = mi if m is None else jnp.maximum(m, mi)
        bounds.append((off, n_i))
        off += n_i
    return m, bounds


def _weighted(m, vt_parts, bounds, s_ref):
    o = None
    for vt, (r, n_i) in zip(vt_parts, bounds):
        v = vt()
        ones = jnp.where(lax.broadcasted_iota(jnp.int32, (BF_ROWS, n_i), 0) == 0, 1.0, 0.0).astype(BF)
        v_ext = jnp.concatenate([v, ones], axis=0)
        for c in range(0, n_i, KEY_CHUNK):
            rows = min(KEY_CHUNK, n_i - c)
            p = jnp.exp2(s_ref[r + c:r + c + rows, :] - m).astype(BF)
            oi = _dot(v_ext[:, c:c + rows], p)
            o = oi if o is None else o + oi
    dv = o.shape[0] - BF_ROWS
    return o[:dv] * (1.0 / o[dv:dv + 1])


def _attn_kernel(latent, lam_init, d, *refs):
    if latent:
        (x_ref, mod_ref, yf_ref, qd_ref, kd_ref, vd_ref, kdc_ref, vdc_ref, qm_ref, km_ref, vm_ref, kmc_ref, vmc_ref,
         lam_ref, gs_ref, wo_ref, o_ref, mix_ref, s_ref) = refs
    else:
        (x_ref, mod_ref, yf_ref, qd_ref, kd_ref, vd_ref, qm_ref, km_ref, vm_ref,
         lam_ref, gs_ref, wo_ref, o_ref, mix_ref, s_ref) = refs
        kdc_ref = vdc_ref = kmc_ref = vmc_ref = None
    lf = lam_ref[...]
    lam = (jnp.exp(jnp.sum(lf[0:1] * lf[1:2], axis=-1, keepdims=True))
           - jnp.exp(jnp.sum(lf[2:3] * lf[3:4], axis=-1, keepdims=True)) + lam_init)
    mix_ref[:, 0:D_FOURIER] = yf_ref[0]
    gs = gs_ref[...]

    def keys(ref, cache_ref, sl):
        parts = [lambda: ref[0, :, sl]]
        if latent:
            parts.append(lambda: cache_ref[0, :, sl])
        return parts

    def values_t(ref, cache_ref, sl):
        parts = [lambda: ref[0, sl, :]]
        if latent:
            parts.append(lambda: cache_ref[0, sl, :])
        return parts

    def diff_query(sl, comp):
        def load():
            qh = qd_ref[0, :, sl]
            lane = lax.broadcasted_iota(jnp.int32, qh.shape, 1)
            keep = (lane >= DIFF_DH) if comp else (lane < DIFF_DH)
            return jnp.where(keep, qh, jnp.zeros_like(qh))
        return load

    items = []
    for hd in range(DIFF_HEADS):
        sl = slice(hd * HEAD_W, (hd + 1) * HEAD_W)
        for comp in range(2):
            items.append((diff_query(sl, comp), keys(kd_ref, kdc_ref, sl), values_t(vd_ref, vdc_ref, sl)))
    for hd in range(MLA_HEADS):
        sl = slice(hd * HEAD_W, (hd + 1) * HEAD_W)
        vsl = slice(hd * MLA_V, (hd + 1) * MLA_V)
        items.append(((lambda sl=sl: qm_ref[0, :, sl]), keys(km_ref, kmc_ref, sl), values_t(vm_ref, vmc_ref, vsl)))

    def start(i):
        return _scores(items[i][0](), items[i][1], s_ref.at[i % SCORE_BUFS])

    outs = []
    state = start(0)
    for i in range(len(items)):
        nxt = start(i + 1) if i + 1 < len(items) else None
        m, bounds = state
        buf = i % SCORE_BUFS
        outs.append(_weighted(m, items[i][2], bounds, s_ref.at[buf]))
        state = nxt
        if i < 2 * DIFF_HEADS and i % 2 == 1:
            hd = i // 2
            o = outs[i - 1] - lam * outs[i]
            o = o * lax.rsqrt(jnp.mean(o * o, axis=0, keepdims=True) + NORM_EPS) * gs * (1.0 - lam_init)
            mix_ref[:, D_FOURIER + hd * HEAD_W:D_FOURIER + (hd + 1) * HEAD_W] = o.T.astype(BF)

    mix_ref[:, D_FOURIER + D_DIFF:D_MIX] = jnp.concatenate(outs[2 * DIFF_HEADS:], axis=0).T.astype(BF)
    y = _dot(mix_ref[...], wo_ref[...])
    gt = mod_ref[0, :, 2 * d:3 * d]
    o_ref[0] = x_ref[0] + gt * y


def _attention(latent, l, x, modv, yfour, qd, kd, vd, qm, km, vm, cache, diff_lambda_l, g_subln_l, w_out_b):
    b, n, d = x.shape
    tq = min(ATTN_ROWS, n)
    bm = modv.shape[0]
    mod_idx = (lambda i, j: (i, 0, 0)) if bm > 1 else (lambda i, j: (0, 0, 0))
    const = lambda i, j: (0, 0)
    tok = lambda i, j: (i, j, 0)
    full = lambda i, j: (i, 0, 0)
    p = cache[0].shape[1] if latent else 0
    in_specs = [
        pl.BlockSpec((1, tq, d), tok),
        pl.BlockSpec((1, 1, modv.shape[2]), mod_idx),
        pl.BlockSpec((1, tq, D_FOURIER), tok),
        pl.BlockSpec((1, tq, D_DIFF), tok),
        pl.BlockSpec((1, n, D_DIFF), full),
        pl.BlockSpec((1, D_DIFF, n), full),
    ]
    args = [x, modv, yfour, qd, kd, vd]
    if latent:
        kdc, vdc, kmc, vmc = cache
        in_specs += [pl.BlockSpec((1, p, D_DIFF), full), pl.BlockSpec((1, D_DIFF, p), full)]
        args += [kdc, vdc]
    in_specs += [pl.BlockSpec((1, tq, D_MLA_P), tok), pl.BlockSpec((1, n, D_MLA_P), full), pl.BlockSpec((1, D_MLA, n), full)]
    args += [qm, km, vm]
    if latent:
        in_specs += [pl.BlockSpec((1, p, D_MLA_P), full), pl.BlockSpec((1, D_MLA, p), full)]
        args += [kmc, vmc]
    in_specs += [pl.BlockSpec((4, DIFF_DH), const), pl.BlockSpec((HEAD_W, 1), const), pl.BlockSpec(w_out_b.shape, const)]
    args += [diff_lambda_l, g_subln_l.reshape(HEAD_W, 1), w_out_b]
    lam_init = 0.8 - 0.6 * math.exp(-0.3 * l)
    return pl.pallas_call(
        functools.partial(_attn_kernel, latent, lam_init, d),
        name="attn_lat" if latent else "attn_ctx",
        grid=(b, n // tq),
        in_specs=in_specs,
        out_specs=pl.BlockSpec((1, tq, d), tok),
        out_shape=jax.ShapeDtypeStruct((b, n, d), F32),
        scratch_shapes=[pltpu.VMEM((tq, D_MIX), BF), pltpu.VMEM((SCORE_BUFS, n + p, tq), F32)],
        compiler_params=_cparams(2),
    )(*args)


def _router_kernel(d, packed, x_ref, mod_ref, g_ref, wr_ref, h_ref, aff_ref):
    sh = mod_ref[0, :, 3 * d:4 * d]
    sc = mod_ref[0, :, 4 * d:5 * d]
    tm = x_ref.shape[1]
    sub = SUB_ROWS if tm > SUB_ROWS else max(tm // 2, BF_ROWS)
    for bi in range(x_ref.shape[0]):
        for r0 in range(0, tm, sub):
            rows = slice(r0, r0 + sub)
            hb = (_rms(x_ref[bi, rows, :], g_ref[...]) * (1.0 + sc) + sh).astype(BF)
            h_ref[bi, rows, :] = _pack_halves(hb) if packed else hb
            logits = _dot_nt(wr_ref[...], hb)
            e = jnp.exp(logits - jnp.max(logits, axis=0, keepdims=True))
            aff_ref[bi, :, rows] = e / jnp.sum(e, axis=0, keepdims=True)


def _router(x, modv, g_ffn_l, w_router_t, packed):
    b, n, d = x.shape
    hw, hdt = (d // 2, jnp.int32) if packed else (d, BF)
    tm = min(STREAM_ROWS, n)
    bm = modv.shape[0]
    bb = _requests_per_step(b, n, tm) if bm == 1 else 1
    mod_idx = (lambda i, j: (i, 0, 0)) if bm > 1 else (lambda i, j: (0, 0, 0))
    const = lambda i, j: (0, 0)
    return pl.pallas_call(
        functools.partial(_router_kernel, d, packed),
        name="router_%d" % n,
        grid=(b // bb, n // tm),
        in_specs=[
            pl.BlockSpec((bb, tm, d), lambda i, j: (i, j, 0)),
            pl.BlockSpec((1, 1, modv.shape[2]), mod_idx),
            pl.BlockSpec((1, d), const),
            pl.BlockSpec(w_router_t.shape, const),
        ],
        out_specs=[pl.BlockSpec((bb, tm, hw), lambda i, j: (i, j, 0)), pl.BlockSpec((bb, N_EXPERTS, tm), lambda i, j: (i, 0, j))],
        out_shape=[jax.ShapeDtypeStruct((b, n, hw), hdt), jax.ShapeDtypeStruct((b, N_EXPERTS, n), F32)],
        compiler_params=_cparams(2),
    )(x, modv, g_ffn_l.reshape(1, d), w_router_t)


def _select_kernel(cap, aff_ref, sel_ref):
    a = aff_ref[...]
    e, n = a.shape
    bits = pltpu.bitcast(a, jnp.int32)
    capf = float(cap)

    def body(_, carry):
        lo, hi = carry
        mid = lo + ((hi - lo) >> 1)
        cnt = jnp.sum(jnp.where(bits >= mid, 1.0, 0.0), axis=1, keepdims=True)
        up = cnt >= capf
        return jnp.where(up, mid, lo), jnp.where(up, hi, mid)

    lo0 = jnp.zeros((e, 1), jnp.int32)
    hi0 = jnp.full((e, 1), 0x7F800000, jnp.int32)
    thr, _ = lax.fori_loop(0, 31, body, (lo0, hi0))
    gt = bits > thr
    eq = bits == thr
    need = capf - jnp.sum(jnp.where(gt, 1.0, 0.0), axis=1, keepdims=True)
    both = jnp.concatenate([jnp.where(gt, 1.0, 0.0), jnp.where(eq, 1.0, 0.0)], axis=0).astype(BF)
    ck = min(256, n)
    tri = jnp.where(lax.broadcasted_iota(jnp.int32, (ck, ck), 0) < lax.broadcasted_iota(jnp.int32, (ck, ck), 1), 1.0, 0.0).astype(BF)
    off = jnp.zeros((2 * e, 1), F32)
    pieces = []
    for k in range(n // ck):
        blk = both[:, k * ck:(k + 1) * ck]
        pieces.append(_dot(blk, tri) + off)
        off = off + jnp.sum(blk.astype(F32), axis=1, keepdims=True)
    cum = jnp.concatenate(pieces, axis=1) if len(pieces) > 1 else pieces[0]
    cum_gt, cum_eq = cum[:e], cum[e:]
    chosen = gt | (eq & (cum_eq < need))
    pos = cum_gt + jnp.minimum(cum_eq, need)
    sel_ref[...] = jnp.where(chosen, pos, -1.0).astype(jnp.int32)


def _select(aff, cap):
    b, e, n = aff.shape
    rows = b * e
    rb = min(rows, SELECT_ROWS)
    sel = pl.pallas_call(
        functools.partial(_select_kernel, cap),
        name="select_%d" % n,
        grid=(rows // rb,),
        in_specs=[pl.BlockSpec((rb, n), lambda i: (i, 0))],
        out_specs=pl.BlockSpec((rb, n), lambda i: (i, 0)),
        out_shape=jax.ShapeDtypeStruct((rows, n), jnp.int32),
        compiler_params=_cparams(1),
    )(aff.reshape(rows, n))
    return sel.reshape(b, e, n)


def _gather_kernel(cap, sel_ref, aff_ref, h_ref, xs_ref, gate_ref):
    h = h_ref[0]
    n = h.shape[0]
    e_tot = sel_ref.shape[1]

    def one(e_idx):
        sel_e = sel_ref[0, pl.ds(e_idx, 1), :]
        aff_e = aff_ref[0, pl.ds(e_idx, 1), :]
        hit = sel_e == lax.broadcasted_iota(jnp.int32, (cap, n), 0)
        g = jnp.sum(jnp.where(hit, aff_e, 0.0), axis=1, keepdims=True)
        return jnp.where(hit, 1.0, 0.0).astype(BF), jnp.broadcast_to(g, (cap, HEAD_W))

    if cap >= 128:
        def body(e_idx, carry):
            p, g = one(e_idx)
            xs_ref[e_idx, 0] = _dot(p, h).astype(BF)
            gate_ref[e_idx, 0] = g
            return carry

        lax.fori_loop(0, e_tot, body, 0)
    else:
        ps, gs = zip(*[one(e_idx) for e_idx in range(e_tot)])
        xs = _dot(jnp.concatenate(ps, axis=0), h).astype(BF)
        for e_idx in range(e_tot):
            xs_ref[e_idx, 0] = xs[e_idx * cap:(e_idx + 1) * cap]
            gate_ref[e_idx, 0] = gs[e_idx]


def _gather(sel, aff, h2, cap):
    b, e, n = sel.shape
    d = h2.shape[2]
    return pl.pallas_call(
        functools.partial(_gather_kernel, cap),
        name="gather_%d" % n,
        grid=(b,),
        in_specs=[
            pl.BlockSpec((1, e, n), lambda i: (i, 0, 0)),
            pl.BlockSpec((1, e, n), lambda i: (i, 0, 0)),
            pl.BlockSpec((1, n, d), lambda i: (i, 0, 0)),
        ],
        out_specs=[pl.BlockSpec((e, 1, cap, d), lambda i: (0, i, 0, 0)), pl.BlockSpec((e, 1, cap, HEAD_W), lambda i: (0, i, 0, 0))],
        out_shape=[jax.ShapeDtypeStruct((e, b, cap, d), BF), jax.ShapeDtypeStruct((e, b, cap, HEAD_W), F32)],
        compiler_params=_cparams(1),
    )(sel, aff, h2)


def _sc_dispatch(table, sel, aff, cap, n_experts):
    info = plsc.get_sparse_core_info()
    n_workers = info.num_cores * info.num_subcores
    lanes = info.num_lanes
    pairs, n = sel.shape
    b_tot = pairs // n_experts
    w = table.shape[1]
    per_w = pairs // n_workers
    ch = min(SC_CHUNK, cap)
    n_ch = cap // ch
    assert pairs % n_workers == 0 and cap % ch == 0 and ch % 8 == 0 and n % lanes == 0
    mesh = plsc.VectorSubcoreMesh(core_axis_name="c", subcore_axis_name="s")

    @functools.partial(
        pl.kernel, mesh=mesh, name="sc_dispatch",
        compiler_params=pltpu.CompilerParams(needs_layout_passes=False),
        out_type=[jax.ShapeDtypeStruct((pairs * cap, w), jnp.int32), jax.ShapeDtypeStruct((pairs * cap,), F32)],
        scratch_types=[pltpu.VMEM((n,), jnp.int32), pltpu.VMEM((n,), F32), pltpu.VMEM((n_ch, ch), jnp.int32),
                       pltpu.VMEM((cap,), F32), pltpu.VMEM((2, ch, w), jnp.int32), pltpu.SemaphoreType.DMA((2,))],
    )
    def k(table_hbm, sel_hbm, aff_hbm, xs_hbm, gate_hbm, sel_v, aff_v, tok_v, gate_v, rows_v, sem):
        wid = lax.axis_index("s") * info.num_cores + lax.axis_index("c")

        @pl.loop(0, per_w)
        def _(i):
            p = wid * per_w + i
            b = p // n_experts
            e = p - b * n_experts
            out_off = pl.multiple_of((e * b_tot + b) * cap, 8)
            pltpu.sync_copy(sel_hbm.at[p], sel_v)
            pltpu.sync_copy(aff_hbm.at[p], aff_v)
            lane = lax.iota(jnp.int32, lanes)

            @plsc.parallel_loop(0, n // lanes, unroll=4)
            def _(j):
                t0 = pl.multiple_of(j * lanes, lanes)
                s = sel_v[pl.ds(t0, lanes)]
                chosen = s >= 0
                slot = jnp.maximum(s, 0)
                plsc.store_scatter(tok_v, [slot // ch, slot % ch], lane + (t0 + b * n), mask=chosen)
                plsc.store_scatter(gate_v, [slot], aff_v[pl.ds(t0, lanes)], mask=chosen)

            prev = None
            for c in range(n_ch + 1):
                cur = None
                if c < n_ch:
                    cur = pltpu.async_copy(table_hbm.at[tok_v.at[c]], rows_v.at[c % 2], sem.at[c % 2])
                if prev is not None:
                    prev.wait()
                    pltpu.sync_copy(rows_v.at[(c - 1) % 2], xs_hbm.at[pl.ds(out_off + (c - 1) * ch, ch)])
                prev = cur
            pltpu.sync_copy(gate_v, gate_hbm.at[pl.ds(out_off, cap)])

    return k(table, sel, aff)


def _expert_kernel(steps_a, xa_ref, ga_ref, xb_ref, gb_ref, wg_ref, wu_ref, wd_ref, ya_ref, yb_ref, wg_s, wu_s, wd_s):
    j = pl.program_id(1)

    @pl.when(j == 0)
    def _():
        wg_s[...] = wg_ref[0, 0].astype(BF)
        wu_s[...] = wu_ref[0, 0].astype(BF)
        wd_s[...] = wd_ref[0, 0].astype(BF)

    def ffn(x_ref, g_ref, y_ref):
        rb = x_ref.shape[1]
        sub = min(rb, EXPERT_SUB_ROWS)
        for r0 in range(0, rb, sub):
            rows = slice(r0, r0 + sub)
            x = _unpack_halves(x_ref[0, rows, :]) if x_ref.dtype == jnp.int32 else x_ref[0, rows, :]
            a = _dot(x, wg_s[...])
            u = _dot(x, wu_s[...])
            mid = (a * (1.0 / (1.0 + jnp.exp(-a))) * u).astype(BF)
            y_ref[0, rows, :] = (_dot(mid, wd_s[...]) * g_ref[0, rows, 0:1]).astype(BF)

    @pl.when(j < steps_a)
    def _():
        ffn(xa_ref, ga_ref, ya_ref)

    @pl.when(j >= steps_a)
    def _():
        ffn(xb_ref, gb_ref, yb_ref)


def _experts(l, xs_a, gate_a, xs_b, gate_b, w_e_gate, w_e_up, w_e_down):
    e, rows_a, wa = xs_a.shape
    rows_b, wb = xs_b.shape[1:]
    d, ff = w_e_gate.shape[2:]
    rb_a, rb_b = math.gcd(rows_a, EXPERT_ROWS), math.gcd(rows_b, EXPERT_ROWS)
    steps_a, steps_b = rows_a // rb_a, rows_b // rb_b
    w_idx = lambda i, j: (l, i, 0, 0)
    idx_a = lambda i, j: (i, jnp.minimum(j, steps_a - 1), 0)
    idx_b = lambda i, j: (i, jnp.maximum(j - steps_a, 0), 0)
    return pl.pallas_call(
        functools.partial(_expert_kernel, steps_a),
        name="experts",
        grid=(e, steps_a + steps_b),
        in_specs=[
            pl.BlockSpec((1, rb_a, wa), idx_a),
            pl.BlockSpec((1, rb_a, HEAD_W), idx_a),
            pl.BlockSpec((1, rb_b, wb), idx_b),
            pl.BlockSpec((1, rb_b, HEAD_W), idx_b),
            pl.BlockSpec((1, 1, d, ff), w_idx),
            pl.BlockSpec((1, 1, d, ff), w_idx),
            pl.BlockSpec((1, 1, ff, d), w_idx),
        ],
        out_specs=[pl.BlockSpec((1, rb_a, d), idx_a), pl.BlockSpec((1, rb_b, d), idx_b)],
        out_shape=[jax.ShapeDtypeStruct((e, rows_a, d), BF), jax.ShapeDtypeStruct((e, rows_b, d), BF)],
        scratch_shapes=[pltpu.VMEM((d, ff), BF), pltpu.VMEM((d, ff), BF), pltpu.VMEM((ff, d), BF)],
        compiler_params=_cparams(2),
    )(xs_a, gate_a, xs_b, gate_b, w_e_gate, w_e_up, w_e_down)


def _combine_rows(cap, d, x_ref, mod_ref, selt_ref, ye_ref, pt_ref, rows):
    e_tot = ye_ref.shape[0]
    selt = selt_ref[0, rows, :]
    r = selt.shape[0]
    if cap % 128 == 0:
        lane = lax.broadcasted_iota(jnp.int32, (r, cap), 1)
        for e_idx in range(e_tot):
            pt_ref[rows, e_idx * cap:(e_idx + 1) * cap] = jnp.where(selt[:, e_idx:e_idx + 1] == lane, 1.0, 0.0).astype(BF)
    else:
        assert cap & (cap - 1) == 0
        shift = cap.bit_length() - 1
        col = lax.broadcasted_iota(jnp.int32, (e_tot, e_tot * cap), 1)
        spread = jnp.where((col >> shift) == lax.broadcasted_iota(jnp.int32, col.shape, 0), 1.0, 0.0).astype(BF)
        slot_of_col = _dot(selt.astype(F32).astype(BF), spread)
        want = (lax.broadcasted_iota(jnp.int32, (r, e_tot * cap), 1) & (cap - 1)).astype(F32)
        pt_ref[rows, :] = jnp.where(slot_of_col == want, 1.0, 0.0).astype(BF)
    y = _dot(pt_ref[rows, :], ye_ref[...].reshape(e_tot * cap, d))
    return x_ref[0, rows, :] + mod_ref[0, :, 5 * d:6 * d] * y


def _scatter_kernel(cap, d, x_ref, mod_ref, selt_ref, ye_ref, gf_ref, o_ref, pt_ref):
    o_ref[0] = _rms(_combine_rows(cap, d, x_ref, mod_ref, selt_ref, ye_ref, pt_ref, slice(None)), gf_ref[...])


def _scatter(x, modv, sel_t, ye, cap, g_final):
    b, n, d = x.shape
    e = ye.shape[0]
    tn = min(PRE_ROWS, n)
    bm = modv.shape[0]
    mod_idx = (lambda i, j: (i, 0, 0)) if bm > 1 else (lambda i, j: (0, 0, 0))
    return pl.pallas_call(
        functools.partial(_scatter_kernel, cap, d),
        name="scatter_%d" % n,
        grid=(b, n // tn),
        in_specs=[
            pl.BlockSpec((1, tn, d), lambda i, j: (i, j, 0)),
            pl.BlockSpec((1, 1, modv.shape[2]), mod_idx),
            pl.BlockSpec((1, tn, e), lambda i, j: (i, j, 0)),
            pl.BlockSpec((e, 1, cap, d), lambda i, j: (0, i, 0, 0)),
            pl.BlockSpec((1, d), lambda i, j: (0, 0)),
        ],
        out_specs=pl.BlockSpec((1, tn, d), lambda i, j: (i, j, 0)),
        out_shape=jax.ShapeDtypeStruct((b, n, d), F32),
        scratch_shapes=[pltpu.VMEM((tn, e * cap), BF)],
        compiler_params=_cparams(2),
    )(x, modv, sel_t, ye, g_final.reshape(1, d))


def _dft_tables(n):
    j = np.arange(n, dtype=np.int64)
    ang = ((j[:, None] * j[None, :]) % n).astype(np.float64) * (2.0 * math.pi / n)
    return jnp.asarray(np.cos(ang), dtype=BF), jnp.asarray(np.sin(ang), dtype=BF)


def _block_diag(blocks):
    g, r, c = blocks.shape
    out = jnp.zeros((g * r, g * c), blocks.dtype)
    for i in range(g):
        out = out.at[i * r:(i + 1) * r, i * c:(i + 1) * c].set(blocks[i])
    return out


def _rope_tables(n):
    rows = n // GRID_W
    row = jnp.repeat(jnp.arange(rows, dtype=F32), GRID_W)
    col = jnp.tile(jnp.arange(GRID_W, dtype=F32), rows)

    def ang(dim):
        nf = dim // 4
        inv = ROPE_BASE ** (-jnp.arange(nf, dtype=F32) / nf)
        return jnp.concatenate([row[:, None] * inv, col[:, None] * inv], axis=-1)

    a = ang(DIFF_DH)
    cos_d = jnp.tile(jnp.cos(a), (1, 4))
    sin_d = jnp.tile(jnp.concatenate([-jnp.sin(a), jnp.sin(a)], axis=1), (1, 2))
    a = ang(MLA_ROPE)
    ones = jnp.ones((n, MLA_NOPE), F32)
    pad = HEAD_W - MLA_NOPE - MLA_ROPE
    cos_m = jnp.concatenate([ones, jnp.cos(a), jnp.cos(a), jnp.ones((n, pad), F32)], axis=1)
    sin_m = jnp.concatenate([0 * ones, -jnp.sin(a), jnp.sin(a), jnp.zeros((n, pad), F32)], axis=1)
    return cos_d, sin_d, cos_m, sin_m


def _pad_heads(w, heads, lo, hi):
    k = w.shape[0]
    w3 = w.reshape(k, heads, -1)[:, :, lo:hi]
    return jnp.pad(w3, ((0, 0), (0, 0), (0, HEAD_W - (hi - lo)))).reshape(k, heads * HEAD_W)


def kernel(x_prompt, x_sample, cache_diff_k, cache_diff_v, cache_mla_ckv, cache_mla_kpe, c, c_ctx, w_ada, b_ada, g_attn, g_ffn, w_in, w_four, diff_lambda, g_subln, g_mla_q, w_mla_uq, g_mla_kv, w_mla_ukv, w_out, w_router, w_e_gate, w_e_up, w_e_down, g_final):
    depth, d = g_attn.shape
    b_ctx, n_ctx, _ = x_prompt.shape
    b_lat, n_lat, _ = x_sample.shape

    r = b_lat + 1
    r_pad = -(-r // 8) * 8
    cond = jnp.concatenate([c, c_ctx[None, :], jnp.zeros((r_pad - r, d), F32)], axis=0)
    mod = _modulation(cond, w_ada, b_ada)

    n_main = w_in.shape[2] - MLA_ROPE
    w_in_p = jnp.concatenate(
        [w_in[:, :, :n_main], jnp.zeros((depth, d, MLA_NOPE), F32), w_in[:, :, n_main:],
         jnp.zeros((depth, d, HEAD_W - MLA_NOPE - MLA_ROPE), F32)], axis=2).astype(BF)
    w_uq_p = jnp.stack([_pad_heads(w_mla_uq[l], MLA_HEADS, 0, MLA_NOPE + MLA_ROPE) for l in range(depth)]).astype(BF)
    w_k_p = jnp.stack([_pad_heads(w_mla_ukv[l], MLA_HEADS, 0, MLA_NOPE) for l in range(depth)]).astype(BF)
    w_v_p = w_mla_ukv.reshape(depth, MLA_KV_RANK, MLA_HEADS, MLA_NOPE + MLA_V)[..., MLA_NOPE:].reshape(depth, MLA_KV_RANK, D_MLA).astype(BF)
    w_out_p = w_out.astype(BF)
    w_router_t = jnp.swapaxes(w_router, 1, 2).astype(BF)
    place = jnp.zeros((MLA_ROPE, HEAD_W), F32).at[jnp.arange(MLA_ROPE), MLA_NOPE + jnp.arange(MLA_ROPE)].set(1.0)
    place = jnp.tile(place, (1, MLA_HEADS)).astype(BF)

    jc = jnp.arange(F_CH, dtype=jnp.int32)
    ang_c = ((jc[:, None] * jc[None, :]) % F_CH).astype(F32) * (2.0 * math.pi / F_CH)
    cc_bd = _block_diag(jnp.broadcast_to(jnp.cos(ang_c), (F_GROUPS, F_CH, F_CH))).astype(BF)
    sc_bd = _block_diag(jnp.broadcast_to(jnp.sin(ang_c), (F_GROUPS, F_CH, F_CH))).astype(BF)
    dft = {n: _dft_tables(n) for n in {n_ctx, n_lat}}
    ropes = _rope_tables(n_lat)

    def pre(latent, l, x, prev=None, combine=None):
        modv = (mod[l, :b_lat] if latent else mod[l, b_lat:b_lat + 1]).reshape(-1, 1, N_MOD * d)
        outs = _pre(latent, x, modv, g_attn[l], w_in_p[l], g_mla_q[l], w_uq_p[l], g_mla_kv[l], w_k_p[l], w_v_p[l],
                    cc_bd, sc_bd, ropes, l, depth, prev, combine)
        if combine is not None:
            x, outs = outs[0], outs[1:]
        return x, modv, outs

    def cache_prep(l):
        return _cache_prep(l, cache_diff_k, cache_diff_v, cache_mla_ckv, cache_mla_kpe, w_k_p[l], w_v_p[l], place)

    def mix_and_route(latent, l, x, modv, outs, cache=None):
        b, n, _ = x.shape
        cap = max(1, EC_CAPACITY_FACTOR * n // N_EXPERTS)
        u, v, qd, kd, vd, qm, km, vm = outs[:8]
        cn, sn = dft[n]
        yfour = _fourier(u, v, cn, sn, _block_diag(w_four[l]).astype(BF))
        x = _attention(latent, l, x, modv, yfour, qd, kd, vd, qm, km, vm, cache, diff_lambda[l], g_subln[l], w_out_p[l])
        h2, aff = _router(x, modv, g_ffn[l], w_router_t[l], packed=latent)
        sel = _select(aff, cap)
        e = sel.shape[1]
        if latent:
            xs, gate = _sc_dispatch(h2.reshape(b * n, h2.shape[2]), sel.reshape(b * e, n), aff.reshape(b * e, n), cap, e)
            gate = jnp.broadcast_to(gate.reshape(e, b * cap, 1), (e, b * cap, HEAD_W))
        else:
            xs, gate = _gather(sel, aff, h2, cap)
            gate = gate.reshape(e, b * cap, HEAD_W)
        return x, jnp.swapaxes(sel, 1, 2), cap, xs.reshape(e, b * cap, -1), gate

    x_l, mod_l, pre_l = pre(True, 0, x_sample)
    x_c, mod_c, pre_c = pre(False, 0, x_prompt)
    cache = cache_prep(0)
    for l in range(depth):
        new = tuple(pre_c[8:])
        x_l, selt_l, cap_l, xs_l, gate_l = mix_and_route(True, l, x_l, mod_l, pre_l, cache)
        x_c, selt_c, cap_c, xs_c, gate_c = mix_and_route(False, l, x_c, mod_c, pre_c)
        if l + 1 < depth:
            cache = cache_prep(l + 1)
        ye_c, ye_l = _experts(l, xs_c, gate_c, xs_l, gate_l, w_e_gate, w_e_up, w_e_down)
        ye_c = ye_c.reshape(-1, b_ctx, cap_c, d)
        ye_l = ye_l.reshape(-1, b_lat, cap_l, d)
        if l + 1 < depth:
            x_c, mod_c, pre_c = pre(False, l + 1, x_c, new, (mod_c, selt_c, ye_c, cap_c))
            x_l, mod_l, pre_l = pre(True, l + 1, x_l, None, (mod_l, selt_l, ye_l, cap_l))
        else:
            x_c = _scatter(x_c, mod_c, selt_c, ye_c, cap_c, g_final)
            x_l = _scatter(x_l, mod_l, selt_l, ye_l, cap_l, g_final)
    y_prompt, y_sample = x_c, x_l
    new_diff_k, new_diff_v, new_mla_ckv, new_mla_kpe = new
    return (y_prompt, y_sample, new_diff_k, new_diff_v, new_mla_ckv, new_mla_kpe)
```

```python
import functools
import math

import jax
import jax.numpy as jnp
import numpy as np
from jax import lax
from jax.experimental import pallas as pl
from jax.experimental.pallas import tpu as pltpu
from jax.experimental.pallas import tpu_sc as plsc

BF = jnp.bfloat16
F32 = jnp.float32

GRID_W = 64
ROPE_BASE = 10000.0
NORM_EPS = 1e-6
F_GROUPS, F_CH = 4, 64
D_FOURIER = F_GROUPS * F_CH
DIFF_HEADS, DIFF_DH = 4, 64
D_DIFF = DIFF_HEADS * 2 * DIFF_DH
MLA_HEADS, MLA_Q_RANK, MLA_KV_RANK = 4, 256, 128
MLA_NOPE, MLA_ROPE, MLA_V = 64, 32, 64
HEAD_W = 128
D_MLA_P = MLA_HEADS * HEAD_W
D_MLA = MLA_HEADS * MLA_V
D_MIX = D_FOURIER + D_DIFF + D_MLA
ATTN_ROWS = 512
KEY_CHUNK = 256
SCORE_BUFS = 2
BF_ROWS = 16
SELECT_ROWS = 128
EXPERT_ROWS = 1024
EXPERT_SUB_ROWS = 512
PRE_ROWS = 512
STREAM_ROWS = 1024
SC_CHUNK = 64
SUB_ROWS = 256
N_EXPERTS = 16
EC_CAPACITY_FACTOR = 2
N_MOD = 6
IN_COLS_P = D_FOURIER + 3 * D_DIFF + MLA_Q_RANK + MLA_KV_RANK + HEAD_W
LOG2E = 1.4426950408889634
VMEM_LIMIT = 56 * 1024 * 1024


def _cparams(n_axes, vmem=VMEM_LIMIT):
    return pltpu.CompilerParams(dimension_semantics=("arbitrary",) * n_axes, vmem_limit_bytes=vmem)


def _dot(a, b):
    return jnp.dot(a, b, preferred_element_type=F32)


def _dot_nt(a, b):
    return lax.dot_general(a, b, (((1,), (1,)), ((), ())), preferred_element_type=F32)


def _rms(x, g):
    return x * lax.rsqrt(jnp.mean(x * x, axis=-1, keepdims=True) + NORM_EPS) * g


def _pack_halves(hb):
    w = hb.shape[1] // 2
    bits = pltpu.bitcast(hb.astype(F32), jnp.uint32)
    packed = (bits[:, :w] >> 16) | (bits[:, w:] & jnp.uint32(0xFFFF0000))
    return pltpu.bitcast(packed, jnp.int32)


def _unpack_halves(xi):
    bits = pltpu.bitcast(xi, jnp.uint32)
    lo = pltpu.bitcast(bits << 16, F32).astype(BF)
    hi = pltpu.bitcast(bits & jnp.uint32(0xFFFF0000), F32).astype(BF)
    return jnp.concatenate([lo, hi], axis=1)


def _rope(z, cos, sin_signed, half, group, lo):
    w = z.shape[1]
    reps = w // cos.shape[1]
    cos_w = jnp.concatenate([cos] * reps, axis=1) if reps > 1 else cos
    sin_w = jnp.concatenate([sin_signed] * reps, axis=1) if reps > 1 else sin_signed
    from_right = pltpu.roll(z, w - half, 1)
    from_left = pltpu.roll(z, half, 1)
    lane = lax.broadcasted_iota(jnp.int32, z.shape, 1) % group
    first = (lane >= lo) & (lane < lo + half)
    partner = jnp.where(first, from_right, from_left)
    return z * cos_w + partner * sin_w


def _mod_kernel(c_ref, w_ref, b_ref, o_ref):
    c = c_ref[...]
    a = (c * (1.0 / (1.0 + jnp.exp(-c)))).astype(BF)
    o_ref[0] = _dot(a, w_ref[0].astype(BF)) + b_ref[0]


def _modulation(cond, w_ada, b_ada):
    depth, d, n6 = w_ada.shape
    r = cond.shape[0]
    tn = 1536
    return pl.pallas_call(
        _mod_kernel,
        name="modulation",
        grid=(depth, n6 // tn),
        in_specs=[
            pl.BlockSpec((r, d), lambda l, j: (0, 0)),
            pl.BlockSpec((1, d, tn), lambda l, j: (l, 0, j)),
            pl.BlockSpec((1, 1, tn), lambda l, j: (l, 0, j)),
        ],
        out_specs=pl.BlockSpec((1, r, tn), lambda l, j: (l, 0, j)),
        out_shape=jax.ShapeDtypeStruct((depth, r, n6), F32),
        compiler_params=_cparams(2),
    )(cond, w_ada, b_ada.reshape(depth, 1, n6))


def _pre_kernel(latent, d, *refs):
    if latent:
        (x_ref, mod_ref, g_ref, win_ref, gq_ref, wuq_ref, gkv_ref, wk_ref, wv_ref, cc_ref, sc_ref,
         cd_ref, sd_ref, cm_ref, sm_ref,
         u_ref, v_ref, qd_ref, kd_ref, vd_ref, qm_ref, km_ref, vm_ref) = refs
    else:
        (x_ref, mod_ref, g_ref, win_ref, gq_ref, wuq_ref, gkv_ref, wk_ref, wv_ref, cc_ref, sc_ref,
         u_ref, v_ref, qd_ref, kd_ref, vd_ref, qm_ref, km_ref, vm_ref,
         k32_ref, v32_ref, ckv32_ref, kpe32_ref) = refs[:11] + refs[-12:]
    sh = mod_ref[0, :, 0:d]
    sc = mod_ref[0, :, d:2 * d]
    tm = x_ref.shape[1]
    sub = SUB_ROWS if tm > SUB_ROWS else max(tm // 2, BF_ROWS)
    def project(r0):
        h = _rms(x_ref[0, r0:r0 + sub, :], g_ref[...]) * (1.0 + sc) + sh
        return _dot(h.astype(BF), win_ref[...])

    starts = list(range(0, tm, sub))
    z_next = project(starts[0])
    for i, r0 in enumerate(starts):
        rows = slice(r0, r0 + sub)
        z = z_next
        z_next = project(starts[i + 1]) if i + 1 < len(starts) else None
        o = 0
        zf = z[:, o:o + D_FOURIER]; o += D_FOURIER
        zq = z[:, o:o + D_DIFF]; o += D_DIFF
        zk = z[:, o:o + D_DIFF]; o += D_DIFF
        zv = z[:, o:o + D_DIFF]; o += D_DIFF
        zcq = z[:, o:o + MLA_Q_RANK]; o += MLA_Q_RANK
        zckv = z[:, o:o + MLA_KV_RANK]; o += MLA_KV_RANK
        kpe = z[:, o:o + HEAD_W]

        zf_b = zf.astype(BF)
        u_ref[0, rows, :] = _dot(zf_b, cc_ref[...]).astype(BF)
        v_ref[0, rows, :] = _dot(zf_b, sc_ref[...]).astype(BF)
        vd_ref[0, :, rows] = zv.T.astype(BF)
        cq = _rms(zcq, gq_ref[...])
        qm = _dot(cq.astype(BF), wuq_ref[...])
        ckv = _rms(zckv, gkv_ref[...])
        ckv_b = ckv.astype(BF)
        k_nope = _dot(ckv_b, wk_ref[...])
        vm_ref[0, :, rows] = _dot(ckv_b, wv_ref[...]).T.astype(BF)
        if latent:
            cd, sd, cm, sm = cd_ref[rows, :], sd_ref[rows, :], cm_ref[rows, :], sm_ref[rows, :]
            zq_r = _rope(zq, cd, sd, DIFF_DH // 2, DIFF_DH, 0)
            zk_r = _rope(zk, cd, sd, DIFF_DH // 2, DIFF_DH, 0)
            qm = _rope(qm, cm, sm, MLA_ROPE // 2, HEAD_W, MLA_NOPE)
            kpe_r = _rope(kpe, cm, sm, MLA_ROPE // 2, HEAD_W, MLA_NOPE)
        else:
            zq_r, zk_r, kpe_r = zq, zk, kpe
            for hd in range(DIFF_HEADS):
                k32_ref[0, 0, rows, hd, :] = zk[:, hd * HEAD_W:(hd + 1) * HEAD_W]
                v32_ref[0, 0, rows, hd, :] = zv[:, hd * HEAD_W:(hd + 1) * HEAD_W]
            ckv32_ref[0, 0, rows, :] = ckv
            kpe32_ref[0, 0, rows, :] = kpe[:, MLA_NOPE:MLA_NOPE + MLA_ROPE]
            if i == 0:
                for ref in (k32_ref, v32_ref, ckv32_ref, kpe32_ref):
                    if ref.shape[1] > 1:
                        ref[0, 1:] = jnp.zeros((ref.shape[1] - 1,) + ref.shape[2:], F32)
        qd_ref[0, rows, :] = (zq_r * (DIFF_DH ** -0.5 * LOG2E)).astype(BF)
        kd_ref[0, rows, :] = zk_r.astype(BF)
        qm_ref[0, rows, :] = (qm * ((MLA_NOPE + MLA_ROPE) ** -0.5 * LOG2E)).astype(BF)
        km_ref[0, rows, :] = (k_nope + jnp.concatenate([kpe_r] * MLA_HEADS, axis=1)).astype(BF)


def _combine_pre_kernel(latent, d, cap, n_pre_in, *refs):
    x_ref, modp_ref, selt_ref, ye_ref = refs[:4]
    pre_in = refs[4:4 + n_pre_in]
    xo_ref = refs[4 + n_pre_in]
    pre_out = refs[5 + n_pre_in:-1]
    xo_ref[0] = _combine_rows(cap, d, x_ref, modp_ref, selt_ref, ye_ref, refs[-1], slice(None))
    _pre_kernel(latent, d, xo_ref, *pre_in, *pre_out)


def _pre(latent, x, modv, g_attn, w_in_p, g_q, w_uq_p, g_kv, w_k_p, w_v_p, cc_bd, sc_bd, ropes, l=0, depth=1, prev=None,
         combine=None):
    b, n, d = x.shape
    tm = min(PRE_ROWS, n)
    bm = modv.shape[0]
    mod_idx = (lambda i, j: (i, 0, 0)) if bm > 1 else (lambda i, j: (0, 0, 0))
    const = lambda i, j: (0, 0)
    tok = lambda i, j: (i, j, 0)
    in_specs = [
        pl.BlockSpec((1, tm, d), tok),
        pl.BlockSpec((1, 1, modv.shape[2]), mod_idx),
        pl.BlockSpec((1, d), const),
        pl.BlockSpec(w_in_p.shape, const),
        pl.BlockSpec((1, MLA_Q_RANK), const),
        pl.BlockSpec(w_uq_p.shape, const),
        pl.BlockSpec((1, MLA_KV_RANK), const),
        pl.BlockSpec(w_k_p.shape, const),
        pl.BlockSpec(w_v_p.shape, const),
        pl.BlockSpec(cc_bd.shape, const),
        pl.BlockSpec(sc_bd.shape, const),
    ]
    args = [x, modv, g_attn.reshape(1, d), w_in_p, g_q.reshape(1, -1), w_uq_p, g_kv.reshape(1, -1), w_k_p, w_v_p, cc_bd, sc_bd]
    if latent:
        in_specs += [pl.BlockSpec((tm, HEAD_W), lambda i, j: (j, 0))] * 4
        args += list(ropes)
    feat = lambda i, j: (i, 0, j)
    out_shape, out_specs = [], []
    for w, token_major in ((D_FOURIER, True), (D_FOURIER, True), (D_DIFF, True), (D_DIFF, True), (D_DIFF, False),
                           (D_MLA_P, True), (D_MLA_P, True), (D_MLA, False)):
        out_shape.append(jax.ShapeDtypeStruct((b, n, w) if token_major else (b, w, n), BF))
        out_specs.append(pl.BlockSpec((1, tm, w), tok) if token_major else pl.BlockSpec((1, w, tm), feat))
    aliases = {}
    if not latent:
        span = depth if prev is None else 1
        for tail in ((DIFF_HEADS, HEAD_W), (DIFF_HEADS, HEAD_W), (MLA_KV_RANK,), (MLA_ROPE,)):
            out_shape.append(jax.ShapeDtypeStruct((b, depth, n) + tail, F32))
            out_specs.append(pl.BlockSpec((1, span, tm) + tail, lambda i, j, t=len(tail): (i, l, j) + (0,) * t))
        if prev is not None:
            for k, buf in enumerate(prev):
                aliases[len(args)] = len(out_shape) - len(prev) + k
                in_specs.append(pl.BlockSpec(memory_space=pl.ANY))
                args.append(buf)
    body, scratch, tag = functools.partial(_pre_kernel, latent, d), [], "pre"
    if combine is not None:
        modp, sel_t, ye, cap = combine
        e = ye.shape[0]
        modp_idx = (lambda i, j: (i, 0, 0)) if modp.shape[0] > 1 else (lambda i, j: (0, 0, 0))
        extra_specs = [pl.BlockSpec((1, 1, modp.shape[2]), modp_idx), pl.BlockSpec((1, tm, e), tok),
                       pl.BlockSpec((e, 1, cap, d), lambda i, j: (0, i, 0, 0))]
        n_extra = len(extra_specs)
        body = functools.partial(_combine_pre_kernel, latent, d, cap, len(args) - 1)
        in_specs = in_specs[:1] + extra_specs + in_specs[1:]
        args = args[:1] + [modp, sel_t, ye] + args[1:]
        out_specs = [pl.BlockSpec((1, tm, d), tok)] + out_specs
        out_shape = [jax.ShapeDtypeStruct((b, n, d), F32)] + out_shape
        aliases = {k + n_extra: v + 1 for k, v in aliases.items()}
        scratch, tag = [pltpu.VMEM((tm, e * cap), BF)], "combine_pre"
    return pl.pallas_call(
        body,
        name=tag + ("_lat" if latent else "_ctx"),
        grid=(b, n // tm),
        in_specs=in_specs,
        out_specs=out_specs,
        out_shape=out_shape,
        input_output_aliases=aliases,
        scratch_shapes=scratch,
        compiler_params=_cparams(2),
    )(*args)


def _cache_kernel(dk_ref, dv_ref, ckv_ref, kpe_ref, wk_ref, wv_ref, place_ref, kd_ref, vd_ref, km_ref, vm_ref):
    for hd in range(DIFF_HEADS):
        sl = slice(hd * HEAD_W, (hd + 1) * HEAD_W)
        kd_ref[0, :, sl] = dk_ref[0, 0, :, hd, :].astype(BF)
        vd_ref[0, sl, :] = dv_ref[0, 0, :, hd, :].T.astype(BF)
    ckv_b = ckv_ref[0, 0].astype(BF)
    kpe_w = _dot(kpe_ref[0, 0].astype(BF), place_ref[...])
    km_ref[0] = (_dot(ckv_b, wk_ref[...]) + kpe_w).astype(BF)
    vm_ref[0] = _dot(ckv_b, wv_ref[...]).T.astype(BF)


def _cache_prep(l, cache_dk, cache_dv, cache_ckv, cache_kpe, w_k_p, w_v_p, place):
    b, _, p = cache_dk.shape[:3]
    at_l = lambda i: (i, l, 0, 0)
    at_l5 = lambda i: (i, l, 0, 0, 0)
    const = lambda i: (0, 0)
    shapes = [(p, D_DIFF), (D_DIFF, p), (p, D_MLA_P), (D_MLA, p)]
    return pl.pallas_call(
        _cache_kernel,
        name="cache_prep",
        grid=(b,),
        in_specs=[
            pl.BlockSpec((1, 1, p, DIFF_HEADS, HEAD_W), at_l5),
            pl.BlockSpec((1, 1, p, DIFF_HEADS, HEAD_W), at_l5),
            pl.BlockSpec((1, 1, p, MLA_KV_RANK), at_l),
            pl.BlockSpec((1, 1, p, MLA_ROPE), at_l),
            pl.BlockSpec(w_k_p.shape, const),
            pl.BlockSpec(w_v_p.shape, const),
            pl.BlockSpec(place.shape, const),
        ],
        out_specs=[pl.BlockSpec((1,) + s, lambda i: (i, 0, 0)) for s in shapes],
        out_shape=[jax.ShapeDtypeStruct((b,) + s, BF) for s in shapes],
        compiler_params=_cparams(1),
    )(cache_dk, cache_dv, cache_ckv, cache_kpe, w_k_p, w_v_p, place)


def _requests_per_step(b, n, tm):
    bb = 1
    if n == tm:
        while bb * 2 * n <= STREAM_ROWS and b % (bb * 2) == 0:
            bb *= 2
    return bb


def _fourier_kernel(scale, u_ref, v_ref, cn_ref, sn_ref, w_ref, o_ref):
    for bi in range(u_ref.shape[0]):
        f = (_dot(cn_ref[...], u_ref[bi]) - _dot(sn_ref[...], v_ref[bi])) * scale
        o_ref[bi] = _dot(f.astype(BF), w_ref[...]).astype(BF)


def _fourier(u, v, cn, sn, w_bd):
    b, n, w = u.shape
    tm = min(STREAM_ROWS, n)
    bb = _requests_per_step(b, n, tm)
    return pl.pallas_call(
        functools.partial(_fourier_kernel, 1.0 / math.sqrt(n * F_CH)),
        name="fourier_%d" % n,
        grid=(n // tm, b // bb),
        in_specs=[
            pl.BlockSpec((bb, n, w), lambda i, j: (j, 0, 0)),
            pl.BlockSpec((bb, n, w), lambda i, j: (j, 0, 0)),
            pl.BlockSpec((tm, n), lambda i, j: (i, 0)),
            pl.BlockSpec((tm, n), lambda i, j: (i, 0)),
            pl.BlockSpec((w, w), lambda i, j: (0, 0)),
        ],
        out_specs=pl.BlockSpec((bb, tm, w), lambda i, j: (j, i, 0)),
        out_shape=jax.ShapeDtypeStruct((b, n, w), BF),
        compiler_params=_cparams(2),
    )(u, v, cn, sn, w_bd)


def _scores(q, k_parts, s_ref):
    off, m, bounds = 0, None, []
    for k in k_parts:
        s = _dot_nt(k(), q)
        n_i = s.shape[0]
        s_ref[off:off + n_i, :] = s
        mi = jnp.max(s, axis=0, keepdims=True)
        m = mi if m is None else jnp.maximum(m, mi)
        bounds.append((off, n_i))
        off += n_i
    return m, bounds


def _weighted(m, vt_parts, bounds, s_ref):
    o = None
    for vt, (r, n_i) in zip(vt_parts, bounds):
        v = vt()
        ones = jnp.where(lax.broadcasted_iota(jnp.int32, (BF_ROWS, n_i), 0) == 0, 1.0, 0.0).astype(BF)
        v_ext = jnp.concatenate([v, ones], axis=0)
        for c in range(0, n_i, KEY_CHUNK):
            rows = min(KEY_CHUNK, n_i - c)
            p = jnp.exp2(s_ref[r + c:r + c + rows, :] - m).astype(BF)
            oi = _dot(v_ext[:, c:c + rows], p)
            o = oi if o is None else o + oi
    dv = o.shape[0] - BF_ROWS
    return o[:dv] * (1.0 / o[dv:dv + 1])


def _attn_kernel(latent, lam_init, d, *refs):
    if latent:
        (x_ref, mod_ref, yf_ref, qd_ref, kd_ref, vd_ref, kdc_ref, vdc_ref, qm_ref, km_ref, vm_ref, kmc_ref, vmc_ref,
         lam_ref, gs_ref, wo_ref, o_ref, mix_ref, s_ref) = refs
    else:
        (x_ref, mod_ref, yf_ref, qd_ref, kd_ref, vd_ref, qm_ref, km_ref, vm_ref,
         lam_ref, gs_ref, wo_ref, o_ref, mix_ref, s_ref) = refs
        kdc_ref = vdc_ref = kmc_ref = vmc_ref = None
    lf = lam_ref[...]
    lam = (jnp.exp(jnp.sum(lf[0:1] * lf[1:2], axis=-1, keepdims=True))
           - jnp.exp(jnp.sum(lf[2:3] * lf[3:4], axis=-1, keepdims=True)) + lam_init)
    mix_ref[:, 0:D_FOURIER] = yf_ref[0]
    gs = gs_ref[...]

    def keys(ref, cache_ref, sl):
        parts = [lambda: ref[0, :, sl]]
        if latent:
            parts.append(lambda: cache_ref[0, :, sl])
        return parts

    def values_t(ref, cache_ref, sl):
        parts = [lambda: ref[0, sl, :]]
        if latent:
            parts.append(lambda: cache_ref[0, sl, :])
        return parts

    def diff_query(sl, comp):
        def load():
            qh = qd_ref[0, :, sl]
            lane = lax.broadcasted_iota(jnp.int32, qh.shape, 1)
            keep = (lane >= DIFF_DH) if comp else (lane < DIFF_DH)
            return jnp.where(keep, qh, jnp.zeros_like(qh))
        return load

    items = []
    for hd in range(DIFF_HEADS):
        sl = slice(hd * HEAD_W, (hd + 1) * HEAD_W)
        for comp in range(2):
            items.append((diff_query(sl, comp), keys(kd_ref, kdc_ref, sl), values_t(vd_ref, vdc_ref, sl)))
    for hd in range(MLA_HEADS):
        sl = slice(hd * HEAD_W, (hd + 1) * HEAD_W)
        vsl = slice(hd * MLA_V, (hd + 1) * MLA_V)
        items.append(((lambda sl=sl: qm_ref[0, :, sl]), keys(km_ref, kmc_ref, sl), values_t(vm_ref, vmc_ref, vsl)))

    def start(i):
        return _scores(items[i][0](), items[i][1], s_ref.at[i % SCORE_BUFS])

    outs = []
    state = start(0)
    for i in range(len(items)):
        nxt = start(i + 1) if i + 1 < len(items) else None
        m, bounds = state
        buf = i % SCORE_BUFS
        outs.append(_weighted(m, items[i][2], bounds, s_ref.at[buf]))
        state = nxt
        if i < 2 * DIFF_HEADS and i % 2 == 1:
            hd = i // 2
            o = outs[i - 1] - lam * outs[i]
            o = o * lax.rsqrt(jnp.mean(o * o, axis=0, keepdims=True) + NORM_EPS) * gs * (1.0 - lam_init)
            mix_ref[:, D_FOURIER + hd * HEAD_W:D_FOURIER + (hd + 1) * HEAD_W] = o.T.astype(BF)

    mix_ref[:, D_FOURIER + D_DIFF:D_MIX] = jnp.concatenate(outs[2 * DIFF_HEADS:], axis=0).T.astype(BF)
    y = _dot(mix_ref[...], wo_ref[...])
    gt = mod_ref[0, :, 2 * d:3 * d]
    o_ref[0] = x_ref[0] + gt * y


def _attention(latent, l, x, modv, yfour, qd, kd, vd, qm, km, vm, cache, diff_lambda_l, g_subln_l, w_out_b):
    b, n, d = x.shape
    tq = min(ATTN_ROWS, n)
    bm = modv.shape[0]
    mod_idx = (lambda i, j: (i, 0, 0)) if bm > 1 else (lambda i, j: (0, 0, 0))
    const = lambda i, j: (0, 0)
    tok = lambda i, j: (i, j, 0)
    full = lambda i, j: (i, 0, 0)
    p = cache[0].shape[1] if latent else 0
    in_specs = [
        pl.BlockSpec((1, tq, d), tok),
        pl.BlockSpec((1, 1, modv.shape[2]), mod_idx),
        pl.BlockSpec((1, tq, D_FOURIER), tok),
        pl.BlockSpec((1, tq, D_DIFF), tok),
        pl.BlockSpec((1, n, D_DIFF), full),
        pl.BlockSpec((1, D_DIFF, n), full),
    ]
    args = [x, modv, yfour, qd, kd, vd]
    if latent:
        kdc, vdc, kmc, vmc = cache
        in_specs += [pl.BlockSpec((1, p, D_DIFF), full), pl.BlockSpec((1, D_DIFF, p), full)]
        args += [kdc, vdc]
    in_specs += [pl.BlockSpec((1, tq, D_MLA_P), tok), pl.BlockSpec((1, n, D_MLA_P), full), pl.BlockSpec((1, D_MLA, n), full)]
    args += [qm, km, vm]
    if latent:
        in_specs += [pl.BlockSpec((1, p, D_MLA_P), full), pl.BlockSpec((1, D_MLA, p), full)]
        args += [kmc, vmc]
    in_specs += [pl.BlockSpec((4, DIFF_DH), const), pl.BlockSpec((HEAD_W, 1), const), pl.BlockSpec(w_out_b.shape, const)]
    args += [diff_lambda_l, g_subln_l.reshape(HEAD_W, 1), w_out_b]
    lam_init = 0.8 - 0.6 * math.exp(-0.3 * l)
    return pl.pallas_call(
        functools.partial(_attn_kernel, latent, lam_init, d),
        name="attn_lat" if latent else "attn_ctx",
        grid=(b, n // tq),
        in_specs=in_specs,
        out_specs=pl.BlockSpec((1, tq, d), tok),
        out_shape=jax.ShapeDtypeStruct((b, n, d), F32),
        scratch_shapes=[pltpu.VMEM((tq, D_MIX), BF), pltpu.VMEM((SCORE_BUFS, n + p, tq), F32)],
        compiler_params=_cparams(2),
    )(*args)


def _router_kernel(d, packed, x_ref, mod_ref, g_ref, wr_ref, h_ref, aff_ref):
    sh = mod_ref[0, :, 3 * d:4 * d]
    sc = mod_ref[0, :, 4 * d:5 * d]
    tm = x_ref.shape[1]
    sub = SUB_ROWS if tm > SUB_ROWS else max(tm // 2, BF_ROWS)
    for bi in range(x_ref.shape[0]):
        for r0 in range(0, tm, sub):
            rows = slice(r0, r0 + sub)
            hb = (_rms(x_ref[bi, rows, :], g_ref[...]) * (1.0 + sc) + sh).astype(BF)
            h_ref[bi, rows, :] = _pack_halves(hb) if packed else hb
            logits = _dot_nt(wr_ref[...], hb)
            e = jnp.exp(logits - jnp.max(logits, axis=0, keepdims=True))
            aff_ref[bi, :, rows] = e / jnp.sum(e, axis=0, keepdims=True)


def _router(x, modv, g_ffn_l, w_router_t, packed):
    b, n, d = x.shape
    hw, hdt = (d // 2, jnp.int32) if packed else (d, BF)
    tm = min(STREAM_ROWS, n)
    bm = modv.shape[0]
    bb = _requests_per_step(b, n, tm) if bm == 1 else 1
    mod_idx = (lambda i, j: (i, 0, 0)) if bm > 1 else (lambda i, j: (0, 0, 0))
    const = lambda i, j: (0, 0)
    return pl.pallas_call(
        functools.partial(_router_kernel, d, packed),
        name="router_%d" % n,
        grid=(b // bb, n // tm),
        in_specs=[
            pl.BlockSpec((bb, tm, d), lambda i, j: (i, j, 0)),
            pl.BlockSpec((1, 1, modv.shape[2]), mod_idx),
            pl.BlockSpec((1, d), const),
            pl.BlockSpec(w_router_t.shape, const),
        ],
        out_specs=[pl.BlockSpec((bb, tm, hw), lambda i, j: (i, j, 0)), pl.BlockSpec((bb, N_EXPERTS, tm), lambda i, j: (i, 0, j))],
        out_shape=[jax.ShapeDtypeStruct((b, n, hw), hdt), jax.ShapeDtypeStruct((b, N_EXPERTS, n), F32)],
        compiler_params=_cparams(2),
    )(x, modv, g_ffn_l.reshape(1, d), w_router_t)


def _select_kernel(cap, aff_ref, sel_ref):
    a = aff_ref[...]
    e, n = a.shape
    bits = pltpu.bitcast(a, jnp.int32)
    capf = float(cap)

    def body(_, carry):
        lo, hi = carry
        mid = lo + ((hi - lo) >> 1)
        cnt = jnp.sum(jnp.where(bits >= mid, 1.0, 0.0), axis=1, keepdims=True)
        up = cnt >= capf
        return jnp.where(up, mid, lo), jnp.where(up, hi, mid)

    lo0 = jnp.zeros((e, 1), jnp.int32)
    hi0 = jnp.full((e, 1), 0x7F800000, jnp.int32)
    thr, _ = lax.fori_loop(0, 31, body, (lo0, hi0))
    gt = bits > thr
    eq = bits == thr
    need = capf - jnp.sum(jnp.where(gt, 1.0, 0.0), axis=1, keepdims=True)
    both = jnp.concatenate([jnp.where(gt, 1.0, 0.0), jnp.where(eq, 1.0, 0.0)], axis=0).astype(BF)
    ck = min(256, n)
    tri = jnp.where(lax.broadcasted_iota(jnp.int32, (ck, ck), 0) < lax.broadcasted_iota(jnp.int32, (ck, ck), 1), 1.0, 0.0).astype(BF)
    off = jnp.zeros((2 * e, 1), F32)
    pieces = []
    for k in range(n // ck):
        blk = both[:, k * ck:(k + 1) * ck]
        pieces.append(_dot(blk, tri) + off)
        off = off + jnp.sum(blk.astype(F32), axis=1, keepdims=True)
    cum = jnp.concatenate(pieces, axis=1) if len(pieces) > 1 else pieces[0]
    cum_gt, cum_eq = cum[:e], cum[e:]
    chosen = gt | (eq & (cum_eq < need))
    pos = cum_gt + jnp.minimum(cum_eq, need)
    sel_ref[...] = jnp.where(chosen, pos, -1.0).astype(jnp.int32)


def _select(aff, cap):
    b, e, n = aff.shape
    rows = b * e
    rb = min(rows, SELECT_ROWS)
    sel = pl.pallas_call(
        functools.partial(_select_kernel, cap),
        name="select_%d" % n,
        grid=(rows // rb,),
        in_specs=[pl.BlockSpec((rb, n), lambda i: (i, 0))],
        out_specs=pl.BlockSpec((rb, n), lambda i: (i, 0)),
        out_shape=jax.ShapeDtypeStruct((rows, n), jnp.int32),
        compiler_params=_cparams(1),
    )(aff.reshape(rows, n))
    return sel.reshape(b, e, n)


def _gather_kernel(cap, sel_ref, aff_ref, h_ref, xs_ref, gate_ref):
    n = h_ref.shape[1]
    e_tot = sel_ref.shape[1]
    for bi in range(h_ref.shape[0]):
        h = h_ref[bi]

        def one(e_idx, bi=bi):
            sel_e = sel_ref[bi, pl.ds(e_idx, 1), :]
            aff_e = aff_ref[bi, pl.ds(e_idx, 1), :]
            hit = sel_e == lax.broadcasted_iota(jnp.int32, (cap, n), 0)
            g = jnp.sum(jnp.where(hit, aff_e, 0.0), axis=1, keepdims=True)
            return jnp.where(hit, 1.0, 0.0).astype(BF), jnp.broadcast_to(g, (cap, HEAD_W))

        if cap >= 128:
            def body(e_idx, carry, bi=bi, h=h, one=one):
                p, g = one(e_idx)
                xs_ref[e_idx, bi] = _dot(p, h).astype(BF)
                gate_ref[e_idx, bi] = g
                return carry

            lax.fori_loop(0, e_tot, body, 0)
        else:
            ps, gs = zip(*[one(e_idx) for e_idx in range(e_tot)])
            xs = _dot(jnp.concatenate(ps, axis=0), h).astype(BF)
            for e_idx in range(e_tot):
                xs_ref[e_idx, bi] = xs[e_idx * cap:(e_idx + 1) * cap]
                gate_ref[e_idx, bi] = gs[e_idx]


def _gather(sel, aff, h2, cap):
    b, e, n = sel.shape
    d = h2.shape[2]
    bb = _requests_per_step(b, n, n)
    return pl.pallas_call(
        functools.partial(_gather_kernel, cap),
        name="gather_%d" % n,
        grid=(b // bb,),
        in_specs=[
            pl.BlockSpec((bb, e, n), lambda i: (i, 0, 0)),
            pl.BlockSpec((bb, e, n), lambda i: (i, 0, 0)),
            pl.BlockSpec((bb, n, d), lambda i: (i, 0, 0)),
        ],
        out_specs=[pl.BlockSpec((e, bb, cap, d), lambda i: (0, i, 0, 0)), pl.BlockSpec((e, bb, cap, HEAD_W), lambda i: (0, i, 0, 0))],
        out_shape=[jax.ShapeDtypeStruct((e, b, cap, d), BF), jax.ShapeDtypeStruct((e, b, cap, HEAD_W), F32)],
        compiler_params=_cparams(1),
    )(sel, aff, h2)


def _sc_dispatch(table, sel, aff, cap, n_experts):
    info = plsc.get_sparse_core_info()
    n_workers = info.num_cores * info.num_subcores
    lanes = info.num_lanes
    pairs, n = sel.shape
    b_tot = pairs // n_experts
    w = table.shape[1]
    per_w = pairs // n_workers
    ch = min(SC_CHUNK, cap)
    n_ch = cap // ch
    assert pairs % n_workers == 0 and cap % ch == 0 and ch % 8 == 0 and n % lanes == 0
    mesh = plsc.VectorSubcoreMesh(core_axis_name="c", subcore_axis_name="s")

    @functools.partial(
        pl.kernel, mesh=mesh, name="sc_dispatch",
        compiler_params=pltpu.CompilerParams(needs_layout_passes=False),
        out_type=[jax.ShapeDtypeStruct((pairs * cap, w), jnp.int32), jax.ShapeDtypeStruct((pairs * cap,), F32)],
        scratch_types=[pltpu.VMEM((n,), jnp.int32), pltpu.VMEM((n,), F32), pltpu.VMEM((n_ch, ch), jnp.int32),
                       pltpu.VMEM((cap,), F32), pltpu.VMEM((2, ch, w), jnp.int32), pltpu.SemaphoreType.DMA((2,))],
    )
    def k(table_hbm, sel_hbm, aff_hbm, xs_hbm, gate_hbm, sel_v, aff_v, tok_v, gate_v, rows_v, sem):
        wid = lax.axis_index("s") * info.num_cores + lax.axis_index("c")

        @pl.loop(0, per_w)
        def _(i):
            p = wid * per_w + i
            b = p // n_experts
            e = p - b * n_experts
            out_off = pl.multiple_of((e * b_tot + b) * cap, 8)
            pltpu.sync_copy(sel_hbm.at[p], sel_v)
            pltpu.sync_copy(aff_hbm.at[p], aff_v)
            lane = lax.iota(jnp.int32, lanes)

            @plsc.parallel_loop(0, n // lanes, unroll=4)
            def _(j):
                t0 = pl.multiple_of(j * lanes, lanes)
                s = sel_v[pl.ds(t0, lanes)]
                chosen = s >= 0
                slot = jnp.maximum(s, 0)
                plsc.store_scatter(tok_v, [slot // ch, slot % ch], lane + (t0 + b * n), mask=chosen)
                plsc.store_scatter(gate_v, [slot], aff_v[pl.ds(t0, lanes)], mask=chosen)

            prev = None
            for c in range(n_ch + 1):
                cur = None
                if c < n_ch:
                    cur = pltpu.async_copy(table_hbm.at[tok_v.at[c]], rows_v.at[c % 2], sem.at[c % 2])
                if prev is not None:
                    prev.wait()
                    pltpu.sync_copy(rows_v.at[(c - 1) % 2], xs_hbm.at[pl.ds(out_off + (c - 1) * ch, ch)])
                prev = cur
            pltpu.sync_copy(gate_v, gate_hbm.at[pl.ds(out_off, cap)])

    return k(table, sel, aff)


def _expert_kernel(steps_a, xa_ref, ga_ref, xb_ref, gb_ref, wg_ref, wu_ref, wd_ref, ya_ref, yb_ref, wg_s, wu_s, wd_s):
    j = pl.program_id(1)

    @pl.when(j == 0)
    def _():
        wg_s[...] = wg_ref[0, 0].astype(BF)
        wu_s[...] = wu_ref[0, 0].astype(BF)
        wd_s[...] = wd_ref[0, 0].astype(BF)

    def ffn(x_ref, g_ref, y_ref):
        rb = x_ref.shape[1]
        sub = min(rb, EXPERT_SUB_ROWS)
        for r0 in range(0, rb, sub):
            rows = slice(r0, r0 + sub)
            x = _unpack_halves(x_ref[0, rows, :]) if x_ref.dtype == jnp.int32 else x_ref[0, rows, :]
            a = _dot(x, wg_s[...])
            u = _dot(x, wu_s[...])
            mid = (a * (1.0 / (1.0 + jnp.exp(-a))) * u).astype(BF)
            y_ref[0, rows, :] = (_dot(mid, wd_s[...]) * g_ref[0, rows, 0:1]).astype(BF)

    @pl.when(j < steps_a)
    def _():
        ffn(xa_ref, ga_ref, ya_ref)

    @pl.when(j >= steps_a)
    def _():
        ffn(xb_ref, gb_ref, yb_ref)


def _experts(l, xs_a, gate_a, xs_b, gate_b, w_e_gate, w_e_up, w_e_down):
    e, rows_a, wa = xs_a.shape
    rows_b, wb = xs_b.shape[1:]
    d, ff = w_e_gate.shape[2:]
    rb_a, rb_b = math.gcd(rows_a, EXPERT_ROWS), math.gcd(rows_b, EXPERT_ROWS)
    steps_a, steps_b = rows_a // rb_a, rows_b // rb_b
    w_idx = lambda i, j: (l, i, 0, 0)
    idx_a = lambda i, j: (i, jnp.minimum(j, steps_a - 1), 0)
    idx_b = lambda i, j: (i, jnp.maximum(j - steps_a, 0), 0)
    return pl.pallas_call(
        functools.partial(_expert_kernel, steps_a),
        name="experts",
        grid=(e, steps_a + steps_b),
        in_specs=[
            pl.BlockSpec((1, rb_a, wa), idx_a),
            pl.BlockSpec((1, rb_a, HEAD_W), idx_a),
            pl.BlockSpec((1, rb_b, wb), idx_b),
            pl.BlockSpec((1, rb_b, HEAD_W), idx_b),
            pl.BlockSpec((1, 1, d, ff), w_idx),
            pl.BlockSpec((1, 1, d, ff), w_idx),
            pl.BlockSpec((1, 1, ff, d), w_idx),
        ],
        out_specs=[pl.BlockSpec((1, rb_a, d), idx_a), pl.BlockSpec((1, rb_b, d), idx_b)],
        out_shape=[jax.ShapeDtypeStruct((e, rows_a, d), BF), jax.ShapeDtypeStruct((e, rows_b, d), BF)],
        scratch_shapes=[pltpu.VMEM((d, ff), BF), pltpu.VMEM((d, ff), BF), pltpu.VMEM((ff, d), BF)],
        compiler_params=_cparams(2),
    )(xs_a, gate_a, xs_b, gate_b, w_e_gate, w_e_up, w_e_down)


def _combine_rows(cap, d, x_ref, mod_ref, selt_ref, ye_ref, pt_ref, rows):
    e_tot = ye_ref.shape[0]
    selt = selt_ref[0, rows, :]
    r = selt.shape[0]
    if cap % 128 == 0:
        lane = lax.broadcasted_iota(jnp.int32, (r, cap), 1)
        for e_idx in range(e_tot):
            pt_ref[rows, e_idx * cap:(e_idx + 1) * cap] = jnp.where(selt[:, e_idx:e_idx + 1] == lane, 1.0, 0.0).astype(BF)
    else:
        assert cap & (cap - 1) == 0
        shift = cap.bit_length() - 1
        col = lax.broadcasted_iota(jnp.int32, (e_tot, e_tot * cap), 1)
        spread = jnp.where((col >> shift) == lax.broadcasted_iota(jnp.int32, col.shape, 0), 1.0, 0.0).astype(BF)
        slot_of_col = _dot(selt.astype(F32).astype(BF), spread)
        want = (lax.broadcasted_iota(jnp.int32, (r, e_tot * cap), 1) & (cap - 1)).astype(F32)
        pt_ref[rows, :] = jnp.where(slot_of_col == want, 1.0, 0.0).astype(BF)
    y = _dot(pt_ref[rows, :], ye_ref[...].reshape(e_tot * cap, d))
    return x_ref[0, rows, :] + mod_ref[0, :, 5 * d:6 * d] * y


def _scatter_kernel(cap, d, x_ref, mod_ref, selt_ref, ye_ref, gf_ref, o_ref, pt_ref):
    o_ref[0] = _rms(_combine_rows(cap, d, x_ref, mod_ref, selt_ref, ye_ref, pt_ref, slice(None)), gf_ref[...])


def _scatter(x, modv, sel_t, ye, cap, g_final):
    b, n, d = x.shape
    e = ye.shape[0]
    tn = min(PRE_ROWS, n)
    bm = modv.shape[0]
    mod_idx = (lambda i, j: (i, 0, 0)) if bm > 1 else (lambda i, j: (0, 0, 0))
    return pl.pallas_call(
        functools.partial(_scatter_kernel, cap, d),
        name="scatter_%d" % n,
        grid=(b, n // tn),
        in_specs=[
            pl.BlockSpec((1, tn, d), lambda i, j: (i, j, 0)),
            pl.BlockSpec((1, 1, modv.shape[2]), mod_idx),
            pl.BlockSpec((1, tn, e), lambda i, j: (i, j, 0)),
            pl.BlockSpec((e, 1, cap, d), lambda i, j: (0, i, 0, 0)),
            pl.BlockSpec((1, d), lambda i, j: (0, 0)),
        ],
        out_specs=pl.BlockSpec((1, tn, d), lambda i, j: (i, j, 0)),
        out_shape=jax.ShapeDtypeStruct((b, n, d), F32),
        scratch_shapes=[pltpu.VMEM((tn, e * cap), BF)],
        compiler_params=_cparams(2),
    )(x, modv, sel_t, ye, g_final.reshape(1, d))


def _dft_tables(n):
    j = np.arange(n, dtype=np.int64)
    ang = ((j[:, None] * j[None, :]) % n).astype(np.float64) * (2.0 * math.pi / n)
    return jnp.asarray(np.cos(ang), dtype=BF), jnp.asarray(np.sin(ang), dtype=BF)


def _block_diag(blocks):
    g, r, c = blocks.shape
    out = jnp.zeros((g * r, g * c), blocks.dtype)
    for i in range(g):
        out = out.at[i * r:(i + 1) * r, i * c:(i + 1) * c].set(blocks[i])
    return out


def _rope_tables(n):
    rows = n // GRID_W
    row = jnp.repeat(jnp.arange(rows, dtype=F32), GRID_W)
    col = jnp.tile(jnp.arange(GRID_W, dtype=F32), rows)

    def ang(dim):
        nf = dim // 4
        inv = ROPE_BASE ** (-jnp.arange(nf, dtype=F32) / nf)
        return jnp.concatenate([row[:, None] * inv, col[:, None] * inv], axis=-1)

    a = ang(DIFF_DH)
    cos_d = jnp.tile(jnp.cos(a), (1, 4))
    sin_d = jnp.tile(jnp.concatenate([-jnp.sin(a), jnp.sin(a)], axis=1), (1, 2))
    a = ang(MLA_ROPE)
    ones = jnp.ones((n, MLA_NOPE), F32)
    pad = HEAD_W - MLA_NOPE - MLA_ROPE
    cos_m = jnp.concatenate([ones, jnp.cos(a), jnp.cos(a), jnp.ones((n, pad), F32)], axis=1)
    sin_m = jnp.concatenate([0 * ones, -jnp.sin(a), jnp.sin(a), jnp.zeros((n, pad), F32)], axis=1)
    return cos_d, sin_d, cos_m, sin_m


def _pad_heads(w, heads, lo, hi):
    k = w.shape[0]
    w3 = w.reshape(k, heads, -1)[:, :, lo:hi]
    return jnp.pad(w3, ((0, 0), (0, 0), (0, HEAD_W - (hi - lo)))).reshape(k, heads * HEAD_W)


def kernel(x_prompt, x_sample, cache_diff_k, cache_diff_v, cache_mla_ckv, cache_mla_kpe, c, c_ctx, w_ada, b_ada, g_attn, g_ffn, w_in, w_four, diff_lambda, g_subln, g_mla_q, w_mla_uq, g_mla_kv, w_mla_ukv, w_out, w_router, w_e_gate, w_e_up, w_e_down, g_final):
    depth, d = g_attn.shape
    b_ctx, n_ctx, _ = x_prompt.shape
    b_lat, n_lat, _ = x_sample.shape

    r = b_lat + 1
    r_pad = -(-r // 8) * 8
    cond = jnp.concatenate([c, c_ctx[None, :], jnp.zeros((r_pad - r, d), F32)], axis=0)
    mod = _modulation(cond, w_ada, b_ada)

    n_main = w_in.shape[2] - MLA_ROPE
    w_in_p = jnp.concatenate(
        [w_in[:, :, :n_main], jnp.zeros((depth, d, MLA_NOPE), F32), w_in[:, :, n_main:],
         jnp.zeros((depth, d, HEAD_W - MLA_NOPE - MLA_ROPE), F32)], axis=2).astype(BF)
    w_uq_p = jnp.stack([_pad_heads(w_mla_uq[l], MLA_HEADS, 0, MLA_NOPE + MLA_ROPE) for l in range(depth)]).astype(BF)
    w_k_p = jnp.stack([_pad_heads(w_mla_ukv[l], MLA_HEADS, 0, MLA_NOPE) for l in range(depth)]).astype(BF)
    w_v_p = w_mla_ukv.reshape(depth, MLA_KV_RANK, MLA_HEADS, MLA_NOPE + MLA_V)[..., MLA_NOPE:].reshape(depth, MLA_KV_RANK, D_MLA).astype(BF)
    w_out_p = w_out.astype(BF)
    w_router_t = jnp.swapaxes(w_router, 1, 2).astype(BF)
    place = jnp.zeros((MLA_ROPE, HEAD_W), F32).at[jnp.arange(MLA_ROPE), MLA_NOPE + jnp.arange(MLA_ROPE)].set(1.0)
    place = jnp.tile(place, (1, MLA_HEADS)).astype(BF)

    jc = jnp.arange(F_CH, dtype=jnp.int32)
    ang_c = ((jc[:, None] * jc[None, :]) % F_CH).astype(F32) * (2.0 * math.pi / F_CH)
    cc_bd = _block_diag(jnp.broadcast_to(jnp.cos(ang_c), (F_GROUPS, F_CH, F_CH))).astype(BF)
    sc_bd = _block_diag(jnp.broadcast_to(jnp.sin(ang_c), (F_GROUPS, F_CH, F_CH))).astype(BF)
    dft = {n: _dft_tables(n) for n in {n_ctx, n_lat}}
    ropes = _rope_tables(n_lat)

    def pre(latent, l, x, prev=None, combine=None):
        modv = (mod[l, :b_lat] if latent else mod[l, b_lat:b_lat + 1]).reshape(-1, 1, N_MOD * d)
        outs = _pre(latent, x, modv, g_attn[l], w_in_p[l], g_mla_q[l], w_uq_p[l], g_mla_kv[l], w_k_p[l], w_v_p[l],
                    cc_bd, sc_bd, ropes, l, depth, prev, combine)
        if combine is not None:
            x, outs = outs[0], outs[1:]
        return x, modv, outs

    def cache_prep(l):
        return _cache_prep(l, cache_diff_k, cache_diff_v, cache_mla_ckv, cache_mla_kpe, w_k_p[l], w_v_p[l], place)

    def mix_and_route(latent, l, x, modv, outs, cache=None):
        b, n, _ = x.shape
        cap = max(1, EC_CAPACITY_FACTOR * n // N_EXPERTS)
        u, v, qd, kd, vd, qm, km, vm = outs[:8]
        cn, sn = dft[n]
        yfour = _fourier(u, v, cn, sn, _block_diag(w_four[l]).astype(BF))
        x = _attention(latent, l, x, modv, yfour, qd, kd, vd, qm, km, vm, cache, diff_lambda[l], g_subln[l], w_out_p[l])
        h2, aff = _router(x, modv, g_ffn[l], w_router_t[l], packed=latent)
        sel = _select(aff, cap)
        e = sel.shape[1]
        if latent:
            xs, gate = _sc_dispatch(h2.reshape(b * n, h2.shape[2]), sel.reshape(b * e, n), aff.reshape(b * e, n), cap, e)
            gate = jnp.broadcast_to(gate.reshape(e, b * cap, 1), (e, b * cap, HEAD_W))
        else:
            xs, gate = _gather(sel, aff, h2, cap)
            gate = gate.reshape(e, b * cap, HEAD_W)
        return x, jnp.swapaxes(sel, 1, 2), cap, xs.reshape(e, b * cap, -1), gate

    x_l, mod_l, pre_l = pre(True, 0, x_sample)
    x_c, mod_c, pre_c = pre(False, 0, x_prompt)
    cache = cache_prep(0)
    for l in range(depth):
        new = tuple(pre_c[8:])
        x_l, selt_l, cap_l, xs_l, gate_l = mix_and_route(True, l, x_l, mod_l, pre_l, cache)
        x_c, selt_c, cap_c, xs_c, gate_c = mix_and_route(False, l, x_c, mod_c, pre_c)
        if l + 1 < depth:
            cache = cache_prep(l + 1)
        ye_c, ye_l = _experts(l, xs_c, gate_c, xs_l, gate_l, w_e_gate, w_e_up, w_e_down)
        ye_c = ye_c.reshape(-1, b_ctx, cap_c, d)
        ye_l = ye_l.reshape(-1, b_lat, cap_l, d)
        if l + 1 < depth:
            x_c, mod_c, pre_c = pre(False, l + 1, x_c, new, (mod_c, selt_c, ye_c, cap_c))
            x_l, mod_l, pre_l = pre(True, l + 1, x_l, None, (mod_l, selt_l, ye_l, cap_l))
        else:
            x_c = _scatter(x_c, mod_c, selt_c, ye_c, cap_c, g_final)
            x_l = _scatter(x_l, mod_l, selt_l, ye_l, cap_l, g_final)
    y_prompt, y_sample = x_c, x_l
    new_diff_k, new_diff_v, new_mla_ckv, new_mla_kpe = new
    return (y_prompt, y_sample, new_diff_k, new_diff_v, new_mla_ckv, new_mla_kpe)
```

```python
import functools
import math

import jax
import jax.numpy as jnp
import numpy as np
from jax import lax
from jax.experimental import pallas as pl
from jax.experimental.pallas import tpu as pltpu
from jax.experimental.pallas import tpu_sc as plsc

BF = jnp.bfloat16
F32 = jnp.float32

GRID_W = 64
ROPE_BASE = 10000.0
NORM_EPS = 1e-6
F_GROUPS, F_CH = 4, 64
D_FOURIER = F_GROUPS * F_CH
DIFF_HEADS, DIFF_DH = 4, 64
D_DIFF = DIFF_HEADS * 2 * DIFF_DH
MLA_HEADS, MLA_Q_RANK, MLA_KV_RANK = 4, 256, 128
MLA_NOPE, MLA_ROPE, MLA_V = 64, 32, 64
HEAD_W = 128
D_MLA_P = MLA_HEADS * HEAD_W
D_MLA = MLA_HEADS * MLA_V
D_MIX = D_FOURIER + D_DIFF + D_MLA
ATTN_ROWS = 512
KEY_CHUNK = 256
SCORE_BUFS = 2
BF_ROWS = 16
SELECT_ROWS = 128
EXPERT_ROWS = 1024
EXPERT_SUB_ROWS = 512
PRE_ROWS = 512
STREAM_ROWS = 1024
SC_CHUNK = 64
SUB_ROWS = 256
N_EXPERTS = 16
EC_CAPACITY_FACTOR = 2
N_MOD = 6
IN_COLS_P = D_FOURIER + 3 * D_DIFF + MLA_Q_RANK + MLA_KV_RANK + HEAD_W
LOG2E = 1.4426950408889634
VMEM_LIMIT = 56 * 1024 * 1024


def _cparams(n_axes, vmem=VMEM_LIMIT):
    return pltpu.CompilerParams(dimension_semantics=("arbitrary",) * n_axes, vmem_limit_bytes=vmem)


def _dot(a, b):
    return jnp.dot(a, b, preferred_element_type=F32)


def _dot_nt(a, b):
    return lax.dot_general(a, b, (((1,), (1,)), ((), ())), preferred_element_type=F32)


def _rms(x, g):
    return x * lax.rsqrt(jnp.mean(x * x, axis=-1, keepdims=True) + NORM_EPS) * g


def _pack_halves(hb):
    w = hb.shape[1] // 2
    bits = pltpu.bitcast(hb.astype(F32), jnp.uint32)
    packed = (bits[:, :w] >> 16) | (bits[:, w:] & jnp.uint32(0xFFFF0000))
    return pltpu.bitcast(packed, jnp.int32)


def _unpack_halves(xi):
    bits = pltpu.bitcast(xi, jnp.uint32)
    lo = pltpu.bitcast(bits << 16, F32).astype(BF)
    hi = pltpu.bitcast(bits & jnp.uint32(0xFFFF0000), F32).astype(BF)
    return jnp.concatenate([lo, hi], axis=1)


def _rope(z, cos, sin_signed, half, group, lo):
    w = z.shape[1]
    reps = w // cos.shape[1]
    cos_w = jnp.concatenate([cos] * reps, axis=1) if reps > 1 else cos
    sin_w = jnp.concatenate([sin_signed] * reps, axis=1) if reps > 1 else sin_signed
    from_right = pltpu.roll(z, w - half, 1)
    from_left = pltpu.roll(z, half, 1)
    lane = lax.broadcasted_iota(jnp.int32, z.shape, 1) % group
    first = (lane >= lo) & (lane < lo + half)
    partner = jnp.where(first, from_right, from_left)
    return z * cos_w + partner * sin_w


def _mod_kernel(c_ref, w_ref, b_ref, o_ref):
    c = c_ref[...]
    a = (c * (1.0 / (1.0 + jnp.exp(-c)))).astype(BF)
    o_ref[0] = _dot(a, w_ref[0].astype(BF)) + b_ref[0]


def _modulation(cond, w_ada, b_ada):
    depth, d, n6 = w_ada.shape
    r = cond.shape[0]
    tn = 1536
    return pl.pallas_call(
        _mod_kernel,
        name="modulation",
        grid=(depth, n6 // tn),
        in_specs=[
            pl.BlockSpec((r, d), lambda l, j: (0, 0)),
            pl.BlockSpec((1, d, tn), lambda l, j: (l, 0, j)),
            pl.BlockSpec((1, 1, tn), lambda l, j: (l, 0, j)),
        ],
        out_specs=pl.BlockSpec((1, r, tn), lambda l, j: (l, 0, j)),
        out_shape=jax.ShapeDtypeStruct((depth, r, n6), F32),
        compiler_params=_cparams(2),
    )(cond, w_ada, b_ada.reshape(depth, 1, n6))


def _pre_kernel(latent, d, *refs):
    if latent:
        (x_ref, mod_ref, g_ref, win_ref, gq_ref, wuq_ref, gkv_ref, wk_ref, wv_ref, cc_ref, sc_ref,
         cd_ref, sd_ref, cm_ref, sm_ref,
         u_ref, v_ref, qd_ref, kd_ref, vd_ref, qm_ref, km_ref, vm_ref) = refs
    else:
        (x_ref, mod_ref, g_ref, win_ref, gq_ref, wuq_ref, gkv_ref, wk_ref, wv_ref, cc_ref, sc_ref,
         u_ref, v_ref, qd_ref, kd_ref, vd_ref, qm_ref, km_ref, vm_ref,
         k32_ref, v32_ref, ckv32_ref, kpe32_ref) = refs[:11] + refs[-12:]
    sh = mod_ref[0, :, 0:d]
    sc = mod_ref[0, :, d:2 * d]
    tm = x_ref.shape[1]
    sub = SUB_ROWS if tm > SUB_ROWS else max(tm // 2, BF_ROWS)
    def project(r0):
        h = _rms(x_ref[0, r0:r0 + sub, :], g_ref[...]) * (1.0 + sc) + sh
        return _dot(h.astype(BF), win_ref[...])

    starts = list(range(0, tm, sub))
    z_next = project(starts[0])
    for i, r0 in enumerate(starts):
        rows = slice(r0, r0 + sub)
        z = z_next
        z_next = project(starts[i + 1]) if i + 1 < len(starts) else None
        o = 0
        zf = z[:, o:o + D_FOURIER]; o += D_FOURIER
        zq = z[:, o:o + D_DIFF]; o += D_DIFF
        zk = z[:, o:o + D_DIFF]; o += D_DIFF
        zv = z[:, o:o + D_DIFF]; o += D_DIFF
        zcq = z[:, o:o + MLA_Q_RANK]; o += MLA_Q_RANK
        zckv = z[:, o:o + MLA_KV_RANK]; o += MLA_KV_RANK
        kpe = z[:, o:o + HEAD_W]

        zf_b = zf.astype(BF)
        u_ref[0, rows, :] = _dot(zf_b, cc_ref[...]).astype(BF)
        v_ref[0, rows, :] = _dot(zf_b, sc_ref[...]).astype(BF)
        vd_ref[0, :, rows] = zv.T.astype(BF)
        cq = _rms(zcq, gq_ref[...])
        qm = _dot(cq.astype(BF), wuq_ref[...])
        ckv = _rms(zckv, gkv_ref[...])
        ckv_b = ckv.astype(BF)
        k_nope = _dot(ckv_b, wk_ref[...])
        vm_ref[0, :, rows] = _dot(ckv_b, wv_ref[...]).T.astype(BF)
        if latent:
            cd, sd, cm, sm = cd_ref[rows, :], sd_ref[rows, :], cm_ref[rows, :], sm_ref[rows, :]
            zq_r = _rope(zq, cd, sd, DIFF_DH // 2, DIFF_DH, 0)
            zk_r = _rope(zk, cd, sd, DIFF_DH // 2, DIFF_DH, 0)
            qm = _rope(qm, cm, sm, MLA_ROPE // 2, HEAD_W, MLA_NOPE)
            kpe_r = _rope(kpe, cm, sm, MLA_ROPE // 2, HEAD_W, MLA_NOPE)
        else:
            zq_r, zk_r, kpe_r = zq, zk, kpe
            for hd in range(DIFF_HEADS):
                k32_ref[0, 0, rows, hd, :] = zk[:, hd * HEAD_W:(hd + 1) * HEAD_W]
                v32_ref[0, 0, rows, hd, :] = zv[:, hd * HEAD_W:(hd + 1) * HEAD_W]
            ckv32_ref[0, 0, rows, :] = ckv
            kpe32_ref[0, 0, rows, :] = kpe[:, MLA_NOPE:MLA_NOPE + MLA_ROPE]
            if i == 0:
                for ref in (k32_ref, v32_ref, ckv32_ref, kpe32_ref):
                    if ref.shape[1] > 1:
                        ref[0, 1:] = jnp.zeros((ref.shape[1] - 1,) + ref.shape[2:], F32)
        qd_ref[0, rows, :] = (zq_r * (DIFF_DH ** -0.5 * LOG2E)).astype(BF)
        kd_ref[0, rows, :] = zk_r.astype(BF)
        qm_ref[0, rows, :] = (qm * ((MLA_NOPE + MLA_ROPE) ** -0.5 * LOG2E)).astype(BF)
        km_ref[0, rows, :] = (k_nope + jnp.concatenate([kpe_r] * MLA_HEADS, axis=1)).astype(BF)


def _combine_pre_kernel(latent, d, cap, n_pre_in, *refs):
    x_ref, modp_ref, selt_ref, ye_ref = refs[:4]
    pre_in = refs[4:4 + n_pre_in]
    xo_ref = refs[4 + n_pre_in]
    pre_out = refs[5 + n_pre_in:-1]
    xo_ref[0] = _combine_rows(cap, d, x_ref, modp_ref, selt_ref, ye_ref, refs[-1], slice(None))
    _pre_kernel(latent, d, xo_ref, *pre_in, *pre_out)


def _pre(latent, x, modv, g_attn, w_in_p, g_q, w_uq_p, g_kv, w_k_p, w_v_p, cc_bd, sc_bd, ropes, l=0, depth=1, prev=None,
         combine=None):
    b, n, d = x.shape
    tm = min(PRE_ROWS, n)
    bm = modv.shape[0]
    mod_idx = (lambda i, j: (i, 0, 0)) if bm > 1 else (lambda i, j: (0, 0, 0))
    const = lambda i, j: (0, 0)
    tok = lambda i, j: (i, j, 0)
    in_specs = [
        pl.BlockSpec((1, tm, d), tok),
        pl.BlockSpec((1, 1, modv.shape[2]), mod_idx),
        pl.BlockSpec((1, d), const),
        pl.BlockSpec(w_in_p.shape, const),
        pl.BlockSpec((1, MLA_Q_RANK), const),
        pl.BlockSpec(w_uq_p.shape, const),
        pl.BlockSpec((1, MLA_KV_RANK), const),
        pl.BlockSpec(w_k_p.shape, const),
        pl.BlockSpec(w_v_p.shape, const),
        pl.BlockSpec(cc_bd.shape, const),
        pl.BlockSpec(sc_bd.shape, const),
    ]
    args = [x, modv, g_attn.reshape(1, d), w_in_p, g_q.reshape(1, -1), w_uq_p, g_kv.reshape(1, -1), w_k_p, w_v_p, cc_bd, sc_bd]
    if latent:
        in_specs += [pl.BlockSpec((tm, HEAD_W), lambda i, j: (j, 0))] * 4
        args += list(ropes)
    feat = lambda i, j: (i, 0, j)
    out_shape, out_specs = [], []
    for w, token_major in ((D_FOURIER, True), (D_FOURIER, True), (D_DIFF, True), (D_DIFF, True), (D_DIFF, False),
                           (D_MLA_P, True), (D_MLA_P, True), (D_MLA, False)):
        out_shape.append(jax.ShapeDtypeStruct((b, n, w) if token_major else (b, w, n), BF))
        out_specs.append(pl.BlockSpec((1, tm, w), tok) if token_major else pl.BlockSpec((1, w, tm), feat))
    aliases = {}
    if not latent:
        span = depth if prev is None else 1
        for tail in ((DIFF_HEADS, HEAD_W), (DIFF_HEADS, HEAD_W), (MLA_KV_RANK,), (MLA_ROPE,)):
            out_shape.append(jax.ShapeDtypeStruct((b, depth, n) + tail, F32))
            out_specs.append(pl.BlockSpec((1, span, tm) + tail, lambda i, j, t=len(tail): (i, l, j) + (0,) * t))
        if prev is not None:
            for k, buf in enumerate(prev):
                aliases[len(args)] = len(out_shape) - len(prev) + k
                in_specs.append(pl.BlockSpec(memory_space=pl.ANY))
                args.append(buf)
    body, scratch, tag = functools.partial(_pre_kernel, latent, d), [], "pre"
    if combine is not None:
        modp, sel_t, ye, cap = combine
        e = ye.shape[0]
        modp_idx = (lambda i, j: (i, 0, 0)) if modp.shape[0] > 1 else (lambda i, j: (0, 0, 0))
        extra_specs = [pl.BlockSpec((1, 1, modp.shape[2]), modp_idx), pl.BlockSpec((1, tm, e), tok),
                       pl.BlockSpec((e, 1, cap, d), lambda i, j: (0, i, 0, 0))]
        n_extra = len(extra_specs)
        body = functools.partial(_combine_pre_kernel, latent, d, cap, len(args) - 1)
        in_specs = in_specs[:1] + extra_specs + in_specs[1:]
        args = args[:1] + [modp, sel_t, ye] + args[1:]
        out_specs = [pl.BlockSpec((1, tm, d), tok)] + out_specs
        out_shape = [jax.ShapeDtypeStruct((b, n, d), F32)] + out_shape
        aliases = {k + n_extra: v + 1 for k, v in aliases.items()}
        scratch, tag = [pltpu.VMEM((tm, e * cap), BF)], "combine_pre"
    return pl.pallas_call(
        body,
        name=tag + ("_lat" if latent else "_ctx"),
        grid=(b, n // tm),
        in_specs=in_specs,
        out_specs=out_specs,
        out_shape=out_shape,
        input_output_aliases=aliases,
        scratch_shapes=scratch,
        compiler_params=_cparams(2),
    )(*args)


def _cache_kernel(dk_ref, dv_ref, ckv_ref, kpe_ref, wk_ref, wv_ref, place_ref, kd_ref, vd_ref, km_ref, vm_ref):
    for bi in range(dk_ref.shape[0]):
        for hd in range(DIFF_HEADS):
            sl = slice(hd * HEAD_W, (hd + 1) * HEAD_W)
            kd_ref[bi, :, sl] = dk_ref[bi, 0, :, hd, :].astype(BF)
            vd_ref[bi, sl, :] = dv_ref[bi, 0, :, hd, :].T.astype(BF)
        ckv_b = ckv_ref[bi, 0].astype(BF)
        kpe_w = _dot(kpe_ref[bi, 0].astype(BF), place_ref[...])
        km_ref[bi] = (_dot(ckv_b, wk_ref[...]) + kpe_w).astype(BF)
        vm_ref[bi] = _dot(ckv_b, wv_ref[...]).T.astype(BF)


def _cache_prep(l, cache_dk, cache_dv, cache_ckv, cache_kpe, w_k_p, w_v_p, place):
    b, _, p = cache_dk.shape[:3]
    at_l = lambda i: (i, l, 0, 0)
    at_l5 = lambda i: (i, l, 0, 0, 0)
    const = lambda i: (0, 0)
    shapes = [(p, D_DIFF), (D_DIFF, p), (p, D_MLA_P), (D_MLA, p)]
    bb = _requests_per_step(b, p, p)
    return pl.pallas_call(
        _cache_kernel,
        name="cache_prep",
        grid=(b // bb,),
        in_specs=[
            pl.BlockSpec((bb, 1, p, DIFF_HEADS, HEAD_W), at_l5),
            pl.BlockSpec((bb, 1, p, DIFF_HEADS, HEAD_W), at_l5),
            pl.BlockSpec((bb, 1, p, MLA_KV_RANK), at_l),
            pl.BlockSpec((bb, 1, p, MLA_ROPE), at_l),
            pl.BlockSpec(w_k_p.shape, const),
            pl.BlockSpec(w_v_p.shape, const),
            pl.BlockSpec(place.shape, const),
        ],
        out_specs=[pl.BlockSpec((bb,) + s, lambda i: (i, 0, 0)) for s in shapes],
        out_shape=[jax.ShapeDtypeStruct((b,) + s, BF) for s in shapes],
        compiler_params=_cparams(1),
    )(cache_dk, cache_dv, cache_ckv, cache_kpe, w_k_p, w_v_p, place)


def _requests_per_step(b, n, tm):
    bb = 1
    if n == tm:
        while bb * 2 * n <= STREAM_ROWS and b % (bb * 2) == 0:
            bb *= 2
    return bb


def _fourier_kernel(scale, u_ref, v_ref, cn_ref, sn_ref, w_ref, o_ref):
    for bi in range(u_ref.shape[0]):
        f = (_dot(cn_ref[...], u_ref[bi]) - _dot(sn_ref[...], v_ref[bi])) * scale
        o_ref[bi] = _dot(f.astype(BF), w_ref[...]).astype(BF)


def _fourier(u, v, cn, sn, w_bd):
    b, n, w = u.shape
    tm = min(STREAM_ROWS, n)
    bb = _requests_per_step(b, n, tm)
    return pl.pallas_call(
        functools.partial(_fourier_kernel, 1.0 / math.sqrt(n * F_CH)),
        name="fourier_%d" % n,
        grid=(n // tm, b // bb),
        in_specs=[
            pl.BlockSpec((bb, n, w), lambda i, j: (j, 0, 0)),
            pl.BlockSpec((bb, n, w), lambda i, j: (j, 0, 0)),
            pl.BlockSpec((tm, n), lambda i, j: (i, 0)),
            pl.BlockSpec((tm, n), lambda i, j: (i, 0)),
            pl.BlockSpec((w, w), lambda i, j: (0, 0)),
        ],
        out_specs=pl.BlockSpec((bb, tm, w), lambda i, j: (j, i, 0)),
        out_shape=jax.ShapeDtypeStruct((b, n, w), BF),
        compiler_params=_cparams(2),
    )(u, v, cn, sn, w_bd)


def _scores(q, k_parts, s_ref):
    off, m, bounds = 0, None, []
    for k in k_parts:
        s = _dot_nt(k(), q)
        n_i = s.shape[0]
        s_ref[off:off + n_i, :] = s
        mi = jnp.max(s, axis=0, keepdims=True)
        m = mi if m is None else jnp.maximum(m, mi)
        bounds.append((off, n_i))
        off += n_i
    return m, bounds


def _weighted(m, vt_parts, bounds, s_ref):
    o = None
    for vt, (r, n_i) in zip(vt_parts, bounds):
        v = vt()
        ones = jnp.where(lax.broadcasted_iota(jnp.int32, (BF_ROWS, n_i), 0) == 0, 1.0, 0.0).astype(BF)
        v_ext = jnp.concatenate([v, ones], axis=0)
        for c in range(0, n_i, KEY_CHUNK):
            rows = min(KEY_CHUNK, n_i - c)
            p = jnp.exp2(s_ref[r + c:r + c + rows, :] - m).astype(BF)
            oi = _dot(v_ext[:, c:c + rows], p)
            o = oi if o is None else o + oi
    dv = o.shape[0] - BF_ROWS
    return o[:dv] * (1.0 / o[dv:dv + 1])


def _attn_kernel(latent, lam_init, d, *refs):
    if latent:
        (x_ref, mod_ref, yf_ref, qd_ref, kd_ref, vd_ref, kdc_ref, vdc_ref, qm_ref, km_ref, vm_ref, kmc_ref, vmc_ref,
         lam_ref, gs_ref, wo_ref, o_ref, mix_ref, s_ref) = refs
    else:
        (x_ref, mod_ref, yf_ref, qd_ref, kd_ref, vd_ref, qm_ref, km_ref, vm_ref,
         lam_ref, gs_ref, wo_ref, o_ref, mix_ref, s_ref) = refs
        kdc_ref = vdc_ref = kmc_ref = vmc_ref = None
    lf = lam_ref[...]
    lam = (jnp.exp(jnp.sum(lf[0:1] * lf[1:2], axis=-1, keepdims=True))
           - jnp.exp(jnp.sum(lf[2:3] * lf[3:4], axis=-1, keepdims=True)) + lam_init)
    mix_ref[:, 0:D_FOURIER] = yf_ref[0]
    gs = gs_ref[...]

    def keys(ref, cache_ref, sl):
        parts = [lambda: ref[0, :, sl]]
        if latent:
            parts.append(lambda: cache_ref[0, :, sl])
        return parts

    def values_t(ref, cache_ref, sl):
        parts = [lambda: ref[0, sl, :]]
        if latent:
            parts.append(lambda: cache_ref[0, sl, :])
        return parts

    def diff_query(sl, comp):
        def load():
            qh = qd_ref[0, :, sl]
            lane = lax.broadcasted_iota(jnp.int32, qh.shape, 1)
            keep = (lane >= DIFF_DH) if comp else (lane < DIFF_DH)
            return jnp.where(keep, qh, jnp.zeros_like(qh))
        return load

    items = []
    for hd in range(DIFF_HEADS):
        sl = slice(hd * HEAD_W, (hd + 1) * HEAD_W)
        for comp in range(2):
            items.append((diff_query(sl, comp), keys(kd_ref, kdc_ref, sl), values_t(vd_ref, vdc_ref, sl)))
    for hd in range(MLA_HEADS):
        sl = slice(hd * HEAD_W, (hd + 1) * HEAD_W)
        vsl = slice(hd * MLA_V, (hd + 1) * MLA_V)
        items.append(((lambda sl=sl: qm_ref[0, :, sl]), keys(km_ref, kmc_ref, sl), values_t(vm_ref, vmc_ref, vsl)))

    def start(i):
        return _scores(items[i][0](), items[i][1], s_ref.at[i % SCORE_BUFS])

    outs = []
    state = start(0)
    for i in range(len(items)):
        nxt = start(i + 1) if i + 1 < len(items) else None
        m, bounds = state
        buf = i % SCORE_BUFS
        outs.append(_weighted(m, items[i][2], bounds, s_ref.at[buf]))
        state = nxt
        if i < 2 * DIFF_HEADS and i % 2 == 1:
            hd = i // 2
            o = outs[i - 1] - lam * outs[i]
            o = o * lax.rsqrt(jnp.mean(o * o, axis=0, keepdims=True) + NORM_EPS) * gs * (1.0 - lam_init)
            mix_ref[:, D_FOURIER + hd * HEAD_W:D_FOURIER + (hd + 1) * HEAD_W] = o.T.astype(BF)

    mix_ref[:, D_FOURIER + D_DIFF:D_MIX] = jnp.concatenate(outs[2 * DIFF_HEADS:], axis=0).T.astype(BF)
    y = _dot(mix_ref[...], wo_ref[...])
    gt = mod_ref[0, :, 2 * d:3 * d]
    o_ref[0] = x_ref[0] + gt * y


def _attention(latent, l, x, modv, yfour, qd, kd, vd, qm, km, vm, cache, diff_lambda_l, g_subln_l, w_out_b):
    b, n, d = x.shape
    tq = min(ATTN_ROWS, n)
    bm = modv.shape[0]
    mod_idx = (lambda i, j: (i, 0, 0)) if bm > 1 else (lambda i, j: (0, 0, 0))
    const = lambda i, j: (0, 0)
    tok = lambda i, j: (i, j, 0)
    full = lambda i, j: (i, 0, 0)
    p = cache[0].shape[1] if latent else 0
    in_specs = [
        pl.BlockSpec((1, tq, d), tok),
        pl.BlockSpec((1, 1, modv.shape[2]), mod_idx),
        pl.BlockSpec((1, tq, D_FOURIER), tok),
        pl.BlockSpec((1, tq, D_DIFF), tok),
        pl.BlockSpec((1, n, D_DIFF), full),
        pl.BlockSpec((1, D_DIFF, n), full),
    ]
    args = [x, modv, yfour, qd, kd, vd]
    if latent:
        kdc, vdc, kmc, vmc = cache
        in_specs += [pl.BlockSpec((1, p, D_DIFF), full), pl.BlockSpec((1, D_DIFF, p), full)]
        args += [kdc, vdc]
    in_specs += [pl.BlockSpec((1, tq, D_MLA_P), tok), pl.BlockSpec((1, n, D_MLA_P), full), pl.BlockSpec((1, D_MLA, n), full)]
    args += [qm, km, vm]
    if latent:
        in_specs += [pl.BlockSpec((1, p, D_MLA_P), full), pl.BlockSpec((1, D_MLA, p), full)]
        args += [kmc, vmc]
    in_specs += [pl.BlockSpec((4, DIFF_DH), const), pl.BlockSpec((HEAD_W, 1), const), pl.BlockSpec(w_out_b.shape, const)]
    args += [diff_lambda_l, g_subln_l.reshape(HEAD_W, 1), w_out_b]
    lam_init = 0.8 - 0.6 * math.exp(-0.3 * l)
    return pl.pallas_call(
        functools.partial(_attn_kernel, latent, lam_init, d),
        name="attn_lat" if latent else "attn_ctx",
        grid=(b, n // tq),
        in_specs=in_specs,
        out_specs=pl.BlockSpec((1, tq, d), tok),
        out_shape=jax.ShapeDtypeStruct((b, n, d), F32),
        scratch_shapes=[pltpu.VMEM((tq, D_MIX), BF), pltpu.VMEM((SCORE_BUFS, n + p, tq), F32)],
        compiler_params=_cparams(2),
    )(*args)


def _router_kernel(d, packed, x_ref, mod_ref, g_ref, wr_ref, h_ref, aff_ref):
    sh = mod_ref[0, :, 3 * d:4 * d]
    sc = mod_ref[0, :, 4 * d:5 * d]
    tm = x_ref.shape[1]
    sub = SUB_ROWS if tm > SUB_ROWS else max(tm // 2, BF_ROWS)
    for bi in range(x_ref.shape[0]):
        for r0 in range(0, tm, sub):
            rows = slice(r0, r0 + sub)
            hb = (_rms(x_ref[bi, rows, :], g_ref[...]) * (1.0 + sc) + sh).astype(BF)
            h_ref[bi, rows, :] = _pack_halves(hb) if packed else hb
            logits = _dot_nt(wr_ref[...], hb)
            e = jnp.exp(logits - jnp.max(logits, axis=0, keepdims=True))
            aff_ref[bi, :, rows] = e / jnp.sum(e, axis=0, keepdims=True)


def _router(x, modv, g_ffn_l, w_router_t, packed):
    b, n, d = x.shape
    hw, hdt = (d // 2, jnp.int32) if packed else (d, BF)
    tm = min(STREAM_ROWS, n)
    bm = modv.shape[0]
    bb = _requests_per_step(b, n, tm) if bm == 1 else 1
    mod_idx = (lambda i, j: (i, 0, 0)) if bm > 1 else (lambda i, j: (0, 0, 0))
    const = lambda i, j: (0, 0)
    return pl.pallas_call(
        functools.partial(_router_kernel, d, packed),
        name="router_%d" % n,
        grid=(b // bb, n // tm),
        in_specs=[
            pl.BlockSpec((bb, tm, d), lambda i, j: (i, j, 0)),
            pl.BlockSpec((1, 1, modv.shape[2]), mod_idx),
            pl.BlockSpec((1, d), const),
            pl.BlockSpec(w_router_t.shape, const),
        ],
        out_specs=[pl.BlockSpec((bb, tm, hw), lambda i, j: (i, j, 0)), pl.BlockSpec((bb, N_EXPERTS, tm), lambda i, j: (i, 0, j))],
        out_shape=[jax.ShapeDtypeStruct((b, n, hw), hdt), jax.ShapeDtypeStruct((b, N_EXPERTS, n), F32)],
        compiler_params=_cparams(2),
    )(x, modv, g_ffn_l.reshape(1, d), w_router_t)


def _select_kernel(cap, aff_ref, sel_ref):
    a = aff_ref[...]
    e, n = a.shape
    bits = pltpu.bitcast(a, jnp.int32)
    capf = float(cap)

    def body(_, carry):
        lo, hi = carry
        mid = lo + ((hi - lo) >> 1)
        cnt = jnp.sum(jnp.where(bits >= mid, 1.0, 0.0), axis=1, keepdims=True)
        up = cnt >= capf
        return jnp.where(up, mid, lo), jnp.where(up, hi, mid)

    lo0 = jnp.zeros((e, 1), jnp.int32)
    hi0 = jnp.full((e, 1), 0x7F800000, jnp.int32)
    thr, _ = lax.fori_loop(0, 31, body, (lo0, hi0))
    gt = bits > thr
    eq = bits == thr
    need = capf - jnp.sum(jnp.where(gt, 1.0, 0.0), axis=1, keepdims=True)
    both = jnp.concatenate([jnp.where(gt, 1.0, 0.0), jnp.where(eq, 1.0, 0.0)], axis=0).astype(BF)
    ck = min(256, n)
    tri = jnp.where(lax.broadcasted_iota(jnp.int32, (ck, ck), 0) < lax.broadcasted_iota(jnp.int32, (ck, ck), 1), 1.0, 0.0).astype(BF)
    off = jnp.zeros((2 * e, 1), F32)
    pieces = []
    for k in range(n // ck):
        blk = both[:, k * ck:(k + 1) * ck]
        pieces.append(_dot(blk, tri) + off)
        off = off + jnp.sum(blk.astype(F32), axis=1, keepdims=True)
    cum = jnp.concatenate(pieces, axis=1) if len(pieces) > 1 else pieces[0]
    cum_gt, cum_eq = cum[:e], cum[e:]
    chosen = gt | (eq & (cum_eq < need))
    pos = cum_gt + jnp.minimum(cum_eq, need)
    sel_ref[...] = jnp.where(chosen, pos, -1.0).astype(jnp.int32)


def _select(aff, cap):
    b, e, n = aff.shape
    rows = b * e
    rb = min(rows, SELECT_ROWS)
    sel = pl.pallas_call(
        functools.partial(_select_kernel, cap),
        name="select_%d" % n,
        grid=(rows // rb,),
        in_specs=[pl.BlockSpec((rb, n), lambda i: (i, 0))],
        out_specs=pl.BlockSpec((rb, n), lambda i: (i, 0)),
        out_shape=jax.ShapeDtypeStruct((rows, n), jnp.int32),
        compiler_params=_cparams(1),
    )(aff.reshape(rows, n))
    return sel.reshape(b, e, n)


def _gather_kernel(cap, sel_ref, aff_ref, h_ref, xs_ref, gate_ref):
    n = h_ref.shape[1]
    e_tot = sel_ref.shape[1]
    for bi in range(h_ref.shape[0]):
        h = h_ref[bi]

        def one(e_idx, bi=bi):
            sel_e = sel_ref[bi, pl.ds(e_idx, 1), :]
            aff_e = aff_ref[bi, pl.ds(e_idx, 1), :]
            hit = sel_e == lax.broadcasted_iota(jnp.int32, (cap, n), 0)
            g = jnp.sum(jnp.where(hit, aff_e, 0.0), axis=1, keepdims=True)
            return jnp.where(hit, 1.0, 0.0).astype(BF), jnp.broadcast_to(g, (cap, HEAD_W))

        if cap >= 128:
            def body(e_idx, carry, bi=bi, h=h, one=one):
                p, g = one(e_idx)
                xs_ref[e_idx, bi] = _dot(p, h).astype(BF)
                gate_ref[e_idx, bi] = g
                return carry

            lax.fori_loop(0, e_tot, body, 0)
        else:
            ps, gs = zip(*[one(e_idx) for e_idx in range(e_tot)])
            xs = _dot(jnp.concatenate(ps, axis=0), h).astype(BF)
            for e_idx in range(e_tot):
                xs_ref[e_idx, bi] = xs[e_idx * cap:(e_idx + 1) * cap]
                gate_ref[e_idx, bi] = gs[e_idx]


def _gather(sel, aff, h2, cap):
    b, e, n = sel.shape
    d = h2.shape[2]
    bb = _requests_per_step(b, n, n)
    return pl.pallas_call(
        functools.partial(_gather_kernel, cap),
        name="gather_%d" % n,
        grid=(b // bb,),
        in_specs=[
            pl.BlockSpec((bb, e, n), lambda i: (i, 0, 0)),
            pl.BlockSpec((bb, e, n), lambda i: (i, 0, 0)),
            pl.BlockSpec((bb, n, d), lambda i: (i, 0, 0)),
        ],
        out_specs=[pl.BlockSpec((e, bb, cap, d), lambda i: (0, i, 0, 0)), pl.BlockSpec((e, bb, cap, HEAD_W), lambda i: (0, i, 0, 0))],
        out_shape=[jax.ShapeDtypeStruct((e, b, cap, d), BF), jax.ShapeDtypeStruct((e, b, cap, HEAD_W), F32)],
        compiler_params=_cparams(1),
    )(sel, aff, h2)


def _sc_dispatch(table, sel, aff, cap, n_experts):
    info = plsc.get_sparse_core_info()
    n_workers = info.num_cores * info.num_subcores
    lanes = info.num_lanes
    pairs, n = sel.shape
    b_tot = pairs // n_experts
    w = table.shape[1]
    per_w = pairs // n_workers
    ch = min(SC_CHUNK, cap)
    n_ch = cap // ch
    assert pairs % n_workers == 0 and cap % ch == 0 and ch % 8 == 0 and n % lanes == 0
    mesh = plsc.VectorSubcoreMesh(core_axis_name="c", subcore_axis_name="s")

    @functools.partial(
        pl.kernel, mesh=mesh, name="sc_dispatch",
        compiler_params=pltpu.CompilerParams(needs_layout_passes=False),
        out_type=[jax.ShapeDtypeStruct((pairs * cap, w), jnp.int32), jax.ShapeDtypeStruct((pairs * cap,), F32)],
        scratch_types=[pltpu.VMEM((n,), jnp.int32), pltpu.VMEM((n,), F32), pltpu.VMEM((n_ch, ch), jnp.int32),
                       pltpu.VMEM((cap,), F32), pltpu.VMEM((2, ch, w), jnp.int32), pltpu.SemaphoreType.DMA((2,))],
    )
    def k(table_hbm, sel_hbm, aff_hbm, xs_hbm, gate_hbm, sel_v, aff_v, tok_v, gate_v, rows_v, sem):
        wid = lax.axis_index("s") * info.num_cores + lax.axis_index("c")

        @pl.loop(0, per_w)
        def _(i):
            p = wid * per_w + i
            b = p // n_experts
            e = p - b * n_experts
            out_off = pl.multiple_of((e * b_tot + b) * cap, 8)
            pltpu.sync_copy(sel_hbm.at[p], sel_v)
            pltpu.sync_copy(aff_hbm.at[p], aff_v)
            lane = lax.iota(jnp.int32, lanes)

            @plsc.parallel_loop(0, n // lanes, unroll=4)
            def _(j):
                t0 = pl.multiple_of(j * lanes, lanes)
                s = sel_v[pl.ds(t0, lanes)]
                chosen = s >= 0
                slot = jnp.maximum(s, 0)
                plsc.store_scatter(tok_v, [slot // ch, slot % ch], lane + (t0 + b * n), mask=chosen)
                plsc.store_scatter(gate_v, [slot], aff_v[pl.ds(t0, lanes)], mask=chosen)

            prev = None
            for c in range(n_ch + 1):
                cur = None
                if c < n_ch:
                    cur = pltpu.async_copy(table_hbm.at[tok_v.at[c]], rows_v.at[c % 2], sem.at[c % 2])
                if prev is not None:
                    prev.wait()
                    pltpu.sync_copy(rows_v.at[(c - 1) % 2], xs_hbm.at[pl.ds(out_off + (c - 1) * ch, ch)])
                prev = cur
            pltpu.sync_copy(gate_v, gate_hbm.at[pl.ds(out_off, cap)])

    return k(table, sel, aff)


def _expert_kernel(steps_a, xa_ref, ga_ref, xb_ref, gb_ref, wg_ref, wu_ref, wd_ref, ya_ref, yb_ref, wg_s, wu_s, wd_s):
    j = pl.program_id(1)

    @pl.when(j == 0)
    def _():
        wg_s[...] = wg_ref[0, 0].astype(BF)
        wu_s[...] = wu_ref[0, 0].astype(BF)
        wd_s[...] = wd_ref[0, 0].astype(BF)

    def ffn(x_ref, g_ref, y_ref):
        rb = x_ref.shape[1]
        sub = min(rb, EXPERT_SUB_ROWS)
        for r0 in range(0, rb, sub):
            rows = slice(r0, r0 + sub)
            x = _unpack_halves(x_ref[0, rows, :]) if x_ref.dtype == jnp.int32 else x_ref[0, rows, :]
            a = _dot(x, wg_s[...])
            u = _dot(x, wu_s[...])
            mid = (a * (1.0 / (1.0 + jnp.exp(-a))) * u).astype(BF)
            y_ref[0, rows, :] = (_dot(mid, wd_s[...]) * g_ref[0, rows, 0:1]).astype(BF)

    @pl.when(j < steps_a)
    def _():
        ffn(xa_ref, ga_ref, ya_ref)

    @pl.when(j >= steps_a)
    def _():
        ffn(xb_ref, gb_ref, yb_ref)


def _experts(l, xs_a, gate_a, xs_b, gate_b, w_e_gate, w_e_up, w_e_down):
    e, rows_a, wa = xs_a.shape
    rows_b, wb = xs_b.shape[1:]
    d, ff = w_e_gate.shape[2:]
    rb_a, rb_b = math.gcd(rows_a, EXPERT_ROWS), math.gcd(rows_b, EXPERT_ROWS)
    steps_a, steps_b = rows_a // rb_a, rows_b // rb_b
    w_idx = lambda i, j: (l, i, 0, 0)
    idx_a = lambda i, j: (i, jnp.minimum(j, steps_a - 1), 0)
    idx_b = lambda i, j: (i, jnp.maximum(j - steps_a, 0), 0)
    return pl.pallas_call(
        functools.partial(_expert_kernel, steps_a),
        name="experts",
        grid=(e, steps_a + steps_b),
        in_specs=[
            pl.BlockSpec((1, rb_a, wa), idx_a),
            pl.BlockSpec((1, rb_a, HEAD_W), idx_a),
            pl.BlockSpec((1, rb_b, wb), idx_b),
            pl.BlockSpec((1, rb_b, HEAD_W), idx_b),
            pl.BlockSpec((1, 1, d, ff), w_idx),
            pl.BlockSpec((1, 1, d, ff), w_idx),
            pl.BlockSpec((1, 1, ff, d), w_idx),
        ],
        out_specs=[pl.BlockSpec((1, rb_a, d), idx_a), pl.BlockSpec((1, rb_b, d), idx_b)],
        out_shape=[jax.ShapeDtypeStruct((e, rows_a, d), BF), jax.ShapeDtypeStruct((e, rows_b, d), BF)],
        scratch_shapes=[pltpu.VMEM((d, ff), BF), pltpu.VMEM((d, ff), BF), pltpu.VMEM((ff, d), BF)],
        compiler_params=_cparams(2),
    )(xs_a, gate_a, xs_b, gate_b, w_e_gate, w_e_up, w_e_down)


def _combine_rows(cap, d, x_ref, mod_ref, selt_ref, ye_ref, pt_ref, rows):
    e_tot = ye_ref.shape[0]
    selt = selt_ref[0, rows, :]
    r = selt.shape[0]
    if cap % 128 == 0:
        lane = lax.broadcasted_iota(jnp.int32, (r, cap), 1)
        for e_idx in range(e_tot):
            pt_ref[rows, e_idx * cap:(e_idx + 1) * cap] = jnp.where(selt[:, e_idx:e_idx + 1] == lane, 1.0, 0.0).astype(BF)
    else:
        assert cap & (cap - 1) == 0
        shift = cap.bit_length() - 1
        col = lax.broadcasted_iota(jnp.int32, (e_tot, e_tot * cap), 1)
        spread = jnp.where((col >> shift) == lax.broadcasted_iota(jnp.int32, col.shape, 0), 1.0, 0.0).astype(BF)
        slot_of_col = _dot(selt.astype(F32).astype(BF), spread)
        want = (lax.broadcasted_iota(jnp.int32, (r, e_tot * cap), 1) & (cap - 1)).astype(F32)
        pt_ref[rows, :] = jnp.where(slot_of_col == want, 1.0, 0.0).astype(BF)
    y = _dot(pt_ref[rows, :], ye_ref[...].reshape(e_tot * cap, d))
    return x_ref[0, rows, :] + mod_ref[0, :, 5 * d:6 * d] * y


def _scatter_kernel(cap, d, x_ref, mod_ref, selt_ref, ye_ref, gf_ref, o_ref, pt_ref):
    o_ref[0] = _rms(_combine_rows(cap, d, x_ref, mod_ref, selt_ref, ye_ref, pt_ref, slice(None)), gf_ref[...])


def _scatter(x, modv, sel_t, ye, cap, g_final):
    b, n, d = x.shape
    e = ye.shape[0]
    tn = min(PRE_ROWS, n)
    bm = modv.shape[0]
    mod_idx = (lambda i, j: (i, 0, 0)) if bm > 1 else (lambda i, j: (0, 0, 0))
    return pl.pallas_call(
        functools.partial(_scatter_kernel, cap, d),
        name="scatter_%d" % n,
        grid=(b, n // tn),
        in_specs=[
            pl.BlockSpec((1, tn, d), lambda i, j: (i, j, 0)),
            pl.BlockSpec((1, 1, modv.shape[2]), mod_idx),
            pl.BlockSpec((1, tn, e), lambda i, j: (i, j, 0)),
            pl.BlockSpec((e, 1, cap, d), lambda i, j: (0, i, 0, 0)),
            pl.BlockSpec((1, d), lambda i, j: (0, 0)),
        ],
        out_specs=pl.BlockSpec((1, tn, d), lambda i, j: (i, j, 0)),
        out_shape=jax.ShapeDtypeStruct((b, n, d), F32),
        scratch_shapes=[pltpu.VMEM((tn, e * cap), BF)],
        compiler_params=_cparams(2),
    )(x, modv, sel_t, ye, g_final.reshape(1, d))


def _dft_tables(n):
    j = np.arange(n, dtype=np.int64)
    ang = ((j[:, None] * j[None, :]) % n).astype(np.float64) * (2.0 * math.pi / n)
    return jnp.asarray(np.cos(ang), dtype=BF), jnp.asarray(np.sin(ang), dtype=BF)


def _block_diag(blocks):
    g, r, c = blocks.shape
    out = jnp.zeros((g * r, g * c), blocks.dtype)
    for i in range(g):
        out = out.at[i * r:(i + 1) * r, i * c:(i + 1) * c].set(blocks[i])
    return out


def _rope_tables(n):
    rows = n // GRID_W
    row = jnp.repeat(jnp.arange(rows, dtype=F32), GRID_W)
    col = jnp.tile(jnp.arange(GRID_W, dtype=F32), rows)

    def ang(dim):
        nf = dim // 4
        inv = ROPE_BASE ** (-jnp.arange(nf, dtype=F32) / nf)
        return jnp.concatenate([row[:, None] * inv, col[:, None] * inv], axis=-1)

    a = ang(DIFF_DH)
    cos_d = jnp.tile(jnp.cos(a), (1, 4))
    sin_d = jnp.tile(jnp.concatenate([-jnp.sin(a), jnp.sin(a)], axis=1), (1, 2))
    a = ang(MLA_ROPE)
    ones = jnp.ones((n, MLA_NOPE), F32)
    pad = HEAD_W - MLA_NOPE - MLA_ROPE
    cos_m = jnp.concatenate([ones, jnp.cos(a), jnp.cos(a), jnp.ones((n, pad), F32)], axis=1)
    sin_m = jnp.concatenate([0 * ones, -jnp.sin(a), jnp.sin(a), jnp.zeros((n, pad), F32)], axis=1)
    return cos_d, sin_d, cos_m, sin_m


def _pad_heads(w, heads, lo, hi):
    k = w.shape[0]
    w3 = w.reshape(k, heads, -1)[:, :, lo:hi]
    return jnp.pad(w3, ((0, 0), (0, 0), (0, HEAD_W - (hi - lo)))).reshape(k, heads * HEAD_W)


def kernel(x_prompt, x_sample, cache_diff_k, cache_diff_v, cache_mla_ckv, cache_mla_kpe, c, c_ctx, w_ada, b_ada, g_attn, g_ffn, w_in, w_four, diff_lambda, g_subln, g_mla_q, w_mla_uq, g_mla_kv, w_mla_ukv, w_out, w_router, w_e_gate, w_e_up, w_e_down, g_final):
    depth, d = g_attn.shape
    b_ctx, n_ctx, _ = x_prompt.shape
    b_lat, n_lat, _ = x_sample.shape

    r = b_lat + 1
    r_pad = -(-r // 8) * 8
    cond = jnp.concatenate([c, c_ctx[None, :], jnp.zeros((r_pad - r, d), F32)], axis=0)
    mod = _modulation(cond, w_ada, b_ada)

    n_main = w_in.shape[2] - MLA_ROPE
    w_in_p = jnp.concatenate(
        [w_in[:, :, :n_main], jnp.zeros((depth, d, MLA_NOPE), F32), w_in[:, :, n_main:],
         jnp.zeros((depth, d, HEAD_W - MLA_NOPE - MLA_ROPE), F32)], axis=2).astype(BF)
    w_uq_p = jnp.stack([_pad_heads(w_mla_uq[l], MLA_HEADS, 0, MLA_NOPE + MLA_ROPE) for l in range(depth)]).astype(BF)
    w_k_p = jnp.stack([_pad_heads(w_mla_ukv[l], MLA_HEADS, 0, MLA_NOPE) for l in range(depth)]).astype(BF)
    w_v_p = w_mla_ukv.reshape(depth, MLA_KV_RANK, MLA_HEADS, MLA_NOPE + MLA_V)[..., MLA_NOPE:].reshape(depth, MLA_KV_RANK, D_MLA).astype(BF)
    w_out_p = w_out.astype(BF)
    w_router_t = jnp.swapaxes(w_router, 1, 2).astype(BF)
    place = jnp.zeros((MLA_ROPE, HEAD_W), F32).at[jnp.arange(MLA_ROPE), MLA_NOPE + jnp.arange(MLA_ROPE)].set(1.0)
    place = jnp.tile(place, (1, MLA_HEADS)).astype(BF)

    jc = jnp.arange(F_CH, dtype=jnp.int32)
    ang_c = ((jc[:, None] * jc[None, :]) % F_CH).astype(F32) * (2.0 * math.pi / F_CH)
    cc_bd = _block_diag(jnp.broadcast_to(jnp.cos(ang_c), (F_GROUPS, F_CH, F_CH))).astype(BF)
    sc_bd = _block_diag(jnp.broadcast_to(jnp.sin(ang_c), (F_GROUPS, F_CH, F_CH))).astype(BF)
    dft = {n: _dft_tables(n) for n in {n_ctx, n_lat}}
    ropes = _rope_tables(n_lat)

    def pre(latent, l, x, prev=None, combine=None):
        modv = (mod[l, :b_lat] if latent else mod[l, b_lat:b_lat + 1]).reshape(-1, 1, N_MOD * d)
        outs = _pre(latent, x, modv, g_attn[l], w_in_p[l], g_mla_q[l], w_uq_p[l], g_mla_kv[l], w_k_p[l], w_v_p[l],
                    cc_bd, sc_bd, ropes, l, depth, prev, combine)
        if combine is not None:
            x, outs = outs[0], outs[1:]
        return x, modv, outs

    def cache_prep(l):
        return _cache_prep(l, cache_diff_k, cache_diff_v, cache_mla_ckv, cache_mla_kpe, w_k_p[l], w_v_p[l], place)

    def mix_and_route(latent, l, x, modv, outs, cache=None):
        b, n, _ = x.shape
        cap = max(1, EC_CAPACITY_FACTOR * n // N_EXPERTS)
        u, v, qd, kd, vd, qm, km, vm = outs[:8]
        cn, sn = dft[n]
        yfour = _fourier(u, v, cn, sn, _block_diag(w_four[l]).astype(BF))
        x = _attention(latent, l, x, modv, yfour, qd, kd, vd, qm, km, vm, cache, diff_lambda[l], g_subln[l], w_out_p[l])
        h2, aff = _router(x, modv, g_ffn[l], w_router_t[l], packed=latent)
        sel = _select(aff, cap)
        e = sel.shape[1]
        if latent:
            xs, gate = _sc_dispatch(h2.reshape(b * n, h2.shape[2]), sel.reshape(b * e, n), aff.reshape(b * e, n), cap, e)
            gate = jnp.broadcast_to(gate.reshape(e, b * cap, 1), (e, b * cap, HEAD_W))
        else:
            xs, gate = _gather(sel, aff, h2, cap)
            gate = gate.reshape(e, b * cap, HEAD_W)
        return x, jnp.swapaxes(sel, 1, 2), cap, xs.reshape(e, b * cap, -1), gate

    x_l, mod_l, pre_l = pre(True, 0, x_sample)
    x_c, mod_c, pre_c = pre(False, 0, x_prompt)
    cache = cache_prep(0)
    for l in range(depth):
        new = tuple(pre_c[8:])
        x_l, selt_l, cap_l, xs_l, gate_l = mix_and_route(True, l, x_l, mod_l, pre_l, cache)
        x_c, selt_c, cap_c, xs_c, gate_c = mix_and_route(False, l, x_c, mod_c, pre_c)
        if l + 1 < depth:
            cache = cache_prep(l + 1)
        ye_c, ye_l = _experts(l, xs_c, gate_c, xs_l, gate_l, w_e_gate, w_e_up, w_e_down)
        ye_c = ye_c.reshape(-1, b_ctx, cap_c, d)
        ye_l = ye_l.reshape(-1, b_lat, cap_l, d)
        if l + 1 < depth:
            x_c, mod_c, pre_c = pre(False, l + 1, x_c, new, (mod_c, selt_c, ye_c, cap_c))
            x_l, mod_l, pre_l = pre(True, l + 1, x_l, None, (mod_l, selt_l, ye_l, cap_l))
        else:
            x_c = _scatter(x_c, mod_c, selt_c, ye_c, cap_c, g_final)
            x_l = _scatter(x_l, mod_l, selt_l, ye_l, cap_l, g_final)
    y_prompt, y_sample = x_c, x_l
    new_diff_k, new_diff_v, new_mla_ckv, new_mla_kpe = new
    return (y_prompt, y_sample, new_diff_k, new_diff_v, new_mla_ckv, new_mla_kpe)
```
